```python
import functools
import jax
import jax.numpy as jnp
from jax import lax
import numpy as np

D_MODEL = 1024
BATCH = 32
SEQ = 256
DEPTH = 2
DEC_BATCH = 4
DEC_SEQ = 2048
PAST_LEN = 512

GRID_W = 64
RMS_EPS = 1e-6
N_BRANCH = 3
A_HEADS = 8
A_KV_HEADS = 2
A_GROUP = A_HEADS // A_KV_HEADS
A_HEAD_DIM = 64
A_WIDTH = A_HEADS * A_HEAD_DIM
A_KV_WIDTH = A_KV_HEADS * A_HEAD_DIM
A_WINDOW = 128
A_BLOCK = 128
ROPE_THETA = 10000.0
NEG_INF = -1e30
B_HEADS = 4
B_KEY_DIM = 128
B_VAL_DIM = 128
B_WIDTH = B_HEADS * B_KEY_DIM
B_CHUNK = 16
C_WIDTH = 512
C_HEADS = 8
C_BLOCK = C_WIDTH // C_HEADS
C_CONV = 4
C_PAD = (2, 1)
C_EXP = 8.0
FFN_DIM = 2816
N_EXPERTS = 8
TOP_K = 2
EXPERT_DIM = 3584
N_DENSE = (DEPTH + 1) // 2
N_MOE = DEPTH // 2

IN_SPLITS = (A_WIDTH, A_KV_WIDTH, A_KV_WIDTH,
             B_WIDTH, B_WIDTH, B_WIDTH, B_WIDTH, B_WIDTH,
             C_WIDTH, C_WIDTH, N_BRANCH * D_MODEL)
IN_COLS = sum(IN_SPLITS)
IN_OFFSETS = tuple(sum(IN_SPLITS[:i + 1]) for i in range(len(IN_SPLITS) - 1))

kernel_name = 'hybrid_diffusion_prefix_step'


def rms_norm(x, g):
    x32 = x.astype(jnp.float32)
    y = x32 * lax.rsqrt(jnp.mean(x32 * x32, axis=-1, keepdims=True) + RMS_EPS)
    return y.astype(x.dtype) * g


def _rotary(x, pos):
    half = x.shape[-1] // 2
    inv = ROPE_THETA ** (-jnp.arange(half, dtype=jnp.float32) / half)
    ang = pos.astype(jnp.float32)[:, None] * inv[None, :]
    cos, sin = jnp.cos(ang)[:, None, :], jnp.sin(ang)[:, None, :]
    x32 = x.astype(jnp.float32)
    x1, x2 = x32[..., :half], x32[..., half:]
    return jnp.concatenate([x1 * cos - x2 * sin, x1 * sin + x2 * cos], axis=-1).astype(x.dtype)


def axial_rope(x, rows):
    row = jnp.repeat(jnp.arange(rows), GRID_W)
    col = jnp.tile(jnp.arange(GRID_W), rows)
    h = x.shape[-1] // 2
    return jnp.concatenate([_rotary(x[..., :h], row), _rotary(x[..., h:], col)], axis=-1)


def _attend(q, ks, vs, masks, sink):
    b, nq = q.shape[0], q.shape[1]
    scale = A_HEAD_DIM ** -0.5
    logits = []
    for k, m in zip(ks, masks):
        s = jnp.einsum('bqhgd,bkhd->bhgqk', q, k).astype(jnp.float32) * scale
        logits.append(s if m is None else jnp.where(m, s, NEG_INF))
    sink_col = jnp.broadcast_to(sink.astype(jnp.float32).reshape(1, A_KV_HEADS, A_GROUP, 1, 1),
                                (b, A_KV_HEADS, A_GROUP, nq, 1))
    p = jax.nn.softmax(jnp.concatenate(logits + [sink_col], axis=-1), axis=-1)
    out, off = None, 0
    for k, v in zip(ks, vs):
        n = k.shape[1]
        o = jnp.einsum('bhgqk,bkhd->bqhgd', p[..., off:off + n].astype(v.dtype), v)
        out = o if out is None else out + o
        off += n
    return out


def context_attention(q, k, v, sink):
    b, t = q.shape[:2]
    nq = t // A_BLOCK
    qb = q.reshape(b, nq, A_BLOCK, A_KV_HEADS, A_GROUP, A_HEAD_DIM).swapaxes(0, 1)
    out = lax.map(lambda qq: _attend(qq, [k], [v], [None], sink), qb)
    return out.swapaxes(0, 1).reshape(b, t, A_WIDTH)


def latent_attention(q, k, v, k_ctx, v_ctx, sink):
    b, t = q.shape[:2]
    nb = t // A_BLOCK
    span = A_BLOCK + 2 * A_WINDOW
    pad = ((0, 0), (A_WINDOW, A_WINDOW), (0, 0), (0, 0))
    kp, vp = jnp.pad(k, pad), jnp.pad(v, pad)
    qb = q.reshape(b, nb, A_BLOCK, A_KV_HEADS, A_GROUP, A_HEAD_DIM).swapaxes(0, 1)
    qi = jnp.arange(A_BLOCK)[:, None]
    ki = jnp.arange(span)[None, :]
    band = jnp.abs(ki - A_WINDOW - qi) <= A_WINDOW

    def block(args):
        j, qq = args
        start = j * A_BLOCK
        kw = lax.dynamic_slice_in_dim(kp, start, span, axis=1)
        vw = lax.dynamic_slice_in_dim(vp, start, span, axis=1)
        kpos = start - A_WINDOW + ki
        mask = band & (kpos >= 0) & (kpos < t)
        return _attend(qq, [kw, k_ctx], [vw, v_ctx], [mask, None], sink)

    out = lax.map(block, (jnp.arange(nb), qb))
    return out.swapaxes(0, 1).reshape(b, t, A_WIDTH)


def gla_chunked(q, k, v, log_f, s0):
    b, t, h, dk = q.shape
    dv = v.shape[-1]
    n = t // B_CHUNK
    qc = q.reshape(b, n, B_CHUNK, h, dk)
    kc = k.reshape(b, n, B_CHUNK, h, dk)
    vc = v.reshape(b, n, B_CHUNK, h, dv)
    cum = jnp.cumsum(log_f.reshape(b, n, B_CHUNK, h, dk), axis=2)
    causal = jnp.tril(jnp.ones((B_CHUNK, B_CHUNK), dtype=bool))[:, :, None, None]
    decay = jnp.exp(jnp.where(causal, cum[:, :, :, None] - cum[:, :, None, :], -jnp.inf))
    scores = jnp.sum(qc[:, :, :, None] * kc[:, :, None] * decay, axis=-1)
    o_intra = jnp.einsum('bntsh,bnshv->bnthv', scores, vc)
    q_in = qc * jnp.exp(cum)
    kv = jnp.einsum('bnchk,bnchv->bnhkv', kc * jnp.exp(cum[:, :, -1:] - cum), vc)
    g_end = jnp.exp(cum[:, :, -1])

    def step(s, xs):
        qn, gn, kvn = xs
        return gn[..., None] * s + kvn, jnp.einsum('bchk,bhkv->bchv', qn, s)

    s_fin, o_inter = lax.scan(step, s0, (q_in.swapaxes(0, 1), g_end.swapaxes(0, 1), kv.swapaxes(0, 1)))
    o = o_intra + o_inter.swapaxes(0, 1)
    return o.reshape(b, t, h, dv), s_fin


def _hgrn2_forget(z, lb):
    lb = lb.reshape(B_HEADS, B_KEY_DIM)
    log_f = jnp.logaddexp(jnp.log(lb), jnp.log1p(-lb) + jax.nn.log_sigmoid(z))
    k = (1.0 - lb) * jax.nn.sigmoid(-z)
    return k, log_f


def hgrn2_mix(q, i, zf_f, zf_b, gate, lb, norm_g, s0_f, s0_b):
    b, t, _ = q.shape
    heads = lambda a: a.astype(jnp.float32).reshape(b, t, B_HEADS, -1)
    flip = functools.partial(jnp.flip, axis=1)
    qh = jax.nn.silu(heads(q)) * (B_KEY_DIM ** -0.5)
    vh = heads(i)
    lb = lb.astype(jnp.float32)
    k_f, lf_f = _hgrn2_forget(heads(zf_f), lb[0])
    k_b, lf_b = _hgrn2_forget(heads(zf_b), lb[1])
    o_f, s_f = gla_chunked(qh, k_f, vh, lf_f, s0_f.astype(jnp.float32))
    o_b, s_b = gla_chunked(flip(qh), flip(k_b), flip(vh), flip(lf_b), s0_b.astype(jnp.float32))
    o = rms_norm(o_f + flip(o_b), norm_g.astype(jnp.float32)) * jax.nn.silu(heads(gate))
    return o.reshape(b, t, B_WIDTH).astype(q.dtype), jnp.stack([s_f, s_b], axis=1)


def short_conv(x, w, bias):
    y = lax.conv_general_dilated(x, w[:, None, :], window_strides=(1,), padding=[C_PAD],
                                 dimension_numbers=('NWC', 'WIO', 'NWC'),
                                 feature_group_count=C_WIDTH)
    return y + bias


def rglru_scan(x, wa, ba, wx, bx, lam, h0, reset):
    b, t, _ = x.shape
    xb = x.reshape(b, t, C_HEADS, C_BLOCK)
    r = jax.nn.sigmoid(jnp.einsum('bthi,hij->bthj', xb, wa.astype(jnp.float32)).reshape(b, t, C_WIDTH)
                       + ba.astype(jnp.float32))
    ig = jax.nn.sigmoid(jnp.einsum('bthi,hij->bthj', xb, wx.astype(jnp.float32)).reshape(b, t, C_WIDTH)
                        + bx.astype(jnp.float32))
    log_a = C_EXP * r * jax.nn.log_sigmoid(lam.astype(jnp.float32))
    a = jnp.exp(log_a)
    mult = jnp.sqrt(-jnp.expm1(2.0 * log_a))
    if reset:
        mult = mult.at[:, 0].set(1.0)
    u = (mult * ig * x).at[:, 0].add(a[:, 0] * h0.astype(jnp.float32))

    def combine(lhs, rhs):
        return lhs[0] * rhs[0], rhs[0] * lhs[1] + rhs[1]

    _, h = lax.associative_scan(combine, (a, u), axis=1)
    return h, h[:, -1]


def rglru_mix(xc, yc, lp, h0_f, h0_b, reset):
    x = short_conv(xc, lp['conv_w'], lp['conv_b']).astype(jnp.float32)
    flip = functools.partial(jnp.flip, axis=1)
    h_f, s_f = rglru_scan(x, lp['wa'][0], lp['ba'][0], lp['wx'][0], lp['bx'][0], lp['lam'][0], h0_f, reset)
    h_b, s_b = rglru_scan(flip(x), lp['wa'][1], lp['ba'][1], lp['wx'][1], lp['bx'][1], lp['lam'][1], h0_b, reset)
    y = (h_f + flip(h_b)) * jax.nn.gelu(yc.astype(jnp.float32))
    return y.astype(xc.dtype), jnp.stack([s_f, s_b], axis=1)


def swiglu(u, wg, wu, wd):
    return (jax.nn.silu(u @ wg) * (u @ wu)) @ wd


def moe_swiglu(u, router, wg, wu, wd):
    b, t, d = u.shape
    xt = u.reshape(b * t, d)
    logits = (xt @ router).astype(jnp.float32)
    top_v, top_i = lax.top_k(logits, TOP_K)
    w = jax.nn.softmax(top_v, axis=-1)
    combine = jnp.sum(jax.nn.one_hot(top_i, N_EXPERTS, dtype=jnp.float32) * w[..., None], axis=1)
    combine = combine.astype(xt.dtype)
    out = None
    for e in range(N_EXPERTS):
        ye = combine[:, e:e + 1] * swiglu(xt, wg[e], wu[e], wd[e])
        out = ye if out is None else out + ye
    return out.reshape(b, t, d)


def token_mixing(u, lp, ctx_side, rows, ctx_tensors):
    b, t, _ = u.shape
    (qa, ka, va, qb, ib, zf_f, zf_b, gb, xc, yc, zg) = jnp.split(u @ lp['w_in'], IN_OFFSETS, axis=-1)
    qa = qa.reshape(b, t, A_HEADS, A_HEAD_DIM)
    ka = ka.reshape(b, t, A_KV_HEADS, A_HEAD_DIM)
    va = va.reshape(b, t, A_KV_HEADS, A_HEAD_DIM)
    if ctx_side:
        attn = context_attention(qa, ka, va, lp['sink'])
        s_h0 = jnp.zeros((b, 2, B_HEADS, B_KEY_DIM, B_VAL_DIM), jnp.float32)
        s_l0 = jnp.zeros((b, 2, C_WIDTH), jnp.float32)
    else:
        k_ctx, v_ctx, s_h0, s_l0 = ctx_tensors
        attn = latent_attention(axial_rope(qa, rows), axial_rope(ka, rows), va,
                                k_ctx.astype(ka.dtype), v_ctx.astype(va.dtype), lp['sink'])
    hg, s_h = hgrn2_mix(qb, ib, zf_f, zf_b, gb, lp['lb'], lp['hgrn_norm'], s_h0[:, 0], s_h0[:, 1])
    lr, s_l = rglru_mix(xc, yc, lp, s_l0[:, 0], s_l0[:, 1], ctx_side)
    gates = jax.nn.sigmoid(zg.astype(jnp.float32)).astype(u.dtype).reshape(b, t, N_BRANCH, D_MODEL)
    merged = (gates[:, :, 0] * (attn @ lp['w_ba']) + gates[:, :, 1] * (hg @ lp['w_bb'])
              + gates[:, :, 2] * (lr @ lp['w_bc']))
    out = merged @ lp['w_out']
    if ctx_side:
        return out, (ka, va, s_h, s_l)
    return out, None


def trunk_layer(x, cond, lp, ctx_side, rows, ctx_tensors):
    mod = (jax.nn.silu(cond) @ lp['w_mod'] + lp['b_mod'])[..., None, :]
    sh1, sc1, g1, sh2, sc2, g2 = jnp.split(mod, 6, axis=-1)
    u = rms_norm(x, lp['norm_mix']) * (1.0 + sc1) + sh1
    mix, ctx_out = token_mixing(u, lp, ctx_side, rows, ctx_tensors)
    x = x + g1 * mix
    u = rms_norm(x, lp['norm_ffn']) * (1.0 + sc2) + sh2
    x = x + g2 * lp['ffn'](u)
    return x, ctx_out


def setup_inputs(seed: int = 0) -> dict:
    key = jax.random.key(seed)
    keys = jax.random.split(key, 64)
    counter = [0]

    def nxt():
        k = keys[counter[0]]
        counter[0] += 1
        return k

    def nrm(shape, scale):
        return scale * jax.random.normal(nxt(), shape, jnp.float32)

    def gain(shape):
        return 1.0 + nrm(shape, 0.02)

    D = D_MODEL
    u = jax.random.uniform(nxt(), (DEPTH, 2, C_WIDTH), jnp.float32, 0.9, 0.999)
    p = u ** (1.0 / C_EXP)
    lam = jnp.log(p) - jnp.log1p(-p)
    return {
        'x_prompt': nrm((BATCH, SEQ, D), 1.0),
        'x_sample': nrm((DEC_BATCH, DEC_SEQ, D), 1.0),
        'cache_attn_k': nrm((DEC_BATCH, DEPTH, PAST_LEN, A_KV_HEADS, A_HEAD_DIM), 1.0),
        'cache_attn_v': nrm((DEC_BATCH, DEPTH, PAST_LEN, A_KV_HEADS, A_HEAD_DIM), 1.0),
        'state_hgrn': nrm((DEC_BATCH, DEPTH, 2, B_HEADS, B_KEY_DIM, B_VAL_DIM), 0.5),
        'state_lru': nrm((DEC_BATCH, DEPTH, 2, C_WIDTH), 0.5),
        'c': nrm((DEC_BATCH, D), 1.0),
        'c_ctx': nrm((D,), 1.0),
        'w_mod': nrm((DEPTH, D, 6 * D), 0.5 * D ** -0.5),
        'b_mod': nrm((DEPTH, 6 * D), 0.02),
        'norm_mix_g': gain((DEPTH, D)),
        'norm_ffn_g': gain((DEPTH, D)),
        'w_in': nrm((DEPTH, D, IN_COLS), D ** -0.5),
        'attn_sink': nrm((DEPTH, A_HEADS), 1.0),
        'hgrn_lower': nrm((DEPTH, 2, B_WIDTH), 1.0),
        'hgrn_norm_g': gain((DEPTH, B_VAL_DIM)),
        'lru_conv_w': nrm((DEPTH, C_CONV, C_WIDTH), C_CONV ** -0.5),
        'lru_conv_b': nrm((DEPTH, C_WIDTH), 0.02),
        'lru_wa': nrm((DEPTH, 2, C_HEADS, C_BLOCK, C_BLOCK), C_BLOCK ** -0.5),
        'lru_ba': nrm((DEPTH, 2, C_WIDTH), 0.02),
        'lru_wx': nrm((DEPTH, 2, C_HEADS, C_BLOCK, C_BLOCK), C_BLOCK ** -0.5),
        'lru_bx': nrm((DEPTH, 2, C_WIDTH), 0.02),
        'lru_lambda': lam,
        'w_branch_a': nrm((DEPTH, A_WIDTH, D), A_WIDTH ** -0.5),
        'w_branch_b': nrm((DEPTH, B_WIDTH, D), B_WIDTH ** -0.5),
        'w_branch_c': nrm((DEPTH, C_WIDTH, D), C_WIDTH ** -0.5),
        'w_out': nrm((DEPTH, D, D), D ** -0.5),
        'ffn_w_gate': nrm((N_DENSE, D, FFN_DIM), D ** -0.5),
        'ffn_w_up': nrm((N_DENSE, D, FFN_DIM), D ** -0.5),
        'ffn_w_down': nrm((N_DENSE, FFN_DIM, D), FFN_DIM ** -0.5),
        'moe_router': nrm((N_MOE, D, N_EXPERTS), D ** -0.5),
        'moe_w_gate': nrm((N_MOE, N_EXPERTS, D, EXPERT_DIM), D ** -0.5),
        'moe_w_up': nrm((N_MOE, N_EXPERTS, D, EXPERT_DIM), D ** -0.5),
        'moe_w_down': nrm((N_MOE, N_EXPERTS, EXPERT_DIM, D), EXPERT_DIM ** -0.5),
        'final_norm_g': gain((D,)),
    }


def reference(x_prompt, x_sample, cache_attn_k, cache_attn_v, state_hgrn, state_lru, c,
              c_ctx, w_mod, b_mod, norm_mix_g, norm_ffn_g, w_in, attn_sink, hgrn_lower,
              hgrn_norm_g, lru_conv_w, lru_conv_b, lru_wa, lru_ba, lru_wx, lru_bx, lru_lambda,
              w_branch_a, w_branch_b, w_branch_c, w_out, ffn_w_gate, ffn_w_up, ffn_w_down,
              moe_router, moe_w_gate, moe_w_up, moe_w_down, final_norm_g):
    lb_cum = jnp.cumsum(jax.nn.softmax(hgrn_lower.astype(jnp.float32), axis=0), axis=0)
    lb_layers = lb_cum - lb_cum[:1]
    layers = []
    for l in range(DEPTH):
        m = l // 2
        if l % 2 == 0:
            ffn = functools.partial(swiglu, wg=ffn_w_gate[m], wu=ffn_w_up[m], wd=ffn_w_down[m])
        else:
            ffn = functools.partial(moe_swiglu, router=moe_router[m], wg=moe_w_gate[m],
                                    wu=moe_w_up[m], wd=moe_w_down[m])
        layers.append(dict(
            w_mod=w_mod[l], b_mod=b_mod[l], norm_mix=norm_mix_g[l], norm_ffn=norm_ffn_g[l],
            w_in=w_in[l], sink=attn_sink[l], lb=lb_layers[l], hgrn_norm=hgrn_norm_g[l],
            conv_w=lru_conv_w[l], conv_b=lru_conv_b[l], wa=lru_wa[l], ba=lru_ba[l],
            wx=lru_wx[l], bx=lru_bx[l], lam=lru_lambda[l], w_ba=w_branch_a[l],
            w_bb=w_branch_b[l], w_bc=w_branch_c[l], w_out=w_out[l], ffn=ffn))

    xp = x_prompt
    ks_, vs_, hs_, ls_ = [], [], [], []
    for lp in layers:
        xp, (k_l, v_l, sh_l, sl_l) = trunk_layer(xp, c_ctx, lp, True, 0, None)
        ks_.append(k_l)
        vs_.append(v_l)
        hs_.append(sh_l)
        ls_.append(sl_l)
    y_prompt = rms_norm(xp, final_norm_g)
    new_attn_k = jnp.stack(ks_, axis=1)
    new_attn_v = jnp.stack(vs_, axis=1)
    new_state_hgrn = jnp.stack(hs_, axis=1)
    new_state_lru = jnp.stack(ls_, axis=1)

    rows = x_sample.shape[1] // GRID_W
    xs = x_sample
    for l, lp in enumerate(layers):
        xs, _ = trunk_layer(xs, c, lp, False, rows,
                            (cache_attn_k[:, l], cache_attn_v[:, l], state_hgrn[:, l], state_lru[:, l]))
    y_sample = rms_norm(xs, final_norm_g)
    return (y_prompt, y_sample, new_attn_k, new_attn_v, new_state_hgrn, new_state_lru)
```

```python
import functools

import numpy as np
import jax
import jax.numpy as jnp
from jax import lax
from jax.experimental import pallas as pl
from jax.experimental.pallas import tpu as pltpu

F32 = jnp.float32
BF16 = jnp.bfloat16
HIGHEST = lax.Precision.HIGHEST

D_MODEL = 1024
BATCH = 32
SEQ = 256
DEPTH = 2
DEC_BATCH = 4
DEC_SEQ = 2048
PAST_LEN = 512
GRID_W = 64
RMS_EPS = 1e-6
A_HEADS = 8
A_KV_HEADS = 2
A_GROUP = 4
A_HEAD_DIM = 64
A_WIDTH = 512
A_WINDOW = 128
ROPE_THETA = 10000.0
NEG_INF = -1e30
B_HEADS = 4
B_KEY_DIM = 128
B_WIDTH = 512
C_WIDTH = 512
C_HEADS = 8
C_BLOCK = 64
C_EXP = 8.0
FFN_DIM = 2816
N_EXPERTS = 8
EXPERT_DIM = 3584

N_CTX_TOK = BATCH * SEQ
N_LAT_TOK = DEC_BATCH * DEC_SEQ
N_TOK = N_CTX_TOK + N_LAT_TOK
N_COND = 8

ZG_OFF = 0
QA_OFF = 3072
KA_OFF = 3584
VA_OFF = 3712
B_OFF = 3840
C_OFF = 6400
IN_COLS = 7424

V7X_VMEM_LIMIT = 56 * 1024 * 1024
HG_CHUNK = 128
HG_LEVELS = (64, 32, 16, 8, 4, 2, 1)


def _cparams(sem, vmem=V7X_VMEM_LIMIT):
    return pltpu.CompilerParams(dimension_semantics=sem, vmem_limit_bytes=vmem)


def _mod_row(i, tm):
    nct = N_CTX_TOK // tm
    return jnp.where(i < nct, 0, 1 + (i - nct) // (DEC_SEQ // tm))


def _silu(x):
    return x * jax.nn.sigmoid(x)


def _log_sigmoid(z):
    return jnp.minimum(z, 0.0) - jnp.log1p(jnp.exp(-jnp.abs(z)))


def _rms(x, g):
    ms = jnp.mean(x * x, axis=-1, keepdims=True)
    return x * lax.rsqrt(ms + RMS_EPS) * g


def _dot(a, b):
    return jnp.dot(a, b, preferred_element_type=F32)


def _dot_nt(a, b):
    return lax.dot_general(a, b, (((1,), (1,)), ((), ())), preferred_element_type=F32)


def _dot_tn(a, b):
    return lax.dot_general(a, b, (((0,), (0,)), ((), ())), preferred_element_type=F32)


def _mod_kernel(c_ref, w_ref, b_ref, o_ref):
    a = _silu(c_ref[...])
    o_ref[0] = jnp.dot(a, w_ref[0], preferred_element_type=F32, precision=HIGHEST) + b_ref[0]


def _modulation(cond, w_mod, b_mod):
    tn = 1536
    return pl.pallas_call(
        _mod_kernel,
        grid=(DEPTH, 6 * D_MODEL // tn),
        in_specs=[pl.BlockSpec((N_COND, D_MODEL), lambda l, j: (0, 0)),
                  pl.BlockSpec((1, D_MODEL, tn), lambda l, j: (l, 0, j)),
                  pl.BlockSpec((1, 1, tn), lambda l, j: (l, 0, j))],
        out_specs=pl.BlockSpec((1, N_COND, tn), lambda l, j: (l, 0, j)),
        out_shape=jax.ShapeDtypeStruct((DEPTH, N_COND, 6 * D_MODEL), F32),
        compiler_params=_cparams(("arbitrary", "arbitrary")),
        name="modulation",
    )(cond, w_mod, b_mod.reshape(DEPTH, 1, 6 * D_MODEL))


IN_TM = 256
IN_TN = 256


def _in_proj_kernel(x_ref, m_ref, g_ref, w_ref, o_ref):
    u = _rms(x_ref[...], g_ref[...]) * (1.0 + m_ref[0, 1]) + m_ref[0, 0]
    ub = u.astype(BF16)
    for j in range(IN_COLS // IN_TN):
        sl = slice(j * IN_TN, (j + 1) * IN_TN)
        o_ref[:, sl] = _dot(ub, w_ref[:, sl])


def _in_proj(x, mod_l, g, w):
    tm = IN_TM
    return pl.pallas_call(
        _in_proj_kernel,
        grid=(N_TOK // tm,),
        in_specs=[pl.BlockSpec((tm, D_MODEL), lambda i: (i, 0)),
                  pl.BlockSpec((1, 6, 1, D_MODEL), lambda i: (_mod_row(i, tm), 0, 0, 0)),
                  pl.BlockSpec((1, D_MODEL), lambda i: (0, 0)),
                  pl.BlockSpec((D_MODEL, IN_COLS), lambda i: (0, 0), pipeline_mode=pl.Buffered(1))],
        out_specs=pl.BlockSpec((tm, IN_COLS), lambda i: (i, 0)),
        out_shape=jax.ShapeDtypeStruct((N_TOK, IN_COLS), F32),
        compiler_params=_cparams(("arbitrary",)),
        name="in_proj",
    )(x, mod_l, g, w)


def _softmax_pv(s_list, v_list, sink):
    m = sink
    for s in s_list:
        m = jnp.maximum(m, jnp.max(s, axis=-1, keepdims=True))
    den = jnp.exp(sink - m)
    out = None
    for s, v in zip(s_list, v_list):
        p = jnp.exp(s - m)
        den = den + jnp.sum(p, axis=-1, keepdims=True)
        o = _dot(p.astype(BF16), v)
        out = o if out is None else out + o
    return out / den


def _sink_column(sink_ref, h, rows_per_head):
    n = A_GROUP * rows_per_head
    row = lax.broadcasted_iota(jnp.int32, (n, 1), 0)
    col = jnp.full((n, 1), sink_ref[A_GROUP * h + A_GROUP - 1], F32)
    for g in range(A_GROUP - 2, -1, -1):
        col = jnp.where(row < (g + 1) * rows_per_head, sink_ref[A_GROUP * h + g], col)
    return col


def _ctx_attn_kernel(sink_ref, q_ref, k_ref, v_ref, o_ref):
    q = q_ref[...]
    k = k_ref[...]
    v = v_ref[...]
    scale = A_HEAD_DIM ** -0.5
    outs = []
    for h in range(A_KV_HEADS):
        hs = slice(h * A_HEAD_DIM, (h + 1) * A_HEAD_DIM)
        kh = k[:, hs].astype(BF16)
        vh = v[:, hs].astype(BF16)
        qs = jnp.concatenate(
            [q[:, (A_GROUP * h + g) * A_HEAD_DIM:(A_GROUP * h + g + 1) * A_HEAD_DIM] for g in range(A_GROUP)],
            axis=0).astype(BF16)
        s = _dot_nt(qs, kh) * scale
        o = _softmax_pv([s], [vh], _sink_column(sink_ref, h, SEQ))
        outs.extend(o[g * SEQ:(g + 1) * SEQ] for g in range(A_GROUP))
    o_ref[...] = jnp.concatenate(outs, axis=-1).astype(BF16)


def _ctx_attention(proj, sink):
    return pl.pallas_call(
        _ctx_attn_kernel,
        grid=(BATCH,),
        in_specs=[pl.BlockSpec(memory_space=pltpu.SMEM),
                  pl.BlockSpec((SEQ, A_WIDTH), lambda b: (b, QA_OFF // A_WIDTH)),
                  pl.BlockSpec((SEQ, 128), lambda b: (b, KA_OFF // 128)),
                  pl.BlockSpec((SEQ, 128), lambda b: (b, VA_OFF // 128))],
        out_specs=pl.BlockSpec((SEQ, A_WIDTH), lambda b: (b, 0)),
        out_shape=jax.ShapeDtypeStruct((N_TOK, A_WIDTH), BF16),
        compiler_params=_cparams(("arbitrary",)),
        name="ctx_attention",
    )(sink, proj, proj, proj)


def _rope_tables():
    half = A_HEAD_DIM // 4
    inv = ROPE_THETA ** (-np.arange(half, dtype=np.float64) / half)
    t = np.arange(DEC_SEQ)
    row = (t // GRID_W)[:, None] * inv[None, :]
    col = (t % GRID_W)[:, None] * inv[None, :]
    cos = np.concatenate([np.cos(row), np.cos(row), np.cos(col), np.cos(col)], axis=1)
    sin = np.concatenate([-np.sin(row), np.sin(row), -np.sin(col), np.sin(col)], axis=1)
    return (jnp.asarray(np.tile(cos, (1, 2)), F32), jnp.asarray(np.tile(sin, (1, 2)), F32))


def _rope(x, cos, sin, first):
    part = jnp.where(first, pltpu.roll(x, 128 - 16, 1), pltpu.roll(x, 16, 1))
    return x * cos + part * sin


def _lat_attn_kernel(sink_ref, q_ref, km_ref, k0_ref, kp_ref, vm_ref, v0_ref, vp_ref, kc_ref, vc_ref,
                     cm_ref, c0_ref, cp_ref, sm_ref, s0_ref, sp_ref, buf_ref, o_ref):
    del buf_ref
    j = pl.program_id(1)
    blk = A_WINDOW
    lane = lax.broadcasted_iota(jnp.int32, (blk, 128), 1)
    first = (lane % 32) < 16
    c0 = c0_ref[...]
    s0 = s0_ref[...]
    q = q_ref[...]
    qr = [_rope(q[:, c * 128:(c + 1) * 128], c0, s0, first) for c in range(A_WIDTH // 128)]
    kw = jnp.concatenate([_rope(km_ref[...], cm_ref[...], sm_ref[...], first),
                          _rope(k0_ref[...], c0, s0, first),
                          _rope(kp_ref[...], cp_ref[...], sp_ref[...], first)], axis=0)
    vw = jnp.concatenate([vm_ref[...], v0_ref[...], vp_ref[...]], axis=0)
    kc = kc_ref[0]
    vc = vc_ref[0]
    nq = A_GROUP * blk
    t = lax.broadcasted_iota(jnp.int32, (nq, 3 * blk), 0) % blk
    s = lax.broadcasted_iota(jnp.int32, (nq, 3 * blk), 1) - blk
    kpos = j * blk + s
    mask = (jnp.abs(s - t) <= A_WINDOW) & (kpos >= 0) & (kpos < DEC_SEQ)
    scale = A_HEAD_DIM ** -0.5
    outs = []
    for h in range(A_KV_HEADS):
        hs = slice(h * A_HEAD_DIM, (h + 1) * A_HEAD_DIM)
        heads = [A_GROUP * h + g for g in range(A_GROUP)]
        qs = jnp.concatenate(
            [qr[hd // 2][:, (hd % 2) * A_HEAD_DIM:(hd % 2 + 1) * A_HEAD_DIM] for hd in heads], axis=0).astype(BF16)
        sw = jnp.where(mask, _dot_nt(qs, kw[:, hs].astype(BF16)) * scale, NEG_INF)
        sc = _dot_nt(qs, kc[:, hs].astype(BF16)) * scale
        o = _softmax_pv([sw, sc], [vw[:, hs].astype(BF16), vc[:, hs].astype(BF16)],
                        _sink_column(sink_ref, h, blk))
        outs.extend(o[g * blk:(g + 1) * blk] for g in range(A_GROUP))
    o_ref[...] = jnp.concatenate(outs, axis=-1).astype(BF16)


def _lat_attention(proj, sink, k_ctx, v_ctx, cos, sin, attn_buf):
    blk = A_WINDOW
    nb = DEC_SEQ // blk
    base = N_CTX_TOK // blk

    def rows(off):
        return lambda b, j: (base + b * nb + jnp.clip(j + off, 0, nb - 1))

    def kv_spec(col, off):
        r = rows(off)
        return pl.BlockSpec((blk, 128), lambda b, j: (r(b, j), col))

    def tab_spec(off):
        return pl.BlockSpec((blk, 128), lambda b, j: (jnp.clip(j + off, 0, nb - 1), 0))

    r0 = rows(0)
    ctx_spec = pl.BlockSpec((1, PAST_LEN, 128), lambda b, j: (b, 0, 0))
    return pl.pallas_call(
        _lat_attn_kernel,
        grid=(DEC_BATCH, nb),
        in_specs=[pl.BlockSpec(memory_space=pltpu.SMEM),
                  pl.BlockSpec((blk, A_WIDTH), lambda b, j: (r0(b, j), QA_OFF // A_WIDTH)),
                  kv_spec(KA_OFF // 128, -1), kv_spec(KA_OFF // 128, 0), kv_spec(KA_OFF // 128, 1),
                  kv_spec(VA_OFF // 128, -1), kv_spec(VA_OFF // 128, 0), kv_spec(VA_OFF // 128, 1),
                  ctx_spec, ctx_spec,
                  tab_spec(-1), tab_spec(0), tab_spec(1), tab_spec(-1), tab_spec(0), tab_spec(1),
                  pl.BlockSpec(memory_space=pl.ANY)],
        out_specs=pl.BlockSpec((blk, A_WIDTH), lambda b, j: (r0(b, j), 0)),
        out_shape=jax.ShapeDtypeStruct((N_TOK, A_WIDTH), BF16),
        input_output_aliases={16: 0},
        compiler_params=_cparams(("arbitrary", "arbitrary")),
        name="lat_attention",
    )(sink, proj, proj, proj, proj, proj, proj, proj, k_ctx, v_ctx, cos, cos, cos, sin, sin, sin, attn_buf)


def _hgrn_constants(reverse):
    L = HG_CHUNK
    t = np.arange(L)[:, None]
    s = np.arange(L)[None, :]
    tri = (s <= t).astype(np.float32)
    blocks = [tri, 1.0 - tri]
    masks = []
    for b in HG_LEVELS:
        anchor = (t // (2 * b)) * 2 * b + b - 1
        blocks.append(tri - (s <= anchor).astype(np.float32))
        masks.append(((t // (2 * b) == s // (2 * b)) & (t % (2 * b) >= b) & (s % (2 * b) < b)).astype(np.float32))
    masks.append((t == s).astype(np.float32))
    if reverse:
        blocks = [m[::-1, ::-1] for m in blocks]
        masks = [m[::-1, ::-1] for m in masks]
    return (jnp.asarray(np.concatenate(blocks, axis=0), BF16), jnp.asarray(np.stack(masks), F32))


def _hgrn_chunk(qraw, vv, z, lbp, st, mat, mask_ref, reverse):
    L = HG_CHUNK
    log_lb, log_1mlb, one_mlb = lbp
    q = _silu(qraw) * (B_KEY_DIM ** -0.5)
    k = one_mlb * jax.nn.sigmoid(-z)
    a = log_lb
    b = log_1mlb + _log_sigmoid(z)
    lf = jnp.maximum(a, b) + jnp.log1p(jnp.exp(-jnp.abs(a - b)))
    hi = lf.astype(BF16)
    r1 = lf - hi.astype(F32)
    mid = r1.astype(BF16)
    lo = (r1 - mid.astype(F32)).astype(BF16)
    g3 = _dot(mat, jnp.concatenate([hi, mid, lo], axis=1))
    g = g3[:, 0:128] + g3[:, 128:256] + g3[:, 256:384]
    cum = g[0:L]
    rem = g[L:2 * L]
    tot = cum[0:1] if reverse else cum[L - 1:L]
    vb = vv.astype(BF16)
    o = _dot_nt((q * jnp.exp(cum)).astype(BF16), st.astype(BF16))
    kd = (k * jnp.exp(rem)).astype(BF16)
    st_new = st * jnp.exp(tot) + _dot_tn(vb, kd)
    nl = len(HG_LEVELS)
    sc = _dot_nt(q.astype(BF16), k.astype(BF16)) * mask_ref[nl]
    for li in range(nl):
        d = g[(2 + li) * L:(3 + li) * L]
        qb = (q * jnp.exp(jnp.minimum(d, 0.0))).astype(BF16)
        kb = (k * jnp.exp(jnp.minimum(-d, 0.0))).astype(BF16)
        sc = sc + _dot_nt(qb, kb) * mask_ref[li]
    o = o + _dot(sc.astype(BF16), vb)
    return o, st_new


def _hgrn(proj, lbp, norm_g, consts, *, nbatch, seq, row_base, state_in=None, out_buf=None, want_state=False):
    mat_f, mask_f, mat_b, mask_b = consts
    rb = row_base // seq
    nl = len(HG_LEVELS) + 1

    def col(k):
        return pl.BlockSpec((seq, 128), lambda b, h: (rb + b, (B_OFF + k * B_WIDTH) // 128 + h))

    const2 = lambda shape: pl.BlockSpec(shape, lambda b, h: (0,) * len(shape))
    in_specs = [col(0), col(1), col(2), col(3), col(4),
                pl.BlockSpec((3, 2, 1, 128), lambda b, h: (0, 0, 0, h)),
                const2((1, 128)),
                const2(mat_f.shape), const2(mat_b.shape), const2(mask_f.shape), const2(mask_b.shape)]
    args = [proj, proj, proj, proj, proj, lbp, norm_g, mat_f, mat_b, mask_f, mask_b]
    if state_in is not None:
        in_specs.append(pl.BlockSpec((1, 2, 1, 128, 128), lambda b, h: (b, 0, h, 0, 0)))
        args.append(state_in)
    aliases = {}
    if out_buf is not None:
        in_specs.append(pl.BlockSpec(memory_space=pl.ANY))
        aliases = {len(args): 0}
        args.append(out_buf)
    out_specs = [pl.BlockSpec((seq, 128), lambda b, h: (rb + b, h))]
    out_shape = [jax.ShapeDtypeStruct((N_TOK, B_WIDTH), BF16)]
    if want_state:
        out_specs.append(pl.BlockSpec((1, 2, 1, 128, 128), lambda b, h: (b, 0, h, 0, 0)))
        out_shape.append(jax.ShapeDtypeStruct((nbatch, 2, B_HEADS, 128, 128), F32))
    kern = functools.partial(_hgrn_kernel_flat, seq=seq, has_state_in=state_in is not None,
                             has_alias=out_buf is not None, has_state_out=want_state)
    return pl.pallas_call(
        kern,
        grid=(nbatch, B_HEADS),
        in_specs=in_specs,
        out_specs=out_specs,
        out_shape=out_shape,
        scratch_shapes=[pltpu.VMEM((seq, 128), F32)],
        input_output_aliases=aliases,
        compiler_params=_cparams(("arbitrary", "arbitrary")),
        name="hgrn_%d" % seq,
    )(*args)


def _hgrn_kernel_flat(*refs, seq, has_state_in, has_alias, has_state_out):
    refs = list(refs)
    head = refs[:11]
    pos = 11
    s0_ref = None
    if has_state_in:
        s0_ref = refs[pos]
        pos += 1
    if has_alias:
        pos += 1
    o_ref = refs[pos]
    pos += 1
    s_ref = None
    if has_state_out:
        s_ref = refs[pos]
        pos += 1
    acc_ref = refs[pos]
    _hgrn_body(*head, s0_ref, o_ref, s_ref, acc_ref, seq=seq)


def _hgrn_body(q_ref, i_ref, zf_ref, zb_ref, gt_ref, lb_ref, ng_ref, mf_ref, mb_ref, kf_ref, kb_ref,
               s0_ref, o_ref, s_ref, acc_ref, *, seq):
    L = HG_CHUNK
    n = seq // L
    acc_ref[...] = jnp.zeros_like(acc_ref)
    lbf = (lb_ref[0, 0], lb_ref[1, 0], lb_ref[2, 0])
    lbb = (lb_ref[0, 1], lb_ref[1, 1], lb_ref[2, 1])
    if s0_ref is not None:
        st_f = s0_ref[0, 0, 0].T
        st_b = s0_ref[0, 1, 0].T
    else:
        st_f = jnp.zeros((B_KEY_DIM, B_KEY_DIM), F32)
        st_b = jnp.zeros((B_KEY_DIM, B_KEY_DIM), F32)

    def body(i, carry):
        st_f, st_b = carry
        sf = pl.ds(pl.multiple_of(i * L, L), L)
        sb = pl.ds(pl.multiple_of((n - 1 - i) * L, L), L)
        o_f, st_f = _hgrn_chunk(q_ref[sf, :], i_ref[sf, :], zf_ref[sf, :], lbf, st_f, mf_ref[...], kf_ref, False)
        acc_ref[sf, :] += o_f
        o_b, st_b = _hgrn_chunk(q_ref[sb, :], i_ref[sb, :], zb_ref[sb, :], lbb, st_b, mb_ref[...], kb_ref, True)
        acc_ref[sb, :] += o_b
        return st_f, st_b

    st_f, st_b = lax.fori_loop(0, n, body, (st_f, st_b))
    o_ref[...] = (_rms(acc_ref[...], ng_ref[...]) * _silu(gt_ref[...])).astype(BF16)
    if s_ref is not None:
        s_ref[0, 0, 0] = st_f.T
        s_ref[0, 1, 0] = st_b.T


def _gelu_tanh(y):
    return 0.5 * y * (1.0 + jnp.tanh(0.7978845608028654 * (y + 0.044715 * (y * y * y))))


def _lru_body(xc_ref, yc_ref, cw_ref, cb_ref, wa_ref, ba_ref, wx_ref, bx_ref, lam_ref, h0_ref, o_ref, s_ref,
              xs_ref, a_ref, u_ref, h_ref, *, seq, reset):
    T = seq
    pad = 8
    zeros = jnp.zeros((pad, 128), F32)
    xs_ref[0:pad, :] = zeros
    xs_ref[pad:pad + T, :] = xc_ref[...]
    xs_ref[pad + T:2 * pad + T, :] = zeros
    cw = cw_ref[...]
    x = cb_ref[...]
    for tap in range(4):
        x = x + cw[tap:tap + 1] * xs_ref[pad - 2 + tap:pad - 2 + tap + T, :]
    xb = x.astype(BF16)
    row = lax.broadcasted_iota(jnp.int32, (T, 1), 0)
    for d in range(2):
        r = jax.nn.sigmoid(_dot(xb, wa_ref[d, 0]) + ba_ref[d])
        ig = jax.nn.sigmoid(_dot(xb, wx_ref[d, 0]) + bx_ref[d])
        log_a = (C_EXP * r) * _log_sigmoid(lam_ref[d])
        a = jnp.exp(log_a)
        mult = jnp.sqrt(jnp.tanh(-log_a) * (a * a + 1.0))
        if reset:
            mult = jnp.where(row == (0 if d == 0 else T - 1), 1.0, mult)
        a_ref[d] = a
        u_ref[d] = mult * ig * x
    if h0_ref is not None:
        h0 = (h0_ref[0, 0:1, :], h0_ref[0, 1:2, :])
    else:
        h0 = (jnp.zeros((1, 128), F32), jnp.zeros((1, 128), F32))

    def body(t, carry):
        hf, hb = carry
        sf = pl.ds(t, 1)
        sb = pl.ds(T - 1 - t, 1)
        hf = a_ref[0, sf, :] * hf + u_ref[0, sf, :]
        h_ref[0, sf, :] = hf
        hb = a_ref[1, sb, :] * hb + u_ref[1, sb, :]
        h_ref[1, sb, :] = hb
        return hf, hb

    hf, hb = lax.fori_loop(0, T, body, h0, unroll=8)
    o_ref[...] = ((h_ref[0] + h_ref[1]) * _gelu_tanh(yc_ref[...])).astype(BF16)
    if s_ref is not None:
        s_ref[0, 0:1, :] = hf
        s_ref[0, 1:2, :] = hb


def _lru_kernel_flat(*refs, seq, reset, has_state_in, has_alias, has_state_out):
    refs = list(refs)
    head = refs[:9]
    pos = 9
    h0_ref = None
    if has_state_in:
        h0_ref = refs[pos]
        pos += 1
    if has_alias:
        pos += 1
    o_ref = refs[pos]
    pos += 1
    s_ref = None
    if has_state_out:
        s_ref = refs[pos]
        pos += 1
    _lru_body(*head, h0_ref, o_ref, s_ref, *refs[pos:], seq=seq, reset=reset)


def _lru(proj, lp, *, nbatch, seq, row_base, reset, state_in=None, out_buf=None, want_state=False):
    rb = row_base // seq
    nblk = C_WIDTH // 128

    def col(k):
        return pl.BlockSpec((seq, 128), lambda b, c: (rb + b, (C_OFF + k * C_WIDTH) // 128 + c))

    vec = lambda rows: pl.BlockSpec((rows, 1, 128), lambda b, c: (0, 0, c))
    mat = pl.BlockSpec((2, 1, 128, 128), lambda b, c: (0, c, 0, 0))
    in_specs = [col(0), col(1),
                pl.BlockSpec((4, 128), lambda b, c: (0, c)), pl.BlockSpec((1, 128), lambda b, c: (0, c)),
                mat, vec(2), mat, vec(2), vec(2)]
    args = [proj, proj, lp['conv_w'], lp['conv_b'], lp['wa'], lp['ba'], lp['wx'], lp['bx'], lp['lam']]
    if state_in is not None:
        in_specs.append(pl.BlockSpec((1, 2, 128), lambda b, c: (b, 0, c)))
        args.append(state_in)
    aliases = {}
    if out_buf is not None:
        in_specs.append(pl.BlockSpec(memory_space=pl.ANY))
        aliases = {len(args): 0}
        args.append(out_buf)
    out_specs = [pl.BlockSpec((seq, 128), lambda b, c: (rb + b, c))]
    out_shape = [jax.ShapeDtypeStruct((N_TOK, C_WIDTH), BF16)]
    if want_state:
        out_specs.append(pl.BlockSpec((1, 2, 128), lambda b, c: (b, 0, c)))
        out_shape.append(jax.ShapeDtypeStruct((nbatch, 2, C_WIDTH), F32))
    kern = functools.partial(_lru_kernel_flat, seq=seq, reset=reset, has_state_in=state_in is not None,
                             has_alias=out_buf is not None, has_state_out=want_state)
    return pl.pallas_call(
        kern,
        grid=(nbatch, nblk),
        in_specs=in_specs,
        out_specs=out_specs,
        out_shape=out_shape,
        scratch_shapes=[pltpu.VMEM((seq + 16, 128), F32), pltpu.VMEM((2, seq, 128), F32),
                        pltpu.VMEM((2, seq, 128), F32), pltpu.VMEM((2, seq, 128), F32)],
        input_output_aliases=aliases,
        compiler_params=_cparams(("arbitrary", "arbitrary")),
        name="lru_%d" % seq,
    )(*args)


MERGE_TM = 256


def _top2_combine(logits):
    lane = lax.broadcasted_iota(jnp.int32, logits.shape, 1).astype(F32)
    lg = jnp.where(lane < N_EXPERTS, logits, -jnp.inf)
    m1 = jnp.max(lg, axis=-1, keepdims=True)
    i1 = jnp.min(jnp.where(lg == m1, lane, 128.0), axis=-1, keepdims=True)
    lg2 = jnp.where(lane == i1, -jnp.inf, lg)
    m2 = jnp.max(lg2, axis=-1, keepdims=True)
    i2 = jnp.min(jnp.where(lg2 == m2, lane, 128.0), axis=-1, keepdims=True)
    e = jnp.exp(m2 - m1)
    w1 = 1.0 / (1.0 + e)
    return jnp.where(lane == i1, w1, 0.0) + jnp.where(lane == i2, e * w1, 0.0)


def _merge_body(x_ref, m_ref, z0_ref, z1_ref, z2_ref, a_ref, h_ref, l_ref, wba_ref, wbb_ref, wbc_ref, wo_ref,
                g_ref, r_ref, xo_ref, u_ref, c_ref):
    mg = (jax.nn.sigmoid(z0_ref[...]) * _dot(a_ref[...], wba_ref[...])
          + jax.nn.sigmoid(z1_ref[...]) * _dot(h_ref[...], wbb_ref[...])
          + jax.nn.sigmoid(z2_ref[...]) * _dot(l_ref[...], wbc_ref[...]))
    x = x_ref[...] + m_ref[0, 2] * _dot(mg.astype(BF16), wo_ref[...])
    xo_ref[...] = x
    u = _rms(x, g_ref[...]) * (1.0 + m_ref[0, 4]) + m_ref[0, 3]
    u_ref[...] = u.astype(BF16)
    if r_ref is not None:
        logits = jnp.dot(u, r_ref[...], preferred_element_type=F32, precision=HIGHEST)
        c_ref[...] = _top2_combine(logits)


def _merge_kernel_dense(x_ref, m_ref, z0_ref, z1_ref, z2_ref, a_ref, h_ref, l_ref, wba_ref, wbb_ref, wbc_ref,
                        wo_ref, g_ref, xo_ref, u_ref):
    _merge_body(x_ref, m_ref, z0_ref, z1_ref, z2_ref, a_ref, h_ref, l_ref, wba_ref, wbb_ref, wbc_ref, wo_ref,
                g_ref, None, xo_ref, u_ref, None)


def _merge(x, mod_l, proj, attn, hg, lr, lp, router=None):
    tm = MERGE_TM
    row = lambda i: (i, 0)
    const = lambda i: (0, 0)
    res = lambda shape: pl.BlockSpec(shape, const, pipeline_mode=pl.Buffered(1))
    in_specs = [pl.BlockSpec((tm, D_MODEL), row),
                pl.BlockSpec((1, 6, 1, D_MODEL), lambda i: (_mod_row(i, tm), 0, 0, 0)),
                pl.BlockSpec((tm, D_MODEL), lambda i: (i, 0)),
                pl.BlockSpec((tm, D_MODEL), lambda i: (i, 1)),
                pl.BlockSpec((tm, D_MODEL), lambda i: (i, 2)),
                pl.BlockSpec((tm, A_WIDTH), row), pl.BlockSpec((tm, B_WIDTH), row), pl.BlockSpec((tm, C_WIDTH), row),
                res((A_WIDTH, D_MODEL)), res((B_WIDTH, D_MODEL)), res((C_WIDTH, D_MODEL)), res((D_MODEL, D_MODEL)),
                pl.BlockSpec((1, D_MODEL), const)]
    args = [x, mod_l, proj, proj, proj, attn, hg, lr, lp['w_ba'], lp['w_bb'], lp['w_bc'], lp['w_out'], lp['norm_ffn']]
    out_specs = [pl.BlockSpec((tm, D_MODEL), row), pl.BlockSpec((tm, D_MODEL), row)]
    out_shape = [jax.ShapeDtypeStruct((N_TOK, D_MODEL), F32), jax.ShapeDtypeStruct((N_TOK, D_MODEL), BF16)]
    kern = _merge_kernel_dense
    if router is not None:
        in_specs.append(pl.BlockSpec((D_MODEL, 128), const))
        args.append(router)
        out_specs.append(pl.BlockSpec((tm, 128), row))
        out_shape.append(jax.ShapeDtypeStruct((N_TOK, 128), F32))
        kern = _merge_body
    return pl.pallas_call(
        kern,
        grid=(N_TOK // tm,),
        in_specs=in_specs,
        out_specs=out_specs,
        out_shape=out_shape,
        compiler_params=_cparams(("arbitrary",)),
        name="merge",
    )(*args)


FFN_TM = 512
FFN_CHUNK = FFN_DIM // 2


def _swiglu_partial(u, wg, wu, wd):
    h = (_silu(_dot(u, wg)) * _dot(u, wu)).astype(BF16)
    return _dot(h, wd)


def _ffn_kernel(u_ref, x_ref, m_ref, wg_ref, wu_ref, wd_ref, o_ref):
    u = u_ref[...]
    acc = None
    for c in range(FFN_DIM // FFN_CHUNK):
        sl = slice(c * FFN_CHUNK, (c + 1) * FFN_CHUNK)
        y = _swiglu_partial(u, wg_ref[:, sl], wu_ref[:, sl], wd_ref[sl, :])
        acc = y if acc is None else acc + y
    o_ref[...] = x_ref[...] + m_ref[0, 5] * acc


def _ffn(u, x, mod_l, wg, wu, wd):
    tm = FFN_TM
    row = lambda i: (i, 0)
    res = lambda shape: pl.BlockSpec(shape, lambda i: (0, 0), pipeline_mode=pl.Buffered(1))
    return pl.pallas_call(
        _ffn_kernel,
        grid=(N_TOK // tm,),
        in_specs=[pl.BlockSpec((tm, D_MODEL), row), pl.BlockSpec((tm, D_MODEL), row),
                  pl.BlockSpec((1, 6, 1, D_MODEL), lambda i: (_mod_row(i, tm), 0, 0, 0)),
                  res((D_MODEL, FFN_DIM)), res((D_MODEL, FFN_DIM)), res((FFN_DIM, D_MODEL))],
        out_specs=pl.BlockSpec((tm, D_MODEL), row),
        out_shape=jax.ShapeDtypeStruct((N_TOK, D_MODEL), F32),
        compiler_params=_cparams(("arbitrary",)),
        name="ffn",
    )(u, x, mod_l, wg, wu, wd)


MOE_TM = 512
MOE_SPLIT = 2
MOE_CHUNK = EXPERT_DIM // MOE_SPLIT


def _moe_kernel(u_ref, c_ref, x_ref, m_ref, g_ref, wg_ref, wu_ref, wd_ref, o_ref, acc_ref):
    e = pl.program_id(1)
    c = pl.program_id(2)

    @pl.when((e == 0) & (c == 0))
    def _():
        acc_ref[...] = jnp.zeros_like(acc_ref)

    y = _swiglu_partial(u_ref[...], wg_ref[0], wu_ref[0], wd_ref[0])
    lane = lax.broadcasted_iota(jnp.int32, c_ref.shape, 1)
    ce = jnp.sum(jnp.where(lane == e, c_ref[...], 0.0), axis=-1, keepdims=True)
    acc_ref[...] += ce * y

    @pl.when((e == N_EXPERTS - 1) & (c == MOE_SPLIT - 1))
    def _():
        o_ref[...] = _rms(x_ref[...] + m_ref[0, 5] * acc_ref[...], g_ref[...])


def _moe(u, comb, x, mod_l, final_g, wg, wu, wd):
    tm = MOE_TM
    row = lambda i, e, c: (i, 0)
    return pl.pallas_call(
        _moe_kernel,
        grid=(N_TOK // tm, N_EXPERTS, MOE_SPLIT),
        in_specs=[pl.BlockSpec((tm, D_MODEL), row), pl.BlockSpec((tm, 128), row), pl.BlockSpec((tm, D_MODEL), row),
                  pl.BlockSpec((1, 6, 1, D_MODEL), lambda i, e, c: (_mod_row(i, tm), 0, 0, 0)),
                  pl.BlockSpec((1, D_MODEL), lambda i, e, c: (0, 0)),
                  pl.BlockSpec((1, D_MODEL, MOE_CHUNK), lambda i, e, c: (e, 0, c)),
                  pl.BlockSpec((1, D_MODEL, MOE_CHUNK), lambda i, e, c: (e, 0, c)),
                  pl.BlockSpec((1, MOE_CHUNK, D_MODEL), lambda i, e, c: (e, c, 0))],
        out_specs=pl.BlockSpec((tm, D_MODEL), row),
        out_shape=jax.ShapeDtypeStruct((N_TOK, D_MODEL), F32),
        scratch_shapes=[pltpu.VMEM((tm, D_MODEL), F32)],
        compiler_params=_cparams(("arbitrary", "arbitrary", "arbitrary")),
        name="moe",
    )(u, comb, x, mod_l, final_g, wg, wu, wd)


def _block_diag(w):
    w = w.reshape(2, C_WIDTH // 128, 2, C_BLOCK, C_BLOCK)
    z = jnp.zeros_like(w[:, :, 0])
    top = jnp.concatenate([w[:, :, 0], z], axis=-1)
    bot = jnp.concatenate([z, w[:, :, 1]], axis=-1)
    return jnp.concatenate([top, bot], axis=-2).astype(BF16)


def kernel(x_prompt, x_sample, cache_attn_k, cache_attn_v, state_hgrn, state_lru, c, c_ctx, w_mod, b_mod, norm_mix_g, norm_ffn_g, w_in, attn_sink, hgrn_lower, hgrn_norm_g, lru_conv_w, lru_conv_b, lru_wa, lru_ba, lru_wx, lru_bx, lru_lambda, w_branch_a, w_branch_b, w_branch_c, w_out, ffn_w_gate, ffn_w_up, ffn_w_down, moe_router, moe_w_gate, moe_w_up, moe_w_down, final_norm_g):
    assert DEPTH == 2
    x = jnp.concatenate([x_prompt.reshape(N_CTX_TOK, D_MODEL), x_sample.reshape(N_LAT_TOK, D_MODEL)], axis=0)
    cond = jnp.concatenate([c_ctx[None, :], c, jnp.zeros((N_COND - 1 - DEC_BATCH, D_MODEL), F32)], axis=0)
    mods = _modulation(cond, w_mod, b_mod).reshape(DEPTH, N_COND, 6, 1, D_MODEL)

    lb_cum = jnp.cumsum(jax.nn.softmax(hgrn_lower.astype(F32), axis=0), axis=0)
    lb_layers = lb_cum - lb_cum[:1]
    cos, sin = _rope_tables()
    hconst = _hgrn_constants(False) + _hgrn_constants(True)
    split = IN_COLS - 3 * D_MODEL

    ks, vs, hs, ls = [], [], [], []
    for l in range(DEPTH):
        w_in_l = jnp.concatenate([w_in[l][:, split:], w_in[l][:, :split]], axis=1).astype(BF16)
        lb = lb_layers[l]
        lbp = jnp.stack([jnp.log(lb), jnp.log1p(-lb), 1.0 - lb]).reshape(3, 2, 1, B_WIDTH)
        lp = dict(
            conv_w=lru_conv_w[l], conv_b=lru_conv_b[l].reshape(1, C_WIDTH),
            wa=_block_diag(lru_wa[l]), wx=_block_diag(lru_wx[l]),
            ba=lru_ba[l].reshape(2, 1, C_WIDTH), bx=lru_bx[l].reshape(2, 1, C_WIDTH),
            lam=lru_lambda[l].reshape(2, 1, C_WIDTH),
            w_ba=w_branch_a[l].astype(BF16), w_bb=w_branch_b[l].astype(BF16), w_bc=w_branch_c[l].astype(BF16),
            w_out=w_out[l].astype(BF16), norm_ffn=norm_ffn_g[l].reshape(1, D_MODEL))
        mod_l = mods[l]

        proj = _in_proj(x, mod_l, norm_mix_g[l].reshape(1, D_MODEL), w_in_l)

        attn = _ctx_attention(proj, attn_sink[l])
        attn = _lat_attention(proj, attn_sink[l], cache_attn_k[:, l].reshape(DEC_BATCH, PAST_LEN, 128),
                              cache_attn_v[:, l].reshape(DEC_BATCH, PAST_LEN, 128), cos, sin, attn)

        ng = hgrn_norm_g[l].reshape(1, B_KEY_DIM)
        hg, s_h = _hgrn(proj, lbp, ng, hconst, nbatch=BATCH, seq=SEQ, row_base=0, want_state=True)
        hg, = _hgrn(proj, lbp, ng, hconst, nbatch=DEC_BATCH, seq=DEC_SEQ, row_base=N_CTX_TOK,
                    state_in=state_hgrn[:, l], out_buf=hg)

        lr, s_l = _lru(proj, lp, nbatch=BATCH, seq=SEQ, row_base=0, reset=True, want_state=True)
        lr, = _lru(proj, lp, nbatch=DEC_BATCH, seq=DEC_SEQ, row_base=N_CTX_TOK, reset=False,
                   state_in=state_lru[:, l], out_buf=lr)

        ks.append(proj[:N_CTX_TOK, KA_OFF:KA_OFF + 128].reshape(BATCH, SEQ, A_KV_HEADS, A_HEAD_DIM))
        vs.append(proj[:N_CTX_TOK, VA_OFF:VA_OFF + 128].reshape(BATCH, SEQ, A_KV_HEADS, A_HEAD_DIM))
        hs.append(s_h)
        ls.append(s_l)

        m = l // 2
        if l % 2 == 0:
            x, u = _merge(x, mod_l, proj, attn, hg, lr, lp)
            x = _ffn(u, x, mod_l, ffn_w_gate[m].astype(BF16), ffn_w_up[m].astype(BF16), ffn_w_down[m].astype(BF16))
        else:
            router = jnp.pad(moe_router[m], ((0, 0), (0, 128 - N_EXPERTS)))
            x, u, comb = _merge(x, mod_l, proj, attn, hg, lr, lp, router=router)
            x = _moe(u, comb, x, mod_l, final_norm_g.reshape(1, D_MODEL), moe_w_gate[m].astype(BF16),
                     moe_w_up[m].astype(BF16), moe_w_down[m].astype(BF16))

    y_prompt = x[:N_CTX_TOK].reshape(BATCH, SEQ, D_MODEL)
    y_sample = x[N_CTX_TOK:].reshape(DEC_BATCH, DEC_SEQ, D_MODEL)
    return (y_prompt, y_sample, jnp.stack(ks, axis=1), jnp.stack(vs, axis=1), jnp.stack(hs, axis=1),
            jnp.stack(ls, axis=1))
```

```python
import functools

import numpy as np
import jax
import jax.numpy as jnp
from jax import lax
from jax.experimental import pallas as pl
from jax.experimental.pallas import tpu as pltpu

F32 = jnp.float32
BF16 = jnp.bfloat16
HIGHEST = lax.Precision.HIGHEST

D_MODEL = 1024
BATCH = 32
SEQ = 256
DEPTH = 2
DEC_BATCH = 4
DEC_SEQ = 2048
PAST_LEN = 512
GRID_W = 64
RMS_EPS = 1e-6
A_HEADS = 8
A_KV_HEADS = 2
A_GROUP = 4
A_HEAD_DIM = 64
A_WIDTH = 512
A_WINDOW = 128
ROPE_THETA = 10000.0
NEG_INF = -1e30
B_HEADS = 4
B_KEY_DIM = 128
B_WIDTH = 512
C_WIDTH = 512
C_HEADS = 8
C_BLOCK = 64
C_EXP = 8.0
FFN_DIM = 2816
N_EXPERTS = 8
EXPERT_DIM = 3584

N_CTX_TOK = BATCH * SEQ
N_LAT_TOK = DEC_BATCH * DEC_SEQ
N_TOK = N_CTX_TOK + N_LAT_TOK
N_COND = 8

ZG_OFF = 0
QA_OFF = 3072
KA_OFF = 3584
VA_OFF = 3712
B_OFF = 3840
C_OFF = 6400
IN_COLS = 7424

V7X_VMEM_LIMIT = 56 * 1024 * 1024
HG_CHUNK = 128
HG_LEVELS = (64, 32, 16, 8, 4, 2, 1)
LRU_SEGS = 8
LRU_PITCH_PAD = 4


def _cparams(sem, vmem=V7X_VMEM_LIMIT):
    return pltpu.CompilerParams(dimension_semantics=sem, vmem_limit_bytes=vmem)


def _mod_row(i, tm):
    nct = N_CTX_TOK // tm
    return jnp.where(i < nct, 0, 1 + (i - nct) // (DEC_SEQ // tm))


def _sigmoid(x):
    return 0.5 * jnp.tanh(0.5 * x) + 0.5


def _silu(x):
    return x * _sigmoid(x)


def _log_sigmoid(z):
    return jnp.minimum(z, 0.0) - jnp.log1p(jnp.exp(-jnp.abs(z)))


def _rms(x, g):
    ms = jnp.mean(x * x, axis=-1, keepdims=True)
    return x * lax.rsqrt(ms + RMS_EPS) * g


def _dot(a, b):
    return jnp.dot(a, b, preferred_element_type=F32)


def _dot_nt(a, b):
    return lax.dot_general(a, b, (((1,), (1,)), ((), ())), preferred_element_type=F32)


def _dot_tn(a, b):
    return lax.dot_general(a, b, (((0,), (0,)), ((), ())), preferred_element_type=F32)


def _mod_kernel(c_ref, w_ref, b_ref, o_ref):
    a = _silu(c_ref[...])
    o_ref[0] = jnp.dot(a, w_ref[0], preferred_element_type=F32, precision=HIGHEST) + b_ref[0]


def _modulation(cond, w_mod, b_mod):
    tn = 1536
    return pl.pallas_call(
        _mod_kernel,
        grid=(DEPTH, 6 * D_MODEL // tn),
        in_specs=[pl.BlockSpec((N_COND, D_MODEL), lambda l, j: (0, 0)),
                  pl.BlockSpec((1, D_MODEL, tn), lambda l, j: (l, 0, j)),
                  pl.BlockSpec((1, 1, tn), lambda l, j: (l, 0, j))],
        out_specs=pl.BlockSpec((1, N_COND, tn), lambda l, j: (l, 0, j)),
        out_shape=jax.ShapeDtypeStruct((DEPTH, N_COND, 6 * D_MODEL), F32),
        compiler_params=_cparams(("arbitrary", "arbitrary")),
        name="modulation",
    )(cond, w_mod, b_mod.reshape(DEPTH, 1, 6 * D_MODEL))


IN_TM = 256
IN_TN = 256


def _in_proj_kernel(x_ref, m_ref, g_ref, w_ref, o_ref):
    u = _rms(x_ref[...], g_ref[...]) * (1.0 + m_ref[0, 1]) + m_ref[0, 0]
    ub = u.astype(BF16)
    for j in range(IN_COLS // IN_TN):
        sl = slice(j * IN_TN, (j + 1) * IN_TN)
        o_ref[:, sl] = _dot(ub, w_ref[:, sl])


def _in_proj(x, mod_l, g, w):
    tm = IN_TM
    return pl.pallas_call(
        _in_proj_kernel,
        grid=(N_TOK // tm,),
        in_specs=[pl.BlockSpec((tm, D_MODEL), lambda i: (i, 0)),
                  pl.BlockSpec((1, 6, 1, D_MODEL), lambda i: (_mod_row(i, tm), 0, 0, 0)),
                  pl.BlockSpec((1, D_MODEL), lambda i: (0, 0)),
                  pl.BlockSpec((D_MODEL, IN_COLS), lambda i: (0, 0), pipeline_mode=pl.Buffered(1))],
        out_specs=pl.BlockSpec((tm, IN_COLS), lambda i: (i, 0)),
        out_shape=jax.ShapeDtypeStruct((N_TOK, IN_COLS), F32),
        compiler_params=_cparams(("arbitrary",)),
        name="in_proj",
    )(x, mod_l, g, w)


def _softmax_pv(s_list, v_list, sink):
    m = sink
    for s in s_list:
        m = jnp.maximum(m, jnp.max(s, axis=-1, keepdims=True))
    den = jnp.exp(sink - m)
    out = None
    for s, v in zip(s_list, v_list):
        p = jnp.exp(s - m)
        den = den + jnp.sum(p, axis=-1, keepdims=True)
        o = _dot(p.astype(BF16), v)
        out = o if out is None else out + o
    return out / den


def _sink_column(sink_ref, h, rows_per_head):
    n = A_GROUP * rows_per_head
    row = lax.broadcasted_iota(jnp.int32, (n, 1), 0)
    col = jnp.full((n, 1), sink_ref[A_GROUP * h + A_GROUP - 1], F32)
    for g in range(A_GROUP - 2, -1, -1):
        col = jnp.where(row < (g + 1) * rows_per_head, sink_ref[A_GROUP * h + g], col)
    return col


def _ctx_attn_kernel(sink_ref, q_ref, k_ref, v_ref, o_ref):
    q = q_ref[...]
    k = k_ref[...]
    v = v_ref[...]
    scale = A_HEAD_DIM ** -0.5
    outs = []
    for h in range(A_KV_HEADS):
        hs = slice(h * A_HEAD_DIM, (h + 1) * A_HEAD_DIM)
        kh = k[:, hs].astype(BF16)
        vh = v[:, hs].astype(BF16)
        qs = jnp.concatenate(
            [q[:, (A_GROUP * h + g) * A_HEAD_DIM:(A_GROUP * h + g + 1) * A_HEAD_DIM] for g in range(A_GROUP)],
            axis=0).astype(BF16)
        s = _dot_nt(qs, kh) * scale
        o = _softmax_pv([s], [vh], _sink_column(sink_ref, h, SEQ))
        outs.extend(o[g * SEQ:(g + 1) * SEQ] for g in range(A_GROUP))
    o_ref[...] = jnp.concatenate(outs, axis=-1).astype(BF16)


def _ctx_attention(proj, sink):
    return pl.pallas_call(
        _ctx_attn_kernel,
        grid=(BATCH,),
        in_specs=[pl.BlockSpec(memory_space=pltpu.SMEM),
                  pl.BlockSpec((SEQ, A_WIDTH), lambda b: (b, QA_OFF // A_WIDTH)),
                  pl.BlockSpec((SEQ, 128), lambda b: (b, KA_OFF // 128)),
                  pl.BlockSpec((SEQ, 128), lambda b: (b, VA_OFF // 128))],
        out_specs=pl.BlockSpec((SEQ, A_WIDTH), lambda b: (b, 0)),
        out_shape=jax.ShapeDtypeStruct((N_TOK, A_WIDTH), BF16),
        compiler_params=_cparams(("arbitrary",)),
        name="ctx_attention",
    )(sink, proj, proj, proj)


def _rope_tables():
    half = A_HEAD_DIM // 4
    inv = ROPE_THETA ** (-np.arange(half, dtype=np.float64) / half)
    t = np.arange(DEC_SEQ)
    row = (t // GRID_W)[:, None] * inv[None, :]
    col = (t % GRID_W)[:, None] * inv[None, :]
    cos = np.concatenate([np.cos(row), np.cos(row), np.cos(col), np.cos(col)], axis=1)
    sin = np.concatenate([-np.sin(row), np.sin(row), -np.sin(col), np.sin(col)], axis=1)
    return (jnp.asarray(np.tile(cos, (1, 2)), F32), jnp.asarray(np.tile(sin, (1, 2)), F32))


def _rope(x, cos, sin, first):
    part = jnp.where(first, pltpu.roll(x, 128 - 16, 1), pltpu.roll(x, 16, 1))
    return x * cos + part * sin


def _lat_attn_kernel(sink_ref, q_ref, km_ref, k0_ref, kp_ref, vm_ref, v0_ref, vp_ref, kc_ref, vc_ref,
                     cm_ref, c0_ref, cp_ref, sm_ref, s0_ref, sp_ref, buf_ref, o_ref):
    del buf_ref
    j = pl.program_id(1)
    blk = A_WINDOW
    lane = lax.broadcasted_iota(jnp.int32, (blk, 128), 1)
    first = (lane % 32) < 16
    c0 = c0_ref[...]
    s0 = s0_ref[...]
    q = q_ref[...]
    qr = [_rope(q[:, c * 128:(c + 1) * 128], c0, s0, first) for c in range(A_WIDTH // 128)]
    kw = jnp.concatenate([_rope(km_ref[...], cm_ref[...], sm_ref[...], first),
                          _rope(k0_ref[...], c0, s0, first),
                          _rope(kp_ref[...], cp_ref[...], sp_ref[...], first)], axis=0)
    vw = jnp.concatenate([vm_ref[...], v0_ref[...], vp_ref[...]], axis=0)
    kc = kc_ref[0]
    vc = vc_ref[0]
    nq = A_GROUP * blk
    t = lax.broadcasted_iota(jnp.int32, (nq, 3 * blk), 0) % blk
    s = lax.broadcasted_iota(jnp.int32, (nq, 3 * blk), 1) - blk
    kpos = j * blk + s
    mask = (jnp.abs(s - t) <= A_WINDOW) & (kpos >= 0) & (kpos < DEC_SEQ)
    scale = A_HEAD_DIM ** -0.5
    outs = []
    for h in range(A_KV_HEADS):
        hs = slice(h * A_HEAD_DIM, (h + 1) * A_HEAD_DIM)
        heads = [A_GROUP * h + g for g in range(A_GROUP)]
        qs = jnp.concatenate(
            [qr[hd // 2][:, (hd % 2) * A_HEAD_DIM:(hd % 2 + 1) * A_HEAD_DIM] for hd in heads], axis=0).astype(BF16)
        sw = jnp.where(mask, _dot_nt(qs, kw[:, hs].astype(BF16)) * scale, NEG_INF)
        sc = _dot_nt(qs, kc[:, hs].astype(BF16)) * scale
        o = _softmax_pv([sw, sc], [vw[:, hs].astype(BF16), vc[:, hs].astype(BF16)],
                        _sink_column(sink_ref, h, blk))
        outs.extend(o[g * blk:(g + 1) * blk] for g in range(A_GROUP))
    o_ref[...] = jnp.concatenate(outs, axis=-1).astype(BF16)


def _lat_attention(proj, sink, k_ctx, v_ctx, cos, sin, attn_buf):
    blk = A_WINDOW
    nb = DEC_SEQ // blk
    base = N_CTX_TOK // blk

    def rows(off):
        return lambda b, j: (base + b * nb + jnp.clip(j + off, 0, nb - 1))

    def kv_spec(col, off):
        r = rows(off)
        return pl.BlockSpec((blk, 128), lambda b, j: (r(b, j), col))

    def tab_spec(off):
        return pl.BlockSpec((blk, 128), lambda b, j: (jnp.clip(j + off, 0, nb - 1), 0))

    r0 = rows(0)
    ctx_spec = pl.BlockSpec((1, PAST_LEN, 128), lambda b, j: (b, 0, 0))
    return pl.pallas_call(
        _lat_attn_kernel,
        grid=(DEC_BATCH, nb),
        in_specs=[pl.BlockSpec(memory_space=pltpu.SMEM),
                  pl.BlockSpec((blk, A_WIDTH), lambda b, j: (r0(b, j), QA_OFF // A_WIDTH)),
                  kv_spec(KA_OFF // 128, -1), kv_spec(KA_OFF // 128, 0), kv_spec(KA_OFF // 128, 1),
                  kv_spec(VA_OFF // 128, -1), kv_spec(VA_OFF // 128, 0), kv_spec(VA_OFF // 128, 1),
                  ctx_spec, ctx_spec,
                  tab_spec(-1), tab_spec(0), tab_spec(1), tab_spec(-1), tab_spec(0), tab_spec(1),
                  pl.BlockSpec(memory_space=pl.ANY)],
        out_specs=pl.BlockSpec((blk, A_WIDTH), lambda b, j: (r0(b, j), 0)),
        out_shape=jax.ShapeDtypeStruct((N_TOK, A_WIDTH), BF16),
        input_output_aliases={16: 0},
        compiler_params=_cparams(("arbitrary", "arbitrary")),
        name="lat_attention",
    )(sink, proj, proj, proj, proj, proj, proj, proj, k_ctx, v_ctx, cos, cos, cos, sin, sin, sin, attn_buf)


def _hgrn_constants(reverse):
    L = HG_CHUNK
    t = np.arange(L)[:, None]
    s = np.arange(L)[None, :]
    tri = (s <= t).astype(np.float32)
    blocks = [tri]
    masks = []
    for b in HG_LEVELS:
        anchor = (t // (2 * b)) * 2 * b + b - 1
        right = np.where(t % (2 * b) >= b, 1.0, -1.0)
        blocks.append(right * (tri - (s <= anchor).astype(np.float32)))
        masks.append(((t // (2 * b) == s // (2 * b)) & (t % (2 * b) >= b) & (s % (2 * b) < b)).astype(np.float32))
    masks.append((t == s).astype(np.float32))
    if reverse:
        blocks = [m[::-1, ::-1] for m in blocks]
        masks = [m[::-1, ::-1] for m in masks]
    mat = np.concatenate(blocks, axis=0)
    return (jnp.asarray(np.concatenate([mat, mat], axis=1), BF16), jnp.asarray(np.stack(masks), F32))


LOG2E = 1.4426950408889634


def _hgrn_chunk(qraw, vv, z, lbp, st, mat, mask_ref, reverse):
    L = HG_CHUNK
    log_lb, log_1mlb, one_mlb = lbp
    q = qraw * _sigmoid(qraw) * (B_KEY_DIM ** -0.5)
    e = jnp.exp(-jnp.abs(z))
    r = 1.0 / (1.0 + e)
    k = one_mlb * jnp.where(z >= 0.0, e * r, r)
    a = log_lb
    b = log_1mlb + (jnp.minimum(z, 0.0) + jnp.log(r))
    lf = (jnp.maximum(a, b) + jnp.log(1.0 + jnp.exp(-jnp.abs(a - b)))) * LOG2E
    hi = lf.astype(BF16)
    lo = (lf - hi.astype(F32)).astype(BF16)
    g = _dot(mat, jnp.concatenate([hi, lo], axis=0))
    cum = g[0:L]
    tot = cum[0:1] if reverse else cum[L - 1:L]
    vb = vv.astype(BF16)
    o = _dot_nt((q * jnp.exp2(cum)).astype(BF16), st.astype(BF16))
    kd = (k * jnp.exp2(tot - cum)).astype(BF16)
    st_new = st * jnp.exp2(tot) + _dot_tn(vb, kd)
    nl = len(HG_LEVELS)
    row = lax.broadcasted_iota(jnp.int32, (L, 1), 0)
    if reverse:
        row = L - 1 - row
    sc = _dot_nt(q.astype(BF16), k.astype(BF16)) * mask_ref[nl]
    for li, bsz in enumerate(HG_LEVELS):
        is_query = (row & (2 * bsz - 1)) >= bsz
        zb = (jnp.where(is_query, q, k) * jnp.exp2(g[(1 + li) * L:(2 + li) * L])).astype(BF16)
        sc = sc + _dot_nt(zb, zb) * mask_ref[li]
    o = o + _dot(sc.astype(BF16), vb)
    return o, st_new


def _hgrn(proj, lbp, norm_g, consts, *, nbatch, seq, row_base, state_in=None, out_buf=None, want_state=False):
    mat_f, mask_f, mat_b, mask_b = consts
    rb = row_base // seq
    nl = len(HG_LEVELS) + 1

    def col(k):
        return pl.BlockSpec((seq, 128), lambda b, h: (rb + b, (B_OFF + k * B_WIDTH) // 128 + h))

    const2 = lambda shape: pl.BlockSpec(shape, lambda b, h: (0,) * len(shape))
    in_specs = [col(0), col(1), col(2), col(3), col(4),
                pl.BlockSpec((3, 2, 1, 128), lambda b, h: (0, 0, 0, h)),
                const2((1, 128)),
                const2(mat_f.shape), const2(mat_b.shape), const2(mask_f.shape), const2(mask_b.shape)]
    args = [proj, proj, proj, proj, proj, lbp, norm_g, mat_f, mat_b, mask_f, mask_b]
    if state_in is not None:
        in_specs.append(pl.BlockSpec((1, 2, 1, 128, 128), lambda b, h: (b, 0, h, 0, 0)))
        args.append(state_in)
    aliases = {}
    if out_buf is not None:
        in_specs.append(pl.BlockSpec(memory_space=pl.ANY))
        aliases = {len(args): 0}
        args.append(out_buf)
    out_specs = [pl.BlockSpec((seq, 128), lambda b, h: (rb + b, h))]
    out_shape = [jax.ShapeDtypeStruct((N_TOK, B_WIDTH), BF16)]
    if want_state:
        out_specs.append(pl.BlockSpec((1, 2, 1, 128, 128), lambda b, h: (b, 0, h, 0, 0)))
        out_shape.append(jax.ShapeDtypeStruct((nbatch, 2, B_HEADS, 128, 128), F32))
    kern = functools.partial(_hgrn_kernel_flat, seq=seq, has_state_in=state_in is not None,
                             has_alias=out_buf is not None, has_state_out=want_state)
    return pl.pallas_call(
        kern,
        grid=(nbatch, B_HEADS),
        in_specs=in_specs,
        out_specs=out_specs,
        out_shape=out_shape,
        scratch_shapes=[pltpu.VMEM((seq, 128), F32)],
        input_output_aliases=aliases,
        compiler_params=_cparams(("arbitrary", "arbitrary")),
        name="hgrn_%d" % seq,
    )(*args)


def _hgrn_kernel_flat(*refs, seq, has_state_in, has_alias, has_state_out):
    refs = list(refs)
    head = refs[:11]
    pos = 11
    s0_ref = None
    if has_state_in:
        s0_ref = refs[pos]
        pos += 1
    if has_alias:
        pos += 1
    o_ref = refs[pos]
    pos += 1
    s_ref = None
    if has_state_out:
        s_ref = refs[pos]
        pos += 1
    acc_ref = refs[pos]
    _hgrn_body(*head, s0_ref, o_ref, s_ref, acc_ref, seq=seq)


def _hgrn_body(q_ref, i_ref, zf_ref, zb_ref, gt_ref, lb_ref, ng_ref, mf_ref, mb_ref, kf_ref, kb_ref,
               s0_ref, o_ref, s_ref, acc_ref, *, seq):
    L = HG_CHUNK
    n = seq // L
    acc_ref[...] = jnp.zeros_like(acc_ref)
    lbf = (lb_ref[0, 0], lb_ref[1, 0], lb_ref[2, 0])
    lbb = (lb_ref[0, 1], lb_ref[1, 1], lb_ref[2, 1])
    if s0_ref is not None:
        st_f = s0_ref[0, 0, 0].T
        st_b = s0_ref[0, 1, 0].T
    else:
        st_f = jnp.zeros((B_KEY_DIM, B_KEY_DIM), F32)
        st_b = jnp.zeros((B_KEY_DIM, B_KEY_DIM), F32)

    def body(i, carry):
        st_f, st_b = carry
        sf = pl.ds(pl.multiple_of(i * L, L), L)
        sb = pl.ds(pl.multiple_of((n - 1 - i) * L, L), L)
        o_f, st_f = _hgrn_chunk(q_ref[sf, :], i_ref[sf, :], zf_ref[sf, :], lbf, st_f, mf_ref[...], kf_ref, False)
        acc_ref[sf, :] += o_f
        o_b, st_b = _hgrn_chunk(q_ref[sb, :], i_ref[sb, :], zb_ref[sb, :], lbb, st_b, mb_ref[...], kb_ref, True)
        acc_ref[sb, :] += o_b
        return st_f, st_b

    st_f, st_b = lax.fori_loop(0, n, body, (st_f, st_b), unroll=2)
    o_ref[...] = (_rms(acc_ref[...], ng_ref[...]) * _silu(gt_ref[...])).astype(BF16)
    if s_ref is not None:
        s_ref[0, 0, 0] = st_f.T
        s_ref[0, 1, 0] = st_b.T


def _gelu_tanh(y):
    return 0.5 * y * (1.0 + jnp.tanh(0.7978845608028654 * (y + 0.044715 * (y * y * y))))


def _lru_body(xc_ref, yc_ref, cw_ref, cb_ref, wa_ref, ba_ref, wx_ref, bx_ref, lam_ref, h0_ref, o_ref, s_ref,
              xs_ref, a_ref, u_ref, *, seq, reset):
    T = seq
    pad = 8
    zeros = jnp.zeros((pad, 128), F32)
    xs_ref[0:pad, :] = zeros
    xs_ref[pad:pad + T, :] = xc_ref[...]
    xs_ref[pad + T:2 * pad + T, :] = zeros
    cw = cw_ref[...]
    x = cb_ref[...]
    for tap in range(4):
        x = x + cw[tap:tap + 1] * xs_ref[pad - 2 + tap:pad - 2 + tap + T, :]
    xb = x.astype(BF16)
    row = lax.broadcasted_iota(jnp.int32, (T, 1), 0)
    seg = T // LRU_SEGS
    pitch = seg + LRU_PITCH_PAD
    for d in range(2):
        r = _sigmoid(_dot(xb, wa_ref[d, 0]) + ba_ref[d])
        ig = _sigmoid(_dot(xb, wx_ref[d, 0]) + bx_ref[d])
        log_a = (C_EXP * r) * _log_sigmoid(lam_ref[d])
        a = jnp.exp(log_a)
        mult = jnp.sqrt(1.0 - a * a)
        if reset:
            mult = jnp.where(row == (0 if d == 0 else T - 1), 1.0, mult)
        u = mult * ig * x
        for r in range(LRU_SEGS):
            a_ref[d, r * pitch:r * pitch + seg, :] = a[r * seg:(r + 1) * seg]
            u_ref[d, r * pitch:r * pitch + seg, :] = u[r * seg:(r + 1) * seg]
    sub =lax.broadcasted_iota(jnp.int32, (LRU_SEGS, 1), 0)
    zrow = jnp.zeros((1, 128), F32)
    if h0_ref is not None:
        hf0 = jnp.where(sub == 0, h0_ref[0, 0:1, :], 0.0)
        hb0 = jnp.where(sub == LRU_SEGS - 1, h0_ref[0, 1:2, :], 0.0)
    else:
        hf0 = jnp.zeros((LRU_SEGS, 128), F32)
        hb0 = hf0
    ones = jnp.ones((LRU_SEGS, 128), F32)

    def local_scan(i, carry):
        hf, pf, hb, pb = carry
        sf = pl.ds(i, LRU_SEGS, stride=pitch)
        sb = pl.ds(seg - 1 - i, LRU_SEGS, stride=pitch)
        af = a_ref[0, sf, :]
        hf = af * hf + u_ref[0, sf, :]
        pf = af * pf
        u_ref[0, sf, :] = hf
        a_ref[0, sf, :] = pf
        ab = a_ref[1, sb, :]
        hb = ab * hb + u_ref[1, sb, :]
        pb = ab * pb
        u_ref[1, sb, :] = hb
        a_ref[1, sb, :] = pb
        return hf, pf, hb, pb

    hf, pf, hb, pb = lax.fori_loop(0, seg, local_scan, (hf0, ones, hb0, ones), unroll=4)
    carry_f = [zrow] * LRU_SEGS
    carry_b = [zrow] * LRU_SEGS
    end_f = hf[0:1]
    for r in range(1, LRU_SEGS):
        carry_f[r] = end_f
        end_f = hf[r:r + 1] + pf[r:r + 1] * end_f
    end_b = hb[LRU_SEGS - 1:LRU_SEGS]
    for r in range(LRU_SEGS - 2, -1, -1):
        carry_b[r] = end_b
        end_b = hb[r:r + 1] + pb[r:r + 1] * end_b
    for r in range(LRU_SEGS):
        sl = slice(r * seg, (r + 1) * seg)
        ps = slice(r * pitch, r * pitch + seg)
        h = (u_ref[0, ps, :] + a_ref[0, ps, :] * carry_f[r]) + (u_ref[1, ps, :] + a_ref[1, ps, :] * carry_b[r])
        o_ref[sl, :] = (h * _gelu_tanh(yc_ref[sl, :])).astype(BF16)
    if s_ref is not None:
        s_ref[0, 0:1, :] = end_f
        s_ref[0, 1:2, :] = end_b


def _lru_kernel_flat(*refs, seq, reset, has_state_in, has_alias, has_state_out):
    refs = list(refs)
    head = refs[:9]
    pos = 9
    h0_ref = None
    if has_state_in:
        h0_ref = refs[pos]
        pos += 1
    if has_alias:
        pos += 1
    o_ref = refs[pos]
    pos += 1
    s_ref = None
    if has_state_out:
        s_ref = refs[pos]
        pos += 1
    _lru_body(*head, h0_ref, o_ref, s_ref, *refs[pos:], seq=seq, reset=reset)


def _lru(proj, lp, *, nbatch, seq, row_base, reset, state_in=None, out_buf=None, want_state=False):
    rb = row_base // seq
    nblk = C_WIDTH // 128

    def col(k):
        return pl.BlockSpec((seq, 128), lambda b, c: (rb + b, (C_OFF + k * C_WIDTH) // 128 + c))

    vec = lambda rows: pl.BlockSpec((rows, 1, 128), lambda b, c: (0, 0, c))
    mat = pl.BlockSpec((2, 1, 128, 128), lambda b, c: (0, c, 0, 0))
    in_specs = [col(0), col(1),
                pl.BlockSpec((4, 128), lambda b, c: (0, c)), pl.BlockSpec((1, 128), lambda b, c: (0, c)),
                mat, vec(2), mat, vec(2), vec(2)]
    args = [proj, proj, lp['conv_w'], lp['conv_b'], lp['wa'], lp['ba'], lp['wx'], lp['bx'], lp['lam']]
    if state_in is not None:
        in_specs.append(pl.BlockSpec((1, 2, 128), lambda b, c: (b, 0, c)))
        args.append(state_in)
    aliases = {}
    if out_buf is not None:
        in_specs.append(pl.BlockSpec(memory_space=pl.ANY))
        aliases = {len(args): 0}
        args.append(out_buf)
    out_specs = [pl.BlockSpec((seq, 128), lambda b, c: (rb + b, c))]
    out_shape = [jax.ShapeDtypeStruct((N_TOK, C_WIDTH), BF16)]
    if want_state:
        out_specs.append(pl.BlockSpec((1, 2, 128), lambda b, c: (b, 0, c)))
        out_shape.append(jax.ShapeDtypeStruct((nbatch, 2, C_WIDTH), F32))
    kern = functools.partial(_lru_kernel_flat, seq=seq, reset=reset, has_state_in=state_in is not None,
                             has_alias=out_buf is not None, has_state_out=want_state)
    return pl.pallas_call(
        kern,
        grid=(nbatch, nblk),
        in_specs=in_specs,
        out_specs=out_specs,
        out_shape=out_shape,
        scratch_shapes=[pltpu.VMEM((seq + 16, 128), F32), pltpu.VMEM((2, seq + LRU_SEGS * LRU_PITCH_PAD, 128), F32),
                        pltpu.VMEM((2, seq + LRU_SEGS * LRU_PITCH_PAD, 128), F32)],
        input_output_aliases=aliases,
        compiler_params=_cparams(("arbitrary", "arbitrary")),
        name="lru_%d" % seq,
    )(*args)


MERGE_TM = 256


def _top2_combine(logits):
    lane = lax.broadcasted_iota(jnp.int32, logits.shape, 1).astype(F32)
    lg = jnp.where(lane < N_EXPERTS, logits, -jnp.inf)
    m1 = jnp.max(lg, axis=-1, keepdims=True)
    i1 = jnp.min(jnp.where(lg == m1, lane, 128.0), axis=-1, keepdims=True)
    lg2 = jnp.where(lane == i1, -jnp.inf, lg)
    m2 = jnp.max(lg2, axis=-1, keepdims=True)
    i2 = jnp.min(jnp.where(lg2 == m2, lane, 128.0), axis=-1, keepdims=True)
    e = jnp.exp(m2 - m1)
    w1 = 1.0 / (1.0 + e)
    return jnp.where(lane == i1, w1, 0.0) + jnp.where(lane == i2, e * w1, 0.0)


def _merge_body(x_ref, m_ref, z0_ref, z1_ref, z2_ref, a_ref, h_ref, l_ref, wba_ref, wbb_ref, wbc_ref, wo_ref,
                g_ref, r_ref, xo_ref, u_ref, c_ref):
    mg = (_sigmoid(z0_ref[...]) * _dot(a_ref[...], wba_ref[...])
          + _sigmoid(z1_ref[...]) * _dot(h_ref[...], wbb_ref[...])
          + _sigmoid(z2_ref[...]) * _dot(l_ref[...], wbc_ref[...]))
    x = x_ref[...] + m_ref[0, 2] * _dot(mg.astype(BF16), wo_ref[...])
    xo_ref[...] = x
    u = _rms(x, g_ref[...]) * (1.0 + m_ref[0, 4]) + m_ref[0, 3]
    ub = u.astype(BF16)
    u_ref[...] = ub
    if r_ref is not None:
        ul = (u - ub.astype(F32)).astype(BF16)
        logits = _dot(ub, r_ref[0]) + (_dot(ub, r_ref[1]) + _dot(ul, r_ref[0]))
        c_ref[...] = _top2_combine(logits)


def _merge_kernel_dense(x_ref, m_ref, z0_ref, z1_ref, z2_ref, a_ref, h_ref, l_ref, wba_ref, wbb_ref, wbc_ref,
                        wo_ref, g_ref, xo_ref, u_ref):
    _merge_body(x_ref, m_ref, z0_ref, z1_ref, z2_ref, a_ref, h_ref, l_ref, wba_ref, wbb_ref, wbc_ref, wo_ref,
                g_ref, None, xo_ref, u_ref, None)


def _merge(x, mod_l, proj, attn, hg, lr, lp, router=None):
    tm = MERGE_TM
    row = lambda i: (i, 0)
    const = lambda i: (0, 0)
    res = lambda shape: pl.BlockSpec(shape, const, pipeline_mode=pl.Buffered(1))
    in_specs = [pl.BlockSpec((tm, D_MODEL), row),
                pl.BlockSpec((1, 6, 1, D_MODEL), lambda i: (_mod_row(i, tm), 0, 0, 0)),
                pl.BlockSpec((tm, D_MODEL), lambda i: (i, 0)),
                pl.BlockSpec((tm, D_MODEL), lambda i: (i, 1)),
                pl.BlockSpec((tm, D_MODEL), lambda i: (i, 2)),
                pl.BlockSpec((tm, A_WIDTH), row), pl.BlockSpec((tm, B_WIDTH), row), pl.BlockSpec((tm, C_WIDTH), row),
                res((A_WIDTH, D_MODEL)), res((B_WIDTH, D_MODEL)), res((C_WIDTH, D_MODEL)), res((D_MODEL, D_MODEL)),
                pl.BlockSpec((1, D_MODEL), const)]
    args = [x, mod_l, proj, proj, proj, attn, hg, lr, lp['w_ba'], lp['w_bb'], lp['w_bc'], lp['w_out'], lp['norm_ffn']]
    out_specs = [pl.BlockSpec((tm, D_MODEL), row), pl.BlockSpec((tm, D_MODEL), row)]
    out_shape = [jax.ShapeDtypeStruct((N_TOK, D_MODEL), F32), jax.ShapeDtypeStruct((N_TOK, D_MODEL), BF16)]
    kern = _merge_kernel_dense
    if router is not None:
        in_specs.append(pl.BlockSpec((2, D_MODEL, 128), lambda i: (0, 0, 0)))
        args.append(router)
        out_specs.append(pl.BlockSpec((tm, 128), row))
        out_shape.append(jax.ShapeDtypeStruct((N_TOK, 128), F32))
        kern = _merge_body
    return pl.pallas_call(
        kern,
        grid=(N_TOK // tm,),
        in_specs=in_specs,
        out_specs=out_specs,
        out_shape=out_shape,
        compiler_params=_cparams(("arbitrary",)),
        name="merge",
    )(*args)


FFN_TM = 512
FFN_CHUNK = FFN_DIM // 2


def _swiglu_partial(u, wg, wu, wd):
    h = (_silu(_dot(u, wg)) * _dot(u, wu)).astype(BF16)
    return _dot(h, wd)


def _ffn_kernel(u_ref, x_ref, m_ref, wg_ref, wu_ref, wd_ref, o_ref):
    u = u_ref[...]
    acc = None
    for c in range(FFN_DIM // FFN_CHUNK):
        sl = slice(c * FFN_CHUNK, (c + 1) * FFN_CHUNK)
        y = _swiglu_partial(u, wg_ref[:, sl], wu_ref[:, sl], wd_ref[sl, :])
        acc = y if acc is None else acc + y
    o_ref[...] = x_ref[...] + m_ref[0, 5] * acc


def _ffn(u, x, mod_l, wg, wu, wd):
    tm = FFN_TM
    row = lambda i: (i, 0)
    res = lambda shape: pl.BlockSpec(shape, lambda i: (0, 0), pipeline_mode=pl.Buffered(1))
    return pl.pallas_call(
        _ffn_kernel,
        grid=(N_TOK // tm,),
        in_specs=[pl.BlockSpec((tm, D_MODEL), row), pl.BlockSpec((tm, D_MODEL), row),
                  pl.BlockSpec((1, 6, 1, D_MODEL), lambda i: (_mod_row(i, tm), 0, 0, 0)),
                  res((D_MODEL, FFN_DIM)), res((D_MODEL, FFN_DIM)), res((FFN_DIM, D_MODEL))],
        out_specs=pl.BlockSpec((tm, D_MODEL), row),
        out_shape=jax.ShapeDtypeStruct((N_TOK, D_MODEL), F32),
        compiler_params=_cparams(("arbitrary",)),
        name="ffn",
    )(u, x, mod_l, wg, wu, wd)


MOE_TM = 512
MOE_SPLIT = 2
MOE_CHUNK = EXPERT_DIM // MOE_SPLIT


def _moe_kernel(u_ref, c_ref, x_ref, m_ref, g_ref, wg_ref, wu_ref, wd_ref, o_ref, acc_ref):
    e = pl.program_id(1)
    c = pl.program_id(2)

    @pl.when((e == 0) & (c == 0))
    def _():
        acc_ref[...] = jnp.zeros_like(acc_ref)

    y = _swiglu_partial(u_ref[...], wg_ref[0], wu_ref[0], wd_ref[0])
    lane = lax.broadcasted_iota(jnp.int32, c_ref.shape, 1)
    ce = jnp.sum(jnp.where(lane == e, c_ref[...], 0.0), axis=-1, keepdims=True)
    acc_ref[...] += ce * y

    @pl.when((e == N_EXPERTS - 1) & (c == MOE_SPLIT - 1))
    def _():
        o_ref[...] = _rms(x_ref[...] + m_ref[0, 5] * acc_ref[...], g_ref[...])


def _moe(u, comb, x, mod_l, final_g, wg, wu, wd):
    tm = MOE_TM
    row = lambda i, e, c: (i, 0)
    return pl.pallas_call(
        _moe_kernel,
        grid=(N_TOK // tm, N_EXPERTS, MOE_SPLIT),
        in_specs=[pl.BlockSpec((tm, D_MODEL), row), pl.BlockSpec((tm, 128), row), pl.BlockSpec((tm, D_MODEL), row),
                  pl.BlockSpec((1, 6, 1, D_MODEL), lambda i, e, c: (_mod_row(i, tm), 0, 0, 0)),
                  pl.BlockSpec((1, D_MODEL), lambda i, e, c: (0, 0)),
                  pl.BlockSpec((1, D_MODEL, MOE_CHUNK), lambda i, e, c: (e, 0, c)),
                  pl.BlockSpec((1, D_MODEL, MOE_CHUNK), lambda i, e, c: (e, 0, c)),
                  pl.BlockSpec((1, MOE_CHUNK, D_MODEL), lambda i, e, c: (e, c, 0))],
        out_specs=pl.BlockSpec((tm, D_MODEL), row),
        out_shape=jax.ShapeDtypeStruct((N_TOK, D_MODEL), F32),
        scratch_shapes=[pltpu.VMEM((tm, D_MODEL), F32)],
        compiler_params=_cparams(("arbitrary", "arbitrary", "arbitrary")),
        name="moe",
    )(u, comb, x, mod_l, final_g, wg, wu, wd)


def _block_diag(w):
    w = w.reshape(2, C_WIDTH // 128, 2, C_BLOCK, C_BLOCK)
    z = jnp.zeros_like(w[:, :, 0])
    top = jnp.concatenate([w[:, :, 0], z], axis=-1)
    bot = jnp.concatenate([z, w[:, :, 1]], axis=-1)
    return jnp.concatenate([top, bot], axis=-2).astype(BF16)


def kernel(x_prompt, x_sample, cache_attn_k, cache_attn_v, state_hgrn, state_lru, c, c_ctx, w_mod, b_mod, norm_mix_g, norm_ffn_g, w_in, attn_sink, hgrn_lower, hgrn_norm_g, lru_conv_w, lru_conv_b, lru_wa, lru_ba, lru_wx, lru_bx, lru_lambda, w_branch_a, w_branch_b, w_branch_c, w_out, ffn_w_gate, ffn_w_up, ffn_w_down, moe_router, moe_w_gate, moe_w_up, moe_w_down, final_norm_g):
    assert DEPTH == 2
    x = jnp.concatenate([x_prompt.reshape(N_CTX_TOK, D_MODEL), x_sample.reshape(N_LAT_TOK, D_MODEL)], axis=0)
    cond = jnp.concatenate([c_ctx[None, :], c, jnp.zeros((N_COND - 1 - DEC_BATCH, D_MODEL), F32)], axis=0)
    mods = _modulation(cond, w_mod, b_mod).reshape(DEPTH, N_COND, 6, 1, D_MODEL)

    lb_cum = jnp.cumsum(jax.nn.softmax(hgrn_lower.astype(F32), axis=0), axis=0)
    lb_layers = lb_cum - lb_cum[:1]
    cos, sin = _rope_tables()
    hconst = _hgrn_constants(False) + _hgrn_constants(True)
    split = IN_COLS - 3 * D_MODEL

    ks, vs, hs, ls = [], [], [], []
    for l in range(DEPTH):
        w_in_l = jnp.concatenate([w_in[l][:, split:], w_in[l][:, :split]], axis=1).astype(BF16)
        lb = lb_layers[l]
        lbp = jnp.stack([jnp.log(lb), jnp.log1p(-lb), 1.0 - lb]).reshape(3, 2, 1, B_WIDTH)
        lp = dict(
            conv_w=lru_conv_w[l], conv_b=lru_conv_b[l].reshape(1, C_WIDTH),
            wa=_block_diag(lru_wa[l]), wx=_block_diag(lru_wx[l]),
            ba=lru_ba[l].reshape(2, 1, C_WIDTH), bx=lru_bx[l].reshape(2, 1, C_WIDTH),
            lam=lru_lambda[l].reshape(2, 1, C_WIDTH),
            w_ba=w_branch_a[l].astype(BF16), w_bb=w_branch_b[l].astype(BF16), w_bc=w_branch_c[l].astype(BF16),
            w_out=w_out[l].astype(BF16), norm_ffn=norm_ffn_g[l].reshape(1, D_MODEL))
        mod_l = mods[l]

        proj = _in_proj(x, mod_l, norm_mix_g[l].reshape(1, D_MODEL), w_in_l)

        attn = _ctx_attention(proj, attn_sink[l])
        attn = _lat_attention(proj, attn_sink[l], cache_attn_k[:, l].reshape(DEC_BATCH, PAST_LEN, 128),
                              cache_attn_v[:, l].reshape(DEC_BATCH, PAST_LEN, 128), cos, sin, attn)

        ng = hgrn_norm_g[l].reshape(1, B_KEY_DIM)
        hg, s_h = _hgrn(proj, lbp, ng, hconst, nbatch=BATCH, seq=SEQ, row_base=0, want_state=True)
        hg, = _hgrn(proj, lbp, ng, hconst, nbatch=DEC_BATCH, seq=DEC_SEQ, row_base=N_CTX_TOK,
                    state_in=state_hgrn[:, l], out_buf=hg)

        lr, s_l = _lru(proj, lp, nbatch=BATCH, seq=SEQ, row_base=0, reset=True, want_state=True)
        lr, = _lru(proj, lp, nbatch=DEC_BATCH, seq=DEC_SEQ, row_base=N_CTX_TOK, reset=False,
                   state_in=state_lru[:, l], out_buf=lr)

        ks.append(proj[:N_CTX_TOK, KA_OFF:KA_OFF + 128].reshape(BATCH, SEQ, A_KV_HEADS, A_HEAD_DIM))
        vs.append(proj[:N_CTX_TOK, VA_OFF:VA_OFF + 128].reshape(BATCH, SEQ, A_KV_HEADS, A_HEAD_DIM))
        hs.append(s_h)
        ls.append(s_l)

        m = l // 2
        if l % 2 == 0:
            x, u = _merge(x, mod_l, proj, attn, hg, lr, lp)
            x = _ffn(u, x, mod_l, ffn_w_gate[m].astype(BF16), ffn_w_up[m].astype(BF16), ffn_w_down[m].astype(BF16))
        else:
            router = jnp.pad(moe_router[m], ((0, 0), (0, 128 - N_EXPERTS)))
            r_hi = router.astype(BF16)
            router = jnp.stack([r_hi, (router - r_hi.astype(F32)).astype(BF16)])
            x, u, comb = _merge(x, mod_l, proj, attn, hg, lr, lp, router=router)
            x = _moe(u, comb, x, mod_l, final_norm_g.reshape(1, D_MODEL), moe_w_gate[m].astype(BF16),
                     moe_w_up[m].astype(BF16), moe_w_down[m].astype(BF16))

    y_prompt = x[:N_CTX_TOK].reshape(BATCH, SEQ, D_MODEL)
    y_sample = x[N_CTX_TOK:].reshape(DEC_BATCH, DEC_SEQ, D_MODEL)
    return (y_prompt, y_sample, jnp.stack(ks, axis=1), jnp.stack(vs, axis=1), jnp.stack(hs, axis=1),
            jnp.stack(ls, axis=1))
```

```python
import functools

import numpy as np
import jax
import jax.numpy as jnp
from jax import lax
from jax.experimental import pallas as pl
from jax.experimental.pallas import tpu as pltpu

F32 = jnp.float32
BF16 = jnp.bfloat16
HIGHEST = lax.Precision.HIGHEST

D_MODEL = 1024
BATCH = 32
SEQ = 256
DEPTH = 2
DEC_BATCH = 4
DEC_SEQ = 2048
PAST_LEN = 512
GRID_W = 64
RMS_EPS = 1e-6
A_HEADS = 8
A_KV_HEADS = 2
A_GROUP = 4
A_HEAD_DIM = 64
A_WIDTH = 512
A_WINDOW = 128
ROPE_THETA = 10000.0
NEG_INF = -1e30
B_HEADS = 4
B_KEY_DIM = 128
B_WIDTH = 512
C_WIDTH = 512
C_HEADS = 8
C_BLOCK = 64
C_EXP = 8.0
FFN_DIM = 2816
N_EXPERTS = 8
EXPERT_DIM = 3584

N_CTX_TOK = BATCH * SEQ
N_LAT_TOK = DEC_BATCH * DEC_SEQ
N_TOK = N_CTX_TOK + N_LAT_TOK
N_COND = 8

ZG_OFF = 0
QA_OFF = 3072
KA_OFF = 3584
VA_OFF = 3712
B_OFF = 3840
C_OFF = 6400
IN_COLS = 7424

V7X_VMEM_LIMIT = 56 * 1024 * 1024
HG_CHUNK = 128
HG_LEVELS = (64, 32, 16, 8, 4, 2, 1)
LRU_SEGS = 8
LRU_PITCH_PAD = 4


def _cparams(sem, vmem=V7X_VMEM_LIMIT):
    return pltpu.CompilerParams(dimension_semantics=sem, vmem_limit_bytes=vmem)


def _mod_row(i, tm):
    nct = N_CTX_TOK // tm
    return jnp.where(i < nct, 0, 1 + (i - nct) // (DEC_SEQ // tm))


def _sigmoid(x):
    return 0.5 * jnp.tanh(0.5 * x) + 0.5


def _silu(x):
    return x * _sigmoid(x)


def _log_sigmoid(z):
    return jnp.minimum(z, 0.0) - jnp.log1p(jnp.exp(-jnp.abs(z)))


def _rms(x, g):
    ms = jnp.mean(x * x, axis=-1, keepdims=True)
    return x * lax.rsqrt(ms + RMS_EPS) * g


def _dot(a, b):
    return jnp.dot(a, b, preferred_element_type=F32)


def _dot_nt(a, b):
    return lax.dot_general(a, b, (((1,), (1,)), ((), ())), preferred_element_type=F32)


def _dot_tn(a, b):
    return lax.dot_general(a, b, (((0,), (0,)), ((), ())), preferred_element_type=F32)


def _mod_kernel(c_ref, w_ref, b_ref, o_ref):
    a = _silu(c_ref[...])
    o_ref[0] = jnp.dot(a, w_ref[0], preferred_element_type=F32, precision=HIGHEST) + b_ref[0]


def _modulation(cond, w_mod, b_mod):
    tn = 1536
    return pl.pallas_call(
        _mod_kernel,
        grid=(DEPTH, 6 * D_MODEL // tn),
        in_specs=[pl.BlockSpec((N_COND, D_MODEL), lambda l, j: (0, 0)),
                  pl.BlockSpec((1, D_MODEL, tn), lambda l, j: (l, 0, j)),
                  pl.BlockSpec((1, 1, tn), lambda l, j: (l, 0, j))],
        out_specs=pl.BlockSpec((1, N_COND, tn), lambda l, j: (l, 0, j)),
        out_shape=jax.ShapeDtypeStruct((DEPTH, N_COND, 6 * D_MODEL), F32),
        compiler_params=_cparams(("arbitrary", "arbitrary")),
        name="modulation",
    )(cond, w_mod, b_mod.reshape(DEPTH, 1, 6 * D_MODEL))


IN_TM = 256
IN_TN = 256


def _in_proj_kernel(x_ref, m_ref, g_ref, w_ref, o_ref):
    u = _rms(x_ref[...], g_ref[...]) * (1.0 + m_ref[0, 1]) + m_ref[0, 0]
    ub = u.astype(BF16)
    for j in range(IN_COLS // IN_TN):
        sl = slice(j * IN_TN, (j + 1) * IN_TN)
        o_ref[:, sl] = _dot(ub, w_ref[:, sl])


def _in_proj(x, mod_l, g, w):
    tm = IN_TM
    return pl.pallas_call(
        _in_proj_kernel,
        grid=(N_TOK // tm,),
        in_specs=[pl.BlockSpec((tm, D_MODEL), lambda i: (i, 0)),
                  pl.BlockSpec((1, 6, 1, D_MODEL), lambda i: (_mod_row(i, tm), 0, 0, 0)),
                  pl.BlockSpec((1, D_MODEL), lambda i: (0, 0)),
                  pl.BlockSpec((D_MODEL, IN_COLS), lambda i: (0, 0), pipeline_mode=pl.Buffered(1))],
        out_specs=pl.BlockSpec((tm, IN_COLS), lambda i: (i, 0)),
        out_shape=jax.ShapeDtypeStruct((N_TOK, IN_COLS), F32),
        compiler_params=_cparams(("arbitrary",)),
        name="in_proj",
    )(x, mod_l, g, w)


def _softmax_pv(s_list, v_list, sink):
    m = sink
    for s in s_list:
        m = jnp.maximum(m, jnp.max(s, axis=-1, keepdims=True))
    den = jnp.exp(sink - m)
    out = None
    for s, v in zip(s_list, v_list):
        p = jnp.exp(s - m)
        den = den + jnp.sum(p, axis=-1, keepdims=True)
        o = _dot(p.astype(BF16), v)
        out = o if out is None else out + o
    return out / den


def _sink_column(sink_ref, h, rows_per_head):
    n = A_GROUP * rows_per_head
    row = lax.broadcasted_iota(jnp.int32, (n, 1), 0)
    col = jnp.full((n, 1), sink_ref[A_GROUP * h + A_GROUP - 1], F32)
    for g in range(A_GROUP - 2, -1, -1):
        col = jnp.where(row < (g + 1) * rows_per_head, sink_ref[A_GROUP * h + g], col)
    return col


def _ctx_attn_kernel(sink_ref, q_ref, k_ref, v_ref, o_ref):
    q = q_ref[...]
    k = k_ref[...]
    v = v_ref[...]
    scale = A_HEAD_DIM ** -0.5
    outs = []
    for h in range(A_KV_HEADS):
        hs = slice(h * A_HEAD_DIM, (h + 1) * A_HEAD_DIM)
        kh = k[:, hs].astype(BF16)
        vh = v[:, hs].astype(BF16)
        qs = jnp.concatenate(
            [q[:, (A_GROUP * h + g) * A_HEAD_DIM:(A_GROUP * h + g + 1) * A_HEAD_DIM] for g in range(A_GROUP)],
            axis=0).astype(BF16)
        s = _dot_nt(qs, kh) * scale
        o = _softmax_pv([s], [vh], _sink_column(sink_ref, h, SEQ))
        outs.extend(o[g * SEQ:(g + 1) * SEQ] for g in range(A_GROUP))
    o_ref[...] = jnp.concatenate(outs, axis=-1).astype(BF16)


def _ctx_attention(proj, sink):
    return pl.pallas_call(
        _ctx_attn_kernel,
        grid=(BATCH,),
        in_specs=[pl.BlockSpec(memory_space=pltpu.SMEM),
                  pl.BlockSpec((SEQ, A_WIDTH), lambda b: (b, QA_OFF // A_WIDTH)),
                  pl.BlockSpec((SEQ, 128), lambda b: (b, KA_OFF // 128)),
                  pl.BlockSpec((SEQ, 128), lambda b: (b, VA_OFF // 128))],
        out_specs=pl.BlockSpec((SEQ, A_WIDTH), lambda b: (b, 0)),
        out_shape=jax.ShapeDtypeStruct((N_CTX_TOK, A_WIDTH), BF16),
        compiler_params=_cparams(("arbitrary",)),
        name="ctx_attention",
    )(sink, proj, proj, proj)


def _rope_tables():
    half = A_HEAD_DIM // 4
    inv = ROPE_THETA ** (-np.arange(half, dtype=np.float64) / half)
    t = np.arange(DEC_SEQ)
    row = (t // GRID_W)[:, None] * inv[None, :]
    col = (t % GRID_W)[:, None] * inv[None, :]
    cos = np.concatenate([np.cos(row), np.cos(row), np.cos(col), np.cos(col)], axis=1)
    sin = np.concatenate([-np.sin(row), np.sin(row), -np.sin(col), np.sin(col)], axis=1)
    return (jnp.asarray(np.tile(cos, (1, 2)), F32), jnp.asarray(np.tile(sin, (1, 2)), F32))


def _rope(x, cos, sin, first):
    part = jnp.where(first, pltpu.roll(x, 128 - 16, 1), pltpu.roll(x, 16, 1))
    return x * cos + part * sin


def _lat_attn_kernel(sink_ref, q_ref, km_ref, k0_ref, kp_ref, vm_ref, v0_ref, vp_ref, kc_ref, vc_ref,
                     cm_ref, c0_ref, cp_ref, sm_ref, s0_ref, sp_ref, o_ref):
    j = pl.program_id(1)
    blk = A_WINDOW
    lane = lax.broadcasted_iota(jnp.int32, (blk, 128), 1)
    first = (lane % 32) < 16
    c0 = c0_ref[...]
    s0 = s0_ref[...]
    q = q_ref[...]
    qr = [_rope(q[:, c * 128:(c + 1) * 128], c0, s0, first) for c in range(A_WIDTH // 128)]
    kw = jnp.concatenate([_rope(km_ref[...], cm_ref[...], sm_ref[...], first),
                          _rope(k0_ref[...], c0, s0, first),
                          _rope(kp_ref[...], cp_ref[...], sp_ref[...], first)], axis=0)
    vw = jnp.concatenate([vm_ref[...], v0_ref[...], vp_ref[...]], axis=0)
    kc = kc_ref[0]
    vc = vc_ref[0]
    nq = A_GROUP * blk
    t = lax.broadcasted_iota(jnp.int32, (nq, 3 * blk), 0) % blk
    s = lax.broadcasted_iota(jnp.int32, (nq, 3 * blk), 1) - blk
    kpos = j * blk + s
    mask = (jnp.abs(s - t) <= A_WINDOW) & (kpos >= 0) & (kpos < DEC_SEQ)
    scale = A_HEAD_DIM ** -0.5
    outs = []
    for h in range(A_KV_HEADS):
        hs = slice(h * A_HEAD_DIM, (h + 1) * A_HEAD_DIM)
        heads = [A_GROUP * h + g for g in range(A_GROUP)]
        qs = jnp.concatenate(
            [qr[hd // 2][:, (hd % 2) * A_HEAD_DIM:(hd % 2 + 1) * A_HEAD_DIM] for hd in heads], axis=0).astype(BF16)
        sw = jnp.where(mask, _dot_nt(qs, kw[:, hs].astype(BF16)) * scale, NEG_INF)
        sc = _dot_nt(qs, kc[:, hs].astype(BF16)) * scale
        o = _softmax_pv([sw, sc], [vw[:, hs].astype(BF16), vc[:, hs].astype(BF16)],
                        _sink_column(sink_ref, h, blk))
        outs.extend(o[g * blk:(g + 1) * blk] for g in range(A_GROUP))
    o_ref[...] = jnp.concatenate(outs, axis=-1).astype(BF16)


def _lat_attention(proj, sink, k_ctx, v_ctx, cos, sin):
    blk = A_WINDOW
    nb = DEC_SEQ // blk
    base = N_CTX_TOK // blk

    def rows(off):
        return lambda b, j: (base + b * nb + jnp.clip(j + off, 0, nb - 1))

    def kv_spec(col, off):
        r = rows(off)
        return pl.BlockSpec((blk, 128), lambda b, j: (r(b, j), col))

    def tab_spec(off):
        return pl.BlockSpec((blk, 128), lambda b, j: (jnp.clip(j + off, 0, nb - 1), 0))

    r0 = rows(0)
    ctx_spec = pl.BlockSpec((1, PAST_LEN, 128), lambda b, j: (b, 0, 0))
    return pl.pallas_call(
        _lat_attn_kernel,
        grid=(DEC_BATCH, nb),
        in_specs=[pl.BlockSpec(memory_space=pltpu.SMEM),
                  pl.BlockSpec((blk, A_WIDTH), lambda b, j: (r0(b, j), QA_OFF // A_WIDTH)),
                  kv_spec(KA_OFF // 128, -1), kv_spec(KA_OFF // 128, 0), kv_spec(KA_OFF // 128, 1),
                  kv_spec(VA_OFF // 128, -1), kv_spec(VA_OFF // 128, 0), kv_spec(VA_OFF // 128, 1),
                  ctx_spec, ctx_spec,
                  tab_spec(-1), tab_spec(0), tab_spec(1), tab_spec(-1), tab_spec(0), tab_spec(1)],
        out_specs=pl.BlockSpec((blk, A_WIDTH), lambda b, j: (b * nb + j, 0)),
        out_shape=jax.ShapeDtypeStruct((N_LAT_TOK, A_WIDTH), BF16),
        compiler_params=_cparams(("arbitrary", "arbitrary")),
        name="lat_attention",
    )(sink, proj, proj, proj, proj, proj, proj, proj, k_ctx, v_ctx, cos, cos, cos, sin, sin, sin)


def _hgrn_constants(reverse):
    L = HG_CHUNK
    t = np.arange(L)[:, None]
    s = np.arange(L)[None, :]
    tri = (s <= t).astype(np.float32)
    blocks = [tri]
    masks = []
    for b in HG_LEVELS:
        anchor = (t // (2 * b)) * 2 * b + b - 1
        right = np.where(t % (2 * b) >= b, 1.0, -1.0)
        blocks.append(right * (tri - (s <= anchor).astype(np.float32)))
        masks.append(((t // (2 * b) == s // (2 * b)) & (t % (2 * b) >= b) & (s % (2 * b) < b)).astype(np.float32))
    masks.append((t == s).astype(np.float32))
    if reverse:
        blocks = [m[::-1, ::-1] for m in blocks]
        masks = [m[::-1, ::-1] for m in masks]
    mat = np.concatenate(blocks, axis=0)
    return (jnp.asarray(np.concatenate([mat, mat], axis=1), BF16), jnp.asarray(np.stack(masks), F32))


LOG2E = 1.4426950408889634


def _hgrn_chunk(qraw, vv, z, lbp, st, mat, mask_ref, reverse):
    L = HG_CHUNK
    log_lb, log_1mlb, one_mlb = lbp
    q = qraw * _sigmoid(qraw) * (B_KEY_DIM ** -0.5)
    e = jnp.exp(-jnp.abs(z))
    r = 1.0 / (1.0 + e)
    k = one_mlb * jnp.where(z >= 0.0, e * r, r)
    a = log_lb
    b = log_1mlb + (jnp.minimum(z, 0.0) + jnp.log(r))
    lf = (jnp.maximum(a, b) + jnp.log(1.0 + jnp.exp(-jnp.abs(a - b)))) * LOG2E
    hi = lf.astype(BF16)
    lo = (lf - hi.astype(F32)).astype(BF16)
    g = _dot(mat, jnp.concatenate([hi, lo], axis=0))
    cum = g[0:L]
    tot = cum[0:1] if reverse else cum[L - 1:L]
    vb = vv.astype(BF16)
    o = _dot_nt((q * jnp.exp2(cum)).astype(BF16), st.astype(BF16))
    kd = (k * jnp.exp2(tot - cum)).astype(BF16)
    st_new = st * jnp.exp2(tot) + _dot_tn(vb, kd)
    nl = len(HG_LEVELS)
    row = lax.broadcasted_iota(jnp.int32, (L, 1), 0)
    if reverse:
        row = L - 1 - row
    sc = _dot_nt(q.astype(BF16), k.astype(BF16)) * mask_ref[nl]
    for li, bsz in enumerate(HG_LEVELS):
        is_query = (row & (2 * bsz - 1)) >= bsz
        zb = (jnp.where(is_query, q, k) * jnp.exp2(g[(1 + li) * L:(2 + li) * L])).astype(BF16)
        sc = sc + _dot_nt(zb, zb) * mask_ref[li]
    o = o + _dot(sc.astype(BF16), vb)
    return o, st_new


def _hgrn(proj, lbp, norm_g, consts, *, nbatch, seq, row_base, state_in=None, want_state=False):
    mat_f, mask_f, mat_b, mask_b = consts
    rb = row_base // seq
    nl = len(HG_LEVELS) + 1

    def col(k):
        return pl.BlockSpec((seq, 128), lambda b, h: (rb + b, (B_OFF + k * B_WIDTH) // 128 + h))

    const2 = lambda shape: pl.BlockSpec(shape, lambda b, h: (0,) * len(shape))
    in_specs = [col(0), col(1), col(2), col(3), col(4),
                pl.BlockSpec((3, 2, 1, 128), lambda b, h: (0, 0, 0, h)),
                const2((1, 128)),
                const2(mat_f.shape), const2(mat_b.shape), const2(mask_f.shape), const2(mask_b.shape)]
    args = [proj, proj, proj, proj, proj, lbp, norm_g, mat_f, mat_b, mask_f, mask_b]
    if state_in is not None:
        in_specs.append(pl.BlockSpec((1, 2, 1, 128, 128), lambda b, h: (b, 0, h, 0, 0)))
        args.append(state_in)
    out_specs = [pl.BlockSpec((seq, 128), lambda b, h: (b, h))]
    out_shape = [jax.ShapeDtypeStruct((nbatch * seq, B_WIDTH), BF16)]
    if want_state:
        out_specs.append(pl.BlockSpec((1, 2, 1, 128, 128), lambda b, h: (b, 0, h, 0, 0)))
        out_shape.append(jax.ShapeDtypeStruct((nbatch, 2, B_HEADS, 128, 128), F32))
    kern = functools.partial(_hgrn_kernel_flat, seq=seq, has_state_in=state_in is not None,
                             has_state_out=want_state)
    return pl.pallas_call(
        kern,
        grid=(nbatch, B_HEADS),
        in_specs=in_specs,
        out_specs=out_specs,
        out_shape=out_shape,
        scratch_shapes=[pltpu.VMEM((seq, 128), F32)],
        compiler_params=_cparams(("arbitrary", "arbitrary")),
        name="hgrn_%d" % seq,
    )(*args)


def _hgrn_kernel_flat(*refs, seq, has_state_in, has_state_out):
    refs = list(refs)
    head = refs[:11]
    pos = 11
    s0_ref = None
    if has_state_in:
        s0_ref = refs[pos]
        pos += 1
    o_ref = refs[pos]
    pos += 1
    s_ref = None
    if has_state_out:
        s_ref = refs[pos]
        pos += 1
    acc_ref = refs[pos]
    _hgrn_body(*head, s0_ref, o_ref, s_ref, acc_ref, seq=seq)


def _hgrn_body(q_ref, i_ref, zf_ref, zb_ref, gt_ref, lb_ref, ng_ref, mf_ref, mb_ref, kf_ref, kb_ref,
               s0_ref, o_ref, s_ref, acc_ref, *, seq):
    L = HG_CHUNK
    n = seq // L
    acc_ref[...] = jnp.zeros_like(acc_ref)
    lbf = (lb_ref[0, 0], lb_ref[1, 0], lb_ref[2, 0])
    lbb = (lb_ref[0, 1], lb_ref[1, 1], lb_ref[2, 1])
    if s0_ref is not None:
        st_f = s0_ref[0, 0, 0].T
        st_b = s0_ref[0, 1, 0].T
    else:
        st_f = jnp.zeros((B_KEY_DIM, B_KEY_DIM), F32)
        st_b = jnp.zeros((B_KEY_DIM, B_KEY_DIM), F32)

    def body(i, carry):
        st_f, st_b = carry
        sf = pl.ds(pl.multiple_of(i * L, L), L)
        sb = pl.ds(pl.multiple_of((n - 1 - i) * L, L), L)
        o_f, st_f = _hgrn_chunk(q_ref[sf, :], i_ref[sf, :], zf_ref[sf, :], lbf, st_f, mf_ref[...], kf_ref, False)
        acc_ref[sf, :] += o_f
        o_b, st_b = _hgrn_chunk(q_ref[sb, :], i_ref[sb, :], zb_ref[sb, :], lbb, st_b, mb_ref[...], kb_ref, True)
        acc_ref[sb, :] += o_b
        return st_f, st_b

    st_f, st_b = lax.fori_loop(0, n, body, (st_f, st_b), unroll=2)
    o_ref[...] = (_rms(acc_ref[...], ng_ref[...]) * _silu(gt_ref[...])).astype(BF16)
    if s_ref is not None:
        s_ref[0, 0, 0] = st_f.T
        s_ref[0, 1, 0] = st_b.T


def _gelu_tanh(y):
    return 0.5 * y * (1.0 + jnp.tanh(0.7978845608028654 * (y + 0.044715 * (y * y * y))))


def _lru_body(xc_ref, yc_ref, cw_ref, cb_ref, wa_ref, ba_ref, wx_ref, bx_ref, lam_ref, h0_ref, o_ref, s_ref,
              xs_ref, a_ref, u_ref, *, seq, reset):
    T = seq
    pad = 8
    zeros = jnp.zeros((pad, 128), F32)
    xs_ref[0:pad, :] = zeros
    xs_ref[pad:pad + T, :] = xc_ref[...]
    xs_ref[pad + T:2 * pad + T, :] = zeros
    cw = cw_ref[...]
    x = cb_ref[...]
    for tap in range(4):
        x = x + cw[tap:tap + 1] * xs_ref[pad - 2 + tap:pad - 2 + tap + T, :]
    xb = x.astype(BF16)
    row = lax.broadcasted_iota(jnp.int32, (T, 1), 0)
    seg = T // LRU_SEGS
    pitch = seg + LRU_PITCH_PAD
    for d in range(2):
        r = _sigmoid(_dot(xb, wa_ref[d, 0]) + ba_ref[d])
        ig = _sigmoid(_dot(xb, wx_ref[d, 0]) + bx_ref[d])
        log_a = (C_EXP * r) * _log_sigmoid(lam_ref[d])
        a = jnp.exp(log_a)
        mult = jnp.sqrt(1.0 - a * a)
        if reset:
            mult = jnp.where(row == (0 if d == 0 else T - 1), 1.0, mult)
        u = mult * ig * x
        for r in range(LRU_SEGS):
            a_ref[d, r * pitch:r * pitch + seg, :] = a[r * seg:(r + 1) * seg]
            u_ref[d, r * pitch:r * pitch + seg, :] = u[r * seg:(r + 1) * seg]
    sub =lax.broadcasted_iota(jnp.int32, (LRU_SEGS, 1), 0)
    zrow = jnp.zeros((1, 128), F32)
    if h0_ref is not None:
        hf0 = jnp.where(sub == 0, h0_ref[0, 0:1, :], 0.0)
        hb0 = jnp.where(sub == LRU_SEGS - 1, h0_ref[0, 1:2, :], 0.0)
    else:
        hf0 = jnp.zeros((LRU_SEGS, 128), F32)
        hb0 = hf0
    ones = jnp.ones((LRU_SEGS, 128), F32)

    def local_scan(i, carry):
        hf, pf, hb, pb = carry
        sf = pl.ds(i, LRU_SEGS, stride=pitch)
        sb = pl.ds(seg - 1 - i, LRU_SEGS, stride=pitch)
        af = a_ref[0, sf, :]
        hf = af * hf + u_ref[0, sf, :]
        pf = af * pf
        u_ref[0, sf, :] = hf
        a_ref[0, sf, :] = pf
        ab = a_ref[1, sb, :]
        hb = ab * hb + u_ref[1, sb, :]
        pb = ab * pb
        u_ref[1, sb, :] = hb
        a_ref[1, sb, :] = pb
        return hf, pf, hb, pb

    hf, pf, hb, pb = lax.fori_loop(0, seg, local_scan, (hf0, ones, hb0, ones), unroll=4)
    carry_f = [zrow] * LRU_SEGS
    carry_b = [zrow] * LRU_SEGS
    end_f = hf[0:1]
    for r in range(1, LRU_SEGS):
        carry_f[r] = end_f
        end_f = hf[r:r + 1] + pf[r:r + 1] * end_f
    end_b = hb[LRU_SEGS - 1:LRU_SEGS]
    for r in range(LRU_SEGS - 2, -1, -1):
        carry_b[r] = end_b
        end_b = hb[r:r + 1] + pb[r:r + 1] * end_b
    for r in range(LRU_SEGS):
        sl = slice(r * seg, (r + 1) * seg)
        ps = slice(r * pitch, r * pitch + seg)
        h = (u_ref[0, ps, :] + a_ref[0, ps, :] * carry_f[r]) + (u_ref[1, ps, :] + a_ref[1, ps, :] * carry_b[r])
        o_ref[sl, :] = (h * _gelu_tanh(yc_ref[sl, :])).astype(BF16)
    if s_ref is not None:
        s_ref[0, 0:1, :] = end_f
        s_ref[0, 1:2, :] = end_b


def _lru_kernel_flat(*refs, seq, reset, has_state_in, has_state_out):
    refs = list(refs)
    head = refs[:9]
    pos = 9
    h0_ref = None
    if has_state_in:
        h0_ref = refs[pos]
        pos += 1
    o_ref = refs[pos]
    pos += 1
    s_ref = None
    if has_state_out:
        s_ref = refs[pos]
        pos += 1
    _lru_body(*head, h0_ref, o_ref, s_ref, *refs[pos:], seq=seq, reset=reset)


def _lru(proj, lp, *, nbatch, seq, row_base, reset, state_in=None, want_state=False):
    rb = row_base // seq
    nblk = C_WIDTH // 128

    def col(k):
        return pl.BlockSpec((seq, 128), lambda b, c: (rb + b, (C_OFF + k * C_WIDTH) // 128 + c))

    vec = lambda rows: pl.BlockSpec((rows, 1, 128), lambda b, c: (0, 0, c))
    mat = pl.BlockSpec((2, 1, 128, 128), lambda b, c: (0, c, 0, 0))
    in_specs = [col(0), col(1),
                pl.BlockSpec((4, 128), lambda b, c: (0, c)), pl.BlockSpec((1, 128), lambda b, c: (0, c)),
                mat, vec(2), mat, vec(2), vec(2)]
    args = [proj, proj, lp['conv_w'], lp['conv_b'], lp['wa'], lp['ba'], lp['wx'], lp['bx'], lp['lam']]
    if state_in is not None:
        in_specs.append(pl.BlockSpec((1, 2, 128), lambda b, c: (b, 0, c)))
        args.append(state_in)
    out_specs = [pl.BlockSpec((seq, 128), lambda b, c: (b, c))]
    out_shape = [jax.ShapeDtypeStruct((nbatch * seq, C_WIDTH), BF16)]
    if want_state:
        out_specs.append(pl.BlockSpec((1, 2, 128), lambda b, c: (b, 0, c)))
        out_shape.append(jax.ShapeDtypeStruct((nbatch, 2, C_WIDTH), F32))
    kern = functools.partial(_lru_kernel_flat, seq=seq, reset=reset, has_state_in=state_in is not None,
                             has_state_out=want_state)
    return pl.pallas_call(
        kern,
        grid=(nbatch, nblk),
        in_specs=in_specs,
        out_specs=out_specs,
        out_shape=out_shape,
        scratch_shapes=[pltpu.VMEM((seq + 16, 128), F32), pltpu.VMEM((2, seq + LRU_SEGS * LRU_PITCH_PAD, 128), F32),
                        pltpu.VMEM((2, seq + LRU_SEGS * LRU_PITCH_PAD, 128), F32)],
        compiler_params=_cparams(("arbitrary", "arbitrary")),
        name="lru_%d" % seq,
    )(*args)


MERGE_TM = 256


def _top2_combine(logits):
    lane = lax.broadcasted_iota(jnp.int32, logits.shape, 1).astype(F32)
    lg = jnp.where(lane < N_EXPERTS, logits, -jnp.inf)
    m1 = jnp.max(lg, axis=-1, keepdims=True)
    i1 = jnp.min(jnp.where(lg == m1, lane, 128.0), axis=-1, keepdims=True)
    lg2 = jnp.where(lane == i1, -jnp.inf, lg)
    m2 = jnp.max(lg2, axis=-1, keepdims=True)
    i2 = jnp.min(jnp.where(lg2 == m2, lane, 128.0), axis=-1, keepdims=True)
    e = jnp.exp(m2 - m1)
    w1 = 1.0 / (1.0 + e)
    return jnp.where(lane == i1, w1, 0.0) + jnp.where(lane == i2, e * w1, 0.0)


def _merge_body(x_ref, m_ref, z0_ref, z1_ref, z2_ref, ac_ref, al_ref, hc_ref, hl_ref, lc_ref, ll_ref,
                wba_ref, wbb_ref, wbc_ref, wo_ref, g_ref, r_ref, xo_ref, u_ref, c_ref):
    is_ctx = pl.program_id(0) < N_CTX_TOK // MERGE_TM
    pick = lambda c_ref_, l_ref_: jnp.where(is_ctx, c_ref_[...], l_ref_[...])
    mg = (_sigmoid(z0_ref[...]) * _dot(pick(ac_ref, al_ref), wba_ref[...])
          + _sigmoid(z1_ref[...]) * _dot(pick(hc_ref, hl_ref), wbb_ref[...])
          + _sigmoid(z2_ref[...]) * _dot(pick(lc_ref, ll_ref), wbc_ref[...]))
    x = x_ref[...] + m_ref[0, 2] * _dot(mg.astype(BF16), wo_ref[...])
    xo_ref[...] = x
    u = _rms(x, g_ref[...]) * (1.0 + m_ref[0, 4]) + m_ref[0, 3]
    ub = u.astype(BF16)
    u_ref[...] = ub
    if r_ref is not None:
        ul = (u - ub.astype(F32)).astype(BF16)
        logits = _dot(ub, r_ref[0]) + (_dot(ub, r_ref[1]) + _dot(ul, r_ref[0]))
        c_ref[...] = _top2_combine(logits)


def _merge_kernel_dense(*refs):
    _merge_body(*refs[:16], None, *refs[16:], None)


def _merge(x, mod_l, proj, mixers, lp, router=None):
    tm = MERGE_TM
    nct = N_CTX_TOK // tm
    row = lambda i: (i, 0)
    const = lambda i: (0, 0)
    ctx_row = lambda i: (jnp.minimum(i, nct - 1), 0)
    lat_row = lambda i: (jnp.maximum(i - nct, 0), 0)
    res = lambda shape: pl.BlockSpec(shape, const, pipeline_mode=pl.Buffered(1))
    in_specs = [pl.BlockSpec((tm, D_MODEL), row),
                pl.BlockSpec((1, 6, 1, D_MODEL), lambda i: (_mod_row(i, tm), 0, 0, 0)),
                pl.BlockSpec((tm, D_MODEL), lambda i: (i, 0)),
                pl.BlockSpec((tm, D_MODEL), lambda i: (i, 1)),
                pl.BlockSpec((tm, D_MODEL), lambda i: (i, 2))]
    in_specs += [pl.BlockSpec((tm, 512), ctx_row), pl.BlockSpec((tm, 512), lat_row)] * 3
    in_specs += [res((A_WIDTH, D_MODEL)), res((B_WIDTH, D_MODEL)), res((C_WIDTH, D_MODEL)), res((D_MODEL, D_MODEL)),
                 pl.BlockSpec((1, D_MODEL), const)]
    args = [x, mod_l, proj, proj, proj, *mixers, lp['w_ba'], lp['w_bb'], lp['w_bc'], lp['w_out'], lp['norm_ffn']]
    out_specs = [pl.BlockSpec((tm, D_MODEL), row), pl.BlockSpec((tm, D_MODEL), row)]
    out_shape = [jax.ShapeDtypeStruct((N_TOK, D_MODEL), F32), jax.ShapeDtypeStruct((N_TOK, D_MODEL), BF16)]
    kern = _merge_kernel_dense
    if router is not None:
        in_specs.append(pl.BlockSpec((2, D_MODEL, 128), lambda i: (0, 0, 0)))
        args.append(router)
        out_specs.append(pl.BlockSpec((tm, 128), row))
        out_shape.append(jax.ShapeDtypeStruct((N_TOK, 128), F32))
        kern = _merge_body
    return pl.pallas_call(
        kern,
        grid=(N_TOK // tm,),
        in_specs=in_specs,
        out_specs=out_specs,
        out_shape=out_shape,
        compiler_params=_cparams(("arbitrary",)),
        name="merge",
    )(*args)


FFN_TM = 512
FFN_CHUNK = FFN_DIM // 2


def _swiglu_partial(u, wg, wu, wd):
    h = (_silu(_dot(u, wg)) * _dot(u, wu)).astype(BF16)
    return _dot(h, wd)


def _ffn_kernel(u_ref, x_ref, m_ref, wg_ref, wu_ref, wd_ref, o_ref):
    u = u_ref[...]
    acc = None
    for c in range(FFN_DIM // FFN_CHUNK):
        sl = slice(c * FFN_CHUNK, (c + 1) * FFN_CHUNK)
        y = _swiglu_partial(u, wg_ref[:, sl], wu_ref[:, sl], wd_ref[sl, :])
        acc = y if acc is None else acc + y
    o_ref[...] = x_ref[...] + m_ref[0, 5] * acc


def _ffn(u, x, mod_l, wg, wu, wd):
    tm = FFN_TM
    row = lambda i: (i, 0)
    res = lambda shape: pl.BlockSpec(shape, lambda i: (0, 0), pipeline_mode=pl.Buffered(1))
    return pl.pallas_call(
        _ffn_kernel,
        grid=(N_TOK // tm,),
        in_specs=[pl.BlockSpec((tm, D_MODEL), row), pl.BlockSpec((tm, D_MODEL), row),
                  pl.BlockSpec((1, 6, 1, D_MODEL), lambda i: (_mod_row(i, tm), 0, 0, 0)),
                  res((D_MODEL, FFN_DIM)), res((D_MODEL, FFN_DIM)), res((FFN_DIM, D_MODEL))],
        out_specs=pl.BlockSpec((tm, D_MODEL), row),
        out_shape=jax.ShapeDtypeStruct((N_TOK, D_MODEL), F32),
        compiler_params=_cparams(("arbitrary",)),
        name="ffn",
    )(u, x, mod_l, wg, wu, wd)


MOE_TM = 512
MOE_NT = N_TOK // MOE_TM
MOE_PIECE = 16
MOE_SLAB = 2 * MOE_TM + N_EXPERTS * MOE_PIECE
MOE_BLK = 512
MOE_NB = -(-(2 * N_TOK + MOE_NT * N_EXPERTS * (MOE_PIECE - 1) + N_EXPERTS * (MOE_BLK - 1)) // MOE_BLK)
MOE_ROWS = MOE_NB * MOE_BLK
MOE_FCHUNK = 896


def _moe_count_kernel(c_ref, n_ref):
    n_ref[0] = jnp.sum(jnp.where(c_ref[...] > 0.0, 1.0, 0.0), axis=0, keepdims=True)


def _moe_count(comb):
    return pl.pallas_call(
        _moe_count_kernel,
        grid=(MOE_NT,),
        in_specs=[pl.BlockSpec((MOE_TM, 128), lambda i: (i, 0))],
        out_specs=pl.BlockSpec((1, 1, 128), lambda i: (i, 0, 0)),
        out_shape=jax.ShapeDtypeStruct((MOE_NT, 1, 128), F32),
        compiler_params=_cparams(("arbitrary",)),
        name="moe_count",
    )(comb)


def _moe_tables(counts):
    cnt = counts[:, 0, :N_EXPERTS].astype(jnp.int32)
    pc = (cnt + MOE_PIECE - 1) // MOE_PIECE * MOE_PIECE
    lo = jnp.cumsum(pc, axis=1) - pc
    tot = jnp.sum(pc, axis=0)
    totp = (tot + MOE_BLK - 1) // MOE_BLK * MOE_BLK
    goff = (jnp.cumsum(totp) - totp)[None, :] + jnp.cumsum(pc, axis=0) - pc
    ends = jnp.cumsum(totp // MOE_BLK)
    nact = ends[-1]
    s = jnp.minimum(jnp.arange(MOE_NB, dtype=jnp.int32), nact - 1)
    blk_e = jnp.minimum(jnp.searchsorted(ends, s, side='right'), N_EXPERTS - 1).astype(jnp.int32)
    lo_vec = jnp.pad(lo.astype(F32), ((0, 0), (0, 128 - N_EXPERTS))).reshape(MOE_NT, 1, 128)
    return dict(lo=lo.reshape(-1), npc=(pc // MOE_PIECE).reshape(-1), goff=goff.reshape(-1),
                nrow=jnp.sum(pc, axis=1), blk_e=blk_e, nact=nact.reshape(1), lo_vec=lo_vec)


def _moe_pieces(i, lo_s, npc_s, goff_s, make):
    for wait in (False, True):
        for e in range(N_EXPERTS):
            k = i * N_EXPERTS + e

            def body(p, carry, k=k, wait=wait):
                local = pl.multiple_of(lo_s[k] + p * MOE_PIECE, MOE_PIECE)
                glob = pl.multiple_of(goff_s[k] + p * MOE_PIECE, MOE_PIECE)
                cp = make(local, glob)
                if wait:
                    cp.wait()
                else:
                    cp.start()
                return carry

            lax.fori_loop(0, npc_s[k], body, 0)


def _slot_pair(slot):
    sel = slot >= 0.0
    s_hi = jnp.max(slot, axis=-1, keepdims=True)
    s_lo = jnp.min(jnp.where(sel, slot, float(MOE_SLAB)), axis=-1, keepdims=True)
    return sel, s_hi, s_lo


def _one_hot_rows(s):
    srow = lax.broadcasted_iota(jnp.int32, (MOE_TM, MOE_SLAB), 1).astype(F32)
    return jnp.where(srow == s, 1.0, 0.0)


def _moe_gather_kernel(lo_s, npc_s, goff_s, u_ref, c_ref, lov_ref, tri_ref, init_ref, xs_ref, slot_ref,
                       slab_ref, sem):
    del init_ref
    i = pl.program_id(0)
    routed = c_ref[...] > 0.0
    rank = _dot(tri_ref[...], jnp.where(routed, 1.0, 0.0).astype(BF16))
    slot = jnp.where(routed, lov_ref[0] + rank, -1.0)
    slot_ref[...] = slot
    _, s_hi, s_lo = _slot_pair(slot)
    pt = jnp.maximum(_one_hot_rows(s_hi), _one_hot_rows(s_lo)).astype(BF16)
    slab_ref[...] = _dot_tn(pt, u_ref[...]).astype(BF16)
    _moe_pieces(i, lo_s, npc_s, goff_s, lambda local, glob: pltpu.make_async_copy(
        slab_ref.at[pl.ds(local, MOE_PIECE), :], xs_ref.at[pl.ds(glob, MOE_PIECE), :], sem))


def _moe_gather(u, comb, tab):
    tri = jnp.asarray(np.tril(np.ones((MOE_TM, MOE_TM), np.float32), -1), BF16)
    init = jnp.zeros((MOE_ROWS, D_MODEL), BF16)
    row = lambda i, *_: (i, 0)
    grid_spec = pltpu.PrefetchScalarGridSpec(
        num_scalar_prefetch=3,
        grid=(MOE_NT,),
        in_specs=[pl.BlockSpec((MOE_TM, D_MODEL), row), pl.BlockSpec((MOE_TM, 128), row),
                  pl.BlockSpec((1, 1, 128), lambda i, *_: (i, 0, 0)),
                  pl.BlockSpec((MOE_TM, MOE_TM), lambda i, *_: (0, 0)),
                  pl.BlockSpec(memory_space=pl.ANY)],
        out_specs=[pl.BlockSpec(memory_space=pl.ANY), pl.BlockSpec((MOE_TM, 128), row)],
        scratch_shapes=[pltpu.VMEM((MOE_SLAB, D_MODEL), BF16), pltpu.SemaphoreType.DMA(())],
    )
    return pl.pallas_call(
        _moe_gather_kernel,
        grid_spec=grid_spec,
        out_shape=[jax.ShapeDtypeStruct((MOE_ROWS, D_MODEL), BF16), jax.ShapeDtypeStruct((N_TOK, 128), F32)],
        input_output_aliases={7: 0},
        compiler_params=_cparams(("arbitrary",)),
        name="moe_gather",
    )(tab['lo'], tab['npc'], tab['goff'], u, comb, tab['lo_vec'], tri, init)


def _moe_expert_kernel(be_s, nact_s, x_ref, wg_ref, wu_ref, wd_ref, y_ref):
    del be_s
    active = pl.program_id(0) < nact_s[0]

    @pl.when(jnp.logical_not(active))
    def _():
        y_ref[...] = jnp.zeros_like(y_ref)

    @pl.when(active)
    def _():
        x = x_ref[...]
        acc = None
        for c in range(EXPERT_DIM // MOE_FCHUNK):
            sl = slice(c * MOE_FCHUNK, (c + 1) * MOE_FCHUNK)
            y = _swiglu_partial(x, wg_ref[0, :, sl], wu_ref[0, :, sl], wd_ref[0, sl, :])
            acc = y if acc is None else acc + y
        y_ref[...] = acc.astype(BF16)


def _moe_expert(xs, tab, wg, wu, wd):
    blk = lambda s, be, nact: (jnp.minimum(s, nact[0] - 1), 0)
    wspec = lambda shape: pl.BlockSpec(shape, lambda s, be, nact: (be[s], 0, 0), pipeline_mode=pl.Buffered(1))
    grid_spec = pltpu.PrefetchScalarGridSpec(
        num_scalar_prefetch=2,
        grid=(MOE_NB,),
        in_specs=[pl.BlockSpec((MOE_BLK, D_MODEL), blk),
                  wspec((1, D_MODEL, EXPERT_DIM)), wspec((1, D_MODEL, EXPERT_DIM)), wspec((1, EXPERT_DIM, D_MODEL))],
        out_specs=pl.BlockSpec((MOE_BLK, D_MODEL), lambda s, be, nact: (s, 0)),
    )
    return pl.pallas_call(
        _moe_expert_kernel,
        grid_spec=grid_spec,
        out_shape=jax.ShapeDtypeStruct((MOE_ROWS, D_MODEL), BF16),
        compiler_params=_cparams(("arbitrary",)),
        name="moe_expert",
    )(tab['blk_e'], tab['nact'], xs, wg, wu, wd)


def _moe_combine_kernel(lo_s, npc_s, goff_s, nrow_s, y_ref, slot_ref, c_ref, x_ref, m_ref, g_ref, o_ref,
                        slab_ref, sem):
    i = pl.program_id(0)
    _moe_pieces(i, lo_s, npc_s, goff_s, lambda local, glob: pltpu.make_async_copy(
        y_ref.at[pl.ds(glob, MOE_PIECE), :], slab_ref.at[pl.ds(local, MOE_PIECE), :], sem))
    slot = slot_ref[...]
    comb = c_ref[...]
    sel, s_hi, s_lo = _slot_pair(slot)
    w_hi = jnp.sum(jnp.where(sel & (slot == s_hi), comb, 0.0), axis=-1, keepdims=True)
    w_lo = jnp.sum(jnp.where(sel & (slot == s_lo), comb, 0.0), axis=-1, keepdims=True)
    w_lo = jnp.where(s_lo == s_hi, 0.0, w_lo)
    srow = lax.broadcasted_iota(jnp.int32, (MOE_SLAB, 1), 0)
    ys = jnp.where(srow < nrow_s[i], slab_ref[...], jnp.zeros((), BF16))
    out = (w_hi * _dot(_one_hot_rows(s_hi).astype(BF16), ys) + w_lo * _dot(_one_hot_rows(s_lo).astype(BF16), ys))
    o_ref[...] = _rms(x_ref[...] + m_ref[0, 5] * out, g_ref[...])


def _moe_combine(y, slot, comb, x, mod_l, final_g, tab):
    row = lambda i, *_: (i, 0)
    grid_spec = pltpu.PrefetchScalarGridSpec(
        num_scalar_prefetch=4,
        grid=(MOE_NT,),
        in_specs=[pl.BlockSpec(memory_space=pl.ANY),
                  pl.BlockSpec((MOE_TM, 128), row), pl.BlockSpec((MOE_TM, 128), row),
                  pl.BlockSpec((MOE_TM, D_MODEL), row),
                  pl.BlockSpec((1, 6, 1, D_MODEL), lambda i, *_: (_mod_row(i, MOE_TM), 0, 0, 0)),
                  pl.BlockSpec((1, D_MODEL), lambda i, *_: (0, 0))],
        out_specs=pl.BlockSpec((MOE_TM, D_MODEL), row),
        scratch_shapes=[pltpu.VMEM((MOE_SLAB, D_MODEL), BF16), pltpu.SemaphoreType.DMA(())],
    )
    return pl.pallas_call(
        _moe_combine_kernel,
        grid_spec=grid_spec,
        out_shape=jax.ShapeDtypeStruct((N_TOK, D_MODEL), F32),
        compiler_params=_cparams(("arbitrary",)),
        name="moe_combine",
    )(tab['lo'], tab['npc'], tab['goff'], tab['nrow'], y, slot, comb, x, mod_l, final_g)


def _moe(u, comb, x, mod_l, final_g, wg, wu, wd):
    tab = _moe_tables(_moe_count(comb))
    xs, slot = _moe_gather(u, comb, tab)
    y = _moe_expert(xs, tab, wg, wu, wd)
    return _moe_combine(y, slot, comb, x, mod_l, final_g, tab)


def _block_diag(w):
    w = w.reshape(2, C_WIDTH // 128, 2, C_BLOCK, C_BLOCK)
    z = jnp.zeros_like(w[:, :, 0])
    top = jnp.concatenate([w[:, :, 0], z], axis=-1)
    bot = jnp.concatenate([z, w[:, :, 1]], axis=-1)
    return jnp.concatenate([top, bot], axis=-2).astype(BF16)


def kernel(x_prompt, x_sample, cache_attn_k, cache_attn_v, state_hgrn, state_lru, c, c_ctx, w_mod, b_mod, norm_mix_g, norm_ffn_g, w_in, attn_sink, hgrn_lower, hgrn_norm_g, lru_conv_w, lru_conv_b, lru_wa, lru_ba, lru_wx, lru_bx, lru_lambda, w_branch_a, w_branch_b, w_branch_c, w_out, ffn_w_gate, ffn_w_up, ffn_w_down, moe_router, moe_w_gate, moe_w_up, moe_w_down, final_norm_g):
    assert DEPTH == 2
    x = jnp.concatenate([x_prompt.reshape(N_CTX_TOK, D_MODEL), x_sample.reshape(N_LAT_TOK, D_MODEL)], axis=0)
    cond = jnp.concatenate([c_ctx[None, :], c, jnp.zeros((N_COND - 1 - DEC_BATCH, D_MODEL), F32)], axis=0)
    mods = _modulation(cond, w_mod, b_mod).reshape(DEPTH, N_COND, 6, 1, D_MODEL)

    lb_cum = jnp.cumsum(jax.nn.softmax(hgrn_lower.astype(F32), axis=0), axis=0)
    lb_layers = lb_cum - lb_cum[:1]
    cos, sin = _rope_tables()
    hconst = _hgrn_constants(False) + _hgrn_constants(True)
    split = IN_COLS - 3 * D_MODEL

    ks, vs, hs, ls = [], [], [], []
    for l in range(DEPTH):
        w_in_l = jnp.concatenate([w_in[l][:, split:], w_in[l][:, :split]], axis=1).astype(BF16)
        lb = lb_layers[l]
        lbp = jnp.stack([jnp.log(lb), jnp.log1p(-lb), 1.0 - lb]).reshape(3, 2, 1, B_WIDTH)
        lp = dict(
            conv_w=lru_conv_w[l], conv_b=lru_conv_b[l].reshape(1, C_WIDTH),
            wa=_block_diag(lru_wa[l]), wx=_block_diag(lru_wx[l]),
            ba=lru_ba[l].reshape(2, 1, C_WIDTH), bx=lru_bx[l].reshape(2, 1, C_WIDTH),
            lam=lru_lambda[l].reshape(2, 1, C_WIDTH),
            w_ba=w_branch_a[l].astype(BF16), w_bb=w_branch_b[l].astype(BF16), w_bc=w_branch_c[l].astype(BF16),
            w_out=w_out[l].astype(BF16), norm_ffn=norm_ffn_g[l].reshape(1, D_MODEL))
        mod_l = mods[l]

        proj = _in_proj(x, mod_l, norm_mix_g[l].reshape(1, D_MODEL), w_in_l)

        attn_c = _ctx_attention(proj, attn_sink[l])
        attn_l = _lat_attention(proj, attn_sink[l], cache_attn_k[:, l].reshape(DEC_BATCH, PAST_LEN, 128),
                                cache_attn_v[:, l].reshape(DEC_BATCH, PAST_LEN, 128), cos, sin)

        ng = hgrn_norm_g[l].reshape(1, B_KEY_DIM)
        hg_c, s_h = _hgrn(proj, lbp, ng, hconst, nbatch=BATCH, seq=SEQ, row_base=0, want_state=True)
        hg_l, = _hgrn(proj, lbp, ng, hconst, nbatch=DEC_BATCH, seq=DEC_SEQ, row_base=N_CTX_TOK,
                      state_in=state_hgrn[:, l])

        lr_c, s_l = _lru(proj, lp, nbatch=BATCH, seq=SEQ, row_base=0, reset=True, want_state=True)
        lr_l, = _lru(proj, lp, nbatch=DEC_BATCH, seq=DEC_SEQ, row_base=N_CTX_TOK, reset=False,
                     state_in=state_lru[:, l])
        mixers = (attn_c, attn_l, hg_c, hg_l, lr_c, lr_l)

        ks.append(proj[:N_CTX_TOK, KA_OFF:KA_OFF + 128].reshape(BATCH, SEQ, A_KV_HEADS, A_HEAD_DIM))
        vs.append(proj[:N_CTX_TOK, VA_OFF:VA_OFF + 128].reshape(BATCH, SEQ, A_KV_HEADS, A_HEAD_DIM))
        hs.append(s_h)
        ls.append(s_l)

        m = l // 2
        if l % 2 == 0:
            x, u = _merge(x, mod_l, proj, mixers, lp)
            x = _ffn(u, x, mod_l, ffn_w_gate[m].astype(BF16), ffn_w_up[m].astype(BF16), ffn_w_down[m].astype(BF16))
        else:
            router = jnp.pad(moe_router[m], ((0, 0), (0, 128 - N_EXPERTS)))
            r_hi = router.astype(BF16)
            router = jnp.stack([r_hi, (router - r_hi.astype(F32)).astype(BF16)])
            x, u, comb = _merge(x, mod_l, proj, mixers, lp, router=router)
            x = _moe(u, comb, x, mod_l, final_norm_g.reshape(1, D_MODEL), moe_w_gate[m].astype(BF16),
                     moe_w_up[m].astype(BF16), moe_w_down[m].astype(BF16))

    y_prompt = x[:N_CTX_TOK].reshape(BATCH, SEQ, D_MODEL)
    y_sample = x[N_CTX_TOK:].reshape(DEC_BATCH, DEC_SEQ, D_MODEL)
    return (y_prompt, y_sample, jnp.stack(ks, axis=1), jnp.stack(vs, axis=1), jnp.stack(hs, axis=1),
            jnp.stack(ls, axis=1))
```

```python
import functools

import numpy as np
import jax
import jax.numpy as jnp
from jax import lax
from jax.experimental import pallas as pl
from jax.experimental.pallas import tpu as pltpu

F32 = jnp.float32
BF16 = jnp.bfloat16
HIGHEST = lax.Precision.HIGHEST

D_MODEL = 1024
BATCH = 32
SEQ = 256
DEPTH = 2
DEC_BATCH = 4
DEC_SEQ = 2048
PAST_LEN = 512
GRID_W = 64
RMS_EPS = 1e-6
A_HEADS = 8
A_KV_HEADS = 2
A_GROUP = 4
A_HEAD_DIM = 64
A_WIDTH = 512
A_WINDOW = 128
ROPE_THETA = 10000.0
NEG_INF = -1e30
B_HEADS = 4
B_KEY_DIM = 128
B_WIDTH = 512
C_WIDTH = 512
C_HEADS = 8
C_BLOCK = 64
C_EXP = 8.0
FFN_DIM = 2816
N_EXPERTS = 8
EXPERT_DIM = 3584

N_CTX_TOK = BATCH * SEQ
N_LAT_TOK = DEC_BATCH * DEC_SEQ
N_TOK = N_CTX_TOK + N_LAT_TOK
N_COND = 8

ZG_OFF = 0
QA_OFF = 3072
KA_OFF = 3584
VA_OFF = 3712
B_OFF = 3840
C_OFF = 6400
IN_COLS = 7424

V7X_VMEM_LIMIT = 56 * 1024 * 1024
HG_CHUNK = 128
HG_LEVELS = (64, 32, 16, 8, 4, 2, 1)
HG_MAT_LEVELS = (4, 2)
HG_CHAINS = 8
LRU_SEGS = 8
LRU_PITCH_PAD = 4


def _cparams(sem, vmem=V7X_VMEM_LIMIT):
    return pltpu.CompilerParams(dimension_semantics=sem, vmem_limit_bytes=vmem)


def _mod_row(i, tm):
    nct = N_CTX_TOK // tm
    return jnp.where(i < nct, 0, 1 + (i - nct) // (DEC_SEQ // tm))


def _pair_specs(tm, cols=D_MODEL):
    nct = N_CTX_TOK // tm
    return [pl.BlockSpec((tm, cols), lambda i, *_: (jnp.minimum(i, nct - 1), 0)),
            pl.BlockSpec((tm, cols), lambda i, *_: (jnp.maximum(i - nct, 0), 0))]


def _pair_shapes(cols, dtype):
    return [jax.ShapeDtypeStruct((N_CTX_TOK, cols), dtype), jax.ShapeDtypeStruct((N_LAT_TOK, cols), dtype)]


def _pair_read(c_ref, l_ref, tm):
    return jnp.where(pl.program_id(0) < N_CTX_TOK // tm, c_ref[...], l_ref[...])


def _pair_write(c_ref, l_ref, tm, value):
    is_ctx = pl.program_id(0) < N_CTX_TOK // tm

    @pl.when(is_ctx)
    def _():
        c_ref[...] = value

    @pl.when(jnp.logical_not(is_ctx))
    def _():
        l_ref[...] = value


def _sigmoid(x):
    return 0.5 * jnp.tanh(0.5 * x) + 0.5


def _silu(x):
    return x * _sigmoid(x)


def _log_sigmoid(z):
    return jnp.minimum(z, 0.0) - jnp.log1p(jnp.exp(-jnp.abs(z)))


def _rms(x, g):
    ms = jnp.mean(x * x, axis=-1, keepdims=True)
    return x * lax.rsqrt(ms + RMS_EPS) * g


def _dot(a, b):
    return jnp.dot(a, b, preferred_element_type=F32)


def _dot_nt(a, b):
    return lax.dot_general(a, b, (((1,), (1,)), ((), ())), preferred_element_type=F32)


def _dot_tn(a, b):
    return lax.dot_general(a, b, (((0,), (0,)), ((), ())), preferred_element_type=F32)


def _mod_kernel(c_ref, w_ref, b_ref, o_ref):
    a = _silu(c_ref[...])
    o_ref[0] = jnp.dot(a, w_ref[0], preferred_element_type=F32, precision=HIGHEST) + b_ref[0]


def _modulation(cond, w_mod, b_mod):
    tn = 1536
    return pl.pallas_call(
        _mod_kernel,
        grid=(DEPTH, 6 * D_MODEL // tn),
        in_specs=[pl.BlockSpec((N_COND, D_MODEL), lambda l, j: (0, 0)),
                  pl.BlockSpec((1, D_MODEL, tn), lambda l, j: (l, 0, j)),
                  pl.BlockSpec((1, 1, tn), lambda l, j: (l, 0, j))],
        out_specs=pl.BlockSpec((1, N_COND, tn), lambda l, j: (l, 0, j)),
        out_shape=jax.ShapeDtypeStruct((DEPTH, N_COND, 6 * D_MODEL), F32),
        compiler_params=_cparams(("arbitrary", "arbitrary")),
        name="modulation",
    )(cond, w_mod, b_mod.reshape(DEPTH, 1, 6 * D_MODEL))


IN_TM = 256
IN_TN = 256


def _in_proj_kernel(xc_ref, xl_ref, m_ref, g_ref, w_ref, o_ref):
    u = _rms(_pair_read(xc_ref, xl_ref, IN_TM), g_ref[...]) * (1.0 + m_ref[0, 1]) + m_ref[0, 0]
    ub = u.astype(BF16)
    n = IN_COLS // IN_TN
    first_gate_chunk = (IN_COLS - 3 * D_MODEL) // IN_TN
    for j in range(n):
        k = (j - first_gate_chunk) % n
        o_ref[:, k * IN_TN:(k + 1) * IN_TN] = _dot(ub, w_ref[:, j * IN_TN:(j + 1) * IN_TN])


def _in_proj(x_pair, mod_l, g, w):
    tm = IN_TM
    return pl.pallas_call(
        _in_proj_kernel,
        grid=(N_TOK // tm,),
        in_specs=_pair_specs(tm) + [
            pl.BlockSpec((1, 6, 1, D_MODEL), lambda i: (_mod_row(i, tm), 0, 0, 0)),
            pl.BlockSpec((1, D_MODEL), lambda i: (0, 0)),
            pl.BlockSpec((D_MODEL, IN_COLS), lambda i: (0, 0), pipeline_mode=pl.Buffered(1))],
        out_specs=pl.BlockSpec((tm, IN_COLS), lambda i: (i, 0)),
        out_shape=jax.ShapeDtypeStruct((N_TOK, IN_COLS), F32),
        compiler_params=_cparams(("arbitrary",)),
        name="in_proj",
    )(*x_pair, mod_l, g, w)


def _softmax_pv(s_list, v_list, sink):
    m = sink
    for s in s_list:
        m = jnp.maximum(m, jnp.max(s, axis=-1, keepdims=True))
    den = jnp.exp(sink - m)
    out = None
    for s, v in zip(s_list, v_list):
        p = jnp.exp(s - m)
        den = den + jnp.sum(p, axis=-1, keepdims=True)
        o = _dot(p.astype(BF16), v)
        out = o if out is None else out + o
    return out / den


def _sink_column(sink_ref, h, rows_per_head):
    n = A_GROUP * rows_per_head
    row = lax.broadcasted_iota(jnp.int32, (n, 1), 0)
    col = jnp.full((n, 1), sink_ref[A_GROUP * h + A_GROUP - 1], F32)
    for g in range(A_GROUP - 2, -1, -1):
        col = jnp.where(row < (g + 1) * rows_per_head, sink_ref[A_GROUP * h + g], col)
    return col


def _ctx_attn_kernel(sink_ref, q_ref, k_ref, v_ref, o_ref):
    q = q_ref[...]
    k = k_ref[...]
    v = v_ref[...]
    scale = A_HEAD_DIM ** -0.5
    outs = []
    for h in range(A_KV_HEADS):
        hs = slice(h * A_HEAD_DIM, (h + 1) * A_HEAD_DIM)
        kh = k[:, hs].astype(BF16)
        vh = v[:, hs].astype(BF16)
        qs = jnp.concatenate(
            [q[:, (A_GROUP * h + g) * A_HEAD_DIM:(A_GROUP * h + g + 1) * A_HEAD_DIM] for g in range(A_GROUP)],
            axis=0).astype(BF16)
        s = _dot_nt(qs, kh) * scale
        o = _softmax_pv([s], [vh], _sink_column(sink_ref, h, SEQ))
        outs.extend(o[g * SEQ:(g + 1) * SEQ] for g in range(A_GROUP))
    o_ref[...] = jnp.concatenate(outs, axis=-1).astype(BF16)


def _ctx_attention(proj, sink):
    return pl.pallas_call(
        _ctx_attn_kernel,
        grid=(BATCH,),
        in_specs=[pl.BlockSpec(memory_space=pltpu.SMEM),
                  pl.BlockSpec((SEQ, A_WIDTH), lambda b: (b, QA_OFF // A_WIDTH)),
                  pl.BlockSpec((SEQ, 128), lambda b: (b, KA_OFF // 128)),
                  pl.BlockSpec((SEQ, 128), lambda b: (b, VA_OFF // 128))],
        out_specs=pl.BlockSpec((SEQ, A_WIDTH), lambda b: (b, 0)),
        out_shape=jax.ShapeDtypeStruct((N_CTX_TOK, A_WIDTH), BF16),
        compiler_params=_cparams(("arbitrary",)),
        name="ctx_attention",
    )(sink, proj, proj, proj)


def _rope_tables():
    half = A_HEAD_DIM // 4
    inv = ROPE_THETA ** (-np.arange(half, dtype=np.float64) / half)
    t = np.arange(DEC_SEQ)
    row = (t // GRID_W)[:, None] * inv[None, :]
    col = (t % GRID_W)[:, None] * inv[None, :]
    cos = np.concatenate([np.cos(row), np.cos(row), np.cos(col), np.cos(col)], axis=1)
    sin = np.concatenate([-np.sin(row), np.sin(row), -np.sin(col), np.sin(col)], axis=1)
    return (jnp.asarray(np.tile(cos, (1, 2)), F32), jnp.asarray(np.tile(sin, (1, 2)), F32))


def _rope(x, cos, sin, first):
    part = jnp.where(first, pltpu.roll(x, 128 - 16, 1), pltpu.roll(x, 16, 1))
    return x * cos + part * sin


def _lat_attn_kernel(sink_ref, q_ref, km_ref, k0_ref, kp_ref, vm_ref, v0_ref, vp_ref, kc_ref, vc_ref,
                     cm_ref, c0_ref, cp_ref, sm_ref, s0_ref, sp_ref, o_ref):
    j = pl.program_id(1)
    blk = A_WINDOW
    lane = lax.broadcasted_iota(jnp.int32, (blk, 128), 1)
    first = (lane % 32) < 16
    c0 = c0_ref[...]
    s0 = s0_ref[...]
    q = q_ref[...]
    qr = [_rope(q[:, c * 128:(c + 1) * 128], c0, s0, first) for c in range(A_WIDTH // 128)]
    kw = jnp.concatenate([_rope(km_ref[...], cm_ref[...], sm_ref[...], first),
                          _rope(k0_ref[...], c0, s0, first),
                          _rope(kp_ref[...], cp_ref[...], sp_ref[...], first)], axis=0)
    vw = jnp.concatenate([vm_ref[...], v0_ref[...], vp_ref[...]], axis=0)
    kc = kc_ref[0]
    vc = vc_ref[0]
    nq = A_GROUP * blk
    t = lax.broadcasted_iota(jnp.int32, (nq, 3 * blk), 0) % blk
    s = lax.broadcasted_iota(jnp.int32, (nq, 3 * blk), 1) - blk
    kpos = j * blk + s
    mask = (jnp.abs(s - t) <= A_WINDOW) & (kpos >= 0) & (kpos < DEC_SEQ)
    scale = A_HEAD_DIM ** -0.5
    outs = []
    for h in range(A_KV_HEADS):
        hs = slice(h * A_HEAD_DIM, (h + 1) * A_HEAD_DIM)
        heads = [A_GROUP * h + g for g in range(A_GROUP)]
        qs = jnp.concatenate(
            [qr[hd // 2][:, (hd % 2) * A_HEAD_DIM:(hd % 2 + 1) * A_HEAD_DIM] for hd in heads], axis=0).astype(BF16)
        sw = jnp.where(mask, _dot_nt(qs, kw[:, hs].astype(BF16)) * scale, NEG_INF)
        sc = _dot_nt(qs, kc[:, hs].astype(BF16)) * scale
        o = _softmax_pv([sw, sc], [vw[:, hs].astype(BF16), vc[:, hs].astype(BF16)],
                        _sink_column(sink_ref, h, blk))
        outs.extend(o[g * blk:(g + 1) * blk] for g in range(A_GROUP))
    o_ref[...] = jnp.concatenate(outs, axis=-1).astype(BF16)


def _lat_attention(proj, sink, k_ctx, v_ctx, cos, sin):
    blk = A_WINDOW
    nb = DEC_SEQ // blk
    base = N_CTX_TOK // blk

    def rows(off):
        return lambda b, j: (base + b * nb + jnp.clip(j + off, 0, nb - 1))

    def kv_spec(col, off):
        r = rows(off)
        return pl.BlockSpec((blk, 128), lambda b, j: (r(b, j), col))

    def tab_spec(off):
        return pl.BlockSpec((blk, 128), lambda b, j: (jnp.clip(j + off, 0, nb - 1), 0))

    r0 = rows(0)
    ctx_spec = pl.BlockSpec((1, PAST_LEN, 128), lambda b, j: (b, 0, 0))
    return pl.pallas_call(
        _lat_attn_kernel,
        grid=(DEC_BATCH, nb),
        in_specs=[pl.BlockSpec(memory_space=pltpu.SMEM),
                  pl.BlockSpec((blk, A_WIDTH), lambda b, j: (r0(b, j), QA_OFF // A_WIDTH)),
                  kv_spec(KA_OFF // 128, -1), kv_spec(KA_OFF // 128, 0), kv_spec(KA_OFF // 128, 1),
                  kv_spec(VA_OFF // 128, -1), kv_spec(VA_OFF // 128, 0), kv_spec(VA_OFF // 128, 1),
                  ctx_spec, ctx_spec,
                  tab_spec(-1), tab_spec(0), tab_spec(1), tab_spec(-1), tab_spec(0), tab_spec(1)],
        out_specs=pl.BlockSpec((blk, A_WIDTH), lambda b, j: (b * nb + j, 0)),
        out_shape=jax.ShapeDtypeStruct((N_LAT_TOK, A_WIDTH), BF16),
        compiler_params=_cparams(("arbitrary", "arbitrary")),
        name="lat_attention",
    )(sink, proj, proj, proj, proj, proj, proj, proj, k_ctx, v_ctx, cos, cos, cos, sin, sin, sin)


def _hgrn_constants(reverse):
    L = HG_CHUNK
    t = np.arange(L)[:, None]
    s = np.arange(L)[None, :]
    tri = (s <= t).astype(np.float32)
    blocks = [tri]
    masks = []
    for b in HG_LEVELS:
        anchor = (t // (2 * b)) * 2 * b + b - 1
        right = np.where(t % (2 * b) >= b, 1.0, -1.0)
        if b in HG_MAT_LEVELS:
            blocks.append(right * (tri - (s <= anchor).astype(np.float32)))
        masks.append(((t // (2 * b) == s // (2 * b)) & (t % (2 * b) >= b) & (s % (2 * b) < b)).astype(np.float32))
    masks.append((t == s).astype(np.float32))
    if reverse:
        blocks = [m[::-1, ::-1] for m in blocks]
        masks = [m[::-1, ::-1] for m in masks]
    mat = np.concatenate(blocks, axis=0)
    return (jnp.asarray(np.concatenate([mat, mat], axis=1), BF16), jnp.asarray(np.stack(masks), F32))


LOG2E = 1.4426950408889634


def _hgrn_chunk(qraw, vv, z, lbp, st, mat, mask_ref, reverse):
    L = HG_CHUNK
    log_lb, log_1mlb, one_mlb = lbp
    q = qraw * _sigmoid(qraw) * (B_KEY_DIM ** -0.5)
    e = jnp.exp(-jnp.abs(z))
    r = 1.0 / (1.0 + e)
    k = one_mlb * jnp.where(z >= 0.0, e * r, r)
    a = log_lb
    b = log_1mlb + (jnp.minimum(z, 0.0) + jnp.log(r))
    lf = (jnp.maximum(a, b) + jnp.log(1.0 + jnp.exp(-jnp.abs(a - b)))) * LOG2E
    hi = lf.astype(BF16)
    lo = (lf - hi.astype(F32)).astype(BF16)
    g = _dot(mat, jnp.concatenate([hi, lo], axis=0))
    cum = g[0:L]
    tot = cum[0:1] if reverse else cum[L - 1:L]
    vb = vv.astype(BF16)
    o = _dot_nt((q * jnp.exp2(cum)).astype(BF16), st.astype(BF16))
    kd = (k * jnp.exp2(tot - cum)).astype(BF16)
    st_new = st * jnp.exp2(tot) + _dot_tn(vb, kd)
    nl = len(HG_LEVELS)
    row = lax.broadcasted_iota(jnp.int32, (L, 1), 0)
    if reverse:
        row = L - 1 - row
    sc = _dot_nt(q.astype(BF16), k.astype(BF16)) * mask_ref[nl]
    for li, bsz in enumerate(HG_LEVELS):
        is_query = (row & (2 * bsz - 1)) >= bsz
        if bsz in HG_MAT_LEVELS:
            j = 1 + HG_MAT_LEVELS.index(bsz)
            d = g[j * L:(j + 1) * L]
        elif bsz == 1:
            d = jnp.where(is_query, lf, 0.0)
        else:
            parts = []
            for base in range(0, L, 2 * bsz):
                lo_half = cum[base:base + bsz]
                hi_half = cum[base + bsz:base + 2 * bsz]
                if reverse:
                    anc = cum[base + bsz:base + bsz + 1]
                    parts += [lo_half - anc, anc - hi_half]
                else:
                    anc = cum[base + bsz - 1:base + bsz]
                    parts += [anc - lo_half, hi_half - anc]
            d = jnp.concatenate(parts, axis=0)
        zb = (jnp.where(is_query, q, k) * jnp.exp2(d)).astype(BF16)
        sc = sc + _dot_nt(zb, zb) * mask_ref[li]
    o = o + _dot(sc.astype(BF16), vb)
    return o, st_new


def _hgrn(proj, lbp, norm_g, consts, *, nbatch, seq, row_base, state_in=None, want_state=False):
    mat_f, mask_f, mat_b, mask_b = consts
    nseq = max(1, HG_CHAINS * HG_CHUNK // (2 * seq))
    rows = nseq * seq
    rb = row_base // rows

    def col(k):
        return pl.BlockSpec((rows, 128), lambda b, h: (rb + b, (B_OFF + k * B_WIDTH) // 128 + h))

    const2 = lambda shape: pl.BlockSpec(shape, lambda b, h: (0,) * len(shape))
    in_specs = [col(0), col(1), col(2), col(3), col(4),
                pl.BlockSpec((3, 2, 1, 128), lambda b, h: (0, 0, 0, h)),
                const2((1, 128)),
                const2(mat_f.shape), const2(mat_b.shape), const2(mask_f.shape), const2(mask_b.shape)]
    args = [proj, proj, proj, proj, proj, lbp, norm_g, mat_f, mat_b, mask_f, mask_b]
    state_spec = pl.BlockSpec((nseq, 2, 1, 128, 128), lambda b, h: (b, 0, h, 0, 0))
    if state_in is not None:
        in_specs.append(state_spec)
        args.append(state_in)
    out_specs = [pl.BlockSpec((rows, 128), lambda b, h: (b, h))]
    out_shape = [jax.ShapeDtypeStruct((nbatch * seq, B_WIDTH), BF16)]
    if want_state:
        out_specs.append(state_spec)
        out_shape.append(jax.ShapeDtypeStruct((nbatch, 2, B_HEADS, 128, 128), F32))
    kern = functools.partial(_hgrn_kernel_flat, seq=seq, nseq=nseq, has_state_in=state_in is not None,
                             has_state_out=want_state)
    return pl.pallas_call(
        kern,
        grid=(nbatch // nseq, B_HEADS),
        in_specs=in_specs,
        out_specs=out_specs,
        out_shape=out_shape,
        scratch_shapes=[pltpu.VMEM((rows, 128), F32)],
        compiler_params=_cparams(("arbitrary", "arbitrary")),
        name="hgrn_%d" % seq,
    )(*args)


def _hgrn_kernel_flat(*refs, seq, nseq, has_state_in, has_state_out):
    refs = list(refs)
    head = refs[:11]
    pos = 11
    s0_ref = None
    if has_state_in:
        s0_ref = refs[pos]
        pos += 1
    o_ref = refs[pos]
    pos += 1
    s_ref = None
    if has_state_out:
        s_ref = refs[pos]
        pos += 1
    acc_ref = refs[pos]
    _hgrn_body(*head, s0_ref, o_ref, s_ref, acc_ref, seq=seq, nseq=nseq)


def _hgrn_body(q_ref, i_ref, zf_ref, zb_ref, gt_ref, lb_ref, ng_ref, mf_ref, mb_ref, kf_ref, kb_ref,
               s0_ref, o_ref, s_ref, acc_ref, *, seq, nseq):
    L = HG_CHUNK
    n = seq // L
    acc_ref[...] = jnp.zeros_like(acc_ref)
    lbf = (lb_ref[0, 0], lb_ref[1, 0], lb_ref[2, 0])
    lbb = (lb_ref[0, 1], lb_ref[1, 1], lb_ref[2, 1])
    states = []
    for j in range(nseq):
        for d in range(2):
            if s0_ref is not None:
                states.append(s0_ref[j, d, 0].T)
            else:
                states.append(jnp.zeros((B_KEY_DIM, B_KEY_DIM), F32))

    def body(i, carry):
        carry = list(carry)
        for j in range(nseq):
            sf = pl.ds(pl.multiple_of(j * seq + i * L, L), L)
            sb = pl.ds(pl.multiple_of(j * seq + (n - 1 - i) * L, L), L)
            o_f, carry[2 * j] = _hgrn_chunk(q_ref[sf, :], i_ref[sf, :], zf_ref[sf, :], lbf, carry[2 * j],
                                            mf_ref[...], kf_ref, False)
            acc_ref[sf, :] += o_f
            o_b, carry[2 * j + 1] = _hgrn_chunk(q_ref[sb, :], i_ref[sb, :], zb_ref[sb, :], lbb, carry[2 * j + 1],
                                                mb_ref[...], kb_ref, True)
            acc_ref[sb, :] += o_b
        return tuple(carry)

    states = lax.fori_loop(0, n, body, tuple(states), unroll=HG_CHAINS // (2 * nseq))
    o_ref[...] = (_rms(acc_ref[...], ng_ref[...]) * _silu(gt_ref[...])).astype(BF16)
    if s_ref is not None:
        for j in range(nseq):
            for d in range(2):
                s_ref[j, d, 0] = states[2 * j + d].T


def _gelu_tanh(y):
    return 0.5 * y * (1.0 + jnp.tanh(0.7978845608028654 * (y + 0.044715 * (y * y * y))))


def _lru_body(xc_ref, yc_ref, cw_ref, cb_ref, wa_ref, ba_ref, wx_ref, bx_ref, lam_ref, h0_ref, o_ref, s_ref,
              xs_ref, a_ref, u_ref, *, rows, reset, independent):
    T = rows
    seg = T // LRU_SEGS
    row = lax.broadcasted_iota(jnp.int32, (T, 1), 0)
    seq_len = seg if independent else T
    pos = row % seg if independent else row
    pad = 8
    zeros = jnp.zeros((pad, 128), F32)
    xs_ref[0:pad, :] = zeros
    xs_ref[pad:pad + T, :] = xc_ref[...]
    xs_ref[pad + T:2 * pad + T, :] = zeros
    cw = cw_ref[...]
    x = cb_ref[...]
    for tap in range(4):
        off = tap - 2
        term = cw[tap:tap + 1] * xs_ref[pad + off:pad + off + T, :]
        if independent and off != 0:
            term = jnp.where((pos + off >= 0) & (pos + off < seq_len), term, 0.0)
        x = x + term
    xb = x.astype(BF16)
    pitch = seg + LRU_PITCH_PAD
    for d in range(2):
        rg = _sigmoid(_dot(xb, wa_ref[d, 0]) + ba_ref[d])
        ig = _sigmoid(_dot(xb, wx_ref[d, 0]) + bx_ref[d])
        log_a = (C_EXP * rg) * _log_sigmoid(lam_ref[d])
        a = jnp.exp(log_a)
        mult = jnp.sqrt(1.0 - a * a)
        if reset:
            mult = jnp.where(pos == (0 if d == 0 else seq_len - 1), 1.0, mult)
        u = mult * ig * x
        for r in range(LRU_SEGS):
            a_ref[d, r * pitch:r * pitch + seg, :] = a[r * seg:(r + 1) * seg]
            u_ref[d, r * pitch:r * pitch + seg, :] = u[r * seg:(r + 1) * seg]
    sub =lax.broadcasted_iota(jnp.int32, (LRU_SEGS, 1), 0)
    zrow = jnp.zeros((1, 128), F32)
    if h0_ref is not None:
        hf0 = jnp.where(sub == 0, h0_ref[0, 0:1, :], 0.0)
        hb0 = jnp.where(sub == LRU_SEGS - 1, h0_ref[0, 1:2, :], 0.0)
    else:
        hf0 = jnp.zeros((LRU_SEGS, 128), F32)
        hb0 = hf0
    ones = jnp.ones((LRU_SEGS, 128), F32)

    def local_scan(i, carry):
        hf, pf, hb, pb = carry
        sf = pl.ds(i, LRU_SEGS, stride=pitch)
        sb = pl.ds(seg - 1 - i, LRU_SEGS, stride=pitch)
        af = a_ref[0, sf, :]
        hf = af * hf + u_ref[0, sf, :]
        pf = af * pf
        u_ref[0, sf, :] = hf
        a_ref[0, sf, :] = pf
        ab = a_ref[1, sb, :]
        hb = ab * hb + u_ref[1, sb, :]
        pb = ab * pb
        u_ref[1, sb, :] = hb
        a_ref[1, sb, :] = pb
        return hf, pf, hb, pb

    hf, pf, hb, pb = lax.fori_loop(0, seg, local_scan, (hf0, ones, hb0, ones), unroll=4)
    if independent:
        for r in range(LRU_SEGS):
            sl = slice(r * seg, (r + 1) * seg)
            ps = slice(r * pitch, r * pitch + seg)
            o_ref[sl, :] = ((u_ref[0, ps, :] + u_ref[1, ps, :]) * _gelu_tanh(yc_ref[sl, :])).astype(BF16)
        if s_ref is not None:
            s_ref[:, 0, :] = hf
            s_ref[:, 1, :] = hb
        return
    carry_f = [zrow] * LRU_SEGS
    carry_b = [zrow] * LRU_SEGS
    end_f = hf[0:1]
    for r in range(1, LRU_SEGS):
        carry_f[r] = end_f
        end_f = hf[r:r + 1] + pf[r:r + 1] * end_f
    end_b = hb[LRU_SEGS - 1:LRU_SEGS]
    for r in range(LRU_SEGS - 2, -1, -1):
        carry_b[r] = end_b
        end_b = hb[r:r + 1] + pb[r:r + 1] * end_b
    for r in range(LRU_SEGS):
        sl = slice(r * seg, (r + 1) * seg)
        ps = slice(r * pitch, r * pitch + seg)
        h = (u_ref[0, ps, :] + a_ref[0, ps, :] * carry_f[r]) + (u_ref[1, ps, :] + a_ref[1, ps, :] * carry_b[r])
        o_ref[sl, :] = (h * _gelu_tanh(yc_ref[sl, :])).astype(BF16)
    if s_ref is not None:
        s_ref[0, 0:1, :] = end_f
        s_ref[0, 1:2, :] = end_b


def _lru_kernel_flat(*refs, rows, reset, independent, has_state_in, has_state_out):
    refs = list(refs)
    head = refs[:9]
    pos = 9
    h0_ref = None
    if has_state_in:
        h0_ref = refs[pos]
        pos += 1
    o_ref = refs[pos]
    pos += 1
    s_ref = None
    if has_state_out:
        s_ref = refs[pos]
        pos += 1
    _lru_body(*head, h0_ref, o_ref, s_ref, *refs[pos:], rows=rows, reset=reset, independent=independent)


def _lru(proj, lp, *, nbatch, seq, row_base, reset, state_in=None, want_state=False):
    independent = seq < LRU_SEGS * 128
    per_step = LRU_SEGS if independent else 1
    rows = seq * per_step
    nstep = nbatch // per_step
    rb = row_base // rows
    nblk = C_WIDTH // 128

    def col(k):
        return pl.BlockSpec((rows, 128), lambda b, c: (rb + b, (C_OFF + k * C_WIDTH) // 128 + c))

    vec = lambda rows: pl.BlockSpec((rows, 1, 128), lambda b, c: (0, 0, c))
    mat = pl.BlockSpec((2, 1, 128, 128), lambda b, c: (0, c, 0, 0))
    in_specs = [col(0), col(1),
                pl.BlockSpec((4, 128), lambda b, c: (0, c)), pl.BlockSpec((1, 128), lambda b, c: (0, c)),
                mat, vec(2), mat, vec(2), vec(2)]
    args = [proj, proj, lp['conv_w'], lp['conv_b'], lp['wa'], lp['ba'], lp['wx'], lp['bx'], lp['lam']]
    if state_in is not None:
        in_specs.append(pl.BlockSpec((1, 2, 128), lambda b, c: (b, 0, c)))
        args.append(state_in)
    out_specs = [pl.BlockSpec((rows, 128), lambda b, c: (b, c))]
    out_shape = [jax.ShapeDtypeStruct((nbatch * seq, C_WIDTH), BF16)]
    if want_state:
        out_specs.append(pl.BlockSpec((per_step, 2, 128), lambda b, c: (b, 0, c)))
        out_shape.append(jax.ShapeDtypeStruct((nbatch, 2, C_WIDTH), F32))
    kern = functools.partial(_lru_kernel_flat, rows=rows, reset=reset, independent=independent,
                             has_state_in=state_in is not None, has_state_out=want_state)
    scan_rows = rows + LRU_SEGS * LRU_PITCH_PAD
    return pl.pallas_call(
        kern,
        grid=(nstep, nblk),
        in_specs=in_specs,
        out_specs=out_specs,
        out_shape=out_shape,
        scratch_shapes=[pltpu.VMEM((rows + 16, 128), F32), pltpu.VMEM((2, scan_rows, 128), F32),
                        pltpu.VMEM((2, scan_rows, 128), F32)],
        compiler_params=_cparams(("arbitrary", "arbitrary")),
        name="lru_%d" % seq,
    )(*args)


MERGE_TM = 256


def _top2_combine(logits):
    lane = lax.broadcasted_iota(jnp.int32, logits.shape, 1).astype(F32)
    lg = jnp.where(lane < N_EXPERTS, logits, -jnp.inf)
    m1 = jnp.max(lg, axis=-1, keepdims=True)
    i1 = jnp.min(jnp.where(lg == m1, lane, 128.0), axis=-1, keepdims=True)
    lg2 = jnp.where(lane == i1, -jnp.inf, lg)
    m2 = jnp.max(lg2, axis=-1, keepdims=True)
    i2 = jnp.min(jnp.where(lg2 == m2, lane, 128.0), axis=-1, keepdims=True)
    e = jnp.exp(m2 - m1)
    w1 = 1.0 / (1.0 + e)
    return jnp.where(lane == i1, w1, 0.0) + jnp.where(lane == i2, e * w1, 0.0)


def _merge_body(xc_ref, xl_ref, m_ref, z0_ref, z1_ref, z2_ref, ac_ref, al_ref, hc_ref, hl_ref, lc_ref, ll_ref,
                wba_ref, wbb_ref, wbc_ref, wo_ref, g_ref, r_ref, xoc_ref, xol_ref, u_ref, c_ref):
    tm = MERGE_TM
    mg = (_sigmoid(z0_ref[...]) * _dot(_pair_read(ac_ref, al_ref, tm), wba_ref[...])
          + _sigmoid(z1_ref[...]) * _dot(_pair_read(hc_ref, hl_ref, tm), wbb_ref[...])
          + _sigmoid(z2_ref[...]) * _dot(_pair_read(lc_ref, ll_ref, tm), wbc_ref[...]))
    x = _pair_read(xc_ref, xl_ref, tm) + m_ref[0, 2] * _dot(mg.astype(BF16), wo_ref[...])
    _pair_write(xoc_ref, xol_ref, tm, x)
    u =_rms(x, g_ref[...]) * (1.0 + m_ref[0, 4]) + m_ref[0, 3]
    ub = u.astype(BF16)
    u_ref[...] = ub
    if r_ref is not None:
        ul = (u - ub.astype(F32)).astype(BF16)
        logits = _dot(ub, r_ref[0]) + (_dot(ub, r_ref[1]) + _dot(ul, r_ref[0]))
        c_ref[...] = _top2_combine(logits)


def _merge_kernel_dense(*refs):
    _merge_body(*refs[:17], None, *refs[17:], None)


def _merge(x_pair, mod_l, proj, mixers, lp, router=None):
    tm = MERGE_TM
    row = lambda i: (i, 0)
    const = lambda i: (0, 0)
    res = lambda shape: pl.BlockSpec(shape, const, pipeline_mode=pl.Buffered(1))
    in_specs = _pair_specs(tm) + [
        pl.BlockSpec((1, 6, 1, D_MODEL), lambda i: (_mod_row(i, tm), 0, 0, 0)),
        pl.BlockSpec((tm, D_MODEL), lambda i: (i, 0)),
        pl.BlockSpec((tm, D_MODEL), lambda i: (i, 1)),
        pl.BlockSpec((tm, D_MODEL), lambda i: (i, 2))]
    in_specs += _pair_specs(tm, 512) * 3
    in_specs += [res((A_WIDTH, D_MODEL)), res((B_WIDTH, D_MODEL)), res((C_WIDTH, D_MODEL)), res((D_MODEL, D_MODEL)),
                 pl.BlockSpec((1, D_MODEL), const)]
    args = [*x_pair, mod_l, proj, proj, proj, *mixers, lp['w_ba'], lp['w_bb'], lp['w_bc'], lp['w_out'],
            lp['norm_ffn']]
    out_specs = _pair_specs(tm) + [pl.BlockSpec((tm, D_MODEL), row)]
    out_shape = _pair_shapes(D_MODEL, F32) + [jax.ShapeDtypeStruct((N_TOK, D_MODEL), BF16)]
    kern = _merge_kernel_dense
    if router is not None:
        in_specs.append(pl.BlockSpec((2, D_MODEL, 128), lambda i: (0, 0, 0)))
        args.append(router)
        out_specs.append(pl.BlockSpec((tm, 128), row))
        out_shape.append(jax.ShapeDtypeStruct((N_TOK, 128), F32))
        kern = _merge_body
    return pl.pallas_call(
        kern,
        grid=(N_TOK // tm,),
        in_specs=in_specs,
        out_specs=out_specs,
        out_shape=out_shape,
        compiler_params=_cparams(("arbitrary",)),
        name="merge",
    )(*args)


FFN_TM = 512
FFN_CHUNK = FFN_DIM // 2


def _swiglu_partial(u, wg, wu, wd):
    h = (_silu(_dot(u, wg)) * _dot(u, wu)).astype(BF16)
    return _dot(h, wd)


def _ffn_kernel(u_ref, xc_ref, xl_ref, m_ref, wg_ref, wu_ref, wd_ref, oc_ref, ol_ref):
    u = u_ref[...]
    acc = None
    for c in range(FFN_DIM // FFN_CHUNK):
        sl = slice(c * FFN_CHUNK, (c + 1) * FFN_CHUNK)
        y = _swiglu_partial(u, wg_ref[:, sl], wu_ref[:, sl], wd_ref[sl, :])
        acc = y if acc is None else acc + y
    _pair_write(oc_ref, ol_ref, FFN_TM, _pair_read(xc_ref, xl_ref, FFN_TM) + m_ref[0, 5] * acc)


def _ffn(u, x_pair, mod_l, wg, wu, wd):
    tm = FFN_TM
    res = lambda shape: pl.BlockSpec(shape, lambda i: (0, 0), pipeline_mode=pl.Buffered(1))
    return pl.pallas_call(
        _ffn_kernel,
        grid=(N_TOK // tm,),
        in_specs=[pl.BlockSpec((tm, D_MODEL), lambda i: (i, 0))] + _pair_specs(tm) + [
            pl.BlockSpec((1, 6, 1, D_MODEL), lambda i: (_mod_row(i, tm), 0, 0, 0)),
            res((D_MODEL, FFN_DIM)), res((D_MODEL, FFN_DIM)), res((FFN_DIM, D_MODEL))],
        out_specs=_pair_specs(tm),
        out_shape=_pair_shapes(D_MODEL, F32),
        compiler_params=_cparams(("arbitrary",)),
        name="ffn",
    )(u, *x_pair, mod_l, wg, wu, wd)


MOE_TM = 512
MOE_NT = N_TOK // MOE_TM
MOE_PIECE = 16
MOE_SLAB = 2 * MOE_TM + N_EXPERTS * MOE_PIECE
MOE_BLK = 512
MOE_NB = -(-(2 * N_TOK + MOE_NT * N_EXPERTS * (MOE_PIECE - 1) + N_EXPERTS * (MOE_BLK - 1)) // MOE_BLK)
MOE_ROWS = MOE_NB * MOE_BLK
MOE_FCHUNK = 896


def _moe_count_kernel(c_ref, n_ref):
    n_ref[0] = jnp.sum(jnp.where(c_ref[...] > 0.0, 1.0, 0.0), axis=0, keepdims=True)


def _moe_count(comb):
    return pl.pallas_call(
        _moe_count_kernel,
        grid=(MOE_NT,),
        in_specs=[pl.BlockSpec((MOE_TM, 128), lambda i: (i, 0))],
        out_specs=pl.BlockSpec((1, 1, 128), lambda i: (i, 0, 0)),
        out_shape=jax.ShapeDtypeStruct((MOE_NT, 1, 128), F32),
        compiler_params=_cparams(("arbitrary",)),
        name="moe_count",
    )(comb)


def _moe_tables(counts):
    cnt = counts[:, 0, :N_EXPERTS].astype(jnp.int32)
    pc = (cnt + MOE_PIECE - 1) // MOE_PIECE * MOE_PIECE
    lo = jnp.cumsum(pc, axis=1) - pc
    tot = jnp.sum(pc, axis=0)
    totp = (tot + MOE_BLK - 1) // MOE_BLK * MOE_BLK
    goff = (jnp.cumsum(totp) - totp)[None, :] + jnp.cumsum(pc, axis=0) - pc
    ends = jnp.cumsum(totp // MOE_BLK)
    nact = ends[-1]
    s = jnp.minimum(jnp.arange(MOE_NB, dtype=jnp.int32), nact - 1)
    blk_e = jnp.minimum(jnp.sum((s[:, None] >= ends[None, :]).astype(jnp.int32), axis=1), N_EXPERTS - 1)
    lo_vec = jnp.pad(lo.astype(F32), ((0, 0), (0, 128 - N_EXPERTS))).reshape(MOE_NT, 1, 128)
    return dict(lo=lo.reshape(-1), npc=(pc // MOE_PIECE).reshape(-1), goff=goff.reshape(-1),
                nrow=jnp.sum(pc, axis=1), blk_e=blk_e, nact=nact.reshape(1), lo_vec=lo_vec)


def _moe_pieces(i, lo_s, npc_s, goff_s, make):
    for wait in (False, True):
        for e in range(N_EXPERTS):
            k = i * N_EXPERTS + e

            def body(p, carry, k=k, wait=wait):
                local = pl.multiple_of(lo_s[k] + p * MOE_PIECE, MOE_PIECE)
                glob = pl.multiple_of(goff_s[k] + p * MOE_PIECE, MOE_PIECE)
                cp = make(local, glob)
                if wait:
                    cp.wait()
                else:
                    cp.start()
                return carry

            lax.fori_loop(0, npc_s[k], body, 0)


def _slot_pair(slot):
    sel = slot >= 0.0
    s_hi = jnp.max(slot, axis=-1, keepdims=True)
    s_lo = jnp.min(jnp.where(sel, slot, float(MOE_SLAB)), axis=-1, keepdims=True)
    return sel, s_hi, s_lo


def _one_hot_rows(s):
    srow = lax.broadcasted_iota(jnp.int32, (MOE_TM, MOE_SLAB), 1).astype(F32)
    return jnp.where(srow == s, 1.0, 0.0)


def _moe_gather_kernel(lo_s, npc_s, goff_s, u_ref, c_ref, lov_ref, tri_ref, init_ref, xs_ref, slot_ref,
                       slab_ref, sem):
    del init_ref
    i = pl.program_id(0)
    routed = c_ref[...] > 0.0
    rank = _dot(tri_ref[...], jnp.where(routed, 1.0, 0.0).astype(BF16))
    slot = jnp.where(routed, lov_ref[0] + rank, -1.0)
    slot_ref[...] = slot
    _, s_hi, s_lo = _slot_pair(slot)
    pt = jnp.maximum(_one_hot_rows(s_hi), _one_hot_rows(s_lo)).astype(BF16)
    slab_ref[...] = _dot_tn(pt, u_ref[...]).astype(BF16)
    _moe_pieces(i, lo_s, npc_s, goff_s, lambda local, glob: pltpu.make_async_copy(
        slab_ref.at[pl.ds(local, MOE_PIECE), :], xs_ref.at[pl.ds(glob, MOE_PIECE), :], sem))


def _moe_gather(u, comb, tab):
    tri = jnp.asarray(np.tril(np.ones((MOE_TM, MOE_TM), np.float32), -1), BF16)
    init = jnp.zeros((MOE_ROWS, D_MODEL), BF16)
    row = lambda i, *_: (i, 0)
    grid_spec = pltpu.PrefetchScalarGridSpec(
        num_scalar_prefetch=3,
        grid=(MOE_NT,),
        in_specs=[pl.BlockSpec((MOE_TM, D_MODEL), row), pl.BlockSpec((MOE_TM, 128), row),
                  pl.BlockSpec((1, 1, 128), lambda i, *_: (i, 0, 0)),
                  pl.BlockSpec((MOE_TM, MOE_TM), lambda i, *_: (0, 0)),
                  pl.BlockSpec(memory_space=pl.ANY)],
        out_specs=[pl.BlockSpec(memory_space=pl.ANY), pl.BlockSpec((MOE_TM, 128), row)],
        scratch_shapes=[pltpu.VMEM((MOE_SLAB, D_MODEL), BF16), pltpu.SemaphoreType.DMA(())],
    )
    return pl.pallas_call(
        _moe_gather_kernel,
        grid_spec=grid_spec,
        out_shape=[jax.ShapeDtypeStruct((MOE_ROWS, D_MODEL), BF16), jax.ShapeDtypeStruct((N_TOK, 128), F32)],
        input_output_aliases={7: 0},
        compiler_params=_cparams(("arbitrary",)),
        name="moe_gather",
    )(tab['lo'], tab['npc'], tab['goff'], u, comb, tab['lo_vec'], tri, init)


def _moe_expert_kernel(be_s, nact_s, x_ref, wg_ref, wu_ref, wd_ref, y_ref):
    del be_s
    active = pl.program_id(0) < nact_s[0]

    @pl.when(jnp.logical_not(active))
    def _():
        y_ref[...] = jnp.zeros_like(y_ref)

    @pl.when(active)
    def _():
        x = x_ref[...]
        acc = None
        for c in range(EXPERT_DIM // MOE_FCHUNK):
            sl = slice(c * MOE_FCHUNK, (c + 1) * MOE_FCHUNK)
            y = _swiglu_partial(x, wg_ref[0, :, sl], wu_ref[0, :, sl], wd_ref[0, sl, :])
            acc = y if acc is None else acc + y
        y_ref[...] = acc.astype(BF16)


def _moe_expert(xs, tab, wg, wu, wd):
    blk = lambda s, be, nact: (jnp.minimum(s, nact[0] - 1), 0)
    wspec = lambda shape: pl.BlockSpec(shape, lambda s, be, nact: (be[s], 0, 0), pipeline_mode=pl.Buffered(1))
    grid_spec = pltpu.PrefetchScalarGridSpec(
        num_scalar_prefetch=2,
        grid=(MOE_NB,),
        in_specs=[pl.BlockSpec((MOE_BLK, D_MODEL), blk),
                  wspec((1, D_MODEL, EXPERT_DIM)), wspec((1, D_MODEL, EXPERT_DIM)), wspec((1, EXPERT_DIM, D_MODEL))],
        out_specs=pl.BlockSpec((MOE_BLK, D_MODEL), lambda s, be, nact: (s, 0)),
    )
    return pl.pallas_call(
        _moe_expert_kernel,
        grid_spec=grid_spec,
        out_shape=jax.ShapeDtypeStruct((MOE_ROWS, D_MODEL), BF16),
        compiler_params=_cparams(("arbitrary",)),
        name="moe_expert",
    )(tab['blk_e'], tab['nact'], xs, wg, wu, wd)


def _moe_combine_kernel(lo_s, npc_s, goff_s, nrow_s, y_ref, slot_ref, c_ref, xc_ref, xl_ref, m_ref, g_ref,
                        oc_ref, ol_ref, slab_ref, sem):
    i = pl.program_id(0)
    _moe_pieces(i, lo_s, npc_s, goff_s, lambda local, glob: pltpu.make_async_copy(
        y_ref.at[pl.ds(glob, MOE_PIECE), :], slab_ref.at[pl.ds(local, MOE_PIECE), :], sem))
    slot = slot_ref[...]
    comb = c_ref[...]
    sel, s_hi, s_lo = _slot_pair(slot)
    w_hi = jnp.sum(jnp.where(sel & (slot == s_hi), comb, 0.0), axis=-1, keepdims=True)
    w_lo = jnp.sum(jnp.where(sel & (slot == s_lo), comb, 0.0), axis=-1, keepdims=True)
    w_lo = jnp.where(s_lo == s_hi, 0.0, w_lo)
    srow = lax.broadcasted_iota(jnp.int32, (MOE_SLAB, 1), 0)
    ys = jnp.where(srow < nrow_s[i], slab_ref[...], jnp.zeros((), BF16))
    out = (w_hi * _dot(_one_hot_rows(s_hi).astype(BF16), ys) + w_lo * _dot(_one_hot_rows(s_lo).astype(BF16), ys))
    x = _pair_read(xc_ref, xl_ref, MOE_TM)
    _pair_write(oc_ref, ol_ref, MOE_TM, _rms(x + m_ref[0, 5] * out, g_ref[...]))


def _moe_combine(y, slot, comb, x_pair, mod_l, final_g, tab):
    row = lambda i, *_: (i, 0)
    grid_spec = pltpu.PrefetchScalarGridSpec(
        num_scalar_prefetch=4,
        grid=(MOE_NT,),
        in_specs=[pl.BlockSpec(memory_space=pl.ANY),
                  pl.BlockSpec((MOE_TM, 128), row), pl.BlockSpec((MOE_TM, 128), row)] + _pair_specs(MOE_TM) + [
                  pl.BlockSpec((1, 6, 1, D_MODEL), lambda i, *_: (_mod_row(i, MOE_TM), 0, 0, 0)),
                  pl.BlockSpec((1, D_MODEL), lambda i, *_: (0, 0))],
        out_specs=_pair_specs(MOE_TM),
        scratch_shapes=[pltpu.VMEM((MOE_SLAB, D_MODEL), BF16), pltpu.SemaphoreType.DMA(())],
    )
    return pl.pallas_call(
        _moe_combine_kernel,
        grid_spec=grid_spec,
        out_shape=_pair_shapes(D_MODEL, F32),
        compiler_params=_cparams(("arbitrary",)),
        name="moe_combine",
    )(tab['lo'], tab['npc'], tab['goff'], tab['nrow'], y, slot, comb, *x_pair, mod_l, final_g)


def _moe(u, comb, x_pair, mod_l, final_g, wg, wu, wd):
    tab = _moe_tables(_moe_count(comb))
    xs, slot = _moe_gather(u, comb, tab)
    y = _moe_expert(xs, tab, wg, wu, wd)
    return _moe_combine(y, slot, comb, x_pair, mod_l, final_g, tab)


def _block_diag(w):
    w = w.reshape(2, C_WIDTH // 128, 2, C_BLOCK, C_BLOCK)
    z = jnp.zeros_like(w[:, :, 0])
    top = jnp.concatenate([w[:, :, 0], z], axis=-1)
    bot = jnp.concatenate([z, w[:, :, 1]], axis=-1)
    return jnp.concatenate([top, bot], axis=-2).astype(BF16)


def kernel(x_prompt, x_sample, cache_attn_k, cache_attn_v, state_hgrn, state_lru, c, c_ctx, w_mod, b_mod, norm_mix_g, norm_ffn_g, w_in, attn_sink, hgrn_lower, hgrn_norm_g, lru_conv_w, lru_conv_b, lru_wa, lru_ba, lru_wx, lru_bx, lru_lambda, w_branch_a, w_branch_b, w_branch_c, w_out, ffn_w_gate, ffn_w_up, ffn_w_down, moe_router, moe_w_gate, moe_w_up, moe_w_down, final_norm_g):
    assert DEPTH == 2
    x = (x_prompt.reshape(N_CTX_TOK, D_MODEL), x_sample.reshape(N_LAT_TOK, D_MODEL))
    cond =jnp.concatenate([c_ctx[None, :], c, jnp.zeros((N_COND - 1 - DEC_BATCH, D_MODEL), F32)], axis=0)
    mods = _modulation(cond, w_mod, b_mod).reshape(DEPTH, N_COND, 6, 1, D_MODEL)

    lb_cum = jnp.cumsum(jax.nn.softmax(hgrn_lower.astype(F32), axis=0), axis=0)
    lb_layers = lb_cum - lb_cum[:1]
    cos, sin = _rope_tables()
    hconst = _hgrn_constants(False) + _hgrn_constants(True)

    ks, vs, hs, ls = [], [], [], []
    for l in range(DEPTH):
        w_in_l = w_in[l].astype(BF16)
        lb = lb_layers[l]
        lbp = jnp.stack([jnp.log(lb), jnp.log1p(-lb), 1.0 - lb]).reshape(3, 2, 1, B_WIDTH)
        lp = dict(
            conv_w=lru_conv_w[l], conv_b=lru_conv_b[l].reshape(1, C_WIDTH),
            wa=_block_diag(lru_wa[l]), wx=_block_diag(lru_wx[l]),
            ba=lru_ba[l].reshape(2, 1, C_WIDTH), bx=lru_bx[l].reshape(2, 1, C_WIDTH),
            lam=lru_lambda[l].reshape(2, 1, C_WIDTH),
            w_ba=w_branch_a[l].astype(BF16), w_bb=w_branch_b[l].astype(BF16), w_bc=w_branch_c[l].astype(BF16),
            w_out=w_out[l].astype(BF16), norm_ffn=norm_ffn_g[l].reshape(1, D_MODEL))
        mod_l = mods[l]

        proj = _in_proj(x, mod_l, norm_mix_g[l].reshape(1, D_MODEL), w_in_l)

        attn_c = _ctx_attention(proj, attn_sink[l])
        attn_l = _lat_attention(proj, attn_sink[l], cache_attn_k[:, l].reshape(DEC_BATCH, PAST_LEN, 128),
                                cache_attn_v[:, l].reshape(DEC_BATCH, PAST_LEN, 128), cos, sin)

        ng = hgrn_norm_g[l].reshape(1, B_KEY_DIM)
        hg_c, s_h = _hgrn(proj, lbp, ng, hconst, nbatch=BATCH, seq=SEQ, row_base=0, want_state=True)
        hg_l, = _hgrn(proj, lbp, ng, hconst, nbatch=DEC_BATCH, seq=DEC_SEQ, row_base=N_CTX_TOK,
                      state_in=state_hgrn[:, l])

        lr_c, s_l = _lru(proj, lp, nbatch=BATCH, seq=SEQ, row_base=0, reset=True, want_state=True)
        lr_l, = _lru(proj, lp, nbatch=DEC_BATCH, seq=DEC_SEQ, row_base=N_CTX_TOK, reset=False,
                     state_in=state_lru[:, l])
        mixers = (attn_c, attn_l, hg_c, hg_l, lr_c, lr_l)

        ks.append(proj[:N_CTX_TOK, KA_OFF:KA_OFF + 128].reshape(BATCH, SEQ, A_KV_HEADS, A_HEAD_DIM))
        vs.append(proj[:N_CTX_TOK, VA_OFF:VA_OFF + 128].reshape(BATCH, SEQ, A_KV_HEADS, A_HEAD_DIM))
        hs.append(s_h)
        ls.append(s_l)

        m = l // 2
        if l % 2 == 0:
            xc, xl, u = _merge(x, mod_l, proj, mixers, lp)
            x = _ffn(u, (xc, xl), mod_l, ffn_w_gate[m].astype(BF16), ffn_w_up[m].astype(BF16), ffn_w_down[m].astype(BF16))
        else:
            router = jnp.pad(moe_router[m], ((0, 0), (0, 128 - N_EXPERTS)))
            r_hi = router.astype(BF16)
            router = jnp.stack([r_hi, (router - r_hi.astype(F32)).astype(BF16)])
            xc, xl, u, comb = _merge(x, mod_l, proj, mixers, lp, router=router)
            x = _moe(u, comb, (xc, xl), mod_l, final_norm_g.reshape(1, D_MODEL), moe_w_gate[m].astype(BF16),
                     moe_w_up[m].astype(BF16), moe_w_down[m].astype(BF16))

    y_prompt = x[0].reshape(BATCH, SEQ, D_MODEL)
    y_sample = x[1].reshape(DEC_BATCH, DEC_SEQ, D_MODEL)
    return (y_prompt, y_sample, jnp.stack(ks, axis=1), jnp.stack(vs, axis=1), jnp.stack(hs, axis=1),
            jnp.stack(ls, axis=1))
```

```python
import functools

import numpy as np
import jax
import jax.numpy as jnp
from jax import lax
from jax.experimental import pallas as pl
from jax.experimental.pallas import tpu as pltpu

F32 = jnp.float32
BF16 = jnp.bfloat16
HIGHEST = lax.Precision.HIGHEST

D_MODEL = 1024
BATCH = 32
SEQ = 256
DEPTH = 2
DEC_BATCH = 4
DEC_SEQ = 2048
PAST_LEN = 512
GRID_W = 64
RMS_EPS = 1e-6
A_HEADS = 8
A_KV_HEADS = 2
A_GROUP = 4
A_HEAD_DIM = 64
A_WIDTH = 512
A_WINDOW = 128
ROPE_THETA = 10000.0
NEG_INF = -1e30
B_HEADS = 4
B_KEY_DIM = 128
B_WIDTH = 512
C_WIDTH = 512
C_HEADS = 8
C_BLOCK = 64
C_EXP = 8.0
FFN_DIM = 2816
N_EXPERTS = 8
EXPERT_DIM = 3584

N_CTX_TOK = BATCH * SEQ
N_LAT_TOK = DEC_BATCH * DEC_SEQ
N_TOK = N_CTX_TOK + N_LAT_TOK
N_COND = 8

QA_OFF = 0
KA_OFF = 512
VA_OFF = 640
B_OFF = 768
C_OFF = 3328
IN_COLS = 7424
IN_MIX_COLS = IN_COLS - 3 * D_MODEL

V7X_VMEM_LIMIT = 56 * 1024 * 1024
HG_CHUNK = 128
HG_LEVELS = (64, 32, 16, 8, 4, 2, 1)
HG_MAT_LEVELS = (4, 2)
HG_CHAINS = 16
LRU_SEGS = 8
LRU_PITCH_PAD = 4


def _cparams(sem, vmem=V7X_VMEM_LIMIT):
    return pltpu.CompilerParams(dimension_semantics=sem, vmem_limit_bytes=vmem)


def _mod_row(i, tm):
    nct = N_CTX_TOK // tm
    return jnp.where(i < nct, 0, 1 + (i - nct) // (DEC_SEQ // tm))


def _pair_specs(tm, cols=D_MODEL):
    nct = N_CTX_TOK // tm
    return [pl.BlockSpec((tm, cols), lambda i, *_: (jnp.minimum(i, nct - 1), 0)),
            pl.BlockSpec((tm, cols), lambda i, *_: (jnp.maximum(i - nct, 0), 0))]


def _pair_shapes(cols, dtype):
    return [jax.ShapeDtypeStruct((N_CTX_TOK, cols), dtype), jax.ShapeDtypeStruct((N_LAT_TOK, cols), dtype)]


def _pair_read(c_ref, l_ref, tm):
    return jnp.where(pl.program_id(0) < N_CTX_TOK // tm, c_ref[...], l_ref[...])


def _pair_write(c_ref, l_ref, tm, value):
    is_ctx = pl.program_id(0) < N_CTX_TOK // tm

    @pl.when(is_ctx)
    def _():
        c_ref[...] = value

    @pl.when(jnp.logical_not(is_ctx))
    def _():
        l_ref[...] = value


def _sigmoid(x):
    return 0.5 * jnp.tanh(0.5 * x) + 0.5


def _silu(x):
    return x * _sigmoid(x)


def _log_sigmoid(z):
    return jnp.minimum(z, 0.0) - jnp.log1p(jnp.exp(-jnp.abs(z)))


def _rms(x, g):
    ms = jnp.mean(x * x, axis=-1, keepdims=True)
    return x * lax.rsqrt(ms + RMS_EPS) * g


def _dot(a, b):
    return jnp.dot(a, b, preferred_element_type=F32)


def _dot_nt(a, b):
    return lax.dot_general(a, b, (((1,), (1,)), ((), ())), preferred_element_type=F32)


def _dot_tn(a, b):
    return lax.dot_general(a, b, (((0,), (0,)), ((), ())), preferred_element_type=F32)


def _mod_kernel(c_ref, w_ref, b_ref, o_ref):
    a = _silu(c_ref[...])
    o_ref[0] = jnp.dot(a, w_ref[0], preferred_element_type=F32, precision=HIGHEST) + b_ref[0]


def _modulation(cond, w_mod, b_mod):
    tn = 1536
    return pl.pallas_call(
        _mod_kernel,
        grid=(DEPTH, 6 * D_MODEL // tn),
        in_specs=[pl.BlockSpec((N_COND, D_MODEL), lambda l, j: (0, 0)),
                  pl.BlockSpec((1, D_MODEL, tn), lambda l, j: (l, 0, j)),
                  pl.BlockSpec((1, 1, tn), lambda l, j: (l, 0, j))],
        out_specs=pl.BlockSpec((1, N_COND, tn), lambda l, j: (l, 0, j)),
        out_shape=jax.ShapeDtypeStruct((DEPTH, N_COND, 6 * D_MODEL), F32),
        compiler_params=_cparams(("arbitrary", "arbitrary")),
        name="modulation",
    )(cond, w_mod, b_mod.reshape(DEPTH, 1, 6 * D_MODEL))


IN_TM = 256
IN_TN = 256


def _in_proj_kernel(xc_ref, xl_ref, m_ref, g_ref, w_ref, o_ref, gate_ref):
    u = _rms(_pair_read(xc_ref, xl_ref, IN_TM), g_ref[...]) * (1.0 + m_ref[0, 1]) + m_ref[0, 0]
    ub = u.astype(BF16)
    n_mix = IN_MIX_COLS // IN_TN
    for j in range(IN_COLS // IN_TN):
        y = _dot(ub, w_ref[:, j * IN_TN:(j + 1) * IN_TN])
        if j < n_mix:
            o_ref[:, j * IN_TN:(j + 1) * IN_TN] = y
        else:
            gate_ref[:, (j - n_mix) * IN_TN:(j - n_mix + 1) * IN_TN] = _sigmoid(y).astype(BF16)


def _in_proj(x_pair, mod_l, g, w):
    tm = IN_TM
    return pl.pallas_call(
        _in_proj_kernel,
        grid=(N_TOK // tm,),
        in_specs=_pair_specs(tm) + [
            pl.BlockSpec((1, 6, 1, D_MODEL), lambda i: (_mod_row(i, tm), 0, 0, 0)),
            pl.BlockSpec((1, D_MODEL), lambda i: (0, 0)),
            pl.BlockSpec((D_MODEL, IN_COLS), lambda i: (0, 0), pipeline_mode=pl.Buffered(1))],
        out_specs=[pl.BlockSpec((tm, IN_MIX_COLS), lambda i: (i, 0)),
                   pl.BlockSpec((tm, 3 * D_MODEL), lambda i: (i, 0))],
        out_shape=[jax.ShapeDtypeStruct((N_TOK, IN_MIX_COLS), F32),
                   jax.ShapeDtypeStruct((N_TOK, 3 * D_MODEL), BF16)],
        compiler_params=_cparams(("arbitrary",)),
        name="in_proj",
    )(*x_pair, mod_l, g, w)


def _softmax_pv(s_list, v_list, sink):
    m = sink
    for s in s_list:
        m = jnp.maximum(m, jnp.max(s, axis=-1, keepdims=True))
    den = jnp.exp(sink - m)
    out = None
    for s, v in zip(s_list, v_list):
        p = jnp.exp(s - m)
        den = den + jnp.sum(p, axis=-1, keepdims=True)
        o = _dot(p.astype(BF16), v)
        out = o if out is None else out + o
    return out / den


def _sink_column(sink_ref, h, rows_per_head):
    n = A_GROUP * rows_per_head
    row = lax.broadcasted_iota(jnp.int32, (n, 1), 0)
    col = jnp.full((n, 1), sink_ref[A_GROUP * h + A_GROUP - 1], F32)
    for g in range(A_GROUP - 2, -1, -1):
        col = jnp.where(row < (g + 1) * rows_per_head, sink_ref[A_GROUP * h + g], col)
    return col


def _ctx_attn_kernel(sink_ref, q_ref, k_ref, v_ref, o_ref):
    q = q_ref[...]
    k = k_ref[...]
    v = v_ref[...]
    scale = A_HEAD_DIM ** -0.5
    outs = []
    for h in range(A_KV_HEADS):
        hs = slice(h * A_HEAD_DIM, (h + 1) * A_HEAD_DIM)
        kh = k[:, hs].astype(BF16)
        vh = v[:, hs].astype(BF16)
        qs = jnp.concatenate(
            [q[:, (A_GROUP * h + g) * A_HEAD_DIM:(A_GROUP * h + g + 1) * A_HEAD_DIM] for g in range(A_GROUP)],
            axis=0).astype(BF16)
        s = _dot_nt(qs, kh) * scale
        o = _softmax_pv([s], [vh], _sink_column(sink_ref, h, SEQ))
        outs.extend(o[g * SEQ:(g + 1) * SEQ] for g in range(A_GROUP))
    o_ref[...] = jnp.concatenate(outs, axis=-1).astype(BF16)


def _ctx_attention(proj, sink):
    return pl.pallas_call(
        _ctx_attn_kernel,
        grid=(BATCH,),
        in_specs=[pl.BlockSpec(memory_space=pltpu.SMEM),
                  pl.BlockSpec((SEQ, A_WIDTH), lambda b: (b, QA_OFF // A_WIDTH)),
                  pl.BlockSpec((SEQ, 128), lambda b: (b, KA_OFF // 128)),
                  pl.BlockSpec((SEQ, 128), lambda b: (b, VA_OFF // 128))],
        out_specs=pl.BlockSpec((SEQ, A_WIDTH), lambda b: (b, 0)),
        out_shape=jax.ShapeDtypeStruct((N_CTX_TOK, A_WIDTH), BF16),
        compiler_params=_cparams(("arbitrary",)),
        name="ctx_attention",
    )(sink, proj, proj, proj)


def _rope_tables():
    half = A_HEAD_DIM // 4
    inv = ROPE_THETA ** (-np.arange(half, dtype=np.float64) / half)
    t = np.arange(DEC_SEQ)
    row = (t // GRID_W)[:, None] * inv[None, :]
    col = (t % GRID_W)[:, None] * inv[None, :]
    cos = np.concatenate([np.cos(row), np.cos(row), np.cos(col), np.cos(col)], axis=1)
    sin = np.concatenate([-np.sin(row), np.sin(row), -np.sin(col), np.sin(col)], axis=1)
    return (jnp.asarray(np.tile(cos, (1, 2)), F32), jnp.asarray(np.tile(sin, (1, 2)), F32))


def _rope(x, cos, sin, first):
    part = jnp.where(first, pltpu.roll(x, 128 - 16, 1), pltpu.roll(x, 16, 1))
    return x * cos + part * sin


def _lat_attn_kernel(sink_ref, q_ref, km_ref, k0_ref, kp_ref, vm_ref, v0_ref, vp_ref, kc_ref, vc_ref,
                     cm_ref, c0_ref, cp_ref, sm_ref, s0_ref, sp_ref, o_ref):
    j = pl.program_id(1)
    blk = A_WINDOW
    lane = lax.broadcasted_iota(jnp.int32, (blk, 128), 1)
    first = (lane % 32) < 16
    c0 = c0_ref[...]
    s0 = s0_ref[...]
    q = q_ref[...]
    qr = [_rope(q[:, c * 128:(c + 1) * 128], c0, s0, first) for c in range(A_WIDTH // 128)]
    kw = jnp.concatenate([_rope(km_ref[...], cm_ref[...], sm_ref[...], first),
                          _rope(k0_ref[...], c0, s0, first),
                          _rope(kp_ref[...], cp_ref[...], sp_ref[...], first)], axis=0)
    vw = jnp.concatenate([vm_ref[...], v0_ref[...], vp_ref[...]], axis=0)
    kc = kc_ref[0]
    vc = vc_ref[0]
    nq = A_GROUP * blk
    t = lax.broadcasted_iota(jnp.int32, (nq, 3 * blk), 0) % blk
    s = lax.broadcasted_iota(jnp.int32, (nq, 3 * blk), 1) - blk
    kpos = j * blk + s
    mask = (jnp.abs(s - t) <= A_WINDOW) & (kpos >= 0) & (kpos < DEC_SEQ)
    scale = A_HEAD_DIM ** -0.5
    outs = []
    for h in range(A_KV_HEADS):
        hs = slice(h * A_HEAD_DIM, (h + 1) * A_HEAD_DIM)
        heads = [A_GROUP * h + g for g in range(A_GROUP)]
        qs = jnp.concatenate(
            [qr[hd // 2][:, (hd % 2) * A_HEAD_DIM:(hd % 2 + 1) * A_HEAD_DIM] for hd in heads], axis=0).astype(BF16)
        sw = jnp.where(mask, _dot_nt(qs, kw[:, hs].astype(BF16)) * scale, NEG_INF)
        sc = _dot_nt(qs, kc[:, hs].astype(BF16)) * scale
        o = _softmax_pv([sw, sc], [vw[:, hs].astype(BF16), vc[:, hs].astype(BF16)],
                        _sink_column(sink_ref, h, blk))
        outs.extend(o[g * blk:(g + 1) * blk] for g in range(A_GROUP))
    o_ref[...] = jnp.concatenate(outs, axis=-1).astype(BF16)


def _lat_attention(proj, sink, k_ctx, v_ctx, cos, sin):
    blk = A_WINDOW
    nb = DEC_SEQ // blk
    base = N_CTX_TOK // blk

    def rows(off):
        return lambda b, j: (base + b * nb + jnp.clip(j + off, 0, nb - 1))

    def kv_spec(col, off):
        r = rows(off)
        return pl.BlockSpec((blk, 128), lambda b, j: (r(b, j), col))

    def tab_spec(off):
        return pl.BlockSpec((blk, 128), lambda b, j: (jnp.clip(j + off, 0, nb - 1), 0))

    r0 = rows(0)
    ctx_spec = pl.BlockSpec((1, PAST_LEN, 128), lambda b, j: (b, 0, 0))
    return pl.pallas_call(
        _lat_attn_kernel,
        grid=(DEC_BATCH, nb),
        in_specs=[pl.BlockSpec(memory_space=pltpu.SMEM),
                  pl.BlockSpec((blk, A_WIDTH), lambda b, j: (r0(b, j), QA_OFF // A_WIDTH)),
                  kv_spec(KA_OFF // 128, -1), kv_spec(KA_OFF // 128, 0), kv_spec(KA_OFF // 128, 1),
                  kv_spec(VA_OFF // 128, -1), kv_spec(VA_OFF // 128, 0), kv_spec(VA_OFF // 128, 1),
                  ctx_spec, ctx_spec,
                  tab_spec(-1), tab_spec(0), tab_spec(1), tab_spec(-1), tab_spec(0), tab_spec(1)],
        out_specs=pl.BlockSpec((blk, A_WIDTH), lambda b, j: (b * nb + j, 0)),
        out_shape=jax.ShapeDtypeStruct((N_LAT_TOK, A_WIDTH), BF16),
        compiler_params=_cparams(("arbitrary", "arbitrary")),
        name="lat_attention",
    )(sink, proj, proj, proj, proj, proj, proj, proj, k_ctx, v_ctx, cos, cos, cos, sin, sin, sin)


def _hgrn_constants(reverse):
    L = HG_CHUNK
    t = np.arange(L)[:, None]
    s = np.arange(L)[None, :]
    tri = (s <= t).astype(np.float32)
    blocks = [tri]
    masks = []
    for b in HG_LEVELS:
        anchor = (t // (2 * b)) * 2 * b + b - 1
        right = np.where(t % (2 * b) >= b, 1.0, -1.0)
        if b in HG_MAT_LEVELS:
            blocks.append(right * (tri - (s <= anchor).astype(np.float32)))
        masks.append(((t // (2 * b) == s // (2 * b)) & (t % (2 * b) >= b) & (s % (2 * b) < b)).astype(np.float32))
    masks.append((t == s).astype(np.float32))
    if reverse:
        blocks = [m[::-1, ::-1] for m in blocks]
        masks = [m[::-1, ::-1] for m in masks]
    mat = np.concatenate(blocks, axis=0)
    return (jnp.asarray(np.concatenate([mat, mat], axis=1), BF16), jnp.asarray(np.stack(masks), F32))


LOG2E = 1.4426950408889634


def _hgrn_chunk(qraw, vv, z, lbp, st, mat, mask_ref, reverse):
    L = HG_CHUNK
    log_lb, log_1mlb, one_mlb = lbp
    q = qraw * _sigmoid(qraw) * (B_KEY_DIM ** -0.5)
    e = jnp.exp(-jnp.abs(z))
    r = 1.0 / (1.0 + e)
    k = one_mlb * jnp.where(z >= 0.0, e * r, r)
    a = log_lb
    b = log_1mlb + (jnp.minimum(z, 0.0) + jnp.log(r))
    lf = (jnp.maximum(a, b) + jnp.log(1.0 + jnp.exp(-jnp.abs(a - b)))) * LOG2E
    hi = lf.astype(BF16)
    lo = (lf - hi.astype(F32)).astype(BF16)
    g = _dot(mat, jnp.concatenate([hi, lo], axis=0))
    cum = g[0:L]
    tot = cum[0:1] if reverse else cum[L - 1:L]
    vb = vv.astype(BF16)
    o = _dot_nt((q * jnp.exp2(cum)).astype(BF16), st.astype(BF16))
    kd = (k * jnp.exp2(tot - cum)).astype(BF16)
    st_new = st * jnp.exp2(tot) + _dot_tn(vb, kd)
    nl = len(HG_LEVELS)
    row = lax.broadcasted_iota(jnp.int32, (L, 1), 0)
    if reverse:
        row = L - 1 - row
    sc = _dot_nt(q.astype(BF16), k.astype(BF16)) * mask_ref[nl]
    for li, bsz in enumerate(HG_LEVELS):
        is_query = (row & (2 * bsz - 1)) >= bsz
        if bsz in HG_MAT_LEVELS:
            j = 1 + HG_MAT_LEVELS.index(bsz)
            d = g[j * L:(j + 1) * L]
        elif bsz == 1:
            d = jnp.where(is_query, lf, 0.0)
        else:
            parts = []
            for base in range(0, L, 2 * bsz):
                lo_half = cum[base:base + bsz]
                hi_half = cum[base + bsz:base + 2 * bsz]
                if reverse:
                    anc = cum[base + bsz:base + bsz + 1]
                    parts += [lo_half - anc, anc - hi_half]
                else:
                    anc = cum[base + bsz - 1:base + bsz]
                    parts += [anc - lo_half, hi_half - anc]
            d = jnp.concatenate(parts, axis=0)
        zb = (jnp.where(is_query, q, k) * jnp.exp2(d)).astype(BF16)
        sc = sc + _dot_nt(zb, zb) * mask_ref[li]
    o = o + _dot(sc.astype(BF16), vb)
    return o, st_new


def _hgrn(proj, lbp, norm_g, consts, *, nbatch, seq, row_base, state_in=None, want_state=False):
    mat_f, mask_f, mat_b, mask_b = consts
    nseq = max(1, HG_CHAINS * HG_CHUNK // (2 * seq))
    rows = nseq * seq
    rb = row_base // rows

    def col(k):
        return pl.BlockSpec((rows, 128), lambda b, h: (rb + b, (B_OFF + k * B_WIDTH) // 128 + h))

    const2 = lambda shape: pl.BlockSpec(shape, lambda b, h: (0,) * len(shape))
    in_specs = [col(0), col(1), col(2), col(3), col(4),
                pl.BlockSpec((3, 2, 1, 128), lambda b, h: (0, 0, 0, h)),
                const2((1, 128)),
                const2(mat_f.shape), const2(mat_b.shape), const2(mask_f.shape), const2(mask_b.shape)]
    args = [proj, proj, proj, proj, proj, lbp, norm_g, mat_f, mat_b, mask_f, mask_b]
    state_spec = pl.BlockSpec((nseq, 2, 1, 128, 128), lambda b, h: (b, 0, h, 0, 0))
    if state_in is not None:
        in_specs.append(state_spec)
        args.append(state_in)
    out_specs = [pl.BlockSpec((rows, 128), lambda b, h: (b, h))]
    out_shape = [jax.ShapeDtypeStruct((nbatch * seq, B_WIDTH), BF16)]
    if want_state:
        out_specs.append(state_spec)
        out_shape.append(jax.ShapeDtypeStruct((nbatch, 2, B_HEADS, 128, 128), F32))
    kern = functools.partial(_hgrn_kernel_flat, seq=seq, nseq=nseq, has_state_in=state_in is not None,
                             has_state_out=want_state)
    return pl.pallas_call(
        kern,
        grid=(nbatch // nseq, B_HEADS),
        in_specs=in_specs,
        out_specs=out_specs,
        out_shape=out_shape,
        scratch_shapes=[pltpu.VMEM((rows, 128), F32)],
        compiler_params=_cparams(("arbitrary", "arbitrary")),
        name="hgrn_%d" % seq,
    )(*args)


def _hgrn_kernel_flat(*refs, seq, nseq, has_state_in, has_state_out):
    refs = list(refs)
    head = refs[:11]
    pos = 11
    s0_ref = None
    if has_state_in:
        s0_ref = refs[pos]
        pos += 1
    o_ref = refs[pos]
    pos += 1
    s_ref = None
    if has_state_out:
        s_ref = refs[pos]
        pos += 1
    acc_ref = refs[pos]
    _hgrn_body(*head, s0_ref, o_ref, s_ref, acc_ref, seq=seq, nseq=nseq)


def _hgrn_body(q_ref, i_ref, zf_ref, zb_ref, gt_ref, lb_ref, ng_ref, mf_ref, mb_ref, kf_ref, kb_ref,
               s0_ref, o_ref, s_ref, acc_ref, *, seq, nseq):
    L = HG_CHUNK
    n = seq // L
    acc_ref[...] = jnp.zeros_like(acc_ref)
    lbf = (lb_ref[0, 0], lb_ref[1, 0], lb_ref[2, 0])
    lbb = (lb_ref[0, 1], lb_ref[1, 1], lb_ref[2, 1])
    states = []
    for j in range(nseq):
        for d in range(2):
            if s0_ref is not None:
                states.append(s0_ref[j, d, 0].T)
            else:
                states.append(jnp.zeros((B_KEY_DIM, B_KEY_DIM), F32))

    def body(i, carry):
        carry = list(carry)
        for j in range(nseq):
            sf = pl.ds(pl.multiple_of(j * seq + i * L, L), L)
            sb = pl.ds(pl.multiple_of(j * seq + (n - 1 - i) * L, L), L)
            o_f, carry[2 * j] = _hgrn_chunk(q_ref[sf, :], i_ref[sf, :], zf_ref[sf, :], lbf, carry[2 * j],
                                            mf_ref[...], kf_ref, False)
            acc_ref[sf, :] += o_f
            o_b, carry[2 * j + 1] = _hgrn_chunk(q_ref[sb, :], i_ref[sb, :], zb_ref[sb, :], lbb, carry[2 * j + 1],
                                                mb_ref[...], kb_ref, True)
            acc_ref[sb, :] += o_b
        return tuple(carry)

    states = lax.fori_loop(0, n, body, tuple(states), unroll=HG_CHAINS // (2 * nseq))
    o_ref[...] = (_rms(acc_ref[...], ng_ref[...]) * _silu(gt_ref[...])).astype(BF16)
    if s_ref is not None:
        for j in range(nseq):
            for d in range(2):
                s_ref[j, d, 0] = states[2 * j + d].T


def _gelu_tanh(y):
    return 0.5 * y * (1.0 + jnp.tanh(0.7978845608028654 * (y + 0.044715 * (y * y * y))))


def _lru_body(xc_ref, yc_ref, cw_ref, cb_ref, wa_ref, ba_ref, wx_ref, bx_ref, lam_ref, h0_ref, o_ref, s_ref,
              xs_ref, a_ref, u_ref, *, rows, reset, independent):
    T = rows
    seg = T // LRU_SEGS
    row = lax.broadcasted_iota(jnp.int32, (T, 1), 0)
    seq_len = seg if independent else T
    pos = row % seg if independent else row
    pad = 8
    zeros = jnp.zeros((pad, 128), F32)
    xs_ref[0:pad, :] = zeros
    xs_ref[pad:pad + T, :] = xc_ref[...]
    xs_ref[pad + T:2 * pad + T, :] = zeros
    cw = cw_ref[...]
    x = cb_ref[...]
    for tap in range(4):
        off = tap - 2
        term = cw[tap:tap + 1] * xs_ref[pad + off:pad + off + T, :]
        if independent and off != 0:
            term = jnp.where((pos + off >= 0) & (pos + off < seq_len), term, 0.0)
        x = x + term
    xb = x.astype(BF16)
    pitch = seg + LRU_PITCH_PAD
    for d in range(2):
        rg = _sigmoid(_dot(xb, wa_ref[d, 0]) + ba_ref[d])
        ig = _sigmoid(_dot(xb, wx_ref[d, 0]) + bx_ref[d])
        log_a = (C_EXP * rg) * _log_sigmoid(lam_ref[d])
        a = jnp.exp(log_a)
        mult = jnp.sqrt(1.0 - a * a)
        if reset:
            mult = jnp.where(pos == (0 if d == 0 else seq_len - 1), 1.0, mult)
        u = mult * ig * x
        for r in range(LRU_SEGS):
            a_ref[d, r * pitch:r * pitch + seg, :] = a[r * seg:(r + 1) * seg]
            u_ref[d, r * pitch:r * pitch + seg, :] = u[r * seg:(r + 1) * seg]
    sub =lax.broadcasted_iota(jnp.int32, (LRU_SEGS, 1), 0)
    zrow = jnp.zeros((1, 128), F32)
    if h0_ref is not None:
        hf0 = jnp.where(sub == 0, h0_ref[0, 0:1, :], 0.0)
        hb0 = jnp.where(sub == LRU_SEGS - 1, h0_ref[0, 1:2, :], 0.0)
    else:
        hf0 = jnp.zeros((LRU_SEGS, 128), F32)
        hb0 = hf0
    ones = jnp.ones((LRU_SEGS, 128), F32)

    def local_scan(i, carry):
        hf, pf, hb, pb = carry
        sf = pl.ds(i, LRU_SEGS, stride=pitch)
        sb = pl.ds(seg - 1 - i, LRU_SEGS, stride=pitch)
        af = a_ref[0, sf, :]
        hf = af * hf + u_ref[0, sf, :]
        pf = af * pf
        u_ref[0, sf, :] = hf
        a_ref[0, sf, :] = pf
        ab = a_ref[1, sb, :]
        hb = ab * hb + u_ref[1, sb, :]
        pb = ab * pb
        u_ref[1, sb, :] = hb
        a_ref[1, sb, :] = pb
        return hf, pf, hb, pb

    hf, pf, hb, pb = lax.fori_loop(0, seg, local_scan, (hf0, ones, hb0, ones), unroll=4)
    if independent:
        for r in range(LRU_SEGS):
            sl = slice(r * seg, (r + 1) * seg)
            ps = slice(r * pitch, r * pitch + seg)
            o_ref[sl, :] = ((u_ref[0, ps, :] + u_ref[1, ps, :]) * _gelu_tanh(yc_ref[sl, :])).astype(BF16)
        if s_ref is not None:
            s_ref[:, 0, :] = hf
            s_ref[:, 1, :] = hb
        return
    carry_f = [zrow] * LRU_SEGS
    carry_b = [zrow] * LRU_SEGS
    end_f = hf[0:1]
    for r in range(1, LRU_SEGS):
        carry_f[r] = end_f
        end_f = hf[r:r + 1] + pf[r:r + 1] * end_f
    end_b = hb[LRU_SEGS - 1:LRU_SEGS]
    for r in range(LRU_SEGS - 2, -1, -1):
        carry_b[r] = end_b
        end_b = hb[r:r + 1] + pb[r:r + 1] * end_b
    for r in range(LRU_SEGS):
        sl = slice(r * seg, (r + 1) * seg)
        ps = slice(r * pitch, r * pitch + seg)
        h = (u_ref[0, ps, :] + a_ref[0, ps, :] * carry_f[r]) + (u_ref[1, ps, :] + a_ref[1, ps, :] * carry_b[r])
        o_ref[sl, :] = (h * _gelu_tanh(yc_ref[sl, :])).astype(BF16)
    if s_ref is not None:
        s_ref[0, 0:1, :] = end_f
        s_ref[0, 1:2, :] = end_b


def _lru_kernel_flat(*refs, rows, reset, independent, has_state_in, has_state_out):
    refs = list(refs)
    head = refs[:9]
    pos = 9
    h0_ref = None
    if has_state_in:
        h0_ref = refs[pos]
        pos += 1
    o_ref = refs[pos]
    pos += 1
    s_ref = None
    if has_state_out:
        s_ref = refs[pos]
        pos += 1
    _lru_body(*head, h0_ref, o_ref, s_ref, *refs[pos:], rows=rows, reset=reset, independent=independent)


def _lru(proj, lp, *, nbatch, seq, row_base, reset, state_in=None, want_state=False):
    independent = seq < LRU_SEGS * 128
    per_step = LRU_SEGS if independent else 1
    rows = seq * per_step
    nstep = nbatch // per_step
    rb = row_base // rows
    nblk = C_WIDTH // 128

    def col(k):
        return pl.BlockSpec((rows, 128), lambda b, c: (rb + b, (C_OFF + k * C_WIDTH) // 128 + c))

    vec = lambda rows: pl.BlockSpec((rows, 1, 128), lambda b, c: (0, 0, c))
    mat = pl.BlockSpec((2, 1, 128, 128), lambda b, c: (0, c, 0, 0))
    in_specs = [col(0), col(1),
                pl.BlockSpec((4, 128), lambda b, c: (0, c)), pl.BlockSpec((1, 128), lambda b, c: (0, c)),
                mat, vec(2), mat, vec(2), vec(2)]
    args = [proj, proj, lp['conv_w'], lp['conv_b'], lp['wa'], lp['ba'], lp['wx'], lp['bx'], lp['lam']]
    if state_in is not None:
        in_specs.append(pl.BlockSpec((1, 2, 128), lambda b, c: (b, 0, c)))
        args.append(state_in)
    out_specs = [pl.BlockSpec((rows, 128), lambda b, c: (b, c))]
    out_shape = [jax.ShapeDtypeStruct((nbatch * seq, C_WIDTH), BF16)]
    if want_state:
        out_specs.append(pl.BlockSpec((per_step, 2, 128), lambda b, c: (b, 0, c)))
        out_shape.append(jax.ShapeDtypeStruct((nbatch, 2, C_WIDTH), F32))
    kern = functools.partial(_lru_kernel_flat, rows=rows, reset=reset, independent=independent,
                             has_state_in=state_in is not None, has_state_out=want_state)
    scan_rows = rows + LRU_SEGS * LRU_PITCH_PAD
    return pl.pallas_call(
        kern,
        grid=(nstep, nblk),
        in_specs=in_specs,
        out_specs=out_specs,
        out_shape=out_shape,
        scratch_shapes=[pltpu.VMEM((rows + 16, 128), F32), pltpu.VMEM((2, scan_rows, 128), F32),
                        pltpu.VMEM((2, scan_rows, 128), F32)],
        compiler_params=_cparams(("arbitrary", "arbitrary")),
        name="lru_%d" % seq,
    )(*args)


MERGE_TM = 256


def _top2_combine(logits):
    lane = lax.broadcasted_iota(jnp.int32, logits.shape, 1).astype(F32)
    lg = jnp.where(lane < N_EXPERTS, logits, -jnp.inf)
    m1 = jnp.max(lg, axis=-1, keepdims=True)
    i1 = jnp.min(jnp.where(lg == m1, lane, 128.0), axis=-1, keepdims=True)
    lg2 = jnp.where(lane == i1, -jnp.inf, lg)
    m2 = jnp.max(lg2, axis=-1, keepdims=True)
    i2 = jnp.min(jnp.where(lg2 == m2, lane, 128.0), axis=-1, keepdims=True)
    e = jnp.exp(m2 - m1)
    w1 = 1.0 / (1.0 + e)
    return jnp.where(lane == i1, w1, 0.0) + jnp.where(lane == i2, e * w1, 0.0)


def _merge_body(xc_ref, xl_ref, m_ref, z0_ref, z1_ref, z2_ref, ac_ref, al_ref, hc_ref, hl_ref, lc_ref, ll_ref,
                wba_ref, wbb_ref, wbc_ref, wo_ref, g_ref, r_ref, xoc_ref, xol_ref, u_ref, c_ref):
    tm = MERGE_TM
    mg = (z0_ref[...].astype(F32) * _dot(_pair_read(ac_ref, al_ref, tm), wba_ref[...])
          + z1_ref[...].astype(F32) * _dot(_pair_read(hc_ref, hl_ref, tm), wbb_ref[...])
          + z2_ref[...].astype(F32) * _dot(_pair_read(lc_ref, ll_ref, tm), wbc_ref[...]))
    x = _pair_read(xc_ref, xl_ref, tm) + m_ref[0, 2] * _dot(mg.astype(BF16), wo_ref[...])
    _pair_write(xoc_ref, xol_ref, tm, x)
    u =_rms(x, g_ref[...]) * (1.0 + m_ref[0, 4]) + m_ref[0, 3]
    ub = u.astype(BF16)
    u_ref[...] = ub
    if r_ref is not None:
        ul = (u - ub.astype(F32)).astype(BF16)
        logits = _dot(ub, r_ref[0]) + (_dot(ub, r_ref[1]) + _dot(ul, r_ref[0]))
        c_ref[...] = _top2_combine(logits)


def _merge_kernel_dense(*refs):
    _merge_body(*refs[:17], None, *refs[17:], None)


def _merge(x_pair, mod_l, gates, mixers, lp, router=None):
    tm = MERGE_TM
    row = lambda i: (i, 0)
    const = lambda i: (0, 0)
    res = lambda shape: pl.BlockSpec(shape, const, pipeline_mode=pl.Buffered(1))
    in_specs = _pair_specs(tm) + [
        pl.BlockSpec((1, 6, 1, D_MODEL), lambda i: (_mod_row(i, tm), 0, 0, 0)),
        pl.BlockSpec((tm, D_MODEL), lambda i: (i, 0)),
        pl.BlockSpec((tm, D_MODEL), lambda i: (i, 1)),
        pl.BlockSpec((tm, D_MODEL), lambda i: (i, 2))]
    in_specs += _pair_specs(tm, 512) * 3
    in_specs += [res((A_WIDTH, D_MODEL)), res((B_WIDTH, D_MODEL)), res((C_WIDTH, D_MODEL)), res((D_MODEL, D_MODEL)),
                 pl.BlockSpec((1, D_MODEL), const)]
    args = [*x_pair, mod_l, gates, gates, gates, *mixers, lp['w_ba'], lp['w_bb'], lp['w_bc'], lp['w_out'],
            lp['norm_ffn']]
    out_specs = _pair_specs(tm) + [pl.BlockSpec((tm, D_MODEL), row)]
    out_shape = _pair_shapes(D_MODEL, F32) + [jax.ShapeDtypeStruct((N_TOK, D_MODEL), BF16)]
    kern = _merge_kernel_dense
    if router is not None:
        in_specs.append(pl.BlockSpec((2, D_MODEL, 128), lambda i: (0, 0, 0)))
        args.append(router)
        out_specs.append(pl.BlockSpec((tm, 128), row))
        out_shape.append(jax.ShapeDtypeStruct((N_TOK, 128), F32))
        kern = _merge_body
    return pl.pallas_call(
        kern,
        grid=(N_TOK // tm,),
        in_specs=in_specs,
        out_specs=out_specs,
        out_shape=out_shape,
        compiler_params=_cparams(("arbitrary",)),
        name="merge",
    )(*args)


FFN_TM = 512
FFN_CHUNK = FFN_DIM // 2


def _swiglu_partial(u, wg, wu, wd):
    h = (_silu(_dot(u, wg)) * _dot(u, wu)).astype(BF16)
    return _dot(h, wd)


def _ffn_kernel(u_ref, xc_ref, xl_ref, m_ref, wg_ref, wu_ref, wd_ref, oc_ref, ol_ref):
    u = u_ref[...]
    acc = None
    for c in range(FFN_DIM // FFN_CHUNK):
        sl = slice(c * FFN_CHUNK, (c + 1) * FFN_CHUNK)
        y = _swiglu_partial(u, wg_ref[:, sl], wu_ref[:, sl], wd_ref[sl, :])
        acc = y if acc is None else acc + y
    _pair_write(oc_ref, ol_ref, FFN_TM, _pair_read(xc_ref, xl_ref, FFN_TM) + m_ref[0, 5] * acc)


def _ffn(u, x_pair, mod_l, wg, wu, wd):
    tm = FFN_TM
    res = lambda shape: pl.BlockSpec(shape, lambda i: (0, 0), pipeline_mode=pl.Buffered(1))
    return pl.pallas_call(
        _ffn_kernel,
        grid=(N_TOK // tm,),
        in_specs=[pl.BlockSpec((tm, D_MODEL), lambda i: (i, 0))] + _pair_specs(tm) + [
            pl.BlockSpec((1, 6, 1, D_MODEL), lambda i: (_mod_row(i, tm), 0, 0, 0)),
            res((D_MODEL, FFN_DIM)), res((D_MODEL, FFN_DIM)), res((FFN_DIM, D_MODEL))],
        out_specs=_pair_specs(tm),
        out_shape=_pair_shapes(D_MODEL, F32),
        compiler_params=_cparams(("arbitrary",)),
        name="ffn",
    )(u, *x_pair, mod_l, wg, wu, wd)


MOE_TM = 512
MOE_NT = N_TOK // MOE_TM
MOE_PIECE = 16
MOE_SLAB = 2 * MOE_TM + N_EXPERTS * MOE_PIECE
MOE_BLK = 512
MOE_NB = -(-(2 * N_TOK + MOE_NT * N_EXPERTS * (MOE_PIECE - 1) + N_EXPERTS * (MOE_BLK - 1)) // MOE_BLK)
MOE_ROWS = MOE_NB * MOE_BLK
MOE_FCHUNK = 896


def _moe_count_kernel(c_ref, n_ref):
    n_ref[0] = jnp.sum(jnp.where(c_ref[...] > 0.0, 1.0, 0.0), axis=0, keepdims=True)


def _moe_count(comb):
    return pl.pallas_call(
        _moe_count_kernel,
        grid=(MOE_NT,),
        in_specs=[pl.BlockSpec((MOE_TM, 128), lambda i: (i, 0))],
        out_specs=pl.BlockSpec((1, 1, 128), lambda i: (i, 0, 0)),
        out_shape=jax.ShapeDtypeStruct((MOE_NT, 1, 128), F32),
        compiler_params=_cparams(("arbitrary",)),
        name="moe_count",
    )(comb)


def _moe_tables(counts):
    cnt = counts[:, 0, :N_EXPERTS].astype(jnp.int32)
    pc = (cnt + MOE_PIECE - 1) // MOE_PIECE * MOE_PIECE
    lo = jnp.cumsum(pc, axis=1) - pc
    tot = jnp.sum(pc, axis=0)
    totp = (tot + MOE_BLK - 1) // MOE_BLK * MOE_BLK
    goff = (jnp.cumsum(totp) - totp)[None, :] + jnp.cumsum(pc, axis=0) - pc
    ends = jnp.cumsum(totp // MOE_BLK)
    nact = ends[-1]
    s = jnp.minimum(jnp.arange(MOE_NB, dtype=jnp.int32), nact - 1)
    blk_e = jnp.minimum(jnp.sum((s[:, None] >= ends[None, :]).astype(jnp.int32), axis=1), N_EXPERTS - 1)
    lo_vec = jnp.pad(lo.astype(F32), ((0, 0), (0, 128 - N_EXPERTS))).reshape(MOE_NT, 1, 128)
    return dict(lo=lo.reshape(-1), npc=(pc // MOE_PIECE).reshape(-1), goff=goff.reshape(-1),
                nrow=jnp.sum(pc, axis=1), blk_e=blk_e, nact=nact.reshape(1), lo_vec=lo_vec)


def _moe_pieces(tile, lo_s, npc_s, goff_s, make, wait):
    for e in range(N_EXPERTS):
        k = tile * N_EXPERTS + e

        def body(p, carry, k=k):
            local = pl.multiple_of(lo_s[k] + p * MOE_PIECE, MOE_PIECE)
            glob = pl.multiple_of(goff_s[k] + p * MOE_PIECE, MOE_PIECE)
            cp = make(local, glob)
            if wait:
                cp.wait()
            else:
                cp.start()
            return carry

        lax.fori_loop(0, npc_s[k], body, 0)


def _slot_pair(slot):
    sel = slot >= 0.0
    s_hi = jnp.max(slot, axis=-1, keepdims=True)
    s_lo = jnp.min(jnp.where(sel, slot, float(MOE_SLAB)), axis=-1, keepdims=True)
    return sel, s_hi, s_lo


def _one_hot_rows(s):
    srow = lax.broadcasted_iota(jnp.int32, (MOE_TM, MOE_SLAB), 1).astype(F32)
    return jnp.where(srow == s, 1.0, 0.0)


def _moe_gather_kernel(lo_s, npc_s, goff_s, u_ref, c_ref, lov_ref, tri_ref, init_ref, xs_ref, slot_ref,
                       slab_ref, sem):
    del init_ref
    i = pl.program_id(0)
    routed = c_ref[...] > 0.0
    rank = _dot(tri_ref[...], jnp.where(routed, 1.0, 0.0).astype(BF16))
    slot = jnp.where(routed, lov_ref[0] + rank, -1.0)
    slot_ref[...] = slot
    _, s_hi, s_lo = _slot_pair(slot)
    pt = jnp.maximum(_one_hot_rows(s_hi), _one_hot_rows(s_lo)).astype(BF16)
    buf = i % 2
    slab_ref[buf] = _dot_tn(pt, u_ref[...]).astype(BF16)

    def pieces(tile, b, wait):
        _moe_pieces(tile, lo_s, npc_s, goff_s, lambda local, glob: pltpu.make_async_copy(
            slab_ref.at[b, pl.ds(local, MOE_PIECE), :], xs_ref.at[pl.ds(glob, MOE_PIECE), :], sem.at[b]), wait)

    pieces(i, buf, False)

    @pl.when(i > 0)
    def _():
        pieces(i - 1, 1 - buf, True)

    @pl.when(i == MOE_NT - 1)
    def _():
        pieces(i, buf, True)


def _moe_gather(u, comb, tab):
    tri = jnp.asarray(np.tril(np.ones((MOE_TM, MOE_TM), np.float32), -1), BF16)
    init = jnp.zeros((MOE_ROWS, D_MODEL), BF16)
    row = lambda i, *_: (i, 0)
    grid_spec = pltpu.PrefetchScalarGridSpec(
        num_scalar_prefetch=3,
        grid=(MOE_NT,),
        in_specs=[pl.BlockSpec((MOE_TM, D_MODEL), row), pl.BlockSpec((MOE_TM, 128), row),
                  pl.BlockSpec((1, 1, 128), lambda i, *_: (i, 0, 0)),
                  pl.BlockSpec((MOE_TM, MOE_TM), lambda i, *_: (0, 0)),
                  pl.BlockSpec(memory_space=pl.ANY)],
        out_specs=[pl.BlockSpec(memory_space=pl.ANY), pl.BlockSpec((MOE_TM, 128), row)],
        scratch_shapes=[pltpu.VMEM((2, MOE_SLAB, D_MODEL), BF16), pltpu.SemaphoreType.DMA((2,))],
    )
    return pl.pallas_call(
        _moe_gather_kernel,
        grid_spec=grid_spec,
        out_shape=[jax.ShapeDtypeStruct((MOE_ROWS, D_MODEL), BF16), jax.ShapeDtypeStruct((N_TOK, 128), F32)],
        input_output_aliases={7: 0},
        compiler_params=_cparams(("arbitrary",)),
        name="moe_gather",
    )(tab['lo'], tab['npc'], tab['goff'], u, comb, tab['lo_vec'], tri, init)


def _moe_expert_kernel(be_s, nact_s, x_ref, wg_ref, wu_ref, wd_ref, y_ref):
    del be_s
    active = pl.program_id(0) < nact_s[0]

    @pl.when(jnp.logical_not(active))
    def _():
        y_ref[...] = jnp.zeros_like(y_ref)

    @pl.when(active)
    def _():
        x = x_ref[...]
        acc = None
        for c in range(EXPERT_DIM // MOE_FCHUNK):
            sl = slice(c * MOE_FCHUNK, (c + 1) * MOE_FCHUNK)
            y = _swiglu_partial(x, wg_ref[0, :, sl], wu_ref[0, :, sl], wd_ref[0, sl, :])
            acc = y if acc is None else acc + y
        y_ref[...] = acc.astype(BF16)


def _moe_expert(xs, tab, wg, wu, wd):
    blk = lambda s, be, nact: (jnp.minimum(s, nact[0] - 1), 0)
    wspec = lambda shape: pl.BlockSpec(shape, lambda s, be, nact: (be[s], 0, 0), pipeline_mode=pl.Buffered(1))
    grid_spec = pltpu.PrefetchScalarGridSpec(
        num_scalar_prefetch=2,
        grid=(MOE_NB,),
        in_specs=[pl.BlockSpec((MOE_BLK, D_MODEL), blk),
                  wspec((1, D_MODEL, EXPERT_DIM)), wspec((1, D_MODEL, EXPERT_DIM)), wspec((1, EXPERT_DIM, D_MODEL))],
        out_specs=pl.BlockSpec((MOE_BLK, D_MODEL), lambda s, be, nact: (s, 0)),
    )
    return pl.pallas_call(
        _moe_expert_kernel,
        grid_spec=grid_spec,
        out_shape=jax.ShapeDtypeStruct((MOE_ROWS, D_MODEL), BF16),
        compiler_params=_cparams(("arbitrary",)),
        name="moe_expert",
    )(tab['blk_e'], tab['nact'], xs, wg, wu, wd)


def _moe_combine_kernel(lo_s, npc_s, goff_s, nrow_s, y_ref, slot_ref, c_ref, xc_ref, xl_ref, m_ref, g_ref,
                        oc_ref, ol_ref, slab_ref, sem):
    i = pl.program_id(0)
    buf = i % 2

    def pieces(tile, b, wait):
        _moe_pieces(tile, lo_s, npc_s, goff_s, lambda local, glob: pltpu.make_async_copy(
            y_ref.at[pl.ds(glob, MOE_PIECE), :], slab_ref.at[b, pl.ds(local, MOE_PIECE), :], sem.at[b]), wait)

    @pl.when(i == 0)
    def _():
        pieces(i, buf, False)

    @pl.when(i + 1 < MOE_NT)
    def _():
        pieces(i + 1, 1 - buf, False)

    slot = slot_ref[...]
    comb = c_ref[...]
    sel, s_hi, s_lo = _slot_pair(slot)
    w_hi = jnp.sum(jnp.where(sel & (slot == s_hi), comb, 0.0), axis=-1, keepdims=True)
    w_lo = jnp.sum(jnp.where(sel & (slot == s_lo), comb, 0.0), axis=-1, keepdims=True)
    w_lo = jnp.where(s_lo == s_hi, 0.0, w_lo)
    srow = lax.broadcasted_iota(jnp.int32, (MOE_SLAB, 1), 0)
    pieces(i, buf, True)
    ys = jnp.where(srow < nrow_s[i], slab_ref[buf], jnp.zeros((), BF16))
    out = (w_hi * _dot(_one_hot_rows(s_hi).astype(BF16), ys) + w_lo * _dot(_one_hot_rows(s_lo).astype(BF16), ys))
    x = _pair_read(xc_ref, xl_ref, MOE_TM)
    _pair_write(oc_ref, ol_ref, MOE_TM, _rms(x + m_ref[0, 5] * out, g_ref[...]))


def _moe_combine(y, slot, comb, x_pair, mod_l, final_g, tab):
    row = lambda i, *_: (i, 0)
    grid_spec = pltpu.PrefetchScalarGridSpec(
        num_scalar_prefetch=4,
        grid=(MOE_NT,),
        in_specs=[pl.BlockSpec(memory_space=pl.ANY),
                  pl.BlockSpec((MOE_TM, 128), row), pl.BlockSpec((MOE_TM, 128), row)] + _pair_specs(MOE_TM) + [
                  pl.BlockSpec((1, 6, 1, D_MODEL), lambda i, *_: (_mod_row(i, MOE_TM), 0, 0, 0)),
                  pl.BlockSpec((1, D_MODEL), lambda i, *_: (0, 0))],
        out_specs=_pair_specs(MOE_TM),
        scratch_shapes=[pltpu.VMEM((2, MOE_SLAB, D_MODEL), BF16), pltpu.SemaphoreType.DMA((2,))],
    )
    return pl.pallas_call(
        _moe_combine_kernel,
        grid_spec=grid_spec,
        out_shape=_pair_shapes(D_MODEL, F32),
        compiler_params=_cparams(("arbitrary",)),
        name="moe_combine",
    )(tab['lo'], tab['npc'], tab['goff'], tab['nrow'], y, slot, comb, *x_pair, mod_l, final_g)


def _moe(u, comb, x_pair, mod_l, final_g, wg, wu, wd):
    tab = _moe_tables(_moe_count(comb))
    xs, slot = _moe_gather(u, comb, tab)
    y = _moe_expert(xs, tab, wg, wu, wd)
    return _moe_combine(y, slot, comb, x_pair, mod_l, final_g, tab)


def _block_diag(w):
    w = w.reshape(2, C_WIDTH // 128, 2, C_BLOCK, C_BLOCK)
    z = jnp.zeros_like(w[:, :, 0])
    top = jnp.concatenate([w[:, :, 0], z], axis=-1)
    bot = jnp.concatenate([z, w[:, :, 1]], axis=-1)
    return jnp.concatenate([top, bot], axis=-2).astype(BF16)


def kernel(x_prompt, x_sample, cache_attn_k, cache_attn_v, state_hgrn, state_lru, c, c_ctx, w_mod, b_mod, norm_mix_g, norm_ffn_g, w_in, attn_sink, hgrn_lower, hgrn_norm_g, lru_conv_w, lru_conv_b, lru_wa, lru_ba, lru_wx, lru_bx, lru_lambda, w_branch_a, w_branch_b, w_branch_c, w_out, ffn_w_gate, ffn_w_up, ffn_w_down, moe_router, moe_w_gate, moe_w_up, moe_w_down, final_norm_g):
    assert DEPTH == 2
    x = (x_prompt.reshape(N_CTX_TOK, D_MODEL), x_sample.reshape(N_LAT_TOK, D_MODEL))
    cond =jnp.concatenate([c_ctx[None, :], c, jnp.zeros((N_COND - 1 - DEC_BATCH, D_MODEL), F32)], axis=0)
    mods = _modulation(cond, w_mod, b_mod).reshape(DEPTH, N_COND, 6, 1, D_MODEL)

    lb_cum = jnp.cumsum(jax.nn.softmax(hgrn_lower.astype(F32), axis=0), axis=0)
    lb_layers = lb_cum - lb_cum[:1]
    cos, sin = _rope_tables()
    hconst = _hgrn_constants(False) + _hgrn_constants(True)

    ks, vs, hs, ls = [], [], [], []
    for l in range(DEPTH):
        w_in_l = w_in[l].astype(BF16)
        lb = lb_layers[l]
        lbp = jnp.stack([jnp.log(lb), jnp.log1p(-lb), 1.0 - lb]).reshape(3, 2, 1, B_WIDTH)
        lp = dict(
            conv_w=lru_conv_w[l], conv_b=lru_conv_b[l].reshape(1, C_WIDTH),
            wa=_block_diag(lru_wa[l]), wx=_block_diag(lru_wx[l]),
            ba=lru_ba[l].reshape(2, 1, C_WIDTH), bx=lru_bx[l].reshape(2, 1, C_WIDTH),
            lam=lru_lambda[l].reshape(2, 1, C_WIDTH),
            w_ba=w_branch_a[l].astype(BF16), w_bb=w_branch_b[l].astype(BF16), w_bc=w_branch_c[l].astype(BF16),
            w_out=w_out[l].astype(BF16), norm_ffn=norm_ffn_g[l].reshape(1, D_MODEL))
        mod_l = mods[l]

        proj, gates = _in_proj(x, mod_l, norm_mix_g[l].reshape(1, D_MODEL), w_in_l)

        attn_c = _ctx_attention(proj, attn_sink[l])
        attn_l = _lat_attention(proj, attn_sink[l], cache_attn_k[:, l].reshape(DEC_BATCH, PAST_LEN, 128),
                                cache_attn_v[:, l].reshape(DEC_BATCH, PAST_LEN, 128), cos, sin)

        ng = hgrn_norm_g[l].reshape(1, B_KEY_DIM)
        hg_c, s_h = _hgrn(proj, lbp, ng, hconst, nbatch=BATCH, seq=SEQ, row_base=0, want_state=True)
        hg_l, = _hgrn(proj, lbp, ng, hconst, nbatch=DEC_BATCH, seq=DEC_SEQ, row_base=N_CTX_TOK,
                      state_in=state_hgrn[:, l])

        lr_c, s_l = _lru(proj, lp, nbatch=BATCH, seq=SEQ, row_base=0, reset=True, want_state=True)
        lr_l, = _lru(proj, lp, nbatch=DEC_BATCH, seq=DEC_SEQ, row_base=N_CTX_TOK, reset=False,
                     state_in=state_lru[:, l])
        mixers = (attn_c, attn_l, hg_c, hg_l, lr_c, lr_l)

        ks.append(proj[:N_CTX_TOK, KA_OFF:KA_OFF + 128].reshape(BATCH, SEQ, A_KV_HEADS, A_HEAD_DIM))
        vs.append(proj[:N_CTX_TOK, VA_OFF:VA_OFF + 128].reshape(BATCH, SEQ, A_KV_HEADS, A_HEAD_DIM))
        hs.append(s_h)
        ls.append(s_l)

        m = l // 2
        if l % 2 == 0:
            xc, xl, u = _merge(x, mod_l, gates, mixers, lp)
            x = _ffn(u, (xc, xl), mod_l, ffn_w_gate[m].astype(BF16), ffn_w_up[m].astype(BF16), ffn_w_down[m].astype(BF16))
        else:
            router = jnp.pad(moe_router[m], ((0, 0), (0, 128 - N_EXPERTS)))
            r_hi = router.astype(BF16)
            router = jnp.stack([r_hi, (router - r_hi.astype(F32)).astype(BF16)])
            xc, xl, u, comb = _merge(x, mod_l, gates, mixers, lp, router=router)
            x = _moe(u, comb, (xc, xl), mod_l, final_norm_g.reshape(1, D_MODEL), moe_w_gate[m].astype(BF16),
                     moe_w_up[m].astype(BF16), moe_w_down[m].astype(BF16))

    y_prompt = x[0].reshape(BATCH, SEQ, D_MODEL)
    y_sample = x[1].reshape(DEC_BATCH, DEC_SEQ, D_MODEL)
    return (y_prompt, y_sample, jnp.stack(ks, axis=1), jnp.stack(vs, axis=1), jnp.stack(hs, axis=1),
            jnp.stack(ls, axis=1))
```

```python
import functools

import numpy as np
import jax
import jax.numpy as jnp
from jax import lax
from jax.experimental import pallas as pl
from jax.experimental.pallas import tpu as pltpu

F32 = jnp.float32
BF16 = jnp.bfloat16
HIGHEST = lax.Precision.HIGHEST

D_MODEL = 1024
BATCH = 32
SEQ = 256
DEPTH = 2
DEC_BATCH = 4
DEC_SEQ = 2048
PAST_LEN = 512
GRID_W = 64
RMS_EPS = 1e-6
A_HEADS = 8
A_KV_HEADS = 2
A_GROUP = 4
A_HEAD_DIM = 64
A_WIDTH = 512
A_WINDOW = 128
ROPE_THETA = 10000.0
NEG_INF = -1e30
B_HEADS = 4
B_KEY_DIM = 128
B_WIDTH = 512
C_WIDTH = 512
C_HEADS = 8
C_BLOCK = 64
C_EXP = 8.0
FFN_DIM = 2816
N_EXPERTS = 8
EXPERT_DIM = 3584

N_CTX_TOK = BATCH * SEQ
N_LAT_TOK = DEC_BATCH * DEC_SEQ
N_TOK = N_CTX_TOK + N_LAT_TOK
N_COND = 8

QA_OFF = 0
KA_OFF = 512
VA_OFF = 640
B_OFF = 768
C_OFF = 3328
IN_COLS = 7424
IN_MIX_COLS = IN_COLS - 3 * D_MODEL

V7X_VMEM_LIMIT = 56 * 1024 * 1024
HG_CHUNK = 128
HG_LEVELS = (64, 32, 16, 8, 4, 2, 1)
HG_MAT_LEVELS = (4, 2)
HG_CHAINS = 16
LRU_SEGS = 8
LRU_PITCH_PAD = 4


def _cparams(sem, vmem=V7X_VMEM_LIMIT):
    return pltpu.CompilerParams(dimension_semantics=sem, vmem_limit_bytes=vmem)


def _mod_row(i, tm):
    nct = N_CTX_TOK // tm
    return jnp.where(i < nct, 0, 1 + (i - nct) // (DEC_SEQ // tm))


def _pair_specs(tm, cols=D_MODEL):
    nct = N_CTX_TOK // tm
    return [pl.BlockSpec((tm, cols), lambda i, *_: (jnp.minimum(i, nct - 1), 0)),
            pl.BlockSpec((tm, cols), lambda i, *_: (jnp.maximum(i - nct, 0), 0))]


def _pair_shapes(cols, dtype):
    return [jax.ShapeDtypeStruct((N_CTX_TOK, cols), dtype), jax.ShapeDtypeStruct((N_LAT_TOK, cols), dtype)]


def _pair_read(c_ref, l_ref, tm):
    return jnp.where(pl.program_id(0) < N_CTX_TOK // tm, c_ref[...], l_ref[...])


def _pair_write(c_ref, l_ref, tm, value):
    is_ctx = pl.program_id(0) < N_CTX_TOK // tm

    @pl.when(is_ctx)
    def _():
        c_ref[...] = value

    @pl.when(jnp.logical_not(is_ctx))
    def _():
        l_ref[...] = value


def _sigmoid(x):
    return 0.5 * jnp.tanh(0.5 * x) + 0.5


def _silu(x):
    return x * _sigmoid(x)


def _log_sigmoid(z):
    return jnp.minimum(z, 0.0) - jnp.log1p(jnp.exp(-jnp.abs(z)))


def _rms(x, g):
    ms = jnp.mean(x * x, axis=-1, keepdims=True)
    return x * lax.rsqrt(ms + RMS_EPS) * g


def _dot(a, b):
    return jnp.dot(a, b, preferred_element_type=F32)


def _dot_nt(a, b):
    return lax.dot_general(a, b, (((1,), (1,)), ((), ())), preferred_element_type=F32)


def _dot_tn(a, b):
    return lax.dot_general(a, b, (((0,), (0,)), ((), ())), preferred_element_type=F32)


def _mod_kernel(c_ref, w_ref, b_ref, o_ref):
    a = _silu(c_ref[...])
    o_ref[0] = jnp.dot(a, w_ref[0], preferred_element_type=F32, precision=HIGHEST) + b_ref[0]


def _modulation(cond, w_mod, b_mod):
    tn = 1536
    return pl.pallas_call(
        _mod_kernel,
        grid=(DEPTH, 6 * D_MODEL // tn),
        in_specs=[pl.BlockSpec((N_COND, D_MODEL), lambda l, j: (0, 0)),
                  pl.BlockSpec((1, D_MODEL, tn), lambda l, j: (l, 0, j)),
                  pl.BlockSpec((1, 1, tn), lambda l, j: (l, 0, j))],
        out_specs=pl.BlockSpec((1, N_COND, tn), lambda l, j: (l, 0, j)),
        out_shape=jax.ShapeDtypeStruct((DEPTH, N_COND, 6 * D_MODEL), F32),
        compiler_params=_cparams(("arbitrary", "arbitrary")),
        name="modulation",
    )(cond, w_mod, b_mod.reshape(DEPTH, 1, 6 * D_MODEL))


IN_TM = 256
IN_TN = 256


def _in_proj_kernel(xc_ref, xl_ref, m_ref, g_ref, w_ref, o_ref, gate_ref):
    u = _rms(_pair_read(xc_ref, xl_ref, IN_TM), g_ref[...]) * (1.0 + m_ref[0, 1]) + m_ref[0, 0]
    ub = u.astype(BF16)
    n_mix = IN_MIX_COLS // IN_TN
    for j in range(IN_COLS // IN_TN):
        y = _dot(ub, w_ref[:, j * IN_TN:(j + 1) * IN_TN])
        if j < n_mix:
            o_ref[:, j * IN_TN:(j + 1) * IN_TN] = y
        else:
            gate_ref[:, (j - n_mix) * IN_TN:(j - n_mix + 1) * IN_TN] = _sigmoid(y).astype(BF16)


def _in_proj(x_pair, mod_l, g, w):
    tm = IN_TM
    return pl.pallas_call(
        _in_proj_kernel,
        grid=(N_TOK // tm,),
        in_specs=_pair_specs(tm) + [
            pl.BlockSpec((1, 6, 1, D_MODEL), lambda i: (_mod_row(i, tm), 0, 0, 0)),
            pl.BlockSpec((1, D_MODEL), lambda i: (0, 0)),
            pl.BlockSpec((D_MODEL, IN_COLS), lambda i: (0, 0), pipeline_mode=pl.Buffered(1))],
        out_specs=[pl.BlockSpec((tm, IN_MIX_COLS), lambda i: (i, 0)),
                   pl.BlockSpec((tm, 3 * D_MODEL), lambda i: (i, 0))],
        out_shape=[jax.ShapeDtypeStruct((N_TOK, IN_MIX_COLS), F32),
                   jax.ShapeDtypeStruct((N_TOK, 3 * D_MODEL), BF16)],
        compiler_params=_cparams(("arbitrary",)),
        name="in_proj",
    )(*x_pair, mod_l, g, w)


LOG2E = 1.4426950408889634
A_LOGIT_SCALE = A_HEAD_DIM ** -0.5 * LOG2E


def _softmax_pv(s_list, v_list, sink):
    m = sink
    for s in s_list:
        m = jnp.maximum(m, jnp.max(s, axis=-1, keepdims=True))
    den = jnp.exp2(sink - m)
    out = None
    for s, v in zip(s_list, v_list):
        p = jnp.exp2(s - m)
        den = den + jnp.sum(p, axis=-1, keepdims=True)
        o = _dot(p.astype(BF16), v)
        out = o if out is None else out + o
    return out / den


def _sink_column(sink_ref, h, rows_per_head):
    n = A_GROUP * rows_per_head
    row = lax.broadcasted_iota(jnp.int32, (n, 1), 0)
    col = jnp.full((n, 1), sink_ref[A_GROUP * h + A_GROUP - 1], F32)
    for g in range(A_GROUP - 2, -1, -1):
        col = jnp.where(row < (g + 1) * rows_per_head, sink_ref[A_GROUP * h + g], col)
    return col


def _ctx_attn_kernel(sink_ref, q_ref, k_ref, v_ref, o_ref):
    q = q_ref[...]
    k = k_ref[...]
    v = v_ref[...]
    outs = []
    for h in range(A_KV_HEADS):
        hs = slice(h * A_HEAD_DIM, (h + 1) * A_HEAD_DIM)
        kh = k[:, hs].astype(BF16)
        vh = v[:, hs].astype(BF16)
        qs = jnp.concatenate(
            [q[:, (A_GROUP * h + g) * A_HEAD_DIM:(A_GROUP * h + g + 1) * A_HEAD_DIM] for g in range(A_GROUP)],
            axis=0)
        s = _dot_nt((qs * A_LOGIT_SCALE).astype(BF16), kh)
        o = _softmax_pv([s], [vh], _sink_column(sink_ref, h, SEQ) * LOG2E)
        outs.extend(o[g * SEQ:(g + 1) * SEQ] for g in range(A_GROUP))
    o_ref[...] = jnp.concatenate(outs, axis=-1).astype(BF16)


def _ctx_attention(proj, sink):
    return pl.pallas_call(
        _ctx_attn_kernel,
        grid=(BATCH,),
        in_specs=[pl.BlockSpec(memory_space=pltpu.SMEM),
                  pl.BlockSpec((SEQ, A_WIDTH), lambda b: (b, QA_OFF // A_WIDTH)),
                  pl.BlockSpec((SEQ, 128), lambda b: (b, KA_OFF // 128)),
                  pl.BlockSpec((SEQ, 128), lambda b: (b, VA_OFF // 128))],
        out_specs=pl.BlockSpec((SEQ, A_WIDTH), lambda b: (b, 0)),
        out_shape=jax.ShapeDtypeStruct((N_CTX_TOK, A_WIDTH), BF16),
        compiler_params=_cparams(("arbitrary",)),
        name="ctx_attention",
    )(sink, proj, proj, proj)


def _rope_tables():
    half = A_HEAD_DIM // 4
    inv = ROPE_THETA ** (-np.arange(half, dtype=np.float64) / half)
    t = np.arange(DEC_SEQ)
    row = (t // GRID_W)[:, None] * inv[None, :]
    col = (t % GRID_W)[:, None] * inv[None, :]
    cos = np.concatenate([np.cos(row), np.cos(row), np.cos(col), np.cos(col)], axis=1)
    sin = np.concatenate([-np.sin(row), np.sin(row), -np.sin(col), np.sin(col)], axis=1)
    return (jnp.asarray(np.tile(cos, (1, 2)), F32), jnp.asarray(np.tile(sin, (1, 2)), F32))


def _rope(x, cos, sin, first):
    part = jnp.where(first, pltpu.roll(x, 128 - 16, 1), pltpu.roll(x, 16, 1))
    return x * cos + part * sin


def _lat_attn_kernel(sink_ref, q_ref, km_ref, k0_ref, kp_ref, vm_ref, v0_ref, vp_ref, kc_ref, vc_ref,
                     cm_ref, c0_ref, cp_ref, sm_ref, s0_ref, sp_ref, o_ref):
    j = pl.program_id(1)
    blk = A_WINDOW
    lane = lax.broadcasted_iota(jnp.int32, (blk, 128), 1)
    first = (lane % 32) < 16
    c0 = c0_ref[...]
    s0 = s0_ref[...]
    q = q_ref[...]
    qr = [_rope(q[:, c * 128:(c + 1) * 128], c0, s0, first) for c in range(A_WIDTH // 128)]
    kw = jnp.concatenate([_rope(km_ref[...], cm_ref[...], sm_ref[...], first),
                          _rope(k0_ref[...], c0, s0, first),
                          _rope(kp_ref[...], cp_ref[...], sp_ref[...], first)], axis=0)
    vw = jnp.concatenate([vm_ref[...], v0_ref[...], vp_ref[...]], axis=0)
    kc = kc_ref[0]
    vc = vc_ref[0]
    t = lax.broadcasted_iota(jnp.int32, (blk, 3 * blk), 0)
    s = lax.broadcasted_iota(jnp.int32, (blk, 3 * blk), 1) - blk
    kpos = j * blk + s
    keep = jnp.where((jnp.abs(s - t) <= A_WINDOW) & (kpos >= 0) & (kpos < DEC_SEQ), 1.0, 0.0)
    keep = jnp.concatenate([keep] * A_GROUP, axis=0) > 0.0
    outs = []
    for h in range(A_KV_HEADS):
        hs = slice(h * A_HEAD_DIM, (h + 1) * A_HEAD_DIM)
        heads = [A_GROUP * h + g for g in range(A_GROUP)]
        qs = jnp.concatenate(
            [qr[hd // 2][:, (hd % 2) * A_HEAD_DIM:(hd % 2 + 1) * A_HEAD_DIM] for hd in heads], axis=0)
        qs = (qs * A_LOGIT_SCALE).astype(BF16)
        sw = jnp.where(keep, _dot_nt(qs, kw[:, hs].astype(BF16)), NEG_INF)
        sc = _dot_nt(qs, kc[:, hs].astype(BF16))
        o = _softmax_pv([sw, sc], [vw[:, hs].astype(BF16), vc[:, hs].astype(BF16)],
                        _sink_column(sink_ref, h, blk) * LOG2E)
        outs.extend(o[g * blk:(g + 1) * blk] for g in range(A_GROUP))
    o_ref[...] = jnp.concatenate(outs, axis=-1).astype(BF16)


def _lat_attention(proj, sink, k_ctx, v_ctx, cos, sin):
    blk = A_WINDOW
    nb = DEC_SEQ // blk
    base = N_CTX_TOK // blk

    def rows(off):
        return lambda b, j: (base + b * nb + jnp.clip(j + off, 0, nb - 1))

    def kv_spec(col, off):
        r = rows(off)
        return pl.BlockSpec((blk, 128), lambda b, j: (r(b, j), col))

    def tab_spec(off):
        return pl.BlockSpec((blk, 128), lambda b, j: (jnp.clip(j + off, 0, nb - 1), 0))

    r0 = rows(0)
    ctx_spec = pl.BlockSpec((1, PAST_LEN, 128), lambda b, j: (b, 0, 0))
    return pl.pallas_call(
        _lat_attn_kernel,
        grid=(DEC_BATCH, nb),
        in_specs=[pl.BlockSpec(memory_space=pltpu.SMEM),
                  pl.BlockSpec((blk, A_WIDTH), lambda b, j: (r0(b, j), QA_OFF // A_WIDTH)),
                  kv_spec(KA_OFF // 128, -1), kv_spec(KA_OFF // 128, 0), kv_spec(KA_OFF // 128, 1),
                  kv_spec(VA_OFF // 128, -1), kv_spec(VA_OFF // 128, 0), kv_spec(VA_OFF // 128, 1),
                  ctx_spec, ctx_spec,
                  tab_spec(-1), tab_spec(0), tab_spec(1), tab_spec(-1), tab_spec(0), tab_spec(1)],
        out_specs=pl.BlockSpec((blk, A_WIDTH), lambda b, j: (b * nb + j, 0)),
        out_shape=jax.ShapeDtypeStruct((N_LAT_TOK, A_WIDTH), BF16),
        compiler_params=_cparams(("arbitrary", "arbitrary")),
        name="lat_attention",
    )(sink, proj, proj, proj, proj, proj, proj, proj, k_ctx, v_ctx, cos, cos, cos, sin, sin, sin)


def _hgrn_constants(reverse):
    L = HG_CHUNK
    t = np.arange(L)[:, None]
    s = np.arange(L)[None, :]
    tri = (s <= t).astype(np.float32)
    blocks = [tri]
    masks = []
    for b in HG_LEVELS:
        anchor = (t // (2 * b)) * 2 * b + b - 1
        right = np.where(t % (2 * b) >= b, 1.0, -1.0)
        if b in HG_MAT_LEVELS:
            blocks.append(right * (tri - (s <= anchor).astype(np.float32)))
        masks.append(((t // (2 * b) == s // (2 * b)) & (t % (2 * b) >= b) & (s % (2 * b) < b)).astype(np.float32))
    masks.append((t == s).astype(np.float32))
    if reverse:
        blocks = [m[::-1, ::-1] for m in blocks]
        masks = [m[::-1, ::-1] for m in masks]
    mat = np.concatenate(blocks, axis=0)
    return (jnp.asarray(np.concatenate([mat, mat], axis=1), BF16), jnp.asarray(np.stack(masks), F32))


def _hgrn_chunk(qraw, vv, z, lbp, st, mat, mask_ref, reverse):
    L = HG_CHUNK
    log_lb, log_1mlb, one_mlb = lbp
    q = qraw * _sigmoid(qraw) * (B_KEY_DIM ** -0.5)
    e = jnp.exp(-jnp.abs(z))
    r = 1.0 / (1.0 + e)
    k = one_mlb * jnp.where(z >= 0.0, e * r, r)
    a = log_lb
    b = log_1mlb + (jnp.minimum(z, 0.0) + jnp.log(r))
    lf = (jnp.maximum(a, b) + jnp.log(1.0 + jnp.exp(-jnp.abs(a - b)))) * LOG2E
    hi = lf.astype(BF16)
    lo = (lf - hi.astype(F32)).astype(BF16)
    g = _dot(mat, jnp.concatenate([hi, lo], axis=0))
    cum = g[0:L]
    tot = cum[0:1] if reverse else cum[L - 1:L]
    vb = vv.astype(BF16)
    o = _dot_nt((q * jnp.exp2(cum)).astype(BF16), st.astype(BF16))
    kd = (k * jnp.exp2(tot - cum)).astype(BF16)
    st_new = st * jnp.exp2(tot) + _dot_tn(vb, kd)
    nl = len(HG_LEVELS)
    row = lax.broadcasted_iota(jnp.int32, (L, 1), 0)
    if reverse:
        row = L - 1 - row
    terms = [(q.astype(BF16), k.astype(BF16), nl)]
    for li, bsz in enumerate(HG_LEVELS):
        is_query = (row & (2 * bsz - 1)) >= bsz
        if bsz in HG_MAT_LEVELS:
            j = 1 + HG_MAT_LEVELS.index(bsz)
            d = g[j * L:(j + 1) * L]
        elif bsz == 1:
            d = jnp.where(is_query, lf, 0.0)
        else:
            parts = []
            for base in range(0, L, 2 * bsz):
                lo_half = cum[base:base + bsz]
                hi_half = cum[base + bsz:base + 2 * bsz]
                if reverse:
                    anc = cum[base + bsz:base + bsz + 1]
                    parts += [lo_half - anc, anc - hi_half]
                else:
                    anc = cum[base + bsz - 1:base + bsz]
                    parts += [anc - lo_half, hi_half - anc]
            d = jnp.concatenate(parts, axis=0)
        zb = (jnp.where(is_query, q, k) * jnp.exp2(d)).astype(BF16)
        terms.append((zb, zb, li))
    zero = jnp.zeros((L, B_KEY_DIM), BF16)
    sc = None
    for (qa, ka, ma), (qb, kb, mb) in zip(terms[0::2], terms[1::2]):
        lhs = jnp.concatenate([qa, qb], axis=1)
        rhs = jnp.concatenate([jnp.concatenate([ka, zero], axis=1), jnp.concatenate([zero, kb], axis=1)], axis=0)
        s2 = _dot_nt(lhs, rhs)
        part = s2[:, 0:L] * mask_ref[ma] + s2[:, L:2 * L] * mask_ref[mb]
        sc = part if sc is None else sc + part
    o = o + _dot(sc.astype(BF16), vb)
    return o, st_new


def _hgrn(proj, lbp, norm_g, consts, *, nbatch, seq, row_base, state_in=None, want_state=False):
    mat_f, mask_f, mat_b, mask_b = consts
    nseq = max(1, HG_CHAINS * HG_CHUNK // (2 * seq))
    rows = nseq * seq
    rb = row_base // rows

    def col(k):
        return pl.BlockSpec((rows, 128), lambda b, h: (rb + b, (B_OFF + k * B_WIDTH) // 128 + h))

    const2 = lambda shape: pl.BlockSpec(shape, lambda b, h: (0,) * len(shape))
    in_specs = [col(0), col(1), col(2), col(3), col(4),
                pl.BlockSpec((3, 2, 1, 128), lambda b, h: (0, 0, 0, h)),
                const2((1, 128)),
                const2(mat_f.shape), const2(mat_b.shape), const2(mask_f.shape), const2(mask_b.shape)]
    args = [proj, proj, proj, proj, proj, lbp, norm_g, mat_f, mat_b, mask_f, mask_b]
    state_spec = pl.BlockSpec((nseq, 2, 1, 128, 128), lambda b, h: (b, 0, h, 0, 0))
    if state_in is not None:
        in_specs.append(state_spec)
        args.append(state_in)
    out_specs = [pl.BlockSpec((rows, 128), lambda b, h: (b, h))]
    out_shape = [jax.ShapeDtypeStruct((nbatch * seq, B_WIDTH), BF16)]
    if want_state:
        out_specs.append(state_spec)
        out_shape.append(jax.ShapeDtypeStruct((nbatch, 2, B_HEADS, 128, 128), F32))
    kern = functools.partial(_hgrn_kernel_flat, seq=seq, nseq=nseq, has_state_in=state_in is not None,
                             has_state_out=want_state)
    return pl.pallas_call(
        kern,
        grid=(nbatch // nseq, B_HEADS),
        in_specs=in_specs,
        out_specs=out_specs,
        out_shape=out_shape,
        scratch_shapes=[pltpu.VMEM((rows, 128), F32)],
        compiler_params=_cparams(("arbitrary", "arbitrary")),
        name="hgrn_%d" % seq,
    )(*args)


def _hgrn_kernel_flat(*refs, seq, nseq, has_state_in, has_state_out):
    refs = list(refs)
    head = refs[:11]
    pos = 11
    s0_ref = None
    if has_state_in:
        s0_ref = refs[pos]
        pos += 1
    o_ref = refs[pos]
    pos += 1
    s_ref = None
    if has_state_out:
        s_ref = refs[pos]
        pos += 1
    acc_ref = refs[pos]
    _hgrn_body(*head, s0_ref, o_ref, s_ref, acc_ref, seq=seq, nseq=nseq)


def _hgrn_body(q_ref, i_ref, zf_ref, zb_ref, gt_ref, lb_ref, ng_ref, mf_ref, mb_ref, kf_ref, kb_ref,
               s0_ref, o_ref, s_ref, acc_ref, *, seq, nseq):
    L = HG_CHUNK
    n = seq // L
    acc_ref[...] = jnp.zeros_like(acc_ref)
    lbf = (lb_ref[0, 0], lb_ref[1, 0], lb_ref[2, 0])
    lbb = (lb_ref[0, 1], lb_ref[1, 1], lb_ref[2, 1])
    states = []
    for j in range(nseq):
        for d in range(2):
            if s0_ref is not None:
                states.append(s0_ref[j, d, 0].T)
            else:
                states.append(jnp.zeros((B_KEY_DIM, B_KEY_DIM), F32))

    def body(i, carry):
        carry = list(carry)
        for j in range(nseq):
            sf = pl.ds(pl.multiple_of(j * seq + i * L, L), L)
            sb = pl.ds(pl.multiple_of(j * seq + (n - 1 - i) * L, L), L)
            o_f, carry[2 * j] = _hgrn_chunk(q_ref[sf, :], i_ref[sf, :], zf_ref[sf, :], lbf, carry[2 * j],
                                            mf_ref[...], kf_ref, False)
            acc_ref[sf, :] += o_f
            o_b, carry[2 * j + 1] = _hgrn_chunk(q_ref[sb, :], i_ref[sb, :], zb_ref[sb, :], lbb, carry[2 * j + 1],
                                                mb_ref[...], kb_ref, True)
            acc_ref[sb, :] += o_b
        return tuple(carry)

    states = lax.fori_loop(0, n, body, tuple(states), unroll=HG_CHAINS // (2 * nseq))
    o_ref[...] = (_rms(acc_ref[...], ng_ref[...]) * _silu(gt_ref[...])).astype(BF16)
    if s_ref is not None:
        for j in range(nseq):
            for d in range(2):
                s_ref[j, d, 0] = states[2 * j + d].T


def _gelu_tanh(y):
    return 0.5 * y * (1.0 + jnp.tanh(0.7978845608028654 * (y + 0.044715 * (y * y * y))))


def _lru_body(xc_ref, yc_ref, cw_ref, cb_ref, wa_ref, ba_ref, wx_ref, bx_ref, lam_ref, h0_ref, o_ref, s_ref,
              xs_ref, a_ref, u_ref, *, rows, reset, independent):
    T = rows
    seg = T // LRU_SEGS
    row = lax.broadcasted_iota(jnp.int32, (T, 1), 0)
    seq_len = seg if independent else T
    pos = row % seg if independent else row
    pad = 8
    zeros = jnp.zeros((pad, 128), F32)
    xs_ref[0:pad, :] = zeros
    xs_ref[pad:pad + T, :] = xc_ref[...]
    xs_ref[pad + T:2 * pad + T, :] = zeros
    cw = cw_ref[...]
    x = cb_ref[...]
    for tap in range(4):
        off = tap - 2
        term = cw[tap:tap + 1] * xs_ref[pad + off:pad + off + T, :]
        if independent and off != 0:
            term = jnp.where((pos + off >= 0) & (pos + off < seq_len), term, 0.0)
        x = x + term
    xb = x.astype(BF16)
    pitch = seg + LRU_PITCH_PAD
    for d in range(2):
        rg = _sigmoid(_dot(xb, wa_ref[d, 0]) + ba_ref[d])
        ig = _sigmoid(_dot(xb, wx_ref[d, 0]) + bx_ref[d])
        log_a = (C_EXP * rg) * _log_sigmoid(lam_ref[d])
        a = jnp.exp(log_a)
        mult = jnp.sqrt(1.0 - a * a)
        if reset:
            mult = jnp.where(pos == (0 if d == 0 else seq_len - 1), 1.0, mult)
        u = mult * ig * x
        for r in range(LRU_SEGS):
            a_ref[d, r * pitch:r * pitch + seg, :] = a[r * seg:(r + 1) * seg]
            u_ref[d, r * pitch:r * pitch + seg, :] = u[r * seg:(r + 1) * seg]
    sub =lax.broadcasted_iota(jnp.int32, (LRU_SEGS, 1), 0)
    zrow = jnp.zeros((1, 128), F32)
    if h0_ref is not None:
        hf0 = jnp.where(sub == 0, h0_ref[0, 0:1, :], 0.0)
        hb0 = jnp.where(sub == LRU_SEGS - 1, h0_ref[0, 1:2, :], 0.0)
    else:
        hf0 = jnp.zeros((LRU_SEGS, 128), F32)
        hb0 = hf0
    ones = jnp.ones((LRU_SEGS, 128), F32)

    def local_scan(i, carry):
        hf, pf, hb, pb = carry
        sf = pl.ds(i, LRU_SEGS, stride=pitch)
        sb = pl.ds(seg - 1 - i, LRU_SEGS, stride=pitch)
        af = a_ref[0, sf, :]
        hf = af * hf + u_ref[0, sf, :]
        pf = af * pf
        u_ref[0, sf, :] = hf
        a_ref[0, sf, :] = pf
        ab = a_ref[1, sb, :]
        hb = ab * hb + u_ref[1, sb, :]
        pb = ab * pb
        u_ref[1, sb, :] = hb
        a_ref[1, sb, :] = pb
        return hf, pf, hb, pb

    hf, pf, hb, pb = lax.fori_loop(0, seg, local_scan, (hf0, ones, hb0, ones), unroll=4)
    if independent:
        for r in range(LRU_SEGS):
            sl = slice(r * seg, (r + 1) * seg)
            ps = slice(r * pitch, r * pitch + seg)
            o_ref[sl, :] = ((u_ref[0, ps, :] + u_ref[1, ps, :]) * _gelu_tanh(yc_ref[sl, :])).astype(BF16)
        if s_ref is not None:
            s_ref[:, 0, :] = hf
            s_ref[:, 1, :] = hb
        return
    carry_f = [zrow] * LRU_SEGS
    carry_b = [zrow] * LRU_SEGS
    end_f = hf[0:1]
    for r in range(1, LRU_SEGS):
        carry_f[r] = end_f
        end_f = hf[r:r + 1] + pf[r:r + 1] * end_f
    end_b = hb[LRU_SEGS - 1:LRU_SEGS]
    for r in range(LRU_SEGS - 2, -1, -1):
        carry_b[r] = end_b
        end_b = hb[r:r + 1] + pb[r:r + 1] * end_b
    for r in range(LRU_SEGS):
        sl = slice(r * seg, (r + 1) * seg)
        ps = slice(r * pitch, r * pitch + seg)
        h = (u_ref[0, ps, :] + a_ref[0, ps, :] * carry_f[r]) + (u_ref[1, ps, :] + a_ref[1, ps, :] * carry_b[r])
        o_ref[sl, :] = (h * _gelu_tanh(yc_ref[sl, :])).astype(BF16)
    if s_ref is not None:
        s_ref[0, 0:1, :] = end_f
        s_ref[0, 1:2, :] = end_b


def _lru_kernel_flat(*refs, rows, reset, independent, has_state_in, has_state_out):
    refs = list(refs)
    head = refs[:9]
    pos = 9
    h0_ref = None
    if has_state_in:
        h0_ref = refs[pos]
        pos += 1
    o_ref = refs[pos]
    pos += 1
    s_ref = None
    if has_state_out:
        s_ref = refs[pos]
        pos += 1
    _lru_body(*head, h0_ref, o_ref, s_ref, *refs[pos:], rows=rows, reset=reset, independent=independent)


def _lru(proj, lp, *, nbatch, seq, row_base, reset, state_in=None, want_state=False):
    independent = seq < LRU_SEGS * 128
    per_step = LRU_SEGS if independent else 1
    rows = seq * per_step
    nstep = nbatch // per_step
    rb = row_base // rows
    nblk = C_WIDTH // 128

    def col(k):
        return pl.BlockSpec((rows, 128), lambda b, c: (rb + b, (C_OFF + k * C_WIDTH) // 128 + c))

    vec = lambda rows: pl.BlockSpec((rows, 1, 128), lambda b, c: (0, 0, c))
    mat = pl.BlockSpec((2, 1, 128, 128), lambda b, c: (0, c, 0, 0))
    in_specs = [col(0), col(1),
                pl.BlockSpec((4, 128), lambda b, c: (0, c)), pl.BlockSpec((1, 128), lambda b, c: (0, c)),
                mat, vec(2), mat, vec(2), vec(2)]
    args = [proj, proj, lp['conv_w'], lp['conv_b'], lp['wa'], lp['ba'], lp['wx'], lp['bx'], lp['lam']]
    if state_in is not None:
        in_specs.append(pl.BlockSpec((1, 2, 128), lambda b, c: (b, 0, c)))
        args.append(state_in)
    out_specs = [pl.BlockSpec((rows, 128), lambda b, c: (b, c))]
    out_shape = [jax.ShapeDtypeStruct((nbatch * seq, C_WIDTH), BF16)]
    if want_state:
        out_specs.append(pl.BlockSpec((per_step, 2, 128), lambda b, c: (b, 0, c)))
        out_shape.append(jax.ShapeDtypeStruct((nbatch, 2, C_WIDTH), F32))
    kern = functools.partial(_lru_kernel_flat, rows=rows, reset=reset, independent=independent,
                             has_state_in=state_in is not None, has_state_out=want_state)
    scan_rows = rows + LRU_SEGS * LRU_PITCH_PAD
    return pl.pallas_call(
        kern,
        grid=(nstep, nblk),
        in_specs=in_specs,
        out_specs=out_specs,
        out_shape=out_shape,
        scratch_shapes=[pltpu.VMEM((rows + 16, 128), F32), pltpu.VMEM((2, scan_rows, 128), F32),
                        pltpu.VMEM((2, scan_rows, 128), F32)],
        compiler_params=_cparams(("arbitrary", "arbitrary")),
        name="lru_%d" % seq,
    )(*args)


MERGE_TM = 256


def _top2_combine(logits):
    lane = lax.broadcasted_iota(jnp.int32, logits.shape, 1).astype(F32)
    lg = jnp.where(lane < N_EXPERTS, logits, -jnp.inf)
    m1 = jnp.max(lg, axis=-1, keepdims=True)
    i1 = jnp.min(jnp.where(lg == m1, lane, 128.0), axis=-1, keepdims=True)
    lg2 = jnp.where(lane == i1, -jnp.inf, lg)
    m2 = jnp.max(lg2, axis=-1, keepdims=True)
    i2 = jnp.min(jnp.where(lg2 == m2, lane, 128.0), axis=-1, keepdims=True)
    e = jnp.exp(m2 - m1)
    w1 = 1.0 / (1.0 + e)
    return jnp.where(lane == i1, w1, 0.0) + jnp.where(lane == i2, e * w1, 0.0)


def _merge_body(xc_ref, xl_ref, m_ref, z0_ref, z1_ref, z2_ref, ac_ref, al_ref, hc_ref, hl_ref, lc_ref, ll_ref,
                wba_ref, wbb_ref, wbc_ref, wo_ref, g_ref, r_ref, xoc_ref, xol_ref, u_ref, c_ref):
    tm = MERGE_TM
    mg = (z0_ref[...].astype(F32) * _dot(_pair_read(ac_ref, al_ref, tm), wba_ref[...])
          + z1_ref[...].astype(F32) * _dot(_pair_read(hc_ref, hl_ref, tm), wbb_ref[...])
          + z2_ref[...].astype(F32) * _dot(_pair_read(lc_ref, ll_ref, tm), wbc_ref[...]))
    x = _pair_read(xc_ref, xl_ref, tm) + m_ref[0, 2] * _dot(mg.astype(BF16), wo_ref[...])
    _pair_write(xoc_ref, xol_ref, tm, x)
    u =_rms(x, g_ref[...]) * (1.0 + m_ref[0, 4]) + m_ref[0, 3]
    ub = u.astype(BF16)
    u_ref[...] = ub
    if r_ref is not None:
        ul = (u - ub.astype(F32)).astype(BF16)
        logits = _dot(ub, r_ref[0]) + (_dot(ub, r_ref[1]) + _dot(ul, r_ref[0]))
        c_ref[...] = _top2_combine(logits)


def _merge_kernel_dense(*refs):
    _merge_body(*refs[:17], None, *refs[17:], None)


def _merge(x_pair, mod_l, gates, mixers, lp, router=None):
    tm = MERGE_TM
    row = lambda i: (i, 0)
    const = lambda i: (0, 0)
    res = lambda shape: pl.BlockSpec(shape, const, pipeline_mode=pl.Buffered(1))
    in_specs = _pair_specs(tm) + [
        pl.BlockSpec((1, 6, 1, D_MODEL), lambda i: (_mod_row(i, tm), 0, 0, 0)),
        pl.BlockSpec((tm, D_MODEL), lambda i: (i, 0)),
        pl.BlockSpec((tm, D_MODEL), lambda i: (i, 1)),
        pl.BlockSpec((tm, D_MODEL), lambda i: (i, 2))]
    in_specs += _pair_specs(tm, 512) * 3
    in_specs += [res((A_WIDTH, D_MODEL)), res((B_WIDTH, D_MODEL)), res((C_WIDTH, D_MODEL)), res((D_MODEL, D_MODEL)),
                 pl.BlockSpec((1, D_MODEL), const)]
    args = [*x_pair, mod_l, gates, gates, gates, *mixers, lp['w_ba'], lp['w_bb'], lp['w_bc'], lp['w_out'],
            lp['norm_ffn']]
    out_specs = _pair_specs(tm) + [pl.BlockSpec((tm, D_MODEL), row)]
    out_shape = _pair_shapes(D_MODEL, F32) + [jax.ShapeDtypeStruct((N_TOK, D_MODEL), BF16)]
    kern = _merge_kernel_dense
    if router is not None:
        in_specs.append(pl.BlockSpec((2, D_MODEL, 128), lambda i: (0, 0, 0)))
        args.append(router)
        out_specs.append(pl.BlockSpec((tm, 128), row))
        out_shape.append(jax.ShapeDtypeStruct((N_TOK, 128), F32))
        kern = _merge_body
    return pl.pallas_call(
        kern,
        grid=(N_TOK // tm,),
        in_specs=in_specs,
        out_specs=out_specs,
        out_shape=out_shape,
        compiler_params=_cparams(("arbitrary",)),
        name="merge",
    )(*args)


FFN_TM = 512
FFN_CHUNK = FFN_DIM // 2


def _swiglu_partial(u, wg, wu, wd):
    h = (_silu(_dot(u, wg)) * _dot(u, wu)).astype(BF16)
    return _dot(h, wd)


def _ffn_kernel(u_ref, xc_ref, xl_ref, m_ref, wg_ref, wu_ref, wd_ref, oc_ref, ol_ref):
    u = u_ref[...]
    acc = None
    for c in range(FFN_DIM // FFN_CHUNK):
        sl = slice(c * FFN_CHUNK, (c + 1) * FFN_CHUNK)
        y = _swiglu_partial(u, wg_ref[:, sl], wu_ref[:, sl], wd_ref[sl, :])
        acc = y if acc is None else acc + y
    _pair_write(oc_ref, ol_ref, FFN_TM, _pair_read(xc_ref, xl_ref, FFN_TM) + m_ref[0, 5] * acc)


def _ffn(u, x_pair, mod_l, wg, wu, wd):
    tm = FFN_TM
    res = lambda shape: pl.BlockSpec(shape, lambda i: (0, 0), pipeline_mode=pl.Buffered(1))
    return pl.pallas_call(
        _ffn_kernel,
        grid=(N_TOK // tm,),
        in_specs=[pl.BlockSpec((tm, D_MODEL), lambda i: (i, 0))] + _pair_specs(tm) + [
            pl.BlockSpec((1, 6, 1, D_MODEL), lambda i: (_mod_row(i, tm), 0, 0, 0)),
            res((D_MODEL, FFN_DIM)), res((D_MODEL, FFN_DIM)), res((FFN_DIM, D_MODEL))],
        out_specs=_pair_specs(tm),
        out_shape=_pair_shapes(D_MODEL, F32),
        compiler_params=_cparams(("arbitrary",)),
        name="ffn",
    )(u, *x_pair, mod_l, wg, wu, wd)


MOE_TM = 512
MOE_NT = N_TOK // MOE_TM
MOE_PIECE = 16
MOE_SLAB = 2 * MOE_TM + N_EXPERTS * MOE_PIECE
MOE_BLK = 512
MOE_NB = -(-(2 * N_TOK + MOE_NT * N_EXPERTS * (MOE_PIECE - 1) + N_EXPERTS * (MOE_BLK - 1)) // MOE_BLK)
MOE_ROWS = MOE_NB * MOE_BLK
MOE_FCHUNK = 896


def _moe_count_kernel(c_ref, n_ref):
    n_ref[0] = jnp.sum(jnp.where(c_ref[...] > 0.0, 1.0, 0.0), axis=0, keepdims=True)


def _moe_count(comb):
    return pl.pallas_call(
        _moe_count_kernel,
        grid=(MOE_NT,),
        in_specs=[pl.BlockSpec((MOE_TM, 128), lambda i: (i, 0))],
        out_specs=pl.BlockSpec((1, 1, 128), lambda i: (i, 0, 0)),
        out_shape=jax.ShapeDtypeStruct((MOE_NT, 1, 128), F32),
        compiler_params=_cparams(("arbitrary",)),
        name="moe_count",
    )(comb)


def _moe_tables(counts):
    cnt = counts[:, 0, :N_EXPERTS].astype(jnp.int32)
    pc = (cnt + MOE_PIECE - 1) // MOE_PIECE * MOE_PIECE
    lo = jnp.cumsum(pc, axis=1) - pc
    tot = jnp.sum(pc, axis=0)
    totp = (tot + MOE_BLK - 1) // MOE_BLK * MOE_BLK
    goff = (jnp.cumsum(totp) - totp)[None, :] + jnp.cumsum(pc, axis=0) - pc
    ends = jnp.cumsum(totp // MOE_BLK)
    nact = ends[-1]
    s = jnp.minimum(jnp.arange(MOE_NB, dtype=jnp.int32), nact - 1)
    blk_e = jnp.minimum(jnp.sum((s[:, None] >= ends[None, :]).astype(jnp.int32), axis=1), N_EXPERTS - 1)
    lo_vec = jnp.pad(lo.astype(F32), ((0, 0), (0, 128 - N_EXPERTS))).reshape(MOE_NT, 1, 128)
    return dict(lo=lo.reshape(-1), npc=(pc // MOE_PIECE).reshape(-1), goff=goff.reshape(-1),
                nrow=jnp.sum(pc, axis=1), blk_e=blk_e, nact=nact.reshape(1), lo_vec=lo_vec)


def _moe_pieces(tile, lo_s, npc_s, goff_s, make, wait):
    for e in range(N_EXPERTS):
        k = tile * N_EXPERTS + e

        def body(p, carry, k=k):
            local = pl.multiple_of(lo_s[k] + p * MOE_PIECE, MOE_PIECE)
            glob = pl.multiple_of(goff_s[k] + p * MOE_PIECE, MOE_PIECE)
            cp = make(local, glob)
            if wait:
                cp.wait()
            else:
                cp.start()
            return carry

        lax.fori_loop(0, npc_s[k], body, 0)


def _slot_pair(slot):
    sel = slot >= 0.0
    s_hi = jnp.max(slot, axis=-1, keepdims=True)
    s_lo = jnp.min(jnp.where(sel, slot, float(MOE_SLAB)), axis=-1, keepdims=True)
    return sel, s_hi, s_lo


def _one_hot_rows(s):
    srow = lax.broadcasted_iota(jnp.int32, (MOE_TM, MOE_SLAB), 1).astype(F32)
    return jnp.where(srow == s, 1.0, 0.0)


def _moe_gather_kernel(lo_s, npc_s, goff_s, u_ref, c_ref, lov_ref, tri_ref, init_ref, xs_ref, slot_ref,
                       slab_ref, sem):
    del init_ref
    i = pl.program_id(0)
    routed = c_ref[...] > 0.0
    rank = _dot(tri_ref[...], jnp.where(routed, 1.0, 0.0).astype(BF16))
    slot = jnp.where(routed, lov_ref[0] + rank, -1.0)
    slot_ref[...] = slot
    _, s_hi, s_lo = _slot_pair(slot)
    pt = jnp.maximum(_one_hot_rows(s_hi), _one_hot_rows(s_lo)).astype(BF16)
    buf = i % 2
    slab_ref[buf] = _dot_tn(pt, u_ref[...]).astype(BF16)

    def pieces(tile, b, wait):
        _moe_pieces(tile, lo_s, npc_s, goff_s, lambda local, glob: pltpu.make_async_copy(
            slab_ref.at[b, pl.ds(local, MOE_PIECE), :], xs_ref.at[pl.ds(glob, MOE_PIECE), :], sem.at[b]), wait)

    pieces(i, buf, False)

    @pl.when(i > 0)
    def _():
        pieces(i - 1, 1 - buf, True)

    @pl.when(i == MOE_NT - 1)
    def _():
        pieces(i, buf, True)


def _moe_gather(u, comb, tab):
    tri = jnp.asarray(np.tril(np.ones((MOE_TM, MOE_TM), np.float32), -1), BF16)
    init = jnp.zeros((MOE_ROWS, D_MODEL), BF16)
    row = lambda i, *_: (i, 0)
    grid_spec = pltpu.PrefetchScalarGridSpec(
        num_scalar_prefetch=3,
        grid=(MOE_NT,),
        in_specs=[pl.BlockSpec((MOE_TM, D_MODEL), row), pl.BlockSpec((MOE_TM, 128), row),
                  pl.BlockSpec((1, 1, 128), lambda i, *_: (i, 0, 0)),
                  pl.BlockSpec((MOE_TM, MOE_TM), lambda i, *_: (0, 0)),
                  pl.BlockSpec(memory_space=pl.ANY)],
        out_specs=[pl.BlockSpec(memory_space=pl.ANY), pl.BlockSpec((MOE_TM, 128), row)],
        scratch_shapes=[pltpu.VMEM((2, MOE_SLAB, D_MODEL), BF16), pltpu.SemaphoreType.DMA((2,))],
    )
    return pl.pallas_call(
        _moe_gather_kernel,
        grid_spec=grid_spec,
        out_shape=[jax.ShapeDtypeStruct((MOE_ROWS, D_MODEL), BF16), jax.ShapeDtypeStruct((N_TOK, 128), F32)],
        input_output_aliases={7: 0},
        compiler_params=_cparams(("arbitrary",)),
        name="moe_gather",
    )(tab['lo'], tab['npc'], tab['goff'], u, comb, tab['lo_vec'], tri, init)


def _moe_expert_kernel(be_s, nact_s, x_ref, wg_ref, wu_ref, wd_ref, y_ref):
    del be_s
    active = pl.program_id(0) < nact_s[0]

    @pl.when(jnp.logical_not(active))
    def _():
        y_ref[...] = jnp.zeros_like(y_ref)

    @pl.when(active)
    def _():
        x = x_ref[...]
        acc = None
        for c in range(EXPERT_DIM // MOE_FCHUNK):
            sl = slice(c * MOE_FCHUNK, (c + 1) * MOE_FCHUNK)
            y = _swiglu_partial(x, wg_ref[0, :, sl], wu_ref[0, :, sl], wd_ref[0, sl, :])
            acc = y if acc is None else acc + y
        y_ref[...] = acc.astype(BF16)


def _moe_expert(xs, tab, wg, wu, wd):
    blk = lambda s, be, nact: (jnp.minimum(s, nact[0] - 1), 0)
    wspec = lambda shape: pl.BlockSpec(shape, lambda s, be, nact: (be[s], 0, 0), pipeline_mode=pl.Buffered(1))
    grid_spec = pltpu.PrefetchScalarGridSpec(
        num_scalar_prefetch=2,
        grid=(MOE_NB,),
        in_specs=[pl.BlockSpec((MOE_BLK, D_MODEL), blk),
                  wspec((1, D_MODEL, EXPERT_DIM)), wspec((1, D_MODEL, EXPERT_DIM)), wspec((1, EXPERT_DIM, D_MODEL))],
        out_specs=pl.BlockSpec((MOE_BLK, D_MODEL), lambda s, be, nact: (s, 0)),
    )
    return pl.pallas_call(
        _moe_expert_kernel,
        grid_spec=grid_spec,
        out_shape=jax.ShapeDtypeStruct((MOE_ROWS, D_MODEL), BF16),
        compiler_params=_cparams(("arbitrary",)),
        name="moe_expert",
    )(tab['blk_e'], tab['nact'], xs, wg, wu, wd)


def _moe_combine_kernel(lo_s, npc_s, goff_s, nrow_s, y_ref, slot_ref, c_ref, xc_ref, xl_ref, m_ref, g_ref,
                        oc_ref, ol_ref, slab_ref, sem):
    i = pl.program_id(0)
    buf = i % 2

    def pieces(tile, b, wait):
        _moe_pieces(tile, lo_s, npc_s, goff_s, lambda local, glob: pltpu.make_async_copy(
            y_ref.at[pl.ds(glob, MOE_PIECE), :], slab_ref.at[b, pl.ds(local, MOE_PIECE), :], sem.at[b]), wait)

    @pl.when(i == 0)
    def _():
        pieces(i, buf, False)

    @pl.when(i + 1 < MOE_NT)
    def _():
        pieces(i + 1, 1 - buf, False)

    slot = slot_ref[...]
    comb = c_ref[...]
    sel, s_hi, s_lo = _slot_pair(slot)
    w_hi = jnp.sum(jnp.where(sel & (slot == s_hi), comb, 0.0), axis=-1, keepdims=True)
    w_lo = jnp.sum(jnp.where(sel & (slot == s_lo), comb, 0.0), axis=-1, keepdims=True)
    w_lo = jnp.where(s_lo == s_hi, 0.0, w_lo)
    srow = lax.broadcasted_iota(jnp.int32, (MOE_SLAB, 1), 0)
    pieces(i, buf, True)
    ys = jnp.where(srow < nrow_s[i], slab_ref[buf], jnp.zeros((), BF16))
    out = (w_hi * _dot(_one_hot_rows(s_hi).astype(BF16), ys) + w_lo * _dot(_one_hot_rows(s_lo).astype(BF16), ys))
    x = _pair_read(xc_ref, xl_ref, MOE_TM)
    _pair_write(oc_ref, ol_ref, MOE_TM, _rms(x + m_ref[0, 5] * out, g_ref[...]))


def _moe_combine(y, slot, comb, x_pair, mod_l, final_g, tab):
    row = lambda i, *_: (i, 0)
    grid_spec = pltpu.PrefetchScalarGridSpec(
        num_scalar_prefetch=4,
        grid=(MOE_NT,),
        in_specs=[pl.BlockSpec(memory_space=pl.ANY),
                  pl.BlockSpec((MOE_TM, 128), row), pl.BlockSpec((MOE_TM, 128), row)] + _pair_specs(MOE_TM) + [
                  pl.BlockSpec((1, 6, 1, D_MODEL), lambda i, *_: (_mod_row(i, MOE_TM), 0, 0, 0)),
                  pl.BlockSpec((1, D_MODEL), lambda i, *_: (0, 0))],
        out_specs=_pair_specs(MOE_TM),
        scratch_shapes=[pltpu.VMEM((2, MOE_SLAB, D_MODEL), BF16), pltpu.SemaphoreType.DMA((2,))],
    )
    return pl.pallas_call(
        _moe_combine_kernel,
        grid_spec=grid_spec,
        out_shape=_pair_shapes(D_MODEL, F32),
        compiler_params=_cparams(("arbitrary",)),
        name="moe_combine",
    )(tab['lo'], tab['npc'], tab['goff'], tab['nrow'], y, slot, comb, *x_pair, mod_l, final_g)


def _moe(u, comb, x_pair, mod_l, final_g, wg, wu, wd):
    tab = _moe_tables(_moe_count(comb))
    xs, slot = _moe_gather(u, comb, tab)
    y = _moe_expert(xs, tab, wg, wu, wd)
    return _moe_combine(y, slot, comb, x_pair, mod_l, final_g, tab)


def _block_diag(w):
    w = w.reshape(2, C_WIDTH // 128, 2, C_BLOCK, C_BLOCK)
    z = jnp.zeros_like(w[:, :, 0])
    top = jnp.concatenate([w[:, :, 0], z], axis=-1)
    bot = jnp.concatenate([z, w[:, :, 1]], axis=-1)
    return jnp.concatenate([top, bot], axis=-2).astype(BF16)


def kernel(x_prompt, x_sample, cache_attn_k, cache_attn_v, state_hgrn, state_lru, c, c_ctx, w_mod, b_mod, norm_mix_g, norm_ffn_g, w_in, attn_sink, hgrn_lower, hgrn_norm_g, lru_conv_w, lru_conv_b, lru_wa, lru_ba, lru_wx, lru_bx, lru_lambda, w_branch_a, w_branch_b, w_branch_c, w_out, ffn_w_gate, ffn_w_up, ffn_w_down, moe_router, moe_w_gate, moe_w_up, moe_w_down, final_norm_g):
    assert DEPTH == 2
    x = (x_prompt.reshape(N_CTX_TOK, D_MODEL), x_sample.reshape(N_LAT_TOK, D_MODEL))
    cond =jnp.concatenate([c_ctx[None, :], c, jnp.zeros((N_COND - 1 - DEC_BATCH, D_MODEL), F32)], axis=0)
    mods = _modulation(cond, w_mod, b_mod).reshape(DEPTH, N_COND, 6, 1, D_MODEL)

    lb_cum = jnp.cumsum(jax.nn.softmax(hgrn_lower.astype(F32), axis=0), axis=0)
    lb_layers = lb_cum - lb_cum[:1]
    cos, sin = _rope_tables()
    hconst = _hgrn_constants(False) + _hgrn_constants(True)

    ks, vs, hs, ls = [], [], [], []
    for l in range(DEPTH):
        w_in_l = w_in[l].astype(BF16)
        lb = lb_layers[l]
        lbp = jnp.stack([jnp.log(lb), jnp.log1p(-lb), 1.0 - lb]).reshape(3, 2, 1, B_WIDTH)
        lp = dict(
            conv_w=lru_conv_w[l], conv_b=lru_conv_b[l].reshape(1, C_WIDTH),
            wa=_block_diag(lru_wa[l]), wx=_block_diag(lru_wx[l]),
            ba=lru_ba[l].reshape(2, 1, C_WIDTH), bx=lru_bx[l].reshape(2, 1, C_WIDTH),
            lam=lru_lambda[l].reshape(2, 1, C_WIDTH),
            w_ba=w_branch_a[l].astype(BF16), w_bb=w_branch_b[l].astype(BF16), w_bc=w_branch_c[l].astype(BF16),
            w_out=w_out[l].astype(BF16), norm_ffn=norm_ffn_g[l].reshape(1, D_MODEL))
        mod_l = mods[l]

        proj, gates = _in_proj(x, mod_l, norm_mix_g[l].reshape(1, D_MODEL), w_in_l)

        attn_c = _ctx_attention(proj, attn_sink[l])
        attn_l = _lat_attention(proj, attn_sink[l], cache_attn_k[:, l].reshape(DEC_BATCH, PAST_LEN, 128),
                                cache_attn_v[:, l].reshape(DEC_BATCH, PAST_LEN, 128), cos, sin)

        ng = hgrn_norm_g[l].reshape(1, B_KEY_DIM)
        hg_c, s_h = _hgrn(proj, lbp, ng, hconst, nbatch=BATCH, seq=SEQ, row_base=0, want_state=True)
        hg_l, = _hgrn(proj, lbp, ng, hconst, nbatch=DEC_BATCH, seq=DEC_SEQ, row_base=N_CTX_TOK,
                      state_in=state_hgrn[:, l])

        lr_c, s_l = _lru(proj, lp, nbatch=BATCH, seq=SEQ, row_base=0, reset=True, want_state=True)
        lr_l, = _lru(proj, lp, nbatch=DEC_BATCH, seq=DEC_SEQ, row_base=N_CTX_TOK, reset=False,
                     state_in=state_lru[:, l])
        mixers = (attn_c, attn_l, hg_c, hg_l, lr_c, lr_l)

        ks.append(proj[:N_CTX_TOK, KA_OFF:KA_OFF + 128].reshape(BATCH, SEQ, A_KV_HEADS, A_HEAD_DIM))
        vs.append(proj[:N_CTX_TOK, VA_OFF:VA_OFF + 128].reshape(BATCH, SEQ, A_KV_HEADS, A_HEAD_DIM))
        hs.append(s_h)
        ls.append(s_l)

        m = l // 2
        if l % 2 == 0:
            xc, xl, u = _merge(x, mod_l, gates, mixers, lp)
            x = _ffn(u, (xc, xl), mod_l, ffn_w_gate[m].astype(BF16), ffn_w_up[m].astype(BF16), ffn_w_down[m].astype(BF16))
        else:
            router = jnp.pad(moe_router[m], ((0, 0), (0, 128 - N_EXPERTS)))
            r_hi = router.astype(BF16)
            router = jnp.stack([r_hi, (router - r_hi.astype(F32)).astype(BF16)])
            xc, xl, u, comb = _merge(x, mod_l, gates, mixers, lp, router=router)
            x = _moe(u, comb, (xc, xl), mod_l, final_norm_g.reshape(1, D_MODEL), moe_w_gate[m].astype(BF16),
                     moe_w_up[m].astype(BF16), moe_w_down[m].astype(BF16))

    y_prompt = x[0].reshape(BATCH, SEQ, D_MODEL)
    y_sample = x[1].reshape(DEC_BATCH, DEC_SEQ, D_MODEL)
    return (y_prompt, y_sample, jnp.stack(ks, axis=1), jnp.stack(vs, axis=1), jnp.stack(hs, axis=1),
            jnp.stack(ls, axis=1))
```

```python
import functools

import numpy as np
import jax
import jax.numpy as jnp
from jax import lax
from jax.experimental import pallas as pl
from jax.experimental.pallas import tpu as pltpu

F32 = jnp.float32
BF16 = jnp.bfloat16
HIGHEST = lax.Precision.HIGHEST

D_MODEL = 1024
BATCH = 32
SEQ = 256
DEPTH = 2
DEC_BATCH = 4
DEC_SEQ = 2048
PAST_LEN = 512
GRID_W = 64
RMS_EPS = 1e-6
A_HEADS = 8
A_KV_HEADS = 2
A_GROUP = 4
A_HEAD_DIM = 64
A_WIDTH = 512
A_WINDOW = 128
ROPE_THETA = 10000.0
NEG_INF = -1e30
B_HEADS = 4
B_KEY_DIM = 128
B_WIDTH = 512
C_WIDTH = 512
C_HEADS = 8
C_BLOCK = 64
C_EXP = 8.0
FFN_DIM = 2816
N_EXPERTS = 8
EXPERT_DIM = 3584

N_CTX_TOK = BATCH * SEQ
N_LAT_TOK = DEC_BATCH * DEC_SEQ
N_TOK = N_CTX_TOK + N_LAT_TOK
N_COND = 8

QA_OFF = 0
KA_OFF = 512
VA_OFF = 640
B_OFF = 768
C_OFF = 3328
IN_COLS = 7424
IN_MIX_COLS = IN_COLS - 3 * D_MODEL

V7X_VMEM_LIMIT = 56 * 1024 * 1024
HG_CHUNK = 128
HG_LEVELS = (64, 32, 16, 8, 4, 2, 1)
HG_MAT_LEVELS = (4, 2)
HG_CHAINS = 16
LRU_SEGS = 8
LRU_PITCH_PAD = 4


def _cparams(sem, vmem=V7X_VMEM_LIMIT):
    return pltpu.CompilerParams(dimension_semantics=sem, vmem_limit_bytes=vmem)


def _mod_row(i, tm):
    nct = N_CTX_TOK // tm
    return jnp.where(i < nct, 0, 1 + (i - nct) // (DEC_SEQ // tm))


def _pair_specs(tm, cols=D_MODEL):
    nct = N_CTX_TOK // tm
    return [pl.BlockSpec((tm, cols), lambda i, *_: (jnp.minimum(i, nct - 1), 0)),
            pl.BlockSpec((tm, cols), lambda i, *_: (jnp.maximum(i - nct, 0), 0))]


def _pair_shapes(cols, dtype):
    return [jax.ShapeDtypeStruct((N_CTX_TOK, cols), dtype), jax.ShapeDtypeStruct((N_LAT_TOK, cols), dtype)]


def _pair_read(c_ref, l_ref, tm):
    return jnp.where(pl.program_id(0) < N_CTX_TOK // tm, c_ref[...], l_ref[...])


def _pair_write(c_ref, l_ref, tm, value):
    is_ctx = pl.program_id(0) < N_CTX_TOK // tm

    @pl.when(is_ctx)
    def _():
        c_ref[...] = value

    @pl.when(jnp.logical_not(is_ctx))
    def _():
        l_ref[...] = value


def _sigmoid(x):
    return 0.5 * jnp.tanh(0.5 * x) + 0.5


def _silu(x):
    return x * _sigmoid(x)


def _log_sigmoid(z):
    return jnp.minimum(z, 0.0) - jnp.log1p(jnp.exp(-jnp.abs(z)))


def _rms(x, g):
    ms = jnp.mean(x * x, axis=-1, keepdims=True)
    return x * lax.rsqrt(ms + RMS_EPS) * g


def _dot(a, b):
    return jnp.dot(a, b, preferred_element_type=F32)


def _dot_nt(a, b):
    return lax.dot_general(a, b, (((1,), (1,)), ((), ())), preferred_element_type=F32)


def _dot_tn(a, b):
    return lax.dot_general(a, b, (((0,), (0,)), ((), ())), preferred_element_type=F32)


def _mod_kernel(c_ref, w_ref, b_ref, o_ref):
    a = _silu(c_ref[...])
    o_ref[0] = jnp.dot(a, w_ref[0], preferred_element_type=F32, precision=HIGHEST) + b_ref[0]


def _modulation(cond, w_mod, b_mod):
    tn = 1536
    return pl.pallas_call(
        _mod_kernel,
        grid=(DEPTH, 6 * D_MODEL // tn),
        in_specs=[pl.BlockSpec((N_COND, D_MODEL), lambda l, j: (0, 0)),
                  pl.BlockSpec((1, D_MODEL, tn), lambda l, j: (l, 0, j)),
                  pl.BlockSpec((1, 1, tn), lambda l, j: (l, 0, j))],
        out_specs=pl.BlockSpec((1, N_COND, tn), lambda l, j: (l, 0, j)),
        out_shape=jax.ShapeDtypeStruct((DEPTH, N_COND, 6 * D_MODEL), F32),
        compiler_params=_cparams(("arbitrary", "arbitrary")),
        name="modulation",
    )(cond, w_mod, b_mod.reshape(DEPTH, 1, 6 * D_MODEL))


IN_TM = 256
IN_TN = 256


def _in_proj_kernel(xc_ref, xl_ref, m_ref, g_ref, w_ref, o_ref, gate_ref):
    u = _rms(_pair_read(xc_ref, xl_ref, IN_TM), g_ref[...]) * (1.0 + m_ref[0, 1]) + m_ref[0, 0]
    ub = u.astype(BF16)
    n_mix = IN_MIX_COLS // IN_TN
    for j in range(IN_COLS // IN_TN):
        y = _dot(ub, w_ref[:, j * IN_TN:(j + 1) * IN_TN])
        if j < n_mix:
            o_ref[:, j * IN_TN:(j + 1) * IN_TN] = y
        else:
            gate_ref[:, (j - n_mix) * IN_TN:(j - n_mix + 1) * IN_TN] = _sigmoid(y).astype(BF16)


def _in_proj(x_pair, mod_l, g, w):
    tm = IN_TM
    return pl.pallas_call(
        _in_proj_kernel,
        grid=(N_TOK // tm,),
        in_specs=_pair_specs(tm) + [
            pl.BlockSpec((1, 6, 1, D_MODEL), lambda i: (_mod_row(i, tm), 0, 0, 0)),
            pl.BlockSpec((1, D_MODEL), lambda i: (0, 0)),
            pl.BlockSpec((D_MODEL, IN_COLS), lambda i: (0, 0), pipeline_mode=pl.Buffered(1))],
        out_specs=[pl.BlockSpec((tm, IN_MIX_COLS), lambda i: (i, 0)),
                   pl.BlockSpec((tm, 3 * D_MODEL), lambda i: (i, 0))],
        out_shape=[jax.ShapeDtypeStruct((N_TOK, IN_MIX_COLS), F32),
                   jax.ShapeDtypeStruct((N_TOK, 3 * D_MODEL), BF16)],
        compiler_params=_cparams(("arbitrary",)),
        name="in_proj",
    )(*x_pair, mod_l, g, w)


LOG2E = 1.4426950408889634
A_LOGIT_SCALE = A_HEAD_DIM ** -0.5 * LOG2E


def _softmax_pv(s_list, v_list, sink):
    m = sink
    for s in s_list:
        m = jnp.maximum(m, jnp.max(s, axis=-1, keepdims=True))
    den = jnp.exp2(sink - m)
    out = None
    for s, v in zip(s_list, v_list):
        p = jnp.exp2(s - m)
        den = den + jnp.sum(p, axis=-1, keepdims=True)
        o = _dot(p.astype(BF16), v)
        out = o if out is None else out + o
    return out / den


def _sink_column(sink_ref, h, rows_per_head):
    n = A_GROUP * rows_per_head
    row = lax.broadcasted_iota(jnp.int32, (n, 1), 0)
    col = jnp.full((n, 1), sink_ref[A_GROUP * h + A_GROUP - 1], F32)
    for g in range(A_GROUP - 2, -1, -1):
        col = jnp.where(row < (g + 1) * rows_per_head, sink_ref[A_GROUP * h + g], col)
    return col


def _ctx_attn_kernel(sink_ref, q_ref, k_ref, v_ref, o_ref):
    q = q_ref[...] * A_LOGIT_SCALE
    k = k_ref[...]
    v = v_ref[...]
    outs = []
    for h in range(A_KV_HEADS):
        hs = slice(h * A_HEAD_DIM, (h + 1) * A_HEAD_DIM)
        kh = k[:, hs].astype(BF16)
        vh = v[:, hs].astype(BF16)
        qs = jnp.concatenate(
            [q[:, (A_GROUP * h + g) * A_HEAD_DIM:(A_GROUP * h + g + 1) * A_HEAD_DIM] for g in range(A_GROUP)],
            axis=0).astype(BF16)
        s = _dot_nt(qs, kh)
        o = _softmax_pv([s], [vh], _sink_column(sink_ref, h, SEQ) * LOG2E)
        outs.extend(o[g * SEQ:(g + 1) * SEQ] for g in range(A_GROUP))
    o_ref[...] = jnp.concatenate(outs, axis=-1).astype(BF16)


def _ctx_attention(proj, sink):
    return pl.pallas_call(
        _ctx_attn_kernel,
        grid=(BATCH,),
        in_specs=[pl.BlockSpec(memory_space=pltpu.SMEM),
                  pl.BlockSpec((SEQ, A_WIDTH), lambda b: (b, QA_OFF // A_WIDTH)),
                  pl.BlockSpec((SEQ, 128), lambda b: (b, KA_OFF // 128)),
                  pl.BlockSpec((SEQ, 128), lambda b: (b, VA_OFF // 128))],
        out_specs=pl.BlockSpec((SEQ, A_WIDTH), lambda b: (b, 0)),
        out_shape=jax.ShapeDtypeStruct((N_CTX_TOK, A_WIDTH), BF16),
        compiler_params=_cparams(("arbitrary",)),
        name="ctx_attention",
    )(sink, proj, proj, proj)


def _rope_tables():
    half = A_HEAD_DIM // 4
    inv = ROPE_THETA ** (-np.arange(half, dtype=np.float64) / half)
    t = np.arange(DEC_SEQ)
    row = (t // GRID_W)[:, None] * inv[None, :]
    col = (t % GRID_W)[:, None] * inv[None, :]
    cos = np.concatenate([np.cos(row), np.cos(row), np.cos(col), np.cos(col)], axis=1)
    sin = np.concatenate([-np.sin(row), np.sin(row), -np.sin(col), np.sin(col)], axis=1)
    return (jnp.asarray(np.tile(cos, (1, 2)), F32), jnp.asarray(np.tile(sin, (1, 2)), F32))


def _rope(x, cos, sin, first):
    part = jnp.where(first, pltpu.roll(x, 128 - 16, 1), pltpu.roll(x, 16, 1))
    return x * cos + part * sin


def _lat_attn_kernel(sink_ref, q_ref, km_ref, k0_ref, kp_ref, vm_ref, v0_ref, vp_ref, kc_ref, vc_ref,
                     cm_ref, c0_ref, cp_ref, sm_ref, s0_ref, sp_ref, o_ref):
    j = pl.program_id(1)
    blk = A_WINDOW
    lane = lax.broadcasted_iota(jnp.int32, (blk, 128), 1)
    first = (lane % 32) < 16
    c0 = c0_ref[...]
    s0 = s0_ref[...]
    q = q_ref[...] * A_LOGIT_SCALE
    qr = [_rope(q[:, c * 128:(c + 1) * 128], c0, s0, first) for c in range(A_WIDTH // 128)]
    kw = jnp.concatenate([_rope(km_ref[...], cm_ref[...], sm_ref[...], first),
                          _rope(k0_ref[...], c0, s0, first),
                          _rope(kp_ref[...], cp_ref[...], sp_ref[...], first)], axis=0)
    vw = jnp.concatenate([vm_ref[...], v0_ref[...], vp_ref[...]], axis=0)
    kc = kc_ref[0]
    vc = vc_ref[0]
    t = lax.broadcasted_iota(jnp.int32, (blk, 3 * blk), 0)
    s = lax.broadcasted_iota(jnp.int32, (blk, 3 * blk), 1) - blk
    kpos = j * blk + s
    keep = jnp.where((jnp.abs(s - t) <= A_WINDOW) & (kpos >= 0) & (kpos < DEC_SEQ), 1.0, 0.0)
    keep = jnp.concatenate([keep] * A_GROUP, axis=0) > 0.0
    outs = []
    for h in range(A_KV_HEADS):
        hs = slice(h * A_HEAD_DIM, (h + 1) * A_HEAD_DIM)
        heads = [A_GROUP * h + g for g in range(A_GROUP)]
        qs = jnp.concatenate(
            [qr[hd // 2][:, (hd % 2) * A_HEAD_DIM:(hd % 2 + 1) * A_HEAD_DIM] for hd in heads], axis=0).astype(BF16)
        sw = jnp.where(keep, _dot_nt(qs, kw[:, hs].astype(BF16)), NEG_INF)
        sc = _dot_nt(qs, kc[:, hs].astype(BF16))
        o = _softmax_pv([sw, sc], [vw[:, hs].astype(BF16), vc[:, hs].astype(BF16)],
                        _sink_column(sink_ref, h, blk) * LOG2E)
        outs.extend(o[g * blk:(g + 1) * blk] for g in range(A_GROUP))
    o_ref[...] = jnp.concatenate(outs, axis=-1).astype(BF16)


def _lat_attention(proj, sink, k_ctx, v_ctx, cos, sin):
    blk = A_WINDOW
    nb = DEC_SEQ // blk
    base = N_CTX_TOK // blk

    def rows(off):
        return lambda b, j: (base + b * nb + jnp.clip(j + off, 0, nb - 1))

    def kv_spec(col, off):
        r = rows(off)
        return pl.BlockSpec((blk, 128), lambda b, j: (r(b, j), col))

    def tab_spec(off):
        return pl.BlockSpec((blk, 128), lambda b, j: (jnp.clip(j + off, 0, nb - 1), 0))

    r0 = rows(0)
    ctx_spec = pl.BlockSpec((1, PAST_LEN, 128), lambda b, j: (b, 0, 0))
    return pl.pallas_call(
        _lat_attn_kernel,
        grid=(DEC_BATCH, nb),
        in_specs=[pl.BlockSpec(memory_space=pltpu.SMEM),
                  pl.BlockSpec((blk, A_WIDTH), lambda b, j: (r0(b, j), QA_OFF // A_WIDTH)),
                  kv_spec(KA_OFF // 128, -1), kv_spec(KA_OFF // 128, 0), kv_spec(KA_OFF // 128, 1),
                  kv_spec(VA_OFF // 128, -1), kv_spec(VA_OFF // 128, 0), kv_spec(VA_OFF // 128, 1),
                  ctx_spec, ctx_spec,
                  tab_spec(-1), tab_spec(0), tab_spec(1), tab_spec(-1), tab_spec(0), tab_spec(1)],
        out_specs=pl.BlockSpec((blk, A_WIDTH), lambda b, j: (b * nb + j, 0)),
        out_shape=jax.ShapeDtypeStruct((N_LAT_TOK, A_WIDTH), BF16),
        compiler_params=_cparams(("arbitrary", "arbitrary")),
        name="lat_attention",
    )(sink, proj, proj, proj, proj, proj, proj, proj, k_ctx, v_ctx, cos, cos, cos, sin, sin, sin)


def _hgrn_constants(reverse):
    L = HG_CHUNK
    t = np.arange(L)[:, None]
    s = np.arange(L)[None, :]
    tri = (s <= t).astype(np.float32)
    blocks = [tri]
    masks = []
    for b in HG_LEVELS:
        anchor = (t // (2 * b)) * 2 * b + b - 1
        right = np.where(t % (2 * b) >= b, 1.0, -1.0)
        if b in HG_MAT_LEVELS:
            blocks.append(right * (tri - (s <= anchor).astype(np.float32)))
        masks.append(((t // (2 * b) == s // (2 * b)) & (t % (2 * b) >= b) & (s % (2 * b) < b)).astype(np.float32))
    masks.append((t == s).astype(np.float32))
    if reverse:
        blocks = [m[::-1, ::-1] for m in blocks]
        masks = [m[::-1, ::-1] for m in masks]
    mat = np.concatenate(blocks, axis=0)
    return (jnp.asarray(np.concatenate([mat, mat], axis=1), BF16), jnp.asarray(np.stack(masks), F32))


def _hgrn_chunk(qraw, vv, z, lbp, st, mat, mask_ref, reverse):
    L = HG_CHUNK
    log_lb, log_1mlb, one_mlb = lbp
    q = qraw * _sigmoid(qraw) * (B_KEY_DIM ** -0.5)
    e = jnp.exp(-jnp.abs(z))
    r = 1.0 / (1.0 + e)
    k = one_mlb * jnp.where(z >= 0.0, e * r, r)
    a = log_lb
    b = log_1mlb + (jnp.minimum(z, 0.0) + jnp.log(r))
    lf = (jnp.maximum(a, b) + jnp.log(1.0 + jnp.exp(-jnp.abs(a - b)))) * LOG2E
    hi = lf.astype(BF16)
    lo = (lf - hi.astype(F32)).astype(BF16)
    g = _dot(mat, jnp.concatenate([hi, lo], axis=0))
    cum = g[0:L]
    tot = cum[0:1] if reverse else cum[L - 1:L]
    vb = vv.astype(BF16)
    o = _dot_nt((q * jnp.exp2(cum)).astype(BF16), st.astype(BF16))
    kd = (k * jnp.exp2(tot - cum)).astype(BF16)
    st_new = st * jnp.exp2(tot) + _dot_tn(vb, kd)
    nl = len(HG_LEVELS)
    row = lax.broadcasted_iota(jnp.int32, (L, 1), 0)
    if reverse:
        row = L - 1 - row
    terms = [(q.astype(BF16), k.astype(BF16), nl)]
    for li, bsz in enumerate(HG_LEVELS):
        is_query = (row & (2 * bsz - 1)) >= bsz
        if bsz in HG_MAT_LEVELS:
            j = 1 + HG_MAT_LEVELS.index(bsz)
            d = g[j * L:(j + 1) * L]
        elif bsz == 1:
            d = jnp.where(is_query, lf, 0.0)
        else:
            parts = []
            roles = []
            for base in range(0, L, 2 * bsz):
                lo, hi = slice(base, base + bsz), slice(base + bsz, base + 2 * bsz)
                if reverse:
                    anc = cum[base + bsz:base + bsz + 1]
                    parts += [cum[lo] - anc, anc - cum[hi]]
                    roles += [q[lo], k[hi]]
                else:
                    anc = cum[base + bsz - 1:base + bsz]
                    parts += [anc - cum[lo], cum[hi] - anc]
                    roles += [k[lo], q[hi]]
            zb = (jnp.concatenate(roles, axis=0) * jnp.exp2(jnp.concatenate(parts, axis=0))).astype(BF16)
            terms.append((zb, zb, li))
            continue
        zb = (jnp.where(is_query, q, k) * jnp.exp2(d)).astype(BF16)
        terms.append((zb, zb, li))
    zero = jnp.zeros((L, B_KEY_DIM), BF16)
    sc = None
    for (qa, ka, ma), (qb, kb, mb) in zip(terms[0::2], terms[1::2]):
        lhs = jnp.concatenate([qa, qb], axis=1)
        rhs = jnp.concatenate([jnp.concatenate([ka, zero], axis=1), jnp.concatenate([zero, kb], axis=1)], axis=0)
        s2 = _dot_nt(lhs, rhs)
        part = s2[:, 0:L] * mask_ref[ma] + s2[:, L:2 * L] * mask_ref[mb]
        sc = part if sc is None else sc + part
    o = o + _dot(sc.astype(BF16), vb)
    return o, st_new


def _hgrn(proj, lbp, norm_g, consts, *, nbatch, seq, row_base, state_in=None, want_state=False):
    mat_f, mask_f, mat_b, mask_b = consts
    nseq = max(1, HG_CHAINS * HG_CHUNK // (2 * seq))
    rows = nseq * seq
    rb = row_base // rows

    def col(k):
        return pl.BlockSpec((rows, 128), lambda b, h: (rb + b, (B_OFF + k * B_WIDTH) // 128 + h))

    const2 = lambda shape: pl.BlockSpec(shape, lambda b, h: (0,) * len(shape))
    in_specs = [col(0), col(1), col(2), col(3), col(4),
                pl.BlockSpec((3, 2, 1, 128), lambda b, h: (0, 0, 0, h)),
                const2((1, 128)),
                const2(mat_f.shape), const2(mat_b.shape), const2(mask_f.shape), const2(mask_b.shape)]
    args = [proj, proj, proj, proj, proj, lbp, norm_g, mat_f, mat_b, mask_f, mask_b]
    state_spec = pl.BlockSpec((nseq, 2, 1, 128, 128), lambda b, h: (b, 0, h, 0, 0))
    if state_in is not None:
        in_specs.append(state_spec)
        args.append(state_in)
    out_specs = [pl.BlockSpec((rows, 128), lambda b, h: (b, h))]
    out_shape = [jax.ShapeDtypeStruct((nbatch * seq, B_WIDTH), BF16)]
    if want_state:
        out_specs.append(state_spec)
        out_shape.append(jax.ShapeDtypeStruct((nbatch, 2, B_HEADS, 128, 128), F32))
    kern = functools.partial(_hgrn_kernel_flat, seq=seq, nseq=nseq, has_state_in=state_in is not None,
                             has_state_out=want_state)
    return pl.pallas_call(
        kern,
        grid=(nbatch // nseq, B_HEADS),
        in_specs=in_specs,
        out_specs=out_specs,
        out_shape=out_shape,
        scratch_shapes=[pltpu.VMEM((rows, 128), F32)],
        compiler_params=_cparams(("arbitrary", "arbitrary")),
        name="hgrn_%d" % seq,
    )(*args)


def _hgrn_kernel_flat(*refs, seq, nseq, has_state_in, has_state_out):
    refs = list(refs)
    head = refs[:11]
    pos = 11
    s0_ref = None
    if has_state_in:
        s0_ref = refs[pos]
        pos += 1
    o_ref = refs[pos]
    pos += 1
    s_ref = None
    if has_state_out:
        s_ref = refs[pos]
        pos += 1
    acc_ref = refs[pos]
    _hgrn_body(*head, s0_ref, o_ref, s_ref, acc_ref, seq=seq, nseq=nseq)


def _hgrn_body(q_ref, i_ref, zf_ref, zb_ref, gt_ref, lb_ref, ng_ref, mf_ref, mb_ref, kf_ref, kb_ref,
               s0_ref, o_ref, s_ref, acc_ref, *, seq, nseq):
    L = HG_CHUNK
    n = seq // L
    acc_ref[...] = jnp.zeros_like(acc_ref)
    lbf = (lb_ref[0, 0], lb_ref[1, 0], lb_ref[2, 0])
    lbb = (lb_ref[0, 1], lb_ref[1, 1], lb_ref[2, 1])
    states = []
    for j in range(nseq):
        for d in range(2):
            if s0_ref is not None:
                states.append(s0_ref[j, d, 0].T)
            else:
                states.append(jnp.zeros((B_KEY_DIM, B_KEY_DIM), F32))

    def body(i, carry):
        carry = list(carry)
        for j in range(nseq):
            sf = pl.ds(pl.multiple_of(j * seq + i * L, L), L)
            sb = pl.ds(pl.multiple_of(j * seq + (n - 1 - i) * L, L), L)
            o_f, carry[2 * j] = _hgrn_chunk(q_ref[sf, :], i_ref[sf, :], zf_ref[sf, :], lbf, carry[2 * j],
                                            mf_ref[...], kf_ref, False)
            acc_ref[sf, :] += o_f
            o_b, carry[2 * j + 1] = _hgrn_chunk(q_ref[sb, :], i_ref[sb, :], zb_ref[sb, :], lbb, carry[2 * j + 1],
                                                mb_ref[...], kb_ref, True)
            acc_ref[sb, :] += o_b
        return tuple(carry)

    states = lax.fori_loop(0, n, body, tuple(states), unroll=HG_CHAINS // (2 * nseq))
    o_ref[...] = (_rms(acc_ref[...], ng_ref[...]) * _silu(gt_ref[...])).astype(BF16)
    if s_ref is not None:
        for j in range(nseq):
            for d in range(2):
                s_ref[j, d, 0] = states[2 * j + d].T


def _gelu_tanh(y):
    return 0.5 * y * (1.0 + jnp.tanh(0.7978845608028654 * (y + 0.044715 * (y * y * y))))


def _lru_body(xc_ref, yc_ref, cw_ref, cb_ref, wa_ref, ba_ref, wx_ref, bx_ref, lam_ref, h0_ref, o_ref, s_ref,
              xs_ref, a_ref, u_ref, *, rows, reset, independent):
    T = rows
    seg = T // LRU_SEGS
    row = lax.broadcasted_iota(jnp.int32, (T, 1), 0)
    seq_len = seg if independent else T
    pos = row % seg if independent else row
    pad = 8
    zeros = jnp.zeros((pad, 128), F32)
    xs_ref[0:pad, :] = zeros
    xs_ref[pad:pad + T, :] = xc_ref[...]
    xs_ref[pad + T:2 * pad + T, :] = zeros
    cw = cw_ref[...]
    x = cb_ref[...]
    for tap in range(4):
        off = tap - 2
        term = cw[tap:tap + 1] * xs_ref[pad + off:pad + off + T, :]
        if independent and off != 0:
            term = jnp.where((pos + off >= 0) & (pos + off < seq_len), term, 0.0)
        x = x + term
    xb = x.astype(BF16)
    pitch = seg + LRU_PITCH_PAD
    for d in range(2):
        rg = _sigmoid(_dot(xb, wa_ref[d, 0]) + ba_ref[d])
        ig = _sigmoid(_dot(xb, wx_ref[d, 0]) + bx_ref[d])
        log_a = (C_EXP * rg) * _log_sigmoid(lam_ref[d])
        a = jnp.exp(log_a)
        mult = jnp.sqrt(1.0 - a * a)
        if reset:
            mult = jnp.where(pos == (0 if d == 0 else seq_len - 1), 1.0, mult)
        u = mult * ig * x
        for r in range(LRU_SEGS):
            a_ref[d, r * pitch:r * pitch + seg, :] = a[r * seg:(r + 1) * seg]
            u_ref[d, r * pitch:r * pitch + seg, :] = u[r * seg:(r + 1) * seg]
    sub =lax.broadcasted_iota(jnp.int32, (LRU_SEGS, 1), 0)
    zrow = jnp.zeros((1, 128), F32)
    if h0_ref is not None:
        hf0 = jnp.where(sub == 0, h0_ref[0, 0:1, :], 0.0)
        hb0 = jnp.where(sub == LRU_SEGS - 1, h0_ref[0, 1:2, :], 0.0)
    else:
        hf0 = jnp.zeros((LRU_SEGS, 128), F32)
        hb0 = hf0
    ones = jnp.ones((LRU_SEGS, 128), F32)

    def local_scan(i, carry):
        hf, pf, hb, pb = carry
        sf = pl.ds(i, LRU_SEGS, stride=pitch)
        sb = pl.ds(seg - 1 - i, LRU_SEGS, stride=pitch)
        af = a_ref[0, sf, :]
        hf = af * hf + u_ref[0, sf, :]
        pf = af * pf
        u_ref[0, sf, :] = hf
        a_ref[0, sf, :] = pf
        ab = a_ref[1, sb, :]
        hb = ab * hb + u_ref[1, sb, :]
        pb = ab * pb
        u_ref[1, sb, :] = hb
        a_ref[1, sb, :] = pb
        return hf, pf, hb, pb

    hf, pf, hb, pb = lax.fori_loop(0, seg, local_scan, (hf0, ones, hb0, ones), unroll=4)
    if independent:
        for r in range(LRU_SEGS):
            sl = slice(r * seg, (r + 1) * seg)
            ps = slice(r * pitch, r * pitch + seg)
            o_ref[sl, :] = ((u_ref[0, ps, :] + u_ref[1, ps, :]) * _gelu_tanh(yc_ref[sl, :])).astype(BF16)
        if s_ref is not None:
            s_ref[:, 0, :] = hf
            s_ref[:, 1, :] = hb
        return
    carry_f = [zrow] * LRU_SEGS
    carry_b = [zrow] * LRU_SEGS
    end_f = hf[0:1]
    for r in range(1, LRU_SEGS):
        carry_f[r] = end_f
        end_f = hf[r:r + 1] + pf[r:r + 1] * end_f
    end_b = hb[LRU_SEGS - 1:LRU_SEGS]
    for r in range(LRU_SEGS - 2, -1, -1):
        carry_b[r] = end_b
        end_b = hb[r:r + 1] + pb[r:r + 1] * end_b
    for r in range(LRU_SEGS):
        sl = slice(r * seg, (r + 1) * seg)
        ps = slice(r * pitch, r * pitch + seg)
        h = (u_ref[0, ps, :] + a_ref[0, ps, :] * carry_f[r]) + (u_ref[1, ps, :] + a_ref[1, ps, :] * carry_b[r])
        o_ref[sl, :] = (h * _gelu_tanh(yc_ref[sl, :])).astype(BF16)
    if s_ref is not None:
        s_ref[0, 0:1, :] = end_f
        s_ref[0, 1:2, :] = end_b


def _lru_kernel_flat(*refs, rows, reset, independent, has_state_in, has_state_out):
    refs = list(refs)
    head = refs[:9]
    pos = 9
    h0_ref = None
    if has_state_in:
        h0_ref = refs[pos]
        pos += 1
    o_ref = refs[pos]
    pos += 1
    s_ref = None
    if has_state_out:
        s_ref = refs[pos]
        pos += 1
    _lru_body(*head, h0_ref, o_ref, s_ref, *refs[pos:], rows=rows, reset=reset, independent=independent)


def _lru(proj, lp, *, nbatch, seq, row_base, reset, state_in=None, want_state=False):
    independent = seq < LRU_SEGS * 128
    per_step = LRU_SEGS if independent else 1
    rows = seq * per_step
    nstep = nbatch // per_step
    rb = row_base // rows
    nblk = C_WIDTH // 128

    def col(k):
        return pl.BlockSpec((rows, 128), lambda b, c: (rb + b, (C_OFF + k * C_WIDTH) // 128 + c))

    vec = lambda rows: pl.BlockSpec((rows, 1, 128), lambda b, c: (0, 0, c))
    mat = pl.BlockSpec((2, 1, 128, 128), lambda b, c: (0, c, 0, 0))
    in_specs = [col(0), col(1),
                pl.BlockSpec((4, 128), lambda b, c: (0, c)), pl.BlockSpec((1, 128), lambda b, c: (0, c)),
                mat, vec(2), mat, vec(2), vec(2)]
    args = [proj, proj, lp['conv_w'], lp['conv_b'], lp['wa'], lp['ba'], lp['wx'], lp['bx'], lp['lam']]
    if state_in is not None:
        in_specs.append(pl.BlockSpec((1, 2, 128), lambda b, c: (b, 0, c)))
        args.append(state_in)
    out_specs = [pl.BlockSpec((rows, 128), lambda b, c: (b, c))]
    out_shape = [jax.ShapeDtypeStruct((nbatch * seq, C_WIDTH), BF16)]
    if want_state:
        out_specs.append(pl.BlockSpec((per_step, 2, 128), lambda b, c: (b, 0, c)))
        out_shape.append(jax.ShapeDtypeStruct((nbatch, 2, C_WIDTH), F32))
    kern = functools.partial(_lru_kernel_flat, rows=rows, reset=reset, independent=independent,
                             has_state_in=state_in is not None, has_state_out=want_state)
    scan_rows = rows + LRU_SEGS * LRU_PITCH_PAD
    return pl.pallas_call(
        kern,
        grid=(nstep, nblk),
        in_specs=in_specs,
        out_specs=out_specs,
        out_shape=out_shape,
        scratch_shapes=[pltpu.VMEM((rows + 16, 128), F32), pltpu.VMEM((2, scan_rows, 128), F32),
                        pltpu.VMEM((2, scan_rows, 128), F32)],
        compiler_params=_cparams(("arbitrary", "arbitrary")),
        name="lru_%d" % seq,
    )(*args)


MERGE_TM = 512


def _top2_combine(logits):
    lane = lax.broadcasted_iota(jnp.int32, logits.shape, 1).astype(F32)
    lg = jnp.where(lane < N_EXPERTS, logits, -jnp.inf)
    m1 = jnp.max(lg, axis=-1, keepdims=True)
    i1 = jnp.min(jnp.where(lg == m1, lane, 128.0), axis=-1, keepdims=True)
    lg2 = jnp.where(lane == i1, -jnp.inf, lg)
    m2 = jnp.max(lg2, axis=-1, keepdims=True)
    i2 = jnp.min(jnp.where(lg2 == m2, lane, 128.0), axis=-1, keepdims=True)
    e = jnp.exp(m2 - m1)
    w1 = 1.0 / (1.0 + e)
    return jnp.where(lane == i1, w1, 0.0) + jnp.where(lane == i2, e * w1, 0.0)


def _merge_body(xc_ref, xl_ref, m_ref, z0_ref, z1_ref, z2_ref, ac_ref, al_ref, hc_ref, hl_ref, lc_ref, ll_ref,
                wba_ref, wbb_ref, wbc_ref, wo_ref, g_ref, r_ref, xoc_ref, xol_ref, u_ref, c_ref):
    tm = MERGE_TM
    mg = (z0_ref[...].astype(F32) * _dot(_pair_read(ac_ref, al_ref, tm), wba_ref[...])
          + z1_ref[...].astype(F32) * _dot(_pair_read(hc_ref, hl_ref, tm), wbb_ref[...])
          + z2_ref[...].astype(F32) * _dot(_pair_read(lc_ref, ll_ref, tm), wbc_ref[...]))
    x = _pair_read(xc_ref, xl_ref, tm) + m_ref[0, 2] * _dot(mg.astype(BF16), wo_ref[...])
    _pair_write(xoc_ref, xol_ref, tm, x)
    u =_rms(x, g_ref[...]) * (1.0 + m_ref[0, 4]) + m_ref[0, 3]
    ub = u.astype(BF16)
    u_ref[...] = ub
    if r_ref is not None:
        ul = (u - ub.astype(F32)).astype(BF16)
        logits = _dot(ub, r_ref[0]) + (_dot(ub, r_ref[1]) + _dot(ul, r_ref[0]))
        c_ref[...] = _top2_combine(logits)


def _merge_kernel_dense(*refs):
    _merge_body(*refs[:17], None, *refs[17:], None)


def _merge(x_pair, mod_l, gates, mixers, lp, router=None):
    tm = MERGE_TM
    row = lambda i: (i, 0)
    const = lambda i: (0, 0)
    res = lambda shape: pl.BlockSpec(shape, const, pipeline_mode=pl.Buffered(1))
    in_specs = _pair_specs(tm) + [
        pl.BlockSpec((1, 6, 1, D_MODEL), lambda i: (_mod_row(i, tm), 0, 0, 0)),
        pl.BlockSpec((tm, D_MODEL), lambda i: (i, 0)),
        pl.BlockSpec((tm, D_MODEL), lambda i: (i, 1)),
        pl.BlockSpec((tm, D_MODEL), lambda i: (i, 2))]
    in_specs += _pair_specs(tm, 512) * 3
    in_specs += [res((A_WIDTH, D_MODEL)), res((B_WIDTH, D_MODEL)), res((C_WIDTH, D_MODEL)), res((D_MODEL, D_MODEL)),
                 pl.BlockSpec((1, D_MODEL), const)]
    args = [*x_pair, mod_l, gates, gates, gates, *mixers, lp['w_ba'], lp['w_bb'], lp['w_bc'], lp['w_out'],
            lp['norm_ffn']]
    out_specs = _pair_specs(tm) + [pl.BlockSpec((tm, D_MODEL), row)]
    out_shape = _pair_shapes(D_MODEL, F32) + [jax.ShapeDtypeStruct((N_TOK, D_MODEL), BF16)]
    kern = _merge_kernel_dense
    if router is not None:
        in_specs.append(pl.BlockSpec((2, D_MODEL, 128), lambda i: (0, 0, 0)))
        args.append(router)
        out_specs.append(pl.BlockSpec((tm, 128), row))
        out_shape.append(jax.ShapeDtypeStruct((N_TOK, 128), F32))
        kern = _merge_body
    return pl.pallas_call(
        kern,
        grid=(N_TOK // tm,),
        in_specs=in_specs,
        out_specs=out_specs,
        out_shape=out_shape,
        compiler_params=_cparams(("arbitrary",)),
        name="merge",
    )(*args)


FFN_TM = 512
FFN_CHUNK = FFN_DIM // 2


def _swiglu_partial(u, wg, wu, wd):
    h = (_silu(_dot(u, wg)) * _dot(u, wu)).astype(BF16)
    return _dot(h, wd)


def _ffn_kernel(u_ref, xc_ref, xl_ref, m_ref, wg_ref, wu_ref, wd_ref, oc_ref, ol_ref):
    u = u_ref[...]
    acc = None
    for c in range(FFN_DIM // FFN_CHUNK):
        sl = slice(c * FFN_CHUNK, (c + 1) * FFN_CHUNK)
        y = _swiglu_partial(u, wg_ref[:, sl], wu_ref[:, sl], wd_ref[sl, :])
        acc = y if acc is None else acc + y
    _pair_write(oc_ref, ol_ref, FFN_TM, _pair_read(xc_ref, xl_ref, FFN_TM) + m_ref[0, 5] * acc)


def _ffn(u, x_pair, mod_l, wg, wu, wd):
    tm = FFN_TM
    res = lambda shape: pl.BlockSpec(shape, lambda i: (0, 0), pipeline_mode=pl.Buffered(1))
    return pl.pallas_call(
        _ffn_kernel,
        grid=(N_TOK // tm,),
        in_specs=[pl.BlockSpec((tm, D_MODEL), lambda i: (i, 0))] + _pair_specs(tm) + [
            pl.BlockSpec((1, 6, 1, D_MODEL), lambda i: (_mod_row(i, tm), 0, 0, 0)),
            res((D_MODEL, FFN_DIM)), res((D_MODEL, FFN_DIM)), res((FFN_DIM, D_MODEL))],
        out_specs=_pair_specs(tm),
        out_shape=_pair_shapes(D_MODEL, F32),
        compiler_params=_cparams(("arbitrary",)),
        name="ffn",
    )(u, *x_pair, mod_l, wg, wu, wd)


MOE_TM = 512
MOE_NT = N_TOK // MOE_TM
MOE_PIECE = 16
MOE_SLAB = 2 * MOE_TM + N_EXPERTS * MOE_PIECE
MOE_BLK = 512
MOE_NB = -(-(2 * N_TOK + MOE_NT * N_EXPERTS * (MOE_PIECE - 1) + N_EXPERTS * (MOE_BLK - 1)) // MOE_BLK)
MOE_ROWS = MOE_NB * MOE_BLK
MOE_FCHUNK = 896
MOE_WCHUNK = 512


def _moe_count_kernel(c_ref, n_ref):
    n_ref[0] = jnp.sum(jnp.where(c_ref[...] > 0.0, 1.0, 0.0), axis=0, keepdims=True)


def _moe_count(comb):
    return pl.pallas_call(
        _moe_count_kernel,
        grid=(MOE_NT,),
        in_specs=[pl.BlockSpec((MOE_TM, 128), lambda i: (i, 0))],
        out_specs=pl.BlockSpec((1, 1, 128), lambda i: (i, 0, 0)),
        out_shape=jax.ShapeDtypeStruct((MOE_NT, 1, 128), F32),
        compiler_params=_cparams(("arbitrary",)),
        name="moe_count",
    )(comb)


def _moe_tables(counts):
    cnt = counts[:, 0, :N_EXPERTS].astype(jnp.int32)
    pc = (cnt + MOE_PIECE - 1) // MOE_PIECE * MOE_PIECE
    lo = jnp.cumsum(pc, axis=1) - pc
    tot = jnp.sum(pc, axis=0)
    totp = (tot + MOE_BLK - 1) // MOE_BLK * MOE_BLK
    goff = (jnp.cumsum(totp) - totp)[None, :] + jnp.cumsum(pc, axis=0) - pc
    ends = jnp.cumsum(totp // MOE_BLK)
    nact = ends[-1]
    s = jnp.minimum(jnp.arange(MOE_NB, dtype=jnp.int32), nact - 1)
    blk_e = jnp.minimum(jnp.sum((s[:, None] >= ends[None, :]).astype(jnp.int32), axis=1), N_EXPERTS - 1)
    lo_vec = jnp.pad(lo.astype(F32), ((0, 0), (0, 128 - N_EXPERTS))).reshape(MOE_NT, 1, 128)
    return dict(lo=lo.reshape(-1), npc=(pc // MOE_PIECE).reshape(-1), goff=goff.reshape(-1),
                nrow=jnp.sum(pc, axis=1), blk_e=blk_e, nact=nact.reshape(1), lo_vec=lo_vec)


def _moe_pieces(tile, lo_s, npc_s, goff_s, make, wait):
    for e in range(N_EXPERTS):
        k = tile * N_EXPERTS + e

        def body(p, carry, k=k):
            local = pl.multiple_of(lo_s[k] + p * MOE_PIECE, MOE_PIECE)
            glob = pl.multiple_of(goff_s[k] + p * MOE_PIECE, MOE_PIECE)
            cp = make(local, glob)
            if wait:
                cp.wait()
            else:
                cp.start()
            return carry

        lax.fori_loop(0, npc_s[k], body, 0)


def _slot_pair(slot):
    sel = slot >= 0.0
    s_hi = jnp.max(slot, axis=-1, keepdims=True)
    s_lo = jnp.min(jnp.where(sel, slot, float(MOE_SLAB)), axis=-1, keepdims=True)
    return sel, s_hi, s_lo


def _one_hot_rows(s):
    srow = lax.broadcasted_iota(jnp.int32, (MOE_TM, MOE_SLAB), 1).astype(F32)
    return jnp.where(srow == s, 1.0, 0.0)


def _moe_gather_kernel(lo_s, npc_s, goff_s, u_ref, c_ref, lov_ref, tri_ref, init_ref, xs_ref, slot_ref,
                       slab_ref, sem):
    del init_ref
    i = pl.program_id(0)
    routed = c_ref[...] > 0.0
    rank = _dot(tri_ref[...], jnp.where(routed, 1.0, 0.0).astype(BF16))
    slot = jnp.where(routed, lov_ref[0] + rank, -1.0)
    slot_ref[...] = slot
    _, s_hi, s_lo = _slot_pair(slot)
    pt = jnp.maximum(_one_hot_rows(s_hi), _one_hot_rows(s_lo)).astype(BF16)
    buf = i % 2
    slab_ref[buf] = _dot_tn(pt, u_ref[...]).astype(BF16)

    def pieces(tile, b, wait):
        _moe_pieces(tile, lo_s, npc_s, goff_s, lambda local, glob: pltpu.make_async_copy(
            slab_ref.at[b, pl.ds(local, MOE_PIECE), :], xs_ref.at[pl.ds(glob, MOE_PIECE), :], sem.at[b]), wait)

    pieces(i, buf, False)

    @pl.when(i > 0)
    def _():
        pieces(i - 1, 1 - buf, True)

    @pl.when(i == MOE_NT - 1)
    def _():
        pieces(i, buf, True)


def _moe_gather(u, comb, tab):
    tri = jnp.asarray(np.tril(np.ones((MOE_TM, MOE_TM), np.float32), -1), BF16)
    init = jnp.zeros((MOE_ROWS, D_MODEL), BF16)
    row = lambda i, *_: (i, 0)
    grid_spec = pltpu.PrefetchScalarGridSpec(
        num_scalar_prefetch=3,
        grid=(MOE_NT,),
        in_specs=[pl.BlockSpec((MOE_TM, D_MODEL), row), pl.BlockSpec((MOE_TM, 128), row),
                  pl.BlockSpec((1, 1, 128), lambda i, *_: (i, 0, 0)),
                  pl.BlockSpec((MOE_TM, MOE_TM), lambda i, *_: (0, 0)),
                  pl.BlockSpec(memory_space=pl.ANY)],
        out_specs=[pl.BlockSpec(memory_space=pl.ANY), pl.BlockSpec((MOE_TM, 128), row)],
        scratch_shapes=[pltpu.VMEM((2, MOE_SLAB, D_MODEL), BF16), pltpu.SemaphoreType.DMA((2,))],
    )
    return pl.pallas_call(
        _moe_gather_kernel,
        grid_spec=grid_spec,
        out_shape=[jax.ShapeDtypeStruct((MOE_ROWS, D_MODEL), BF16), jax.ShapeDtypeStruct((N_TOK, 128), F32)],
        input_output_aliases={7: 0},
        compiler_params=_cparams(("arbitrary",)),
        name="moe_gather",
    )(tab['lo'], tab['npc'], tab['goff'], u, comb, tab['lo_vec'], tri, init)


def _moe_load_expert(e, wg_hbm, wu_hbm, wd_hbm, wg_ref, wu_ref, wd_ref, col_stage, row_stage, sem):
    nch = EXPERT_DIM // MOE_WCHUNK

    def piece(c):
        w, j = divmod(c, nch)
        sl = slice(j * MOE_WCHUNK, (j + 1) * MOE_WCHUNK)
        if w == 0:
            return wg_hbm.at[e, :, sl], col_stage.at[c % 2], wg_ref.at[:, sl]
        if w == 1:
            return wu_hbm.at[e, :, sl], col_stage.at[c % 2], wu_ref.at[:, sl]
        return wd_hbm.at[e, sl, :], row_stage.at[c % 2], wd_ref.at[sl, :]

    def copy(c):
        src, stage, _ = piece(c)
        return pltpu.make_async_copy(src, stage, sem.at[c % 2])

    copy(0).start()
    for c in range(3 * nch):
        if c + 1 < 3 * nch:
            copy(c + 1).start()
        copy(c).wait()
        _, stage, dst = piece(c)
        dst[...] = stage[...].astype(BF16)


def _moe_expert_kernel(be_s, nact_s, x_ref, wg_hbm, wu_hbm, wd_hbm, y_ref, wg_ref, wu_ref, wd_ref,
                       col_stage, row_stage, sem):
    s = pl.program_id(0)
    active = s < nact_s[0]
    e = be_s[s]
    first_of_expert = active & ((s == 0) | (e != be_s[jnp.maximum(s - 1, 0)]))

    @pl.when(first_of_expert)
    def _():
        _moe_load_expert(e, wg_hbm, wu_hbm, wd_hbm, wg_ref, wu_ref, wd_ref, col_stage, row_stage, sem)

    @pl.when(jnp.logical_not(active))
    def _():
        y_ref[...] = jnp.zeros_like(y_ref)

    @pl.when(active)
    def _():
        x = x_ref[...]
        acc = None
        for c in range(EXPERT_DIM // MOE_FCHUNK):
            sl = slice(c * MOE_FCHUNK, (c + 1) * MOE_FCHUNK)
            y = _swiglu_partial(x, wg_ref[:, sl], wu_ref[:, sl], wd_ref[sl, :])
            acc = y if acc is None else acc + y
        y_ref[...] = acc.astype(BF16)


def _moe_expert(xs, tab, wg, wu, wd):
    blk = lambda s, be, nact: (jnp.minimum(s, nact[0] - 1), 0)
    hbm = pl.BlockSpec(memory_space=pl.ANY)
    grid_spec = pltpu.PrefetchScalarGridSpec(
        num_scalar_prefetch=2,
        grid=(MOE_NB,),
        in_specs=[pl.BlockSpec((MOE_BLK, D_MODEL), blk), hbm, hbm, hbm],
        out_specs=pl.BlockSpec((MOE_BLK, D_MODEL), lambda s, be, nact: (s, 0)),
        scratch_shapes=[pltpu.VMEM((D_MODEL, EXPERT_DIM), BF16), pltpu.VMEM((D_MODEL, EXPERT_DIM), BF16),
                        pltpu.VMEM((EXPERT_DIM, D_MODEL), BF16),
                        pltpu.VMEM((2, D_MODEL, MOE_WCHUNK), F32), pltpu.VMEM((2, MOE_WCHUNK, D_MODEL), F32),
                        pltpu.SemaphoreType.DMA((2,))],
    )
    return pl.pallas_call(
        _moe_expert_kernel,
        grid_spec=grid_spec,
        out_shape=jax.ShapeDtypeStruct((MOE_ROWS, D_MODEL), BF16),
        compiler_params=_cparams(("arbitrary",)),
        name="moe_expert",
    )(tab['blk_e'], tab['nact'], xs, wg, wu, wd)


def _moe_combine_kernel(lo_s, npc_s, goff_s, nrow_s, y_ref, slot_ref, c_ref, xc_ref, xl_ref, m_ref, g_ref,
                        oc_ref, ol_ref, slab_ref, sem):
    i = pl.program_id(0)
    buf = i % 2

    def pieces(tile, b, wait):
        _moe_pieces(tile, lo_s, npc_s, goff_s, lambda local, glob: pltpu.make_async_copy(
            y_ref.at[pl.ds(glob, MOE_PIECE), :], slab_ref.at[b, pl.ds(local, MOE_PIECE), :], sem.at[b]), wait)

    @pl.when(i == 0)
    def _():
        pieces(i, buf, False)

    @pl.when(i + 1 < MOE_NT)
    def _():
        pieces(i + 1, 1 - buf, False)

    slot = slot_ref[...]
    comb = c_ref[...]
    sel, s_hi, s_lo = _slot_pair(slot)
    w_hi = jnp.sum(jnp.where(sel & (slot == s_hi), comb, 0.0), axis=-1, keepdims=True)
    w_lo = jnp.sum(jnp.where(sel & (slot == s_lo), comb, 0.0), axis=-1, keepdims=True)
    w_lo = jnp.where(s_lo == s_hi, 0.0, w_lo)
    srow = lax.broadcasted_iota(jnp.int32, (MOE_SLAB, 1), 0)
    pieces(i, buf, True)
    ys = jnp.where(srow < nrow_s[i], slab_ref[buf], jnp.zeros((), BF16))
    out = (w_hi * _dot(_one_hot_rows(s_hi).astype(BF16), ys) + w_lo * _dot(_one_hot_rows(s_lo).astype(BF16), ys))
    x = _pair_read(xc_ref, xl_ref, MOE_TM)
    _pair_write(oc_ref, ol_ref, MOE_TM, _rms(x + m_ref[0, 5] * out, g_ref[...]))


def _moe_combine(y, slot, comb, x_pair, mod_l, final_g, tab):
    row = lambda i, *_: (i, 0)
    grid_spec = pltpu.PrefetchScalarGridSpec(
        num_scalar_prefetch=4,
        grid=(MOE_NT,),
        in_specs=[pl.BlockSpec(memory_space=pl.ANY),
                  pl.BlockSpec((MOE_TM, 128), row), pl.BlockSpec((MOE_TM, 128), row)] + _pair_specs(MOE_TM) + [
                  pl.BlockSpec((1, 6, 1, D_MODEL), lambda i, *_: (_mod_row(i, MOE_TM), 0, 0, 0)),
                  pl.BlockSpec((1, D_MODEL), lambda i, *_: (0, 0))],
        out_specs=_pair_specs(MOE_TM),
        scratch_shapes=[pltpu.VMEM((2, MOE_SLAB, D_MODEL), BF16), pltpu.SemaphoreType.DMA((2,))],
    )
    return pl.pallas_call(
        _moe_combine_kernel,
        grid_spec=grid_spec,
        out_shape=_pair_shapes(D_MODEL, F32),
        compiler_params=_cparams(("arbitrary",)),
        name="moe_combine",
    )(tab['lo'], tab['npc'], tab['goff'], tab['nrow'], y, slot, comb, *x_pair, mod_l, final_g)


def _moe(u, comb, x_pair, mod_l, final_g, wg, wu, wd):
    tab = _moe_tables(_moe_count(comb))
    xs, slot = _moe_gather(u, comb, tab)
    y = _moe_expert(xs, tab, wg, wu, wd)
    return _moe_combine(y, slot, comb, x_pair, mod_l, final_g, tab)


def _block_diag(w):
    w = w.reshape(2, C_WIDTH // 128, 2, C_BLOCK, C_BLOCK)
    z = jnp.zeros_like(w[:, :, 0])
    top = jnp.concatenate([w[:, :, 0], z], axis=-1)
    bot = jnp.concatenate([z, w[:, :, 1]], axis=-1)
    return jnp.concatenate([top, bot], axis=-2).astype(BF16)


def kernel(x_prompt, x_sample, cache_attn_k, cache_attn_v, state_hgrn, state_lru, c, c_ctx, w_mod, b_mod, norm_mix_g, norm_ffn_g, w_in, attn_sink, hgrn_lower, hgrn_norm_g, lru_conv_w, lru_conv_b, lru_wa, lru_ba, lru_wx, lru_bx, lru_lambda, w_branch_a, w_branch_b, w_branch_c, w_out, ffn_w_gate, ffn_w_up, ffn_w_down, moe_router, moe_w_gate, moe_w_up, moe_w_down, final_norm_g):
    assert DEPTH == 2
    x = (x_prompt.reshape(N_CTX_TOK, D_MODEL), x_sample.reshape(N_LAT_TOK, D_MODEL))
    cond =jnp.concatenate([c_ctx[None, :], c, jnp.zeros((N_COND - 1 - DEC_BATCH, D_MODEL), F32)], axis=0)
    mods = _modulation(cond, w_mod, b_mod).reshape(DEPTH, N_COND, 6, 1, D_MODEL)

    lb_cum = jnp.cumsum(jax.nn.softmax(hgrn_lower.astype(F32), axis=0), axis=0)
    lb_layers = lb_cum - lb_cum[:1]
    cos, sin = _rope_tables()
    hconst = _hgrn_constants(False) + _hgrn_constants(True)

    ks, vs, hs, ls = [], [], [], []
    for l in range(DEPTH):
        w_in_l = w_in[l].astype(BF16)
        lb = lb_layers[l]
        lbp = jnp.stack([jnp.log(lb), jnp.log1p(-lb), 1.0 - lb]).reshape(3, 2, 1, B_WIDTH)
        lp = dict(
            conv_w=lru_conv_w[l], conv_b=lru_conv_b[l].reshape(1, C_WIDTH),
            wa=_block_diag(lru_wa[l]), wx=_block_diag(lru_wx[l]),
            ba=lru_ba[l].reshape(2, 1, C_WIDTH), bx=lru_bx[l].reshape(2, 1, C_WIDTH),
            lam=lru_lambda[l].reshape(2, 1, C_WIDTH),
            w_ba=w_branch_a[l].astype(BF16), w_bb=w_branch_b[l].astype(BF16), w_bc=w_branch_c[l].astype(BF16),
            w_out=w_out[l].astype(BF16), norm_ffn=norm_ffn_g[l].reshape(1, D_MODEL))
        mod_l = mods[l]

        proj, gates = _in_proj(x, mod_l, norm_mix_g[l].reshape(1, D_MODEL), w_in_l)

        attn_c = _ctx_attention(proj, attn_sink[l])
        attn_l = _lat_attention(proj, attn_sink[l], cache_attn_k[:, l].reshape(DEC_BATCH, PAST_LEN, 128),
                                cache_attn_v[:, l].reshape(DEC_BATCH, PAST_LEN, 128), cos, sin)

        ng = hgrn_norm_g[l].reshape(1, B_KEY_DIM)
        hg_c, s_h = _hgrn(proj, lbp, ng, hconst, nbatch=BATCH, seq=SEQ, row_base=0, want_state=True)
        hg_l, = _hgrn(proj, lbp, ng, hconst, nbatch=DEC_BATCH, seq=DEC_SEQ, row_base=N_CTX_TOK,
                      state_in=state_hgrn[:, l])

        lr_c, s_l = _lru(proj, lp, nbatch=BATCH, seq=SEQ, row_base=0, reset=True, want_state=True)
        lr_l, = _lru(proj, lp, nbatch=DEC_BATCH, seq=DEC_SEQ, row_base=N_CTX_TOK, reset=False,
                     state_in=state_lru[:, l])
        mixers = (attn_c, attn_l, hg_c, hg_l, lr_c, lr_l)

        ks.append(proj[:N_CTX_TOK, KA_OFF:KA_OFF + 128].reshape(BATCH, SEQ, A_KV_HEADS, A_HEAD_DIM))
        vs.append(proj[:N_CTX_TOK, VA_OFF:VA_OFF + 128].reshape(BATCH, SEQ, A_KV_HEADS, A_HEAD_DIM))
        hs.append(s_h)
        ls.append(s_l)

        m = l // 2
        if l % 2 == 0:
            xc, xl, u = _merge(x, mod_l, gates, mixers, lp)
            x = _ffn(u, (xc, xl), mod_l, ffn_w_gate[m].astype(BF16), ffn_w_up[m].astype(BF16), ffn_w_down[m].astype(BF16))
        else:
            router = jnp.pad(moe_router[m], ((0, 0), (0, 128 - N_EXPERTS)))
            r_hi = router.astype(BF16)
            router = jnp.stack([r_hi, (router - r_hi.astype(F32)).astype(BF16)])
            xc, xl, u, comb = _merge(x, mod_l, gates, mixers, lp, router=router)
            x = _moe(u, comb, (xc, xl), mod_l, final_norm_g.reshape(1, D_MODEL), moe_w_gate[m], moe_w_up[m],
                     moe_w_down[m])

    y_prompt = x[0].reshape(BATCH, SEQ, D_MODEL)
    y_sample = x[1].reshape(DEC_BATCH, DEC_SEQ, D_MODEL)
    return (y_prompt, y_sample, jnp.stack(ks, axis=1), jnp.stack(vs, axis=1), jnp.stack(hs, axis=1),
            jnp.stack(ls, axis=1))
```

```python
import functools

import numpy as np
import jax
import jax.numpy as jnp
from jax import lax
from jax.experimental import pallas as pl
from jax.experimental.pallas import tpu as pltpu

F32 = jnp.float32
BF16 = jnp.bfloat16
HIGHEST = lax.Precision.HIGHEST

D_MODEL = 1024
BATCH = 32
SEQ = 256
DEPTH = 2
DEC_BATCH = 4
DEC_SEQ = 2048
PAST_LEN = 512
GRID_W = 64
RMS_EPS = 1e-6
A_HEADS = 8
A_KV_HEADS = 2
A_GROUP = 4
A_HEAD_DIM = 64
A_WIDTH = 512
A_WINDOW = 128
ROPE_THETA = 10000.0
NEG_INF = -1e30
B_HEADS = 4
B_KEY_DIM = 128
B_WIDTH = 512
C_WIDTH = 512
C_HEADS = 8
C_BLOCK = 64
C_EXP = 8.0
FFN_DIM = 2816
N_EXPERTS = 8
EXPERT_DIM = 3584

N_CTX_TOK = BATCH * SEQ
N_LAT_TOK = DEC_BATCH * DEC_SEQ
N_TOK = N_CTX_TOK + N_LAT_TOK
N_COND = 8

QA_OFF = 0
KA_OFF = 512
VA_OFF = 640
B_OFF = 768
C_OFF = 3328
IN_COLS = 7424
IN_MIX_COLS = IN_COLS - 3 * D_MODEL

V7X_VMEM_LIMIT = 56 * 1024 * 1024
HG_CHUNK = 128
HG_LEVELS = (64, 32, 16, 8, 4, 2, 1)
HG_MAT_LEVELS = (4, 2)
HG_CHAINS = 16
LRU_SEGS = 8
LRU_PITCH_PAD = 4


def _cparams(sem, vmem=V7X_VMEM_LIMIT):
    return pltpu.CompilerParams(dimension_semantics=sem, vmem_limit_bytes=vmem)


def _mod_row(i, tm):
    nct = N_CTX_TOK // tm
    return jnp.where(i < nct, 0, 1 + (i - nct) // (DEC_SEQ // tm))


def _pair_specs(tm, cols=D_MODEL):
    nct = N_CTX_TOK // tm
    return [pl.BlockSpec((tm, cols), lambda i, *_: (jnp.minimum(i, nct - 1), 0)),
            pl.BlockSpec((tm, cols), lambda i, *_: (jnp.maximum(i - nct, 0), 0))]


def _pair_shapes(cols, dtype):
    return [jax.ShapeDtypeStruct((N_CTX_TOK, cols), dtype), jax.ShapeDtypeStruct((N_LAT_TOK, cols), dtype)]


def _pair_read(c_ref, l_ref, tm):
    return jnp.where(pl.program_id(0) < N_CTX_TOK // tm, c_ref[...], l_ref[...])


def _pair_write(c_ref, l_ref, tm, value):
    is_ctx = pl.program_id(0) < N_CTX_TOK // tm

    @pl.when(is_ctx)
    def _():
        c_ref[...] = value

    @pl.when(jnp.logical_not(is_ctx))
    def _():
        l_ref[...] = value


def _sigmoid(x):
    return 0.5 * jnp.tanh(0.5 * x) + 0.5


def _silu(x):
    return x * _sigmoid(x)


def _log_sigmoid(z):
    return jnp.minimum(z, 0.0) - jnp.log1p(jnp.exp(-jnp.abs(z)))


def _rms(x, g):
    ms = jnp.mean(x * x, axis=-1, keepdims=True)
    return x * lax.rsqrt(ms + RMS_EPS) * g


def _dot(a, b):
    return jnp.dot(a, b, preferred_element_type=F32)


def _dot_nt(a, b):
    return lax.dot_general(a, b, (((1,), (1,)), ((), ())), preferred_element_type=F32)


def _dot_tn(a, b):
    return lax.dot_general(a, b, (((0,), (0,)), ((), ())), preferred_element_type=F32)


def _mod_kernel(c_ref, w_ref, b_ref, o_ref):
    a = _silu(c_ref[...])
    o_ref[0] = jnp.dot(a, w_ref[0], preferred_element_type=F32, precision=HIGHEST) + b_ref[0]


def _modulation(cond, w_mod, b_mod):
    tn = 1536
    return pl.pallas_call(
        _mod_kernel,
        grid=(DEPTH, 6 * D_MODEL // tn),
        in_specs=[pl.BlockSpec((N_COND, D_MODEL), lambda l, j: (0, 0)),
                  pl.BlockSpec((1, D_MODEL, tn), lambda l, j: (l, 0, j)),
                  pl.BlockSpec((1, 1, tn), lambda l, j: (l, 0, j))],
        out_specs=pl.BlockSpec((1, N_COND, tn), lambda l, j: (l, 0, j)),
        out_shape=jax.ShapeDtypeStruct((DEPTH, N_COND, 6 * D_MODEL), F32),
        compiler_params=_cparams(("arbitrary", "arbitrary")),
        name="modulation",
    )(cond, w_mod, b_mod.reshape(DEPTH, 1, 6 * D_MODEL))


IN_TM = 256
IN_TN = 256


def _in_proj_kernel(xc_ref, xl_ref, m_ref, g_ref, w_ref, o_ref, gate_ref):
    u = _rms(_pair_read(xc_ref, xl_ref, IN_TM), g_ref[...]) * (1.0 + m_ref[0, 1]) + m_ref[0, 0]
    ub = u.astype(BF16)
    n_mix = IN_MIX_COLS // IN_TN
    for j in range(IN_COLS // IN_TN):
        y = _dot(ub, w_ref[:, j * IN_TN:(j + 1) * IN_TN])
        if j < n_mix:
            o_ref[:, j * IN_TN:(j + 1) * IN_TN] = y
        else:
            gate_ref[:, (j - n_mix) * IN_TN:(j - n_mix + 1) * IN_TN] = _sigmoid(y).astype(BF16)


def _in_proj(x_pair, mod_l, g, w):
    tm = IN_TM
    return pl.pallas_call(
        _in_proj_kernel,
        grid=(N_TOK // tm,),
        in_specs=_pair_specs(tm) + [
            pl.BlockSpec((1, 6, 1, D_MODEL), lambda i: (_mod_row(i, tm), 0, 0, 0)),
            pl.BlockSpec((1, D_MODEL), lambda i: (0, 0)),
            pl.BlockSpec((D_MODEL, IN_COLS), lambda i: (0, 0), pipeline_mode=pl.Buffered(1))],
        out_specs=[pl.BlockSpec((tm, IN_MIX_COLS), lambda i: (i, 0)),
                   pl.BlockSpec((tm, 3 * D_MODEL), lambda i: (i, 0))],
        out_shape=[jax.ShapeDtypeStruct((N_TOK, IN_MIX_COLS), F32),
                   jax.ShapeDtypeStruct((N_TOK, 3 * D_MODEL), BF16)],
        compiler_params=_cparams(("arbitrary",)),
        name="in_proj",
    )(*x_pair, mod_l, g, w)


LOG2E = 1.4426950408889634
A_LOGIT_SCALE = A_HEAD_DIM ** -0.5 * LOG2E


def _softmax_pv(s_list, v_list, sink):
    m = sink
    for s in s_list:
        m = jnp.maximum(m, jnp.max(s, axis=-1, keepdims=True))
    den = jnp.exp2(sink - m)
    out = None
    for s, v in zip(s_list, v_list):
        p = jnp.exp2(s - m)
        den = den + jnp.sum(p, axis=-1, keepdims=True)
        o = _dot(p.astype(BF16), v)
        out = o if out is None else out + o
    return out / den


def _sink_column(sink_ref, h, rows_per_head):
    n = A_GROUP * rows_per_head
    row = lax.broadcasted_iota(jnp.int32, (n, 1), 0)
    col = jnp.full((n, 1), sink_ref[A_GROUP * h + A_GROUP - 1], F32)
    for g in range(A_GROUP - 2, -1, -1):
        col = jnp.where(row < (g + 1) * rows_per_head, sink_ref[A_GROUP * h + g], col)
    return col


def _ctx_attn_kernel(sink_ref, q_ref, k_ref, v_ref, o_ref):
    q = q_ref[...] * A_LOGIT_SCALE
    k = k_ref[...]
    v = v_ref[...]
    outs = []
    for h in range(A_KV_HEADS):
        hs = slice(h * A_HEAD_DIM, (h + 1) * A_HEAD_DIM)
        kh = k[:, hs].astype(BF16)
        vh = v[:, hs].astype(BF16)
        qs = jnp.concatenate(
            [q[:, (A_GROUP * h + g) * A_HEAD_DIM:(A_GROUP * h + g + 1) * A_HEAD_DIM] for g in range(A_GROUP)],
            axis=0).astype(BF16)
        s = _dot_nt(qs, kh)
        o = _softmax_pv([s], [vh], _sink_column(sink_ref, h, SEQ) * LOG2E)
        outs.extend(o[g * SEQ:(g + 1) * SEQ] for g in range(A_GROUP))
    o_ref[...] = jnp.concatenate(outs, axis=-1).astype(BF16)


def _ctx_attention(proj, sink):
    return pl.pallas_call(
        _ctx_attn_kernel,
        grid=(BATCH,),
        in_specs=[pl.BlockSpec(memory_space=pltpu.SMEM),
                  pl.BlockSpec((SEQ, A_WIDTH), lambda b: (b, QA_OFF // A_WIDTH)),
                  pl.BlockSpec((SEQ, 128), lambda b: (b, KA_OFF // 128)),
                  pl.BlockSpec((SEQ, 128), lambda b: (b, VA_OFF // 128))],
        out_specs=pl.BlockSpec((SEQ, A_WIDTH), lambda b: (b, 0)),
        out_shape=jax.ShapeDtypeStruct((N_CTX_TOK, A_WIDTH), BF16),
        compiler_params=_cparams(("arbitrary",)),
        name="ctx_attention",
    )(sink, proj, proj, proj)


def _rope_tables():
    half = A_HEAD_DIM // 4
    inv = ROPE_THETA ** (-np.arange(half, dtype=np.float64) / half)
    t = np.arange(DEC_SEQ)
    row = (t // GRID_W)[:, None] * inv[None, :]
    col = (t % GRID_W)[:, None] * inv[None, :]
    cos = np.concatenate([np.cos(row), np.cos(row), np.cos(col), np.cos(col)], axis=1)
    sin = np.concatenate([-np.sin(row), np.sin(row), -np.sin(col), np.sin(col)], axis=1)
    return (jnp.asarray(np.tile(cos, (1, 2)), F32), jnp.asarray(np.tile(sin, (1, 2)), F32))


def _rope(x, cos, sin, first):
    part = jnp.where(first, pltpu.roll(x, 128 - 16, 1), pltpu.roll(x, 16, 1))
    return x * cos + part * sin


def _lat_attn_kernel(sink_ref, q_ref, km_ref, k0_ref, kp_ref, vm_ref, v0_ref, vp_ref, kc_ref, vc_ref,
                     cm_ref, c0_ref, cp_ref, sm_ref, s0_ref, sp_ref, o_ref):
    j = pl.program_id(1)
    blk = A_WINDOW
    lane = lax.broadcasted_iota(jnp.int32, (blk, 128), 1)
    first = (lane % 32) < 16
    c0 = c0_ref[...]
    s0 = s0_ref[...]
    q = q_ref[...] * A_LOGIT_SCALE
    qr = [_rope(q[:, c * 128:(c + 1) * 128], c0, s0, first) for c in range(A_WIDTH // 128)]
    kw = jnp.concatenate([_rope(km_ref[...], cm_ref[...], sm_ref[...], first),
                          _rope(k0_ref[...], c0, s0, first),
                          _rope(kp_ref[...], cp_ref[...], sp_ref[...], first)], axis=0)
    vw = jnp.concatenate([vm_ref[...], v0_ref[...], vp_ref[...]], axis=0)
    kc = kc_ref[0]
    vc = vc_ref[0]
    t = lax.broadcasted_iota(jnp.int32, (blk, 3 * blk), 0)
    s = lax.broadcasted_iota(jnp.int32, (blk, 3 * blk), 1) - blk
    kpos = j * blk + s
    keep = jnp.where((jnp.abs(s - t) <= A_WINDOW) & (kpos >= 0) & (kpos < DEC_SEQ), 1.0, 0.0)
    keep = jnp.concatenate([keep] * A_GROUP, axis=0) > 0.0
    outs = []
    for h in range(A_KV_HEADS):
        hs = slice(h * A_HEAD_DIM, (h + 1) * A_HEAD_DIM)
        heads = [A_GROUP * h + g for g in range(A_GROUP)]
        qs = jnp.concatenate(
            [qr[hd // 2][:, (hd % 2) * A_HEAD_DIM:(hd % 2 + 1) * A_HEAD_DIM] for hd in heads], axis=0).astype(BF16)
        sw = jnp.where(keep, _dot_nt(qs, kw[:, hs].astype(BF16)), NEG_INF)
        sc = _dot_nt(qs, kc[:, hs].astype(BF16))
        o = _softmax_pv([sw, sc], [vw[:, hs].astype(BF16), vc[:, hs].astype(BF16)],
                        _sink_column(sink_ref, h, blk) * LOG2E)
        outs.extend(o[g * blk:(g + 1) * blk] for g in range(A_GROUP))
    o_ref[...] = jnp.concatenate(outs, axis=-1).astype(BF16)


def _lat_attention(proj, sink, k_ctx, v_ctx, cos, sin):
    blk = A_WINDOW
    nb = DEC_SEQ // blk
    base = N_CTX_TOK // blk

    def rows(off):
        return lambda b, j: (base + b * nb + jnp.clip(j + off, 0, nb - 1))

    def kv_spec(col, off):
        r = rows(off)
        return pl.BlockSpec((blk, 128), lambda b, j: (r(b, j), col))

    def tab_spec(off):
        return pl.BlockSpec((blk, 128), lambda b, j: (jnp.clip(j + off, 0, nb - 1), 0))

    r0 = rows(0)
    ctx_spec = pl.BlockSpec((1, PAST_LEN, 128), lambda b, j: (b, 0, 0))
    return pl.pallas_call(
        _lat_attn_kernel,
        grid=(DEC_BATCH, nb),
        in_specs=[pl.BlockSpec(memory_space=pltpu.SMEM),
                  pl.BlockSpec((blk, A_WIDTH), lambda b, j: (r0(b, j), QA_OFF // A_WIDTH)),
                  kv_spec(KA_OFF // 128, -1), kv_spec(KA_OFF // 128, 0), kv_spec(KA_OFF // 128, 1),
                  kv_spec(VA_OFF // 128, -1), kv_spec(VA_OFF // 128, 0), kv_spec(VA_OFF // 128, 1),
                  ctx_spec, ctx_spec,
                  tab_spec(-1), tab_spec(0), tab_spec(1), tab_spec(-1), tab_spec(0), tab_spec(1)],
        out_specs=pl.BlockSpec((blk, A_WIDTH), lambda b, j: (b * nb + j, 0)),
        out_shape=jax.ShapeDtypeStruct((N_LAT_TOK, A_WIDTH), BF16),
        compiler_params=_cparams(("arbitrary", "arbitrary")),
        name="lat_attention",
    )(sink, proj, proj, proj, proj, proj, proj, proj, k_ctx, v_ctx, cos, cos, cos, sin, sin, sin)


def _hgrn_constants(reverse):
    L = HG_CHUNK
    t = np.arange(L)[:, None]
    s = np.arange(L)[None, :]
    tri = (s <= t).astype(np.float32)
    blocks = [tri]
    masks = []
    for b in HG_LEVELS:
        anchor = (t // (2 * b)) * 2 * b + b - 1
        right = np.where(t % (2 * b) >= b, 1.0, -1.0)
        if b in HG_MAT_LEVELS:
            blocks.append(right * (tri - (s <= anchor).astype(np.float32)))
        masks.append(((t // (2 * b) == s // (2 * b)) & (t % (2 * b) >= b) & (s % (2 * b) < b)).astype(np.float32))
    masks.append((t == s).astype(np.float32))
    if reverse:
        blocks = [m[::-1, ::-1] for m in blocks]
        masks = [m[::-1, ::-1] for m in masks]
    mat = np.concatenate(blocks, axis=0)
    return (jnp.asarray(np.concatenate([mat, mat], axis=1), BF16), jnp.asarray(np.stack(masks), F32))


def _hgrn_chunk(qraw, vv, z, lbp, st, mat, mask_ref, reverse):
    L = HG_CHUNK
    log_lb, log_1mlb, one_mlb = lbp
    q = qraw * _sigmoid(qraw) * (B_KEY_DIM ** -0.5)
    e = jnp.exp(-jnp.abs(z))
    r = 1.0 / (1.0 + e)
    k = one_mlb * jnp.where(z >= 0.0, e * r, r)
    a = log_lb
    b = log_1mlb + (jnp.minimum(z, 0.0) + jnp.log(r))
    lf = (jnp.maximum(a, b) + jnp.log(1.0 + jnp.exp(-jnp.abs(a - b)))) * LOG2E
    hi = lf.astype(BF16)
    lo = (lf - hi.astype(F32)).astype(BF16)
    g = _dot(mat, jnp.concatenate([hi, lo], axis=0))
    cum = g[0:L]
    tot = cum[0:1] if reverse else cum[L - 1:L]
    vb = vv.astype(BF16)
    o = _dot_nt((q * jnp.exp2(cum)).astype(BF16), st.astype(BF16))
    kd = (k * jnp.exp2(tot - cum)).astype(BF16)
    st_new = st * jnp.exp2(tot) + _dot_tn(vb, kd)
    nl = len(HG_LEVELS)
    row = lax.broadcasted_iota(jnp.int32, (L, 1), 0)
    if reverse:
        row = L - 1 - row
    terms = [(q.astype(BF16), k.astype(BF16), nl)]
    for li, bsz in enumerate(HG_LEVELS):
        is_query = (row & (2 * bsz - 1)) >= bsz
        if bsz in HG_MAT_LEVELS:
            j = 1 + HG_MAT_LEVELS.index(bsz)
            d = g[j * L:(j + 1) * L]
        elif bsz == 1:
            d = jnp.where(is_query, lf, 0.0)
        else:
            parts = []
            roles = []
            for base in range(0, L, 2 * bsz):
                lo, hi = slice(base, base + bsz), slice(base + bsz, base + 2 * bsz)
                if reverse:
                    anc = cum[base + bsz:base + bsz + 1]
                    parts += [cum[lo] - anc, anc - cum[hi]]
                    roles += [q[lo], k[hi]]
                else:
                    anc = cum[base + bsz - 1:base + bsz]
                    parts += [anc - cum[lo], cum[hi] - anc]
                    roles += [k[lo], q[hi]]
            zb = (jnp.concatenate(roles, axis=0) * jnp.exp2(jnp.concatenate(parts, axis=0))).astype(BF16)
            terms.append((zb, zb, li))
            continue
        zb = (jnp.where(is_query, q, k) * jnp.exp2(d)).astype(BF16)
        terms.append((zb, zb, li))
    zero = jnp.zeros((L, B_KEY_DIM), BF16)
    sc = None
    for (qa, ka, ma), (qb, kb, mb) in zip(terms[0::2], terms[1::2]):
        lhs = jnp.concatenate([qa, qb], axis=1)
        rhs = jnp.concatenate([jnp.concatenate([ka, zero], axis=1), jnp.concatenate([zero, kb], axis=1)], axis=0)
        s2 = _dot_nt(lhs, rhs)
        part = s2[:, 0:L] * mask_ref[ma] + s2[:, L:2 * L] * mask_ref[mb]
        sc = part if sc is None else sc + part
    o = o + _dot(sc.astype(BF16), vb)
    return o, st_new


def _hgrn(proj, lbp, norm_g, consts, *, nbatch, seq, row_base, state_in=None, want_state=False):
    mat_f, mask_f, mat_b, mask_b = consts
    nseq = max(1, HG_CHAINS * HG_CHUNK // (2 * seq))
    rows = nseq * seq
    rb = row_base // rows

    def col(k):
        return pl.BlockSpec((rows, 128), lambda b, h: (rb + b, (B_OFF + k * B_WIDTH) // 128 + h))

    const2 = lambda shape: pl.BlockSpec(shape, lambda b, h: (0,) * len(shape))
    in_specs = [col(0), col(1), col(2), col(3), col(4),
                pl.BlockSpec((3, 2, 1, 128), lambda b, h: (0, 0, 0, h)),
                const2((1, 128)),
                const2(mat_f.shape), const2(mat_b.shape), const2(mask_f.shape), const2(mask_b.shape)]
    args = [proj, proj, proj, proj, proj, lbp, norm_g, mat_f, mat_b, mask_f, mask_b]
    state_spec = pl.BlockSpec((nseq, 2, 1, 128, 128), lambda b, h: (b, 0, h, 0, 0))
    if state_in is not None:
        in_specs.append(state_spec)
        args.append(state_in)
    out_specs = [pl.BlockSpec((rows, 128), lambda b, h: (b, h))]
    out_shape = [jax.ShapeDtypeStruct((nbatch * seq, B_WIDTH), BF16)]
    if want_state:
        out_specs.append(state_spec)
        out_shape.append(jax.ShapeDtypeStruct((nbatch, 2, B_HEADS, 128, 128), F32))
    kern = functools.partial(_hgrn_kernel_flat, seq=seq, nseq=nseq, has_state_in=state_in is not None,
                             has_state_out=want_state)
    return pl.pallas_call(
        kern,
        grid=(nbatch // nseq, B_HEADS),
        in_specs=in_specs,
        out_specs=out_specs,
        out_shape=out_shape,
        scratch_shapes=[pltpu.VMEM((rows, 128), F32)],
        compiler_params=_cparams(("arbitrary", "arbitrary")),
        name="hgrn_%d" % seq,
    )(*args)


def _hgrn_kernel_flat(*refs, seq, nseq, has_state_in, has_state_out):
    refs = list(refs)
    head = refs[:11]
    pos = 11
    s0_ref = None
    if has_state_in:
        s0_ref = refs[pos]
        pos += 1
    o_ref = refs[pos]
    pos += 1
    s_ref = None
    if has_state_out:
        s_ref = refs[pos]
        pos += 1
    acc_ref = refs[pos]
    _hgrn_body(*head, s0_ref, o_ref, s_ref, acc_ref, seq=seq, nseq=nseq)


def _hgrn_body(q_ref, i_ref, zf_ref, zb_ref, gt_ref, lb_ref, ng_ref, mf_ref, mb_ref, kf_ref, kb_ref,
               s0_ref, o_ref, s_ref, acc_ref, *, seq, nseq):
    L = HG_CHUNK
    n = seq // L
    acc_ref[...] = jnp.zeros_like(acc_ref)
    lbf = (lb_ref[0, 0], lb_ref[1, 0], lb_ref[2, 0])
    lbb = (lb_ref[0, 1], lb_ref[1, 1], lb_ref[2, 1])
    states = []
    for j in range(nseq):
        for d in range(2):
            if s0_ref is not None:
                states.append(s0_ref[j, d, 0].T)
            else:
                states.append(jnp.zeros((B_KEY_DIM, B_KEY_DIM), F32))

    def body(i, carry):
        carry = list(carry)
        for j in range(nseq):
            sf = pl.ds(pl.multiple_of(j * seq + i * L, L), L)
            sb = pl.ds(pl.multiple_of(j * seq + (n - 1 - i) * L, L), L)
            o_f, carry[2 * j] = _hgrn_chunk(q_ref[sf, :], i_ref[sf, :], zf_ref[sf, :], lbf, carry[2 * j],
                                            mf_ref[...], kf_ref, False)
            acc_ref[sf, :] += o_f
            o_b, carry[2 * j + 1] = _hgrn_chunk(q_ref[sb, :], i_ref[sb, :], zb_ref[sb, :], lbb, carry[2 * j + 1],
                                                mb_ref[...], kb_ref, True)
            acc_ref[sb, :] += o_b
        return tuple(carry)

    states = lax.fori_loop(0, n, body, tuple(states), unroll=HG_CHAINS // (2 * nseq))
    o_ref[...] = (_rms(acc_ref[...], ng_ref[...]) * _silu(gt_ref[...])).astype(BF16)
    if s_ref is not None:
        for j in range(nseq):
            for d in range(2):
                s_ref[j, d, 0] = states[2 * j + d].T


def _gelu_tanh(y):
    return 0.5 * y * (1.0 + jnp.tanh(0.7978845608028654 * (y + 0.044715 * (y * y * y))))


def _lru_body(xc_ref, yc_ref, cw_ref, cb_ref, wa_ref, ba_ref, wx_ref, bx_ref, lam_ref, h0_ref, o_ref, s_ref,
              xs_ref, a_ref, u_ref, *, rows, reset, independent):
    T = rows
    seg = T // LRU_SEGS
    row = lax.broadcasted_iota(jnp.int32, (T, 1), 0)
    seq_len = seg if independent else T
    pos = row % seg if independent else row
    pad = 8
    zeros = jnp.zeros((pad, 128), F32)
    xs_ref[0:pad, :] = zeros
    xs_ref[pad:pad + T, :] = xc_ref[...]
    xs_ref[pad + T:2 * pad + T, :] = zeros
    cw = cw_ref[...]
    x = cb_ref[...]
    for tap in range(4):
        off = tap - 2
        term = cw[tap:tap + 1] * xs_ref[pad + off:pad + off + T, :]
        if independent and off != 0:
            term = jnp.where((pos + off >= 0) & (pos + off < seq_len), term, 0.0)
        x = x + term
    xb = x.astype(BF16)
    pitch = seg + LRU_PITCH_PAD
    for d in range(2):
        rg = _sigmoid(_dot(xb, wa_ref[d, 0]) + ba_ref[d])
        ig = _sigmoid(_dot(xb, wx_ref[d, 0]) + bx_ref[d])
        log_a = (C_EXP * rg) * _log_sigmoid(lam_ref[d])
        a = jnp.exp(log_a)
        mult = jnp.sqrt(1.0 - a * a)
        if reset:
            mult = jnp.where(pos == (0 if d == 0 else seq_len - 1), 1.0, mult)
        u = mult * ig * x
        for r in range(LRU_SEGS):
            a_ref[d, r * pitch:r * pitch + seg, :] = a[r * seg:(r + 1) * seg]
            u_ref[d, r * pitch:r * pitch + seg, :] = u[r * seg:(r + 1) * seg]
    sub =lax.broadcasted_iota(jnp.int32, (LRU_SEGS, 1), 0)
    zrow = jnp.zeros((1, 128), F32)
    if h0_ref is not None:
        hf0 = jnp.where(sub == 0, h0_ref[0, 0:1, :], 0.0)
        hb0 = jnp.where(sub == LRU_SEGS - 1, h0_ref[0, 1:2, :], 0.0)
    else:
        hf0 = jnp.zeros((LRU_SEGS, 128), F32)
        hb0 = hf0
    ones = jnp.ones((LRU_SEGS, 128), F32)

    def local_scan(i, carry):
        hf, pf, hb, pb = carry
        sf = pl.ds(i, LRU_SEGS, stride=pitch)
        sb = pl.ds(seg - 1 - i, LRU_SEGS, stride=pitch)
        af = a_ref[0, sf, :]
        hf = af * hf + u_ref[0, sf, :]
        pf = af * pf
        u_ref[0, sf, :] = hf
        a_ref[0, sf, :] = pf
        ab = a_ref[1, sb, :]
        hb = ab * hb + u_ref[1, sb, :]
        pb = ab * pb
        u_ref[1, sb, :] = hb
        a_ref[1, sb, :] = pb
        return hf, pf, hb, pb

    hf, pf, hb, pb = lax.fori_loop(0, seg, local_scan, (hf0, ones, hb0, ones), unroll=4)
    if independent:
        for r in range(LRU_SEGS):
            sl = slice(r * seg, (r + 1) * seg)
            ps = slice(r * pitch, r * pitch + seg)
            o_ref[sl, :] = ((u_ref[0, ps, :] + u_ref[1, ps, :]) * _gelu_tanh(yc_ref[sl, :])).astype(BF16)
        if s_ref is not None:
            s_ref[:, 0, :] = hf
            s_ref[:, 1, :] = hb
        return
    carry_f = [zrow] * LRU_SEGS
    carry_b = [zrow] * LRU_SEGS
    end_f = hf[0:1]
    for r in range(1, LRU_SEGS):
        carry_f[r] = end_f
        end_f = hf[r:r + 1] + pf[r:r + 1] * end_f
    end_b = hb[LRU_SEGS - 1:LRU_SEGS]
    for r in range(LRU_SEGS - 2, -1, -1):
        carry_b[r] = end_b
        end_b = hb[r:r + 1] + pb[r:r + 1] * end_b
    for r in range(LRU_SEGS):
        sl = slice(r * seg, (r + 1) * seg)
        ps = slice(r * pitch, r * pitch + seg)
        h = (u_ref[0, ps, :] + a_ref[0, ps, :] * carry_f[r]) + (u_ref[1, ps, :] + a_ref[1, ps, :] * carry_b[r])
        o_ref[sl, :] = (h * _gelu_tanh(yc_ref[sl, :])).astype(BF16)
    if s_ref is not None:
        s_ref[0, 0:1, :] = end_f
        s_ref[0, 1:2, :] = end_b


def _lru_kernel_flat(*refs, rows, reset, independent, has_state_in, has_state_out):
    refs = list(refs)
    head = refs[:9]
    pos = 9
    h0_ref = None
    if has_state_in:
        h0_ref = refs[pos]
        pos += 1
    o_ref = refs[pos]
    pos += 1
    s_ref = None
    if has_state_out:
        s_ref = refs[pos]
        pos += 1
    _lru_body(*head, h0_ref, o_ref, s_ref, *refs[pos:], rows=rows, reset=reset, independent=independent)


def _lru(proj, lp, *, nbatch, seq, row_base, reset, state_in=None, want_state=False):
    independent = seq < LRU_SEGS * 128
    per_step = LRU_SEGS if independent else 1
    rows = seq * per_step
    nstep = nbatch // per_step
    rb = row_base // rows
    nblk = C_WIDTH // 128

    def col(k):
        return pl.BlockSpec((rows, 128), lambda b, c: (rb + b, (C_OFF + k * C_WIDTH) // 128 + c))

    vec = lambda rows: pl.BlockSpec((rows, 1, 128), lambda b, c: (0, 0, c))
    mat = pl.BlockSpec((2, 1, 128, 128), lambda b, c: (0, c, 0, 0))
    in_specs = [col(0), col(1),
                pl.BlockSpec((4, 128), lambda b, c: (0, c)), pl.BlockSpec((1, 128), lambda b, c: (0, c)),
                mat, vec(2), mat, vec(2), vec(2)]
    args = [proj, proj, lp['conv_w'], lp['conv_b'], lp['wa'], lp['ba'], lp['wx'], lp['bx'], lp['lam']]
    if state_in is not None:
        in_specs.append(pl.BlockSpec((1, 2, 128), lambda b, c: (b, 0, c)))
        args.append(state_in)
    out_specs = [pl.BlockSpec((rows, 128), lambda b, c: (b, c))]
    out_shape = [jax.ShapeDtypeStruct((nbatch * seq, C_WIDTH), BF16)]
    if want_state:
        out_specs.append(pl.BlockSpec((per_step, 2, 128), lambda b, c: (b, 0, c)))
        out_shape.append(jax.ShapeDtypeStruct((nbatch, 2, C_WIDTH), F32))
    kern = functools.partial(_lru_kernel_flat, rows=rows, reset=reset, independent=independent,
                             has_state_in=state_in is not None, has_state_out=want_state)
    scan_rows = rows + LRU_SEGS * LRU_PITCH_PAD
    return pl.pallas_call(
        kern,
        grid=(nstep, nblk),
        in_specs=in_specs,
        out_specs=out_specs,
        out_shape=out_shape,
        scratch_shapes=[pltpu.VMEM((rows + 16, 128), F32), pltpu.VMEM((2, scan_rows, 128), F32),
                        pltpu.VMEM((2, scan_rows, 128), F32)],
        compiler_params=_cparams(("arbitrary", "arbitrary")),
        name="lru_%d" % seq,
    )(*args)


MERGE_TM = 512
ROUTER_ROWS = 16


def _top2_combine(logits):
    idx = lax.broadcasted_iota(jnp.int32, logits.shape, 0).astype(F32)
    none = float(N_EXPERTS)
    m1 = jnp.max(logits, axis=0, keepdims=True)
    i1 = jnp.min(jnp.where(logits == m1, idx, none), axis=0, keepdims=True)
    rest = jnp.where(idx == i1, -jnp.inf, logits)
    m2 = jnp.max(rest, axis=0, keepdims=True)
    i2 = jnp.min(jnp.where(rest == m2, idx, none), axis=0, keepdims=True)
    e = jnp.exp(m2 - m1)
    w1 = 1.0 / (1.0 + e)
    return jnp.where(idx == i1, w1, 0.0) + jnp.where(idx == i2, e * w1, 0.0)


def _merge_body(xc_ref, xl_ref, m_ref, z0_ref, z1_ref, z2_ref, ac_ref, al_ref, hc_ref, hl_ref, lc_ref, ll_ref,
                wba_ref, wbb_ref, wbc_ref, wo_ref, g_ref, r_ref, xoc_ref, xol_ref, u_ref, c_ref):
    tm = MERGE_TM
    mg = (z0_ref[...].astype(F32) * _dot(_pair_read(ac_ref, al_ref, tm), wba_ref[...])
          + z1_ref[...].astype(F32) * _dot(_pair_read(hc_ref, hl_ref, tm), wbb_ref[...])
          + z2_ref[...].astype(F32) * _dot(_pair_read(lc_ref, ll_ref, tm), wbc_ref[...]))
    x = _pair_read(xc_ref, xl_ref, tm) + m_ref[0, 2] * _dot(mg.astype(BF16), wo_ref[...])
    _pair_write(xoc_ref, xol_ref, tm, x)
    u =_rms(x, g_ref[...]) * (1.0 + m_ref[0, 4]) + m_ref[0, 3]
    ub = u.astype(BF16)
    u_ref[...] = ub
    if r_ref is not None:
        ul = (u - ub.astype(F32)).astype(BF16)
        logits = _dot_nt(r_ref[0], ub) + (_dot_nt(r_ref[1], ub) + _dot_nt(r_ref[0], ul))
        comb = _top2_combine(logits[0:N_EXPERTS])
        comb = jnp.concatenate([comb, jnp.zeros((128 - N_EXPERTS, tm), F32)], axis=0)
        c_ref[...] = comb.T


def _merge_kernel_dense(*refs):
    _merge_body(*refs[:17], None, *refs[17:], None)


def _merge(x_pair, mod_l, gates, mixers, lp, router=None):
    tm = MERGE_TM
    row = lambda i: (i, 0)
    const = lambda i: (0, 0)
    res = lambda shape: pl.BlockSpec(shape, const, pipeline_mode=pl.Buffered(1))
    in_specs = _pair_specs(tm) + [
        pl.BlockSpec((1, 6, 1, D_MODEL), lambda i: (_mod_row(i, tm), 0, 0, 0)),
        pl.BlockSpec((tm, D_MODEL), lambda i: (i, 0)),
        pl.BlockSpec((tm, D_MODEL), lambda i: (i, 1)),
        pl.BlockSpec((tm, D_MODEL), lambda i: (i, 2))]
    in_specs += _pair_specs(tm, 512) * 3
    in_specs += [res((A_WIDTH, D_MODEL)), res((B_WIDTH, D_MODEL)), res((C_WIDTH, D_MODEL)), res((D_MODEL, D_MODEL)),
                 pl.BlockSpec((1, D_MODEL), const)]
    args = [*x_pair, mod_l, gates, gates, gates, *mixers, lp['w_ba'], lp['w_bb'], lp['w_bc'], lp['w_out'],
            lp['norm_ffn']]
    out_specs = _pair_specs(tm) + [pl.BlockSpec((tm, D_MODEL), row)]
    out_shape = _pair_shapes(D_MODEL, F32) + [jax.ShapeDtypeStruct((N_TOK, D_MODEL), BF16)]
    kern = _merge_kernel_dense
    if router is not None:
        in_specs.append(pl.BlockSpec((2, ROUTER_ROWS, D_MODEL), lambda i: (0, 0, 0)))
        args.append(router)
        out_specs.append(pl.BlockSpec((tm, 128), row))
        out_shape.append(jax.ShapeDtypeStruct((N_TOK, 128), F32))
        kern = _merge_body
    return pl.pallas_call(
        kern,
        grid=(N_TOK // tm,),
        in_specs=in_specs,
        out_specs=out_specs,
        out_shape=out_shape,
        compiler_params=_cparams(("arbitrary",)),
        name="merge",
    )(*args)


FFN_TM = 512
FFN_CHUNK = FFN_DIM // 2


def _swiglu_partial(u, wg, wu, wd):
    h = (_silu(_dot(u, wg)) * _dot(u, wu)).astype(BF16)
    return _dot(h, wd)


def _ffn_kernel(u_ref, xc_ref, xl_ref, m_ref, wg_ref, wu_ref, wd_ref, oc_ref, ol_ref):
    u = u_ref[...]
    acc = None
    for c in range(FFN_DIM // FFN_CHUNK):
        sl = slice(c * FFN_CHUNK, (c + 1) * FFN_CHUNK)
        y = _swiglu_partial(u, wg_ref[:, sl], wu_ref[:, sl], wd_ref[sl, :])
        acc = y if acc is None else acc + y
    _pair_write(oc_ref, ol_ref, FFN_TM, _pair_read(xc_ref, xl_ref, FFN_TM) + m_ref[0, 5] * acc)


def _ffn(u, x_pair, mod_l, wg, wu, wd):
    tm = FFN_TM
    res = lambda shape: pl.BlockSpec(shape, lambda i: (0, 0), pipeline_mode=pl.Buffered(1))
    return pl.pallas_call(
        _ffn_kernel,
        grid=(N_TOK // tm,),
        in_specs=[pl.BlockSpec((tm, D_MODEL), lambda i: (i, 0))] + _pair_specs(tm) + [
            pl.BlockSpec((1, 6, 1, D_MODEL), lambda i: (_mod_row(i, tm), 0, 0, 0)),
            res((D_MODEL, FFN_DIM)), res((D_MODEL, FFN_DIM)), res((FFN_DIM, D_MODEL))],
        out_specs=_pair_specs(tm),
        out_shape=_pair_shapes(D_MODEL, F32),
        compiler_params=_cparams(("arbitrary",)),
        name="ffn",
    )(u, *x_pair, mod_l, wg, wu, wd)


MOE_TM = 512
MOE_NT = N_TOK // MOE_TM
MOE_PIECE = 16
MOE_SLAB = 2 * MOE_TM + N_EXPERTS * MOE_PIECE
MOE_BLK = 512
MOE_NB = -(-(2 * N_TOK + MOE_NT * N_EXPERTS * (MOE_PIECE - 1) + N_EXPERTS * (MOE_BLK - 1)) // MOE_BLK)
MOE_ROWS = MOE_NB * MOE_BLK
MOE_FCHUNK = 512


def _moe_count_kernel(c_ref, n_ref):
    n_ref[0] = jnp.sum(jnp.where(c_ref[...] > 0.0, 1.0, 0.0), axis=0, keepdims=True)


def _moe_count(comb):
    return pl.pallas_call(
        _moe_count_kernel,
        grid=(MOE_NT,),
        in_specs=[pl.BlockSpec((MOE_TM, 128), lambda i: (i, 0))],
        out_specs=pl.BlockSpec((1, 1, 128), lambda i: (i, 0, 0)),
        out_shape=jax.ShapeDtypeStruct((MOE_NT, 1, 128), F32),
        compiler_params=_cparams(("arbitrary",)),
        name="moe_count",
    )(comb)


def _moe_tables(counts):
    cnt = counts[:, 0, :N_EXPERTS].astype(jnp.int32)
    pc = (cnt + MOE_PIECE - 1) // MOE_PIECE * MOE_PIECE
    lo = jnp.cumsum(pc, axis=1) - pc
    tot = jnp.sum(pc, axis=0)
    totp = (tot + MOE_BLK - 1) // MOE_BLK * MOE_BLK
    goff = (jnp.cumsum(totp) - totp)[None, :] + jnp.cumsum(pc, axis=0) - pc
    ends = jnp.cumsum(totp // MOE_BLK)
    nact = ends[-1]
    s = jnp.minimum(jnp.arange(MOE_NB, dtype=jnp.int32), nact - 1)
    blk_e = jnp.minimum(jnp.sum((s[:, None] >= ends[None, :]).astype(jnp.int32), axis=1), N_EXPERTS - 1)
    lo_vec = jnp.pad(lo.astype(F32), ((0, 0), (0, 128 - N_EXPERTS))).reshape(MOE_NT, 1, 128)
    return dict(lo=lo.reshape(-1), npc=(pc // MOE_PIECE).reshape(-1), goff=goff.reshape(-1),
                nrow=jnp.sum(pc, axis=1), blk_e=blk_e, nact=nact.reshape(1), lo_vec=lo_vec)


def _moe_pieces(tile, lo_s, npc_s, goff_s, make, wait):
    for e in range(N_EXPERTS):
        k = tile * N_EXPERTS + e

        def body(p, carry, k=k):
            local = pl.multiple_of(lo_s[k] + p * MOE_PIECE, MOE_PIECE)
            glob = pl.multiple_of(goff_s[k] + p * MOE_PIECE, MOE_PIECE)
            cp = make(local, glob)
            if wait:
                cp.wait()
            else:
                cp.start()
            return carry

        lax.fori_loop(0, npc_s[k], body, 0)


def _slot_pair(slot):
    sel = slot >= 0.0
    s_hi = jnp.max(slot, axis=-1, keepdims=True)
    s_lo = jnp.min(jnp.where(sel, slot, float(MOE_SLAB)), axis=-1, keepdims=True)
    return sel, s_hi, s_lo


def _one_hot_rows(s):
    srow = lax.broadcasted_iota(jnp.int32, (MOE_TM, MOE_SLAB), 1).astype(F32)
    return jnp.where(srow == s, 1.0, 0.0)


def _moe_gather_kernel(lo_s, npc_s, goff_s, u_ref, c_ref, lov_ref, tri_ref, init_ref, xs_ref, slot_ref,
                       slab_ref, sem):
    del init_ref
    i = pl.program_id(0)
    routed = c_ref[...] > 0.0
    rank = _dot(tri_ref[...], jnp.where(routed, 1.0, 0.0).astype(BF16))
    slot = jnp.where(routed, lov_ref[0] + rank, -1.0)
    slot_ref[...] = slot
    _, s_hi, s_lo = _slot_pair(slot)
    pt = jnp.maximum(_one_hot_rows(s_hi), _one_hot_rows(s_lo)).astype(BF16)
    buf = i % 2
    slab_ref[buf] = _dot_tn(pt, u_ref[...]).astype(BF16)

    def pieces(tile, b, wait):
        _moe_pieces(tile, lo_s, npc_s, goff_s, lambda local, glob: pltpu.make_async_copy(
            slab_ref.at[b, pl.ds(local, MOE_PIECE), :], xs_ref.at[pl.ds(glob, MOE_PIECE), :], sem.at[b]), wait)

    pieces(i, buf, False)

    @pl.when(i > 0)
    def _():
        pieces(i - 1, 1 - buf, True)

    @pl.when(i == MOE_NT - 1)
    def _():
        pieces(i, buf, True)


def _moe_gather(u, comb, tab):
    tri = jnp.asarray(np.tril(np.ones((MOE_TM, MOE_TM), np.float32), -1), BF16)
    init = jnp.zeros((MOE_ROWS, D_MODEL), BF16)
    row = lambda i, *_: (i, 0)
    grid_spec = pltpu.PrefetchScalarGridSpec(
        num_scalar_prefetch=3,
        grid=(MOE_NT,),
        in_specs=[pl.BlockSpec((MOE_TM, D_MODEL), row), pl.BlockSpec((MOE_TM, 128), row),
                  pl.BlockSpec((1, 1, 128), lambda i, *_: (i, 0, 0)),
                  pl.BlockSpec((MOE_TM, MOE_TM), lambda i, *_: (0, 0)),
                  pl.BlockSpec(memory_space=pl.ANY)],
        out_specs=[pl.BlockSpec(memory_space=pl.ANY), pl.BlockSpec((MOE_TM, 128), row)],
        scratch_shapes=[pltpu.VMEM((2, MOE_SLAB, D_MODEL), BF16), pltpu.SemaphoreType.DMA((2,))],
    )
    return pl.pallas_call(
        _moe_gather_kernel,
        grid_spec=grid_spec,
        out_shape=[jax.ShapeDtypeStruct((MOE_ROWS, D_MODEL), BF16), jax.ShapeDtypeStruct((N_TOK, 128), F32)],
        input_output_aliases={7: 0},
        compiler_params=_cparams(("arbitrary",)),
        name="moe_gather",
    )(tab['lo'], tab['npc'], tab['goff'], u, comb, tab['lo_vec'], tri, init)


class _ExpertWeights:
    def __init__(self, hbm, resident, stages, sem):
        self.hbm, self.resident, self.stages, self.sem = hbm, resident, stages, sem

    def _copies(self, e, c):
        sl = slice(c * MOE_FCHUNK, (c + 1) * MOE_FCHUNK)
        b = c % 2
        srcs = (self.hbm[0].at[e, :, sl], self.hbm[1].at[e, :, sl], self.hbm[2].at[e, sl, :])
        return [pltpu.make_async_copy(src, stage.at[b], self.sem.at[b, k])
                for k, (src, stage) in enumerate(zip(srcs, self.stages))]

    def start(self, e, c):
        for cp in self._copies(e, c):
            cp.start()

    def finish(self, e, c):
        for cp in self._copies(e, c):
            cp.wait()
        sl = slice(c * MOE_FCHUNK, (c + 1) * MOE_FCHUNK)
        b = c % 2
        self.resident[0][:, sl] = self.stages[0][b].astype(BF16)
        self.resident[1][:, sl] = self.stages[1][b].astype(BF16)
        self.resident[2][sl, :] = self.stages[2][b].astype(BF16)


def _moe_block(x_ref, y_ref, w, next_expert):
    nch = EXPERT_DIM // MOE_FCHUNK
    wg_ref, wu_ref, wd_ref = w.resident
    if next_expert is not None:
        w.start(next_expert, 0)
        w.start(next_expert, 1)
    x = x_ref[...]
    acc = None
    for c in range(nch):
        sl = slice(c * MOE_FCHUNK, (c + 1) * MOE_FCHUNK)
        y = _swiglu_partial(x, wg_ref[:, sl], wu_ref[:, sl], wd_ref[sl, :])
        acc = y if acc is None else acc + y
        if next_expert is not None:
            w.finish(next_expert, c)
            if c + 2 < nch:
                w.start(next_expert, c + 2)
    y_ref[...] = acc.astype(BF16)


def _moe_expert_kernel(be_s, nact_s, x_ref, wg_hbm, wu_hbm, wd_hbm, y_ref, wg_ref, wu_ref, wd_ref,
                       g_stage, u_stage, d_stage, sem):
    s = pl.program_id(0)
    nch = EXPERT_DIM // MOE_FCHUNK
    active = s < nact_s[0]
    e = be_s[s]
    e_next = be_s[jnp.minimum(s + 1, MOE_NB - 1)]
    hands_over = e_next != e
    w = _ExpertWeights((wg_hbm, wu_hbm, wd_hbm), (wg_ref, wu_ref, wd_ref), (g_stage, u_stage, d_stage), sem)

    @pl.when(s == 0)
    def _():
        w.start(e, 0)
        for c in range(nch):
            if c + 1 < nch:
                w.start(e, c + 1)
            w.finish(e, c)

    @pl.when(jnp.logical_not(active))
    def _():
        y_ref[...] = jnp.zeros_like(y_ref)

    @pl.when(active & hands_over)
    def _():
        _moe_block(x_ref, y_ref, w, e_next)

    @pl.when(active & jnp.logical_not(hands_over))
    def _():
        _moe_block(x_ref, y_ref, w, None)


def _moe_expert(xs, tab, wg, wu, wd):
    blk = lambda s, be, nact: (jnp.minimum(s, nact[0] - 1), 0)
    hbm = pl.BlockSpec(memory_space=pl.ANY)
    grid_spec = pltpu.PrefetchScalarGridSpec(
        num_scalar_prefetch=2,
        grid=(MOE_NB,),
        in_specs=[pl.BlockSpec((MOE_BLK, D_MODEL), blk), hbm, hbm, hbm],
        out_specs=pl.BlockSpec((MOE_BLK, D_MODEL), lambda s, be, nact: (s, 0)),
        scratch_shapes=[pltpu.VMEM((D_MODEL, EXPERT_DIM), BF16), pltpu.VMEM((D_MODEL, EXPERT_DIM), BF16),
                        pltpu.VMEM((EXPERT_DIM, D_MODEL), BF16),
                        pltpu.VMEM((2, D_MODEL, MOE_FCHUNK), F32), pltpu.VMEM((2, D_MODEL, MOE_FCHUNK), F32),
                        pltpu.VMEM((2, MOE_FCHUNK, D_MODEL), F32), pltpu.SemaphoreType.DMA((2, 3))],
    )
    return pl.pallas_call(
        _moe_expert_kernel,
        grid_spec=grid_spec,
        out_shape=jax.ShapeDtypeStruct((MOE_ROWS, D_MODEL), BF16),
        compiler_params=_cparams(("arbitrary",)),
        name="moe_expert",
    )(tab['blk_e'], tab['nact'], xs, wg, wu, wd)


def _moe_combine_kernel(lo_s, npc_s, goff_s, nrow_s, y_ref, slot_ref, c_ref, xc_ref, xl_ref, m_ref, g_ref,
                        oc_ref, ol_ref, slab_ref, sem):
    i = pl.program_id(0)
    buf = i % 2

    def pieces(tile, b, wait):
        _moe_pieces(tile, lo_s, npc_s, goff_s, lambda local, glob: pltpu.make_async_copy(
            y_ref.at[pl.ds(glob, MOE_PIECE), :], slab_ref.at[b, pl.ds(local, MOE_PIECE), :], sem.at[b]), wait)

    @pl.when(i == 0)
    def _():
        pieces(i, buf, False)

    @pl.when(i + 1 < MOE_NT)
    def _():
        pieces(i + 1, 1 - buf, False)

    slot = slot_ref[...]
    comb = c_ref[...]
    sel, s_hi, s_lo = _slot_pair(slot)
    w_hi = jnp.sum(jnp.where(sel & (slot == s_hi), comb, 0.0), axis=-1, keepdims=True)
    w_lo = jnp.sum(jnp.where(sel & (slot == s_lo), comb, 0.0), axis=-1, keepdims=True)
    w_lo = jnp.where(s_lo == s_hi, 0.0, w_lo)
    srow = lax.broadcasted_iota(jnp.int32, (MOE_SLAB, 1), 0)
    pieces(i, buf, True)
    ys = jnp.where(srow < nrow_s[i], slab_ref[buf], jnp.zeros((), BF16))
    out = (w_hi * _dot(_one_hot_rows(s_hi).astype(BF16), ys) + w_lo * _dot(_one_hot_rows(s_lo).astype(BF16), ys))
    x = _pair_read(xc_ref, xl_ref, MOE_TM)
    _pair_write(oc_ref, ol_ref, MOE_TM, _rms(x + m_ref[0, 5] * out, g_ref[...]))


def _moe_combine(y, slot, comb, x_pair, mod_l, final_g, tab):
    row = lambda i, *_: (i, 0)
    grid_spec = pltpu.PrefetchScalarGridSpec(
        num_scalar_prefetch=4,
        grid=(MOE_NT,),
        in_specs=[pl.BlockSpec(memory_space=pl.ANY),
                  pl.BlockSpec((MOE_TM, 128), row), pl.BlockSpec((MOE_TM, 128), row)] + _pair_specs(MOE_TM) + [
                  pl.BlockSpec((1, 6, 1, D_MODEL), lambda i, *_: (_mod_row(i, MOE_TM), 0, 0, 0)),
                  pl.BlockSpec((1, D_MODEL), lambda i, *_: (0, 0))],
        out_specs=_pair_specs(MOE_TM),
        scratch_shapes=[pltpu.VMEM((2, MOE_SLAB, D_MODEL), BF16), pltpu.SemaphoreType.DMA((2,))],
    )
    return pl.pallas_call(
        _moe_combine_kernel,
        grid_spec=grid_spec,
        out_shape=_pair_shapes(D_MODEL, F32),
        compiler_params=_cparams(("arbitrary",)),
        name="moe_combine",
    )(tab['lo'], tab['npc'], tab['goff'], tab['nrow'], y, slot, comb, *x_pair, mod_l, final_g)


def _moe(u, comb, x_pair, mod_l, final_g, wg, wu, wd):
    tab = _moe_tables(_moe_count(comb))
    xs, slot = _moe_gather(u, comb, tab)
    y = _moe_expert(xs, tab, wg, wu, wd)
    return _moe_combine(y, slot, comb, x_pair, mod_l, final_g, tab)


def _block_diag(w):
    w = w.reshape(2, C_WIDTH // 128, 2, C_BLOCK, C_BLOCK)
    z = jnp.zeros_like(w[:, :, 0])
    top = jnp.concatenate([w[:, :, 0], z], axis=-1)
    bot = jnp.concatenate([z, w[:, :, 1]], axis=-1)
    return jnp.concatenate([top, bot], axis=-2).astype(BF16)


def kernel(x_prompt, x_sample, cache_attn_k, cache_attn_v, state_hgrn, state_lru, c, c_ctx, w_mod, b_mod, norm_mix_g, norm_ffn_g, w_in, attn_sink, hgrn_lower, hgrn_norm_g, lru_conv_w, lru_conv_b, lru_wa, lru_ba, lru_wx, lru_bx, lru_lambda, w_branch_a, w_branch_b, w_branch_c, w_out, ffn_w_gate, ffn_w_up, ffn_w_down, moe_router, moe_w_gate, moe_w_up, moe_w_down, final_norm_g):
    assert DEPTH == 2
    x = (x_prompt.reshape(N_CTX_TOK, D_MODEL), x_sample.reshape(N_LAT_TOK, D_MODEL))
    cond =jnp.concatenate([c_ctx[None, :], c, jnp.zeros((N_COND - 1 - DEC_BATCH, D_MODEL), F32)], axis=0)
    mods = _modulation(cond, w_mod, b_mod).reshape(DEPTH, N_COND, 6, 1, D_MODEL)

    lb_cum = jnp.cumsum(jax.nn.softmax(hgrn_lower.astype(F32), axis=0), axis=0)
    lb_layers = lb_cum - lb_cum[:1]
    cos, sin = _rope_tables()
    hconst = _hgrn_constants(False) + _hgrn_constants(True)

    ks, vs, hs, ls = [], [], [], []
    for l in range(DEPTH):
        w_in_l = w_in[l].astype(BF16)
        lb = lb_layers[l]
        lbp = jnp.stack([jnp.log(lb), jnp.log1p(-lb), 1.0 - lb]).reshape(3, 2, 1, B_WIDTH)
        lp = dict(
            conv_w=lru_conv_w[l], conv_b=lru_conv_b[l].reshape(1, C_WIDTH),
            wa=_block_diag(lru_wa[l]), wx=_block_diag(lru_wx[l]),
            ba=lru_ba[l].reshape(2, 1, C_WIDTH), bx=lru_bx[l].reshape(2, 1, C_WIDTH),
            lam=lru_lambda[l].reshape(2, 1, C_WIDTH),
            w_ba=w_branch_a[l].astype(BF16), w_bb=w_branch_b[l].astype(BF16), w_bc=w_branch_c[l].astype(BF16),
            w_out=w_out[l].astype(BF16), norm_ffn=norm_ffn_g[l].reshape(1, D_MODEL))
        mod_l = mods[l]

        proj, gates = _in_proj(x, mod_l, norm_mix_g[l].reshape(1, D_MODEL), w_in_l)

        attn_c = _ctx_attention(proj, attn_sink[l])
        attn_l = _lat_attention(proj, attn_sink[l], cache_attn_k[:, l].reshape(DEC_BATCH, PAST_LEN, 128),
                                cache_attn_v[:, l].reshape(DEC_BATCH, PAST_LEN, 128), cos, sin)

        ng = hgrn_norm_g[l].reshape(1, B_KEY_DIM)
        hg_c, s_h = _hgrn(proj, lbp, ng, hconst, nbatch=BATCH, seq=SEQ, row_base=0, want_state=True)
        hg_l, = _hgrn(proj, lbp, ng, hconst, nbatch=DEC_BATCH, seq=DEC_SEQ, row_base=N_CTX_TOK,
                      state_in=state_hgrn[:, l])

        lr_c, s_l = _lru(proj, lp, nbatch=BATCH, seq=SEQ, row_base=0, reset=True, want_state=True)
        lr_l, = _lru(proj, lp, nbatch=DEC_BATCH, seq=DEC_SEQ, row_base=N_CTX_TOK, reset=False,
                     state_in=state_lru[:, l])
        mixers = (attn_c, attn_l, hg_c, hg_l, lr_c, lr_l)

        ks.append(proj[:N_CTX_TOK, KA_OFF:KA_OFF + 128].reshape(BATCH, SEQ, A_KV_HEADS, A_HEAD_DIM))
        vs.append(proj[:N_CTX_TOK, VA_OFF:VA_OFF + 128].reshape(BATCH, SEQ, A_KV_HEADS, A_HEAD_DIM))
        hs.append(s_h)
        ls.append(s_l)

        m = l // 2
        if l % 2 == 0:
            xc, xl, u = _merge(x, mod_l, gates, mixers, lp)
            x = _ffn(u, (xc, xl), mod_l, ffn_w_gate[m].astype(BF16), ffn_w_up[m].astype(BF16), ffn_w_down[m].astype(BF16))
        else:
            router = jnp.pad(moe_router[m].T, ((0, ROUTER_ROWS - N_EXPERTS), (0, 0)))
            r_hi = router.astype(BF16)
            router = jnp.stack([r_hi, (router - r_hi.astype(F32)).astype(BF16)])
            xc, xl, u, comb = _merge(x, mod_l, gates, mixers, lp, router=router)
            x = _moe(u, comb, (xc, xl), mod_l, final_norm_g.reshape(1, D_MODEL), moe_w_gate[m], moe_w_up[m],
                     moe_w_down[m])

    y_prompt = x[0].reshape(BATCH, SEQ, D_MODEL)
    y_sample = x[1].reshape(DEC_BATCH, DEC_SEQ, D_MODEL)
    return (y_prompt, y_sample, jnp.stack(ks, axis=1), jnp.stack(vs, axis=1), jnp.stack(hs, axis=1),
            jnp.stack(ls, axis=1))
```

```python
import functools

import numpy as np
import jax
import jax.numpy as jnp
from jax import lax
from jax.experimental import pallas as pl
from jax.experimental.pallas import tpu as pltpu

F32 = jnp.float32
BF16 = jnp.bfloat16
HIGHEST = lax.Precision.HIGHEST

D_MODEL = 1024
BATCH = 32
SEQ = 256
DEPTH = 2
DEC_BATCH = 4
DEC_SEQ = 2048
PAST_LEN = 512
GRID_W = 64
RMS_EPS = 1e-6
A_HEADS = 8
A_KV_HEADS = 2
A_GROUP = 4
A_HEAD_DIM = 64
A_WIDTH = 512
A_WINDOW = 128
ROPE_THETA = 10000.0
NEG_INF = -1e30
B_HEADS = 4
B_KEY_DIM = 128
B_WIDTH = 512
C_WIDTH = 512
C_HEADS = 8
C_BLOCK = 64
C_EXP = 8.0
FFN_DIM = 2816
N_EXPERTS = 8
EXPERT_DIM = 3584

N_CTX_TOK = BATCH * SEQ
N_LAT_TOK = DEC_BATCH * DEC_SEQ
N_TOK = N_CTX_TOK + N_LAT_TOK
N_COND = 8

QA_OFF = 0
KA_OFF = 512
VA_OFF = 640
B_OFF = 768
C_OFF = 3328
IN_COLS = 7424
IN_MIX_COLS = IN_COLS - 3 * D_MODEL

V7X_VMEM_LIMIT = 56 * 1024 * 1024
HG_CHUNK = 128
HG_LEVELS = (64, 32, 16, 8, 4, 2, 1)
HG_MAT_LEVELS = (4, 2)
HG_CHAINS = 32
LRU_SEGS = 16
LRU_PITCH_PAD = 4


def _cparams(sem, vmem=V7X_VMEM_LIMIT):
    return pltpu.CompilerParams(dimension_semantics=sem, vmem_limit_bytes=vmem)


def _mod_row(i, tm):
    nct = N_CTX_TOK // tm
    return jnp.where(i < nct, 0, 1 + (i - nct) // (DEC_SEQ // tm))


def _pair_specs(tm, cols=D_MODEL):
    nct = N_CTX_TOK // tm
    return [pl.BlockSpec((tm, cols), lambda i, *_: (jnp.minimum(i, nct - 1), 0)),
            pl.BlockSpec((tm, cols), lambda i, *_: (jnp.maximum(i - nct, 0), 0))]


def _pair_shapes(cols, dtype):
    return [jax.ShapeDtypeStruct((N_CTX_TOK, cols), dtype), jax.ShapeDtypeStruct((N_LAT_TOK, cols), dtype)]


def _pair_read(c_ref, l_ref, tm):
    return jnp.where(pl.program_id(0) < N_CTX_TOK // tm, c_ref[...], l_ref[...])


def _pair_write(c_ref, l_ref, tm, value):
    is_ctx = pl.program_id(0) < N_CTX_TOK // tm

    @pl.when(is_ctx)
    def _():
        c_ref[...] = value

    @pl.when(jnp.logical_not(is_ctx))
    def _():
        l_ref[...] = value


def _sigmoid(x):
    return 0.5 * jnp.tanh(0.5 * x) + 0.5


def _silu(x):
    return x * _sigmoid(x)


def _log_sigmoid(z):
    return jnp.minimum(z, 0.0) - jnp.log1p(jnp.exp(-jnp.abs(z)))


def _rms(x, g):
    ms = jnp.mean(x * x, axis=-1, keepdims=True)
    return x * lax.rsqrt(ms + RMS_EPS) * g


def _dot(a, b):
    return jnp.dot(a, b, preferred_element_type=F32)


def _dot_nt(a, b):
    return lax.dot_general(a, b, (((1,), (1,)), ((), ())), preferred_element_type=F32)


def _dot_tn(a, b):
    return lax.dot_general(a, b, (((0,), (0,)), ((), ())), preferred_element_type=F32)


def _mod_kernel(c_ref, w_ref, b_ref, o_ref):
    a = _silu(c_ref[...])
    o_ref[0] = jnp.dot(a, w_ref[0], preferred_element_type=F32, precision=HIGHEST) + b_ref[0]


def _modulation(cond, w_mod, b_mod):
    tn = 1536
    return pl.pallas_call(
        _mod_kernel,
        grid=(DEPTH, 6 * D_MODEL // tn),
        in_specs=[pl.BlockSpec((N_COND, D_MODEL), lambda l, j: (0, 0)),
                  pl.BlockSpec((1, D_MODEL, tn), lambda l, j: (l, 0, j)),
                  pl.BlockSpec((1, 1, tn), lambda l, j: (l, 0, j))],
        out_specs=pl.BlockSpec((1, N_COND, tn), lambda l, j: (l, 0, j)),
        out_shape=jax.ShapeDtypeStruct((DEPTH, N_COND, 6 * D_MODEL), F32),
        compiler_params=_cparams(("arbitrary", "arbitrary")),
        name="modulation",
    )(cond, w_mod, b_mod.reshape(DEPTH, 1, 6 * D_MODEL))


IN_TM = 256
IN_TN = 256


def _in_proj_kernel(xc_ref, xl_ref, m_ref, g_ref, w_ref, o_ref, gate_ref):
    u = _rms(_pair_read(xc_ref, xl_ref, IN_TM), g_ref[...]) * (1.0 + m_ref[0, 1]) + m_ref[0, 0]
    ub = u.astype(BF16)
    n_mix = IN_MIX_COLS // IN_TN
    for j in range(IN_COLS // IN_TN):
        y = _dot(ub, w_ref[:, j * IN_TN:(j + 1) * IN_TN])
        if j < n_mix:
            o_ref[:, j * IN_TN:(j + 1) * IN_TN] = y
        else:
            gate_ref[:, (j - n_mix) * IN_TN:(j - n_mix + 1) * IN_TN] = _sigmoid(y).astype(BF16)


def _in_proj(x_pair, mod_l, g, w):
    tm = IN_TM
    return pl.pallas_call(
        _in_proj_kernel,
        grid=(N_TOK // tm,),
        in_specs=_pair_specs(tm) + [
            pl.BlockSpec((1, 6, 1, D_MODEL), lambda i: (_mod_row(i, tm), 0, 0, 0)),
            pl.BlockSpec((1, D_MODEL), lambda i: (0, 0)),
            pl.BlockSpec((D_MODEL, IN_COLS), lambda i: (0, 0), pipeline_mode=pl.Buffered(1))],
        out_specs=[pl.BlockSpec((tm, IN_MIX_COLS), lambda i: (i, 0)),
                   pl.BlockSpec((tm, 3 * D_MODEL), lambda i: (i, 0))],
        out_shape=[jax.ShapeDtypeStruct((N_TOK, IN_MIX_COLS), F32),
                   jax.ShapeDtypeStruct((N_TOK, 3 * D_MODEL), BF16)],
        compiler_params=_cparams(("arbitrary",)),
        name="in_proj",
    )(*x_pair, mod_l, g, w)


LOG2E = 1.4426950408889634
A_LOGIT_SCALE = A_HEAD_DIM ** -0.5 * LOG2E


def _softmax_pv(s_list, v_list, sink):
    m = sink
    for s in s_list:
        m = jnp.maximum(m, jnp.max(s, axis=-1, keepdims=True))
    den = jnp.exp2(sink - m)
    out = None
    for s, v in zip(s_list, v_list):
        p = jnp.exp2(s - m)
        den = den + jnp.sum(p, axis=-1, keepdims=True)
        o = _dot(p.astype(BF16), v)
        out = o if out is None else out + o
    return out / den


def _sink_column(sink_ref, h, rows_per_head):
    n = A_GROUP * rows_per_head
    row = lax.broadcasted_iota(jnp.int32, (n, 1), 0)
    col = jnp.full((n, 1), sink_ref[A_GROUP * h + A_GROUP - 1], F32)
    for g in range(A_GROUP - 2, -1, -1):
        col = jnp.where(row < (g + 1) * rows_per_head, sink_ref[A_GROUP * h + g], col)
    return col


def _ctx_attn_kernel(sink_ref, q_ref, k_ref, v_ref, o_ref):
    q = q_ref[...] * A_LOGIT_SCALE
    k = k_ref[...]
    v = v_ref[...]
    outs = []
    for h in range(A_KV_HEADS):
        hs = slice(h * A_HEAD_DIM, (h + 1) * A_HEAD_DIM)
        kh = k[:, hs].astype(BF16)
        vh = v[:, hs].astype(BF16)
        qs = jnp.concatenate(
            [q[:, (A_GROUP * h + g) * A_HEAD_DIM:(A_GROUP * h + g + 1) * A_HEAD_DIM] for g in range(A_GROUP)],
            axis=0).astype(BF16)
        s = _dot_nt(qs, kh)
        o = _softmax_pv([s], [vh], _sink_column(sink_ref, h, SEQ) * LOG2E)
        outs.extend(o[g * SEQ:(g + 1) * SEQ] for g in range(A_GROUP))
    o_ref[...] = jnp.concatenate(outs, axis=-1).astype(BF16)


def _ctx_attention(proj, sink):
    return pl.pallas_call(
        _ctx_attn_kernel,
        grid=(BATCH,),
        in_specs=[pl.BlockSpec(memory_space=pltpu.SMEM),
                  pl.BlockSpec((SEQ, A_WIDTH), lambda b: (b, QA_OFF // A_WIDTH)),
                  pl.BlockSpec((SEQ, 128), lambda b: (b, KA_OFF // 128)),
                  pl.BlockSpec((SEQ, 128), lambda b: (b, VA_OFF // 128))],
        out_specs=pl.BlockSpec((SEQ, A_WIDTH), lambda b: (b, 0)),
        out_shape=jax.ShapeDtypeStruct((N_CTX_TOK, A_WIDTH), BF16),
        compiler_params=_cparams(("arbitrary",)),
        name="ctx_attention",
    )(sink, proj, proj, proj)


def _rope_tables():
    half = A_HEAD_DIM // 4
    inv = ROPE_THETA ** (-np.arange(half, dtype=np.float64) / half)
    t = np.arange(DEC_SEQ)
    row = (t // GRID_W)[:, None] * inv[None, :]
    col = (t % GRID_W)[:, None] * inv[None, :]
    cos = np.concatenate([np.cos(row), np.cos(row), np.cos(col), np.cos(col)], axis=1)
    sin = np.concatenate([-np.sin(row), np.sin(row), -np.sin(col), np.sin(col)], axis=1)
    return (jnp.asarray(np.tile(cos, (1, 2)), F32), jnp.asarray(np.tile(sin, (1, 2)), F32))


def _rope(x, cos, sin, first):
    part = jnp.where(first, pltpu.roll(x, 128 - 16, 1), pltpu.roll(x, 16, 1))
    return x * cos + part * sin


def _lat_attn_kernel(sink_ref, q_ref, km_ref, k0_ref, kp_ref, vm_ref, v0_ref, vp_ref, kc_ref, vc_ref,
                     cm_ref, c0_ref, cp_ref, sm_ref, s0_ref, sp_ref, o_ref):
    j = pl.program_id(1)
    blk = A_WINDOW
    lane = lax.broadcasted_iota(jnp.int32, (blk, 128), 1)
    first = (lane % 32) < 16
    c0 = c0_ref[...]
    s0 = s0_ref[...]
    q = q_ref[...] * A_LOGIT_SCALE
    qr = [_rope(q[:, c * 128:(c + 1) * 128], c0, s0, first) for c in range(A_WIDTH // 128)]
    kw = jnp.concatenate([_rope(km_ref[...], cm_ref[...], sm_ref[...], first),
                          _rope(k0_ref[...], c0, s0, first),
                          _rope(kp_ref[...], cp_ref[...], sp_ref[...], first)], axis=0)
    vw = jnp.concatenate([vm_ref[...], v0_ref[...], vp_ref[...]], axis=0)
    kc = kc_ref[0]
    vc = vc_ref[0]
    t = lax.broadcasted_iota(jnp.int32, (blk, 3 * blk), 0)
    s = lax.broadcasted_iota(jnp.int32, (blk, 3 * blk), 1) - blk
    kpos = j * blk + s
    keep = jnp.where((jnp.abs(s - t) <= A_WINDOW) & (kpos >= 0) & (kpos < DEC_SEQ), 1.0, 0.0)
    keep = jnp.concatenate([keep] * A_GROUP, axis=0) > 0.0
    outs = []
    for h in range(A_KV_HEADS):
        hs = slice(h * A_HEAD_DIM, (h + 1) * A_HEAD_DIM)
        heads = [A_GROUP * h + g for g in range(A_GROUP)]
        qs = jnp.concatenate(
            [qr[hd // 2][:, (hd % 2) * A_HEAD_DIM:(hd % 2 + 1) * A_HEAD_DIM] for hd in heads], axis=0).astype(BF16)
        sw = jnp.where(keep, _dot_nt(qs, kw[:, hs].astype(BF16)), NEG_INF)
        sc = _dot_nt(qs, kc[:, hs].astype(BF16))
        o = _softmax_pv([sw, sc], [vw[:, hs].astype(BF16), vc[:, hs].astype(BF16)],
                        _sink_column(sink_ref, h, blk) * LOG2E)
        outs.extend(o[g * blk:(g + 1) * blk] for g in range(A_GROUP))
    o_ref[...] = jnp.concatenate(outs, axis=-1).astype(BF16)


def _lat_attention(proj, sink, k_ctx, v_ctx, cos, sin):
    blk = A_WINDOW
    nb = DEC_SEQ // blk
    base = N_CTX_TOK // blk

    def rows(off):
        return lambda b, j: (base + b * nb + jnp.clip(j + off, 0, nb - 1))

    def kv_spec(col, off):
        r = rows(off)
        return pl.BlockSpec((blk, 128), lambda b, j: (r(b, j), col))

    def tab_spec(off):
        return pl.BlockSpec((blk, 128), lambda b, j: (jnp.clip(j + off, 0, nb - 1), 0))

    r0 = rows(0)
    ctx_spec = pl.BlockSpec((1, PAST_LEN, 128), lambda b, j: (b, 0, 0))
    return pl.pallas_call(
        _lat_attn_kernel,
        grid=(DEC_BATCH, nb),
        in_specs=[pl.BlockSpec(memory_space=pltpu.SMEM),
                  pl.BlockSpec((blk, A_WIDTH), lambda b, j: (r0(b, j), QA_OFF // A_WIDTH)),
                  kv_spec(KA_OFF // 128, -1), kv_spec(KA_OFF // 128, 0), kv_spec(KA_OFF // 128, 1),
                  kv_spec(VA_OFF // 128, -1), kv_spec(VA_OFF // 128, 0), kv_spec(VA_OFF // 128, 1),
                  ctx_spec, ctx_spec,
                  tab_spec(-1), tab_spec(0), tab_spec(1), tab_spec(-1), tab_spec(0), tab_spec(1)],
        out_specs=pl.BlockSpec((blk, A_WIDTH), lambda b, j: (b * nb + j, 0)),
        out_shape=jax.ShapeDtypeStruct((N_LAT_TOK, A_WIDTH), BF16),
        compiler_params=_cparams(("arbitrary", "arbitrary")),
        name="lat_attention",
    )(sink, proj, proj, proj, proj, proj, proj, proj, k_ctx, v_ctx, cos, cos, cos, sin, sin, sin)


def _hgrn_constants(reverse):
    L = HG_CHUNK
    t = np.arange(L)[:, None]
    s = np.arange(L)[None, :]
    tri = (s <= t).astype(np.float32)
    blocks = [tri]
    masks = []
    for b in HG_LEVELS:
        anchor = (t // (2 * b)) * 2 * b + b - 1
        right = np.where(t % (2 * b) >= b, 1.0, -1.0)
        if b in HG_MAT_LEVELS:
            blocks.append(right * (tri - (s <= anchor).astype(np.float32)))
        masks.append(((t // (2 * b) == s // (2 * b)) & (t % (2 * b) >= b) & (s % (2 * b) < b)).astype(np.float32))
    masks.append((t == s).astype(np.float32))
    if reverse:
        blocks = [m[::-1, ::-1] for m in blocks]
        masks = [m[::-1, ::-1] for m in masks]
    mat = np.concatenate(blocks, axis=0)
    return (jnp.asarray(np.concatenate([mat, mat], axis=1), BF16), jnp.asarray(np.stack(masks), F32))


def _hgrn_chunk(qraw, vv, z, lbp, st, mat, mask_ref, reverse):
    L = HG_CHUNK
    log_lb, log_1mlb, one_mlb = lbp
    q = qraw * _sigmoid(qraw) * (B_KEY_DIM ** -0.5)
    e = jnp.exp(-jnp.abs(z))
    r = 1.0 / (1.0 + e)
    k = one_mlb * jnp.where(z >= 0.0, e * r, r)
    a = log_lb
    b = log_1mlb + (jnp.minimum(z, 0.0) + jnp.log(r))
    lf = (jnp.maximum(a, b) + jnp.log(1.0 + jnp.exp(-jnp.abs(a - b)))) * LOG2E
    hi = lf.astype(BF16)
    lo = (lf - hi.astype(F32)).astype(BF16)
    g = _dot(mat, jnp.concatenate([hi, lo], axis=0))
    cum = g[0:L]
    tot = cum[0:1] if reverse else cum[L - 1:L]
    vb = vv.astype(BF16)
    o = _dot_nt((q * jnp.exp2(cum)).astype(BF16), st.astype(BF16))
    kd = (k * jnp.exp2(tot - cum)).astype(BF16)
    st_new = st * jnp.exp2(tot) + _dot_tn(vb, kd)
    nl = len(HG_LEVELS)
    row = lax.broadcasted_iota(jnp.int32, (L, 1), 0)
    if reverse:
        row = L - 1 - row
    terms = [(q.astype(BF16), k.astype(BF16), nl)]
    for li, bsz in enumerate(HG_LEVELS):
        is_query = (row & (2 * bsz - 1)) >= bsz
        if bsz in HG_MAT_LEVELS:
            j = 1 + HG_MAT_LEVELS.index(bsz)
            d = g[j * L:(j + 1) * L]
        elif bsz == 1:
            d = jnp.where(is_query, lf, 0.0)
        else:
            parts = []
            roles = []
            for base in range(0, L, 2 * bsz):
                lo, hi = slice(base, base + bsz), slice(base + bsz, base + 2 * bsz)
                if reverse:
                    anc = cum[base + bsz:base + bsz + 1]
                    parts += [cum[lo] - anc, anc - cum[hi]]
                    roles += [q[lo], k[hi]]
                else:
                    anc = cum[base + bsz - 1:base + bsz]
                    parts += [anc - cum[lo], cum[hi] - anc]
                    roles += [k[lo], q[hi]]
            zb = (jnp.concatenate(roles, axis=0) * jnp.exp2(jnp.concatenate(parts, axis=0))).astype(BF16)
            terms.append((zb, zb, li))
            continue
        zb = (jnp.where(is_query, q, k) * jnp.exp2(d)).astype(BF16)
        terms.append((zb, zb, li))
    zero = jnp.zeros((L, B_KEY_DIM), BF16)
    sc = None
    for (qa, ka, ma), (qb, kb, mb) in zip(terms[0::2], terms[1::2]):
        lhs = jnp.concatenate([qa, qb], axis=1)
        rhs = jnp.concatenate([jnp.concatenate([ka, zero], axis=1), jnp.concatenate([zero, kb], axis=1)], axis=0)
        s2 = _dot_nt(lhs, rhs)
        part = s2[:, 0:L] * mask_ref[ma] + s2[:, L:2 * L] * mask_ref[mb]
        sc = part if sc is None else sc + part
    o = o + _dot(sc.astype(BF16), vb)
    return o, st_new


def _hgrn(proj, lbp, norm_g, consts, *, nbatch, seq, row_base, state_in=None, want_state=False):
    mat_f, mask_f, mat_b, mask_b = consts
    nseq = max(1, HG_CHAINS * HG_CHUNK // (2 * seq))
    rows = nseq * seq
    rb = row_base // rows

    def col(k):
        return pl.BlockSpec((rows, 128), lambda b, h: (rb + b, (B_OFF + k * B_WIDTH) // 128 + h))

    const2 = lambda shape: pl.BlockSpec(shape, lambda b, h: (0,) * len(shape))
    in_specs = [col(0), col(1), col(2), col(3), col(4),
                pl.BlockSpec((3, 2, 1, 128), lambda b, h: (0, 0, 0, h)),
                const2((1, 128)),
                const2(mat_f.shape), const2(mat_b.shape), const2(mask_f.shape), const2(mask_b.shape)]
    args = [proj, proj, proj, proj, proj, lbp, norm_g, mat_f, mat_b, mask_f, mask_b]
    state_spec = pl.BlockSpec((nseq, 2, 1, 128, 128), lambda b, h: (b, 0, h, 0, 0))
    if state_in is not None:
        in_specs.append(state_spec)
        args.append(state_in)
    out_specs = [pl.BlockSpec((rows, 128), lambda b, h: (b, h))]
    out_shape = [jax.ShapeDtypeStruct((nbatch * seq, B_WIDTH), BF16)]
    if want_state:
        out_specs.append(state_spec)
        out_shape.append(jax.ShapeDtypeStruct((nbatch, 2, B_HEADS, 128, 128), F32))
    kern = functools.partial(_hgrn_kernel_flat, seq=seq, nseq=nseq, has_state_in=state_in is not None,
                             has_state_out=want_state)
    return pl.pallas_call(
        kern,
        grid=(nbatch // nseq, B_HEADS),
        in_specs=in_specs,
        out_specs=out_specs,
        out_shape=out_shape,
        scratch_shapes=[pltpu.VMEM((rows, 128), F32)],
        compiler_params=_cparams(("arbitrary", "arbitrary")),
        name="hgrn_%d" % seq,
    )(*args)


def _hgrn_kernel_flat(*refs, seq, nseq, has_state_in, has_state_out):
    refs = list(refs)
    head = refs[:11]
    pos = 11
    s0_ref = None
    if has_state_in:
        s0_ref = refs[pos]
        pos += 1
    o_ref = refs[pos]
    pos += 1
    s_ref = None
    if has_state_out:
        s_ref = refs[pos]
        pos += 1
    acc_ref = refs[pos]
    _hgrn_body(*head, s0_ref, o_ref, s_ref, acc_ref, seq=seq, nseq=nseq)


def _hgrn_body(q_ref, i_ref, zf_ref, zb_ref, gt_ref, lb_ref, ng_ref, mf_ref, mb_ref, kf_ref, kb_ref,
               s0_ref, o_ref, s_ref, acc_ref, *, seq, nseq):
    L = HG_CHUNK
    n = seq // L
    acc_ref[...] = jnp.zeros_like(acc_ref)
    lbf = (lb_ref[0, 0], lb_ref[1, 0], lb_ref[2, 0])
    lbb = (lb_ref[0, 1], lb_ref[1, 1], lb_ref[2, 1])
    states = []
    for j in range(nseq):
        for d in range(2):
            if s0_ref is not None:
                states.append(s0_ref[j, d, 0].T)
            else:
                states.append(jnp.zeros((B_KEY_DIM, B_KEY_DIM), F32))

    def body(i, carry):
        carry = list(carry)
        for j in range(nseq):
            sf = pl.ds(pl.multiple_of(j * seq + i * L, L), L)
            sb = pl.ds(pl.multiple_of(j * seq + (n - 1 - i) * L, L), L)
            o_f, carry[2 * j] = _hgrn_chunk(q_ref[sf, :], i_ref[sf, :], zf_ref[sf, :], lbf, carry[2 * j],
                                            mf_ref[...], kf_ref, False)
            acc_ref[sf, :] += o_f
            o_b, carry[2 * j + 1] = _hgrn_chunk(q_ref[sb, :], i_ref[sb, :], zb_ref[sb, :], lbb, carry[2 * j + 1],
                                                mb_ref[...], kb_ref, True)
            acc_ref[sb, :] += o_b
        return tuple(carry)

    states = lax.fori_loop(0, n, body, tuple(states), unroll=HG_CHAINS // (2 * nseq))
    o_ref[...] = (_rms(acc_ref[...], ng_ref[...]) * _silu(gt_ref[...])).astype(BF16)
    if s_ref is not None:
        for j in range(nseq):
            for d in range(2):
                s_ref[j, d, 0] = states[2 * j + d].T


def _gelu_tanh(y):
    return 0.5 * y * (1.0 + jnp.tanh(0.7978845608028654 * (y + 0.044715 * (y * y * y))))


def _lru_body(xc_ref, yc_ref, cw_ref, cb_ref, wa_ref, ba_ref, wx_ref, bx_ref, lam_ref, h0_ref, o_ref, s_ref,
              xs_ref, a_ref, u_ref, *, rows, reset, independent):
    T = rows
    seg = T // LRU_SEGS
    row = lax.broadcasted_iota(jnp.int32, (T, 1), 0)
    seq_len = seg if independent else T
    pos = row % seg if independent else row
    pad = 8
    zeros = jnp.zeros((pad, 128), F32)
    xs_ref[0:pad, :] = zeros
    xs_ref[pad:pad + T, :] = xc_ref[...]
    xs_ref[pad + T:2 * pad + T, :] = zeros
    cw = cw_ref[...]
    x = cb_ref[...]
    for tap in range(4):
        off = tap - 2
        term = cw[tap:tap + 1] * xs_ref[pad + off:pad + off + T, :]
        if independent and off != 0:
            term = jnp.where((pos + off >= 0) & (pos + off < seq_len), term, 0.0)
        x = x + term
    xb = x.astype(BF16)
    pitch = seg + LRU_PITCH_PAD
    for d in range(2):
        rg = _sigmoid(_dot(xb, wa_ref[d, 0]) + ba_ref[d])
        ig = _sigmoid(_dot(xb, wx_ref[d, 0]) + bx_ref[d])
        log_a = (C_EXP * rg) * _log_sigmoid(lam_ref[d])
        a = jnp.exp(log_a)
        mult = jnp.sqrt(1.0 - a * a)
        if reset:
            mult = jnp.where(pos == (0 if d == 0 else seq_len - 1), 1.0, mult)
        u = mult * ig * x
        for r in range(LRU_SEGS):
            a_ref[d, r * pitch:r * pitch + seg, :] = a[r * seg:(r + 1) * seg]
            u_ref[d, r * pitch:r * pitch + seg, :] = u[r * seg:(r + 1) * seg]
    sub =lax.broadcasted_iota(jnp.int32, (LRU_SEGS, 1), 0)
    zrow = jnp.zeros((1, 128), F32)
    if h0_ref is not None:
        hf0 = jnp.where(sub == 0, h0_ref[0, 0:1, :], 0.0)
        hb0 = jnp.where(sub == LRU_SEGS - 1, h0_ref[0, 1:2, :], 0.0)
    else:
        hf0 = jnp.zeros((LRU_SEGS, 128), F32)
        hb0 = hf0
    ones = jnp.ones((LRU_SEGS, 128), F32)

    def local_scan(i, carry):
        hf, pf, hb, pb = carry
        sf = pl.ds(i, LRU_SEGS, stride=pitch)
        sb = pl.ds(seg - 1 - i, LRU_SEGS, stride=pitch)
        af = a_ref[0, sf, :]
        hf = af * hf + u_ref[0, sf, :]
        pf = af * pf
        u_ref[0, sf, :] = hf
        a_ref[0, sf, :] = pf
        ab = a_ref[1, sb, :]
        hb = ab * hb + u_ref[1, sb, :]
        pb = ab * pb
        u_ref[1, sb, :] = hb
        a_ref[1, sb, :] = pb
        return hf, pf, hb, pb

    hf, pf, hb, pb = lax.fori_loop(0, seg, local_scan, (hf0, ones, hb0, ones), unroll=4)
    if independent:
        for r in range(LRU_SEGS):
            sl = slice(r * seg, (r + 1) * seg)
            ps = slice(r * pitch, r * pitch + seg)
            o_ref[sl, :] = ((u_ref[0, ps, :] + u_ref[1, ps, :]) * _gelu_tanh(yc_ref[sl, :])).astype(BF16)
        if s_ref is not None:
            s_ref[:, 0, :] = hf
            s_ref[:, 1, :] = hb
        return
    carry_f = [zrow] * LRU_SEGS
    carry_b = [zrow] * LRU_SEGS
    end_f = hf[0:1]
    for r in range(1, LRU_SEGS):
        carry_f[r] = end_f
        end_f = hf[r:r + 1] + pf[r:r + 1] * end_f
    end_b = hb[LRU_SEGS - 1:LRU_SEGS]
    for r in range(LRU_SEGS - 2, -1, -1):
        carry_b[r] = end_b
        end_b = hb[r:r + 1] + pb[r:r + 1] * end_b
    for r in range(LRU_SEGS):
        sl = slice(r * seg, (r + 1) * seg)
        ps = slice(r * pitch, r * pitch + seg)
        h = (u_ref[0, ps, :] + a_ref[0, ps, :] * carry_f[r]) + (u_ref[1, ps, :] + a_ref[1, ps, :] * carry_b[r])
        o_ref[sl, :] = (h * _gelu_tanh(yc_ref[sl, :])).astype(BF16)
    if s_ref is not None:
        s_ref[0, 0:1, :] = end_f
        s_ref[0, 1:2, :] = end_b


def _lru_kernel_flat(*refs, rows, reset, independent, has_state_in, has_state_out):
    refs = list(refs)
    head = refs[:9]
    pos = 9
    h0_ref = None
    if has_state_in:
        h0_ref = refs[pos]
        pos += 1
    o_ref = refs[pos]
    pos += 1
    s_ref = None
    if has_state_out:
        s_ref = refs[pos]
        pos += 1
    _lru_body(*head, h0_ref, o_ref, s_ref, *refs[pos:], rows=rows, reset=reset, independent=independent)


def _lru(proj, lp, *, nbatch, seq, row_base, reset, state_in=None, want_state=False):
    independent = seq < LRU_SEGS * 128
    per_step = LRU_SEGS if independent else 1
    rows = seq * per_step
    nstep = nbatch // per_step
    rb = row_base // rows
    nblk = C_WIDTH // 128

    def col(k):
        return pl.BlockSpec((rows, 128), lambda b, c: (rb + b, (C_OFF + k * C_WIDTH) // 128 + c))

    vec = lambda rows: pl.BlockSpec((rows, 1, 128), lambda b, c: (0, 0, c))
    mat = pl.BlockSpec((2, 1, 128, 128), lambda b, c: (0, c, 0, 0))
    in_specs = [col(0), col(1),
                pl.BlockSpec((4, 128), lambda b, c: (0, c)), pl.BlockSpec((1, 128), lambda b, c: (0, c)),
                mat, vec(2), mat, vec(2), vec(2)]
    args = [proj, proj, lp['conv_w'], lp['conv_b'], lp['wa'], lp['ba'], lp['wx'], lp['bx'], lp['lam']]
    if state_in is not None:
        in_specs.append(pl.BlockSpec((1, 2, 128), lambda b, c: (b, 0, c)))
        args.append(state_in)
    out_specs = [pl.BlockSpec((rows, 128), lambda b, c: (b, c))]
    out_shape = [jax.ShapeDtypeStruct((nbatch * seq, C_WIDTH), BF16)]
    if want_state:
        out_specs.append(pl.BlockSpec((per_step, 2, 128), lambda b, c: (b, 0, c)))
        out_shape.append(jax.ShapeDtypeStruct((nbatch, 2, C_WIDTH), F32))
    kern = functools.partial(_lru_kernel_flat, rows=rows, reset=reset, independent=independent,
                             has_state_in=state_in is not None, has_state_out=want_state)
    scan_rows = rows + LRU_SEGS * LRU_PITCH_PAD
    return pl.pallas_call(
        kern,
        grid=(nstep, nblk),
        in_specs=in_specs,
        out_specs=out_specs,
        out_shape=out_shape,
        scratch_shapes=[pltpu.VMEM((rows + 16, 128), F32), pltpu.VMEM((2, scan_rows, 128), F32),
                        pltpu.VMEM((2, scan_rows, 128), F32)],
        compiler_params=_cparams(("arbitrary", "arbitrary")),
        name="lru_%d" % seq,
    )(*args)


MERGE_TM = 512
ROUTER_ROWS = 16


def _top2_combine(logits):
    idx = lax.broadcasted_iota(jnp.int32, logits.shape, 0).astype(F32)
    none = float(N_EXPERTS)
    m1 = jnp.max(logits, axis=0, keepdims=True)
    i1 = jnp.min(jnp.where(logits == m1, idx, none), axis=0, keepdims=True)
    rest = jnp.where(idx == i1, -jnp.inf, logits)
    m2 = jnp.max(rest, axis=0, keepdims=True)
    i2 = jnp.min(jnp.where(rest == m2, idx, none), axis=0, keepdims=True)
    e = jnp.exp(m2 - m1)
    w1 = 1.0 / (1.0 + e)
    return jnp.where(idx == i1, w1, 0.0) + jnp.where(idx == i2, e * w1, 0.0)


def _merge_body(xc_ref, xl_ref, m_ref, z0_ref, z1_ref, z2_ref, ac_ref, al_ref, hc_ref, hl_ref, lc_ref, ll_ref,
                wba_ref, wbb_ref, wbc_ref, wo_ref, g_ref, r_ref, xoc_ref, xol_ref, u_ref, c_ref):
    tm = MERGE_TM
    mg = (z0_ref[...].astype(F32) * _dot(_pair_read(ac_ref, al_ref, tm), wba_ref[...])
          + z1_ref[...].astype(F32) * _dot(_pair_read(hc_ref, hl_ref, tm), wbb_ref[...])
          + z2_ref[...].astype(F32) * _dot(_pair_read(lc_ref, ll_ref, tm), wbc_ref[...]))
    x = _pair_read(xc_ref, xl_ref, tm) + m_ref[0, 2] * _dot(mg.astype(BF16), wo_ref[...])
    _pair_write(xoc_ref, xol_ref, tm, x)
    u =_rms(x, g_ref[...]) * (1.0 + m_ref[0, 4]) + m_ref[0, 3]
    ub = u.astype(BF16)
    u_ref[...] = ub
    if r_ref is not None:
        ul = (u - ub.astype(F32)).astype(BF16)
        logits = _dot_nt(r_ref[0], ub) + (_dot_nt(r_ref[1], ub) + _dot_nt(r_ref[0], ul))
        comb = _top2_combine(logits[0:N_EXPERTS])
        comb = jnp.concatenate([comb, jnp.zeros((128 - N_EXPERTS, tm), F32)], axis=0)
        c_ref[...] = comb.T


def _merge_kernel_dense(*refs):
    _merge_body(*refs[:17], None, *refs[17:], None)


def _merge(x_pair, mod_l, gates, mixers, lp, router=None):
    tm = MERGE_TM
    row = lambda i: (i, 0)
    const = lambda i: (0, 0)
    res = lambda shape: pl.BlockSpec(shape, const, pipeline_mode=pl.Buffered(1))
    in_specs = _pair_specs(tm) + [
        pl.BlockSpec((1, 6, 1, D_MODEL), lambda i: (_mod_row(i, tm), 0, 0, 0)),
        pl.BlockSpec((tm, D_MODEL), lambda i: (i, 0)),
        pl.BlockSpec((tm, D_MODEL), lambda i: (i, 1)),
        pl.BlockSpec((tm, D_MODEL), lambda i: (i, 2))]
    in_specs += _pair_specs(tm, 512) * 3
    in_specs += [res((A_WIDTH, D_MODEL)), res((B_WIDTH, D_MODEL)), res((C_WIDTH, D_MODEL)), res((D_MODEL, D_MODEL)),
                 pl.BlockSpec((1, D_MODEL), const)]
    args = [*x_pair, mod_l, gates, gates, gates, *mixers, lp['w_ba'], lp['w_bb'], lp['w_bc'], lp['w_out'],
            lp['norm_ffn']]
    out_specs = _pair_specs(tm) + [pl.BlockSpec((tm, D_MODEL), row)]
    out_shape = _pair_shapes(D_MODEL, F32) + [jax.ShapeDtypeStruct((N_TOK, D_MODEL), BF16)]
    kern = _merge_kernel_dense
    if router is not None:
        in_specs.append(pl.BlockSpec((2, ROUTER_ROWS, D_MODEL), lambda i: (0, 0, 0)))
        args.append(router)
        out_specs.append(pl.BlockSpec((tm, 128), row))
        out_shape.append(jax.ShapeDtypeStruct((N_TOK, 128), F32))
        kern = _merge_body
    return pl.pallas_call(
        kern,
        grid=(N_TOK // tm,),
        in_specs=in_specs,
        out_specs=out_specs,
        out_shape=out_shape,
        compiler_params=_cparams(("arbitrary",)),
        name="merge",
    )(*args)


FFN_TM = 512
FFN_CHUNK = 256


def _swiglu_partial(u, wg, wu, wd):
    h = (_silu(_dot(u, wg)) * _dot(u, wu)).astype(BF16)
    return _dot(h, wd)


def _ffn_kernel(u_ref, xc_ref, xl_ref, m_ref, wg_ref, wu_ref, wd_ref, oc_ref, ol_ref):
    u = u_ref[...]
    acc = None
    for c in range(FFN_DIM // FFN_CHUNK):
        sl = slice(c * FFN_CHUNK, (c + 1) * FFN_CHUNK)
        y = _swiglu_partial(u, wg_ref[:, sl], wu_ref[:, sl], wd_ref[sl, :])
        acc = y if acc is None else acc + y
    _pair_write(oc_ref, ol_ref, FFN_TM, _pair_read(xc_ref, xl_ref, FFN_TM) + m_ref[0, 5] * acc)


def _ffn(u, x_pair, mod_l, wg, wu, wd):
    tm = FFN_TM
    res = lambda shape: pl.BlockSpec(shape, lambda i: (0, 0), pipeline_mode=pl.Buffered(1))
    return pl.pallas_call(
        _ffn_kernel,
        grid=(N_TOK // tm,),
        in_specs=[pl.BlockSpec((tm, D_MODEL), lambda i: (i, 0))] + _pair_specs(tm) + [
            pl.BlockSpec((1, 6, 1, D_MODEL), lambda i: (_mod_row(i, tm), 0, 0, 0)),
            res((D_MODEL, FFN_DIM)), res((D_MODEL, FFN_DIM)), res((FFN_DIM, D_MODEL))],
        out_specs=_pair_specs(tm),
        out_shape=_pair_shapes(D_MODEL, F32),
        compiler_params=_cparams(("arbitrary",)),
        name="ffn",
    )(u, *x_pair, mod_l, wg, wu, wd)


MOE_TM = 512
MOE_NT = N_TOK // MOE_TM
MOE_PIECE = 16
MOE_SLAB = 2 * MOE_TM + N_EXPERTS * MOE_PIECE
MOE_BLK = 512
MOE_NB = -(-(2 * N_TOK + MOE_NT * N_EXPERTS * (MOE_PIECE - 1) + N_EXPERTS * (MOE_BLK - 1)) // MOE_BLK)
MOE_ROWS = MOE_NB * MOE_BLK
MOE_FCHUNK = 512


def _moe_count_kernel(c_ref, n_ref):
    n_ref[0] = jnp.sum(jnp.where(c_ref[...] > 0.0, 1.0, 0.0), axis=0, keepdims=True)


def _moe_count(comb):
    return pl.pallas_call(
        _moe_count_kernel,
        grid=(MOE_NT,),
        in_specs=[pl.BlockSpec((MOE_TM, 128), lambda i: (i, 0))],
        out_specs=pl.BlockSpec((1, 1, 128), lambda i: (i, 0, 0)),
        out_shape=jax.ShapeDtypeStruct((MOE_NT, 1, 128), F32),
        compiler_params=_cparams(("arbitrary",)),
        name="moe_count",
    )(comb)


def _moe_tables(counts):
    cnt = counts[:, 0, :N_EXPERTS].astype(jnp.int32)
    pc = (cnt + MOE_PIECE - 1) // MOE_PIECE * MOE_PIECE
    lo = jnp.cumsum(pc, axis=1) - pc
    tot = jnp.sum(pc, axis=0)
    totp = (tot + MOE_BLK - 1) // MOE_BLK * MOE_BLK
    goff = (jnp.cumsum(totp) - totp)[None, :] + jnp.cumsum(pc, axis=0) - pc
    ends = jnp.cumsum(totp // MOE_BLK)
    nact = ends[-1]
    s = jnp.minimum(jnp.arange(MOE_NB, dtype=jnp.int32), nact - 1)
    blk_e = jnp.minimum(jnp.sum((s[:, None] >= ends[None, :]).astype(jnp.int32), axis=1), N_EXPERTS - 1)
    lo_vec = jnp.pad(lo.astype(F32), ((0, 0), (0, 128 - N_EXPERTS))).reshape(MOE_NT, 1, 128)
    return dict(lo=lo.reshape(-1), npc=(pc // MOE_PIECE).reshape(-1), goff=goff.reshape(-1),
                nrow=jnp.sum(pc, axis=1), blk_e=blk_e, nact=nact.reshape(1), lo_vec=lo_vec)


def _moe_pieces(tile, lo_s, npc_s, goff_s, make, wait):
    for e in range(N_EXPERTS):
        k = tile * N_EXPERTS + e

        def body(p, carry, k=k):
            local = pl.multiple_of(lo_s[k] + p * MOE_PIECE, MOE_PIECE)
            glob = pl.multiple_of(goff_s[k] + p * MOE_PIECE, MOE_PIECE)
            cp = make(local, glob)
            if wait:
                cp.wait()
            else:
                cp.start()
            return carry

        lax.fori_loop(0, npc_s[k], body, 0)


def _slot_pair(slot):
    sel = slot >= 0.0
    s_hi = jnp.max(slot, axis=-1, keepdims=True)
    s_lo = jnp.min(jnp.where(sel, slot, float(MOE_SLAB)), axis=-1, keepdims=True)
    return sel, s_hi, s_lo


def _one_hot_rows(s):
    srow = lax.broadcasted_iota(jnp.int32, (MOE_TM, MOE_SLAB), 1).astype(F32)
    return jnp.where(srow == s, 1.0, 0.0)


def _moe_gather_kernel(lo_s, npc_s, goff_s, u_ref, c_ref, lov_ref, tri_ref, init_ref, xs_ref, slot_ref,
                       slab_ref, sem):
    del init_ref
    i = pl.program_id(0)
    routed = c_ref[...] > 0.0
    rank = _dot(tri_ref[...], jnp.where(routed, 1.0, 0.0).astype(BF16))
    slot = jnp.where(routed, lov_ref[0] + rank, -1.0)
    slot_ref[...] = slot
    _, s_hi, s_lo = _slot_pair(slot)
    pt = jnp.maximum(_one_hot_rows(s_hi), _one_hot_rows(s_lo)).astype(BF16)
    buf = i % 2
    slab_ref[buf] = _dot_tn(pt, u_ref[...]).astype(BF16)

    def pieces(tile, b, wait):
        _moe_pieces(tile, lo_s, npc_s, goff_s, lambda local, glob: pltpu.make_async_copy(
            slab_ref.at[b, pl.ds(local, MOE_PIECE), :], xs_ref.at[pl.ds(glob, MOE_PIECE), :], sem.at[b]), wait)

    pieces(i, buf, False)

    @pl.when(i > 0)
    def _():
        pieces(i - 1, 1 - buf, True)

    @pl.when(i == MOE_NT - 1)
    def _():
        pieces(i, buf, True)


def _moe_gather(u, comb, tab):
    tri = jnp.asarray(np.tril(np.ones((MOE_TM, MOE_TM), np.float32), -1), BF16)
    init = jnp.zeros((MOE_ROWS, D_MODEL), BF16)
    row = lambda i, *_: (i, 0)
    grid_spec = pltpu.PrefetchScalarGridSpec(
        num_scalar_prefetch=3,
        grid=(MOE_NT,),
        in_specs=[pl.BlockSpec((MOE_TM, D_MODEL), row), pl.BlockSpec((MOE_TM, 128), row),
                  pl.BlockSpec((1, 1, 128), lambda i, *_: (i, 0, 0)),
                  pl.BlockSpec((MOE_TM, MOE_TM), lambda i, *_: (0, 0)),
                  pl.BlockSpec(memory_space=pl.ANY)],
        out_specs=[pl.BlockSpec(memory_space=pl.ANY), pl.BlockSpec((MOE_TM, 128), row)],
        scratch_shapes=[pltpu.VMEM((2, MOE_SLAB, D_MODEL), BF16), pltpu.SemaphoreType.DMA((2,))],
    )
    return pl.pallas_call(
        _moe_gather_kernel,
        grid_spec=grid_spec,
        out_shape=[jax.ShapeDtypeStruct((MOE_ROWS, D_MODEL), BF16), jax.ShapeDtypeStruct((N_TOK, 128), F32)],
        input_output_aliases={7: 0},
        compiler_params=_cparams(("arbitrary",)),
        name="moe_gather",
    )(tab['lo'], tab['npc'], tab['goff'], u, comb, tab['lo_vec'], tri, init)


class _ExpertWeights:
    def __init__(self, hbm, resident, stages, sem):
        self.hbm, self.resident, self.stages, self.sem = hbm, resident, stages, sem

    def _copies(self, e, c):
        sl = slice(c * MOE_FCHUNK, (c + 1) * MOE_FCHUNK)
        b = c % 2
        srcs = (self.hbm[0].at[e, :, sl], self.hbm[1].at[e, :, sl], self.hbm[2].at[e, sl, :])
        return [pltpu.make_async_copy(src, stage.at[b], self.sem.at[b, k])
                for k, (src, stage) in enumerate(zip(srcs, self.stages))]

    def start(self, e, c):
        for cp in self._copies(e, c):
            cp.start()

    def finish(self, e, c):
        for cp in self._copies(e, c):
            cp.wait()
        sl = slice(c * MOE_FCHUNK, (c + 1) * MOE_FCHUNK)
        b = c % 2
        self.resident[0][:, sl] = self.stages[0][b].astype(BF16)
        self.resident[1][:, sl] = self.stages[1][b].astype(BF16)
        self.resident[2][sl, :] = self.stages[2][b].astype(BF16)


def _moe_block(x_ref, y_ref, w, next_expert):
    nch = EXPERT_DIM // MOE_FCHUNK
    wg_ref, wu_ref, wd_ref = w.resident
    if next_expert is not None:
        w.start(next_expert, 0)
        w.start(next_expert, 1)
    x = x_ref[...]
    acc = None
    for c in range(nch):
        sl = slice(c * MOE_FCHUNK, (c + 1) * MOE_FCHUNK)
        y = _swiglu_partial(x, wg_ref[:, sl], wu_ref[:, sl], wd_ref[sl, :])
        acc = y if acc is None else acc + y
        if next_expert is not None:
            w.finish(next_expert, c)
            if c + 2 < nch:
                w.start(next_expert, c + 2)
    y_ref[...] = acc.astype(BF16)


def _moe_expert_kernel(be_s, nact_s, x_ref, wg_hbm, wu_hbm, wd_hbm, y_ref, wg_ref, wu_ref, wd_ref,
                       g_stage, u_stage, d_stage, sem):
    s = pl.program_id(0)
    nch = EXPERT_DIM // MOE_FCHUNK
    active = s < nact_s[0]
    e = be_s[s]
    e_next = be_s[jnp.minimum(s + 1, MOE_NB - 1)]
    hands_over = e_next != e
    w = _ExpertWeights((wg_hbm, wu_hbm, wd_hbm), (wg_ref, wu_ref, wd_ref), (g_stage, u_stage, d_stage), sem)

    @pl.when(s == 0)
    def _():
        w.start(e, 0)
        for c in range(nch):
            if c + 1 < nch:
                w.start(e, c + 1)
            w.finish(e, c)

    @pl.when(jnp.logical_not(active))
    def _():
        y_ref[...] = jnp.zeros_like(y_ref)

    @pl.when(active & hands_over)
    def _():
        _moe_block(x_ref, y_ref, w, e_next)

    @pl.when(active & jnp.logical_not(hands_over))
    def _():
        _moe_block(x_ref, y_ref, w, None)


def _moe_expert(xs, tab, wg, wu, wd):
    blk = lambda s, be, nact: (jnp.minimum(s, nact[0] - 1), 0)
    hbm = pl.BlockSpec(memory_space=pl.ANY)
    grid_spec = pltpu.PrefetchScalarGridSpec(
        num_scalar_prefetch=2,
        grid=(MOE_NB,),
        in_specs=[pl.BlockSpec((MOE_BLK, D_MODEL), blk), hbm, hbm, hbm],
        out_specs=pl.BlockSpec((MOE_BLK, D_MODEL), lambda s, be, nact: (s, 0)),
        scratch_shapes=[pltpu.VMEM((D_MODEL, EXPERT_DIM), BF16), pltpu.VMEM((D_MODEL, EXPERT_DIM), BF16),
                        pltpu.VMEM((EXPERT_DIM, D_MODEL), BF16),
                        pltpu.VMEM((2, D_MODEL, MOE_FCHUNK), F32), pltpu.VMEM((2, D_MODEL, MOE_FCHUNK), F32),
                        pltpu.VMEM((2, MOE_FCHUNK, D_MODEL), F32), pltpu.SemaphoreType.DMA((2, 3))],
    )
    return pl.pallas_call(
        _moe_expert_kernel,
        grid_spec=grid_spec,
        out_shape=jax.ShapeDtypeStruct((MOE_ROWS, D_MODEL), BF16),
        compiler_params=_cparams(("arbitrary",)),
        name="moe_expert",
    )(tab['blk_e'], tab['nact'], xs, wg, wu, wd)


def _moe_combine_kernel(lo_s, npc_s, goff_s, nrow_s, y_ref, slot_ref, c_ref, xc_ref, xl_ref, m_ref, g_ref,
                        oc_ref, ol_ref, slab_ref, sem):
    i = pl.program_id(0)
    buf = i % 2

    def pieces(tile, b, wait):
        _moe_pieces(tile, lo_s, npc_s, goff_s, lambda local, glob: pltpu.make_async_copy(
            y_ref.at[pl.ds(glob, MOE_PIECE), :], slab_ref.at[b, pl.ds(local, MOE_PIECE), :], sem.at[b]), wait)

    @pl.when(i == 0)
    def _():
        pieces(i, buf, False)

    @pl.when(i + 1 < MOE_NT)
    def _():
        pieces(i + 1, 1 - buf, False)

    slot = slot_ref[...]
    comb = c_ref[...]
    sel, s_hi, s_lo = _slot_pair(slot)
    w_hi = jnp.sum(jnp.where(sel & (slot == s_hi), comb, 0.0), axis=-1, keepdims=True)
    w_lo = jnp.sum(jnp.where(sel & (slot == s_lo), comb, 0.0), axis=-1, keepdims=True)
    w_lo = jnp.where(s_lo == s_hi, 0.0, w_lo)
    srow = lax.broadcasted_iota(jnp.int32, (MOE_SLAB, 1), 0)
    pieces(i, buf, True)
    ys = jnp.where(srow < nrow_s[i], slab_ref[buf], jnp.zeros((), BF16))
    out = (w_hi * _dot(_one_hot_rows(s_hi).astype(BF16), ys) + w_lo * _dot(_one_hot_rows(s_lo).astype(BF16), ys))
    x = _pair_read(xc_ref, xl_ref, MOE_TM)
    _pair_write(oc_ref, ol_ref, MOE_TM, _rms(x + m_ref[0, 5] * out, g_ref[...]))


def _moe_combine(y, slot, comb, x_pair, mod_l, final_g, tab):
    row = lambda i, *_: (i, 0)
    grid_spec = pltpu.PrefetchScalarGridSpec(
        num_scalar_prefetch=4,
        grid=(MOE_NT,),
        in_specs=[pl.BlockSpec(memory_space=pl.ANY),
                  pl.BlockSpec((MOE_TM, 128), row), pl.BlockSpec((MOE_TM, 128), row)] + _pair_specs(MOE_TM) + [
                  pl.BlockSpec((1, 6, 1, D_MODEL), lambda i, *_: (_mod_row(i, MOE_TM), 0, 0, 0)),
                  pl.BlockSpec((1, D_MODEL), lambda i, *_: (0, 0))],
        out_specs=_pair_specs(MOE_TM),
        scratch_shapes=[pltpu.VMEM((2, MOE_SLAB, D_MODEL), BF16), pltpu.SemaphoreType.DMA((2,))],
    )
    return pl.pallas_call(
        _moe_combine_kernel,
        grid_spec=grid_spec,
        out_shape=_pair_shapes(D_MODEL, F32),
        compiler_params=_cparams(("arbitrary",)),
        name="moe_combine",
    )(tab['lo'], tab['npc'], tab['goff'], tab['nrow'], y, slot, comb, *x_pair, mod_l, final_g)


def _moe(u, comb, x_pair, mod_l, final_g, wg, wu, wd):
    tab = _moe_tables(_moe_count(comb))
    xs, slot = _moe_gather(u, comb, tab)
    y = _moe_expert(xs, tab, wg, wu, wd)
    return _moe_combine(y, slot, comb, x_pair, mod_l, final_g, tab)


def _block_diag(w):
    w = w.reshape(2, C_WIDTH // 128, 2, C_BLOCK, C_BLOCK)
    z = jnp.zeros_like(w[:, :, 0])
    top = jnp.concatenate([w[:, :, 0], z], axis=-1)
    bot = jnp.concatenate([z, w[:, :, 1]], axis=-1)
    return jnp.concatenate([top, bot], axis=-2).astype(BF16)


def kernel(x_prompt, x_sample, cache_attn_k, cache_attn_v, state_hgrn, state_lru, c, c_ctx, w_mod, b_mod, norm_mix_g, norm_ffn_g, w_in, attn_sink, hgrn_lower, hgrn_norm_g, lru_conv_w, lru_conv_b, lru_wa, lru_ba, lru_wx, lru_bx, lru_lambda, w_branch_a, w_branch_b, w_branch_c, w_out, ffn_w_gate, ffn_w_up, ffn_w_down, moe_router, moe_w_gate, moe_w_up, moe_w_down, final_norm_g):
    assert DEPTH == 2
    x = (x_prompt.reshape(N_CTX_TOK, D_MODEL), x_sample.reshape(N_LAT_TOK, D_MODEL))
    cond =jnp.concatenate([c_ctx[None, :], c, jnp.zeros((N_COND - 1 - DEC_BATCH, D_MODEL), F32)], axis=0)
    mods = _modulation(cond, w_mod, b_mod).reshape(DEPTH, N_COND, 6, 1, D_MODEL)

    lb_cum = jnp.cumsum(jax.nn.softmax(hgrn_lower.astype(F32), axis=0), axis=0)
    lb_layers = lb_cum - lb_cum[:1]
    cos, sin = _rope_tables()
    hconst = _hgrn_constants(False) + _hgrn_constants(True)

    ks, vs, hs, ls = [], [], [], []
    for l in range(DEPTH):
        w_in_l = w_in[l].astype(BF16)
        lb = lb_layers[l]
        lbp = jnp.stack([jnp.log(lb), jnp.log1p(-lb), 1.0 - lb]).reshape(3, 2, 1, B_WIDTH)
        lp = dict(
            conv_w=lru_conv_w[l], conv_b=lru_conv_b[l].reshape(1, C_WIDTH),
            wa=_block_diag(lru_wa[l]), wx=_block_diag(lru_wx[l]),
            ba=lru_ba[l].reshape(2, 1, C_WIDTH), bx=lru_bx[l].reshape(2, 1, C_WIDTH),
            lam=lru_lambda[l].reshape(2, 1, C_WIDTH),
            w_ba=w_branch_a[l].astype(BF16), w_bb=w_branch_b[l].astype(BF16), w_bc=w_branch_c[l].astype(BF16),
            w_out=w_out[l].astype(BF16), norm_ffn=norm_ffn_g[l].reshape(1, D_MODEL))
        mod_l = mods[l]

        proj, gates = _in_proj(x, mod_l, norm_mix_g[l].reshape(1, D_MODEL), w_in_l)

        attn_c = _ctx_attention(proj, attn_sink[l])
        attn_l = _lat_attention(proj, attn_sink[l], cache_attn_k[:, l].reshape(DEC_BATCH, PAST_LEN, 128),
                                cache_attn_v[:, l].reshape(DEC_BATCH, PAST_LEN, 128), cos, sin)

        ng = hgrn_norm_g[l].reshape(1, B_KEY_DIM)
        hg_c, s_h = _hgrn(proj, lbp, ng, hconst, nbatch=BATCH, seq=SEQ, row_base=0, want_state=True)
        hg_l, = _hgrn(proj, lbp, ng, hconst, nbatch=DEC_BATCH, seq=DEC_SEQ, row_base=N_CTX_TOK,
                      state_in=state_hgrn[:, l])

        lr_c, s_l = _lru(proj, lp, nbatch=BATCH, seq=SEQ, row_base=0, reset=True, want_state=True)
        lr_l, = _lru(proj, lp, nbatch=DEC_BATCH, seq=DEC_SEQ, row_base=N_CTX_TOK, reset=False,
                     state_in=state_lru[:, l])
        mixers = (attn_c, attn_l, hg_c, hg_l, lr_c, lr_l)

        ks.append(proj[:N_CTX_TOK, KA_OFF:KA_OFF + 128].reshape(BATCH, SEQ, A_KV_HEADS, A_HEAD_DIM))
        vs.append(proj[:N_CTX_TOK, VA_OFF:VA_OFF + 128].reshape(BATCH, SEQ, A_KV_HEADS, A_HEAD_DIM))
        hs.append(s_h)
        ls.append(s_l)

        m = l // 2
        if l % 2 == 0:
            xc, xl, u = _merge(x, mod_l, gates, mixers, lp)
            x = _ffn(u, (xc, xl), mod_l, ffn_w_gate[m].astype(BF16), ffn_w_up[m].astype(BF16), ffn_w_down[m].astype(BF16))
        else:
            router = jnp.pad(moe_router[m].T, ((0, ROUTER_ROWS - N_EXPERTS), (0, 0)))
            r_hi = router.astype(BF16)
            router = jnp.stack([r_hi, (router - r_hi.astype(F32)).astype(BF16)])
            xc, xl, u, comb = _merge(x, mod_l, gates, mixers, lp, router=router)
            x = _moe(u, comb, (xc, xl), mod_l, final_norm_g.reshape(1, D_MODEL), moe_w_gate[m], moe_w_up[m],
                     moe_w_down[m])

    y_prompt = x[0].reshape(BATCH, SEQ, D_MODEL)
    y_sample = x[1].reshape(DEC_BATCH, DEC_SEQ, D_MODEL)
    return (y_prompt, y_sample, jnp.stack(ks, axis=1), jnp.stack(vs, axis=1), jnp.stack(hs, axis=1),
            jnp.stack(ls, axis=1))
```

```python
import functools

import numpy as np
import jax
import jax.numpy as jnp
from jax import lax
from jax.experimental import pallas as pl
from jax.experimental.pallas import tpu as pltpu

F32 = jnp.float32
BF16 = jnp.bfloat16
HIGHEST = lax.Precision.HIGHEST

D_MODEL = 1024
BATCH = 32
SEQ = 256
DEPTH = 2
DEC_BATCH = 4
DEC_SEQ = 2048
PAST_LEN = 512
GRID_W = 64
RMS_EPS = 1e-6
A_HEADS = 8
A_KV_HEADS = 2
A_GROUP = 4
A_HEAD_DIM = 64
A_WIDTH = 512
A_WINDOW = 128
ROPE_THETA = 10000.0
NEG_INF = -1e30
B_HEADS = 4
B_KEY_DIM = 128
B_WIDTH = 512
C_WIDTH = 512
C_HEADS = 8
C_BLOCK = 64
C_EXP = 8.0
FFN_DIM = 2816
N_EXPERTS = 8
EXPERT_DIM = 3584

N_CTX_TOK = BATCH * SEQ
N_LAT_TOK = DEC_BATCH * DEC_SEQ
N_TOK = N_CTX_TOK + N_LAT_TOK
N_COND = 8

QA_OFF = 0
KA_OFF = 512
VA_OFF = 640
B_OFF = 768
C_OFF = 3328
IN_COLS = 7424
IN_MIX_COLS = IN_COLS - 3 * D_MODEL

V7X_VMEM_LIMIT = 56 * 1024 * 1024
HG_CHUNK = 128
HG_LEVELS = (64, 32, 16, 8, 4, 2, 1)
HG_MAT_LEVELS = (4, 2)
HG_CHAINS = 32
LRU_SEGS = 16
LRU_PITCH_PAD = 4


def _cparams(sem, vmem=V7X_VMEM_LIMIT):
    return pltpu.CompilerParams(dimension_semantics=sem, vmem_limit_bytes=vmem)


def _mod_row(i, tm):
    nct = N_CTX_TOK // tm
    return jnp.where(i < nct, 0, 1 + (i - nct) // (DEC_SEQ // tm))


def _pair_specs(tm, cols=D_MODEL):
    nct = N_CTX_TOK // tm
    return [pl.BlockSpec((tm, cols), lambda i, *_: (jnp.minimum(i, nct - 1), 0)),
            pl.BlockSpec((tm, cols), lambda i, *_: (jnp.maximum(i - nct, 0), 0))]


def _pair_shapes(cols, dtype):
    return [jax.ShapeDtypeStruct((N_CTX_TOK, cols), dtype), jax.ShapeDtypeStruct((N_LAT_TOK, cols), dtype)]


def _pair_read(c_ref, l_ref, tm):
    return jnp.where(pl.program_id(0) < N_CTX_TOK // tm, c_ref[...], l_ref[...])


def _pair_write(c_ref, l_ref, tm, value):
    is_ctx = pl.program_id(0) < N_CTX_TOK // tm

    @pl.when(is_ctx)
    def _():
        c_ref[...] = value

    @pl.when(jnp.logical_not(is_ctx))
    def _():
        l_ref[...] = value


def _sigmoid(x):
    return 0.5 * jnp.tanh(0.5 * x) + 0.5


def _silu(x):
    return x * _sigmoid(x)


def _log_sigmoid(z):
    return jnp.minimum(z, 0.0) - jnp.log1p(jnp.exp(-jnp.abs(z)))


def _rms(x, g):
    ms = jnp.mean(x * x, axis=-1, keepdims=True)
    return x * lax.rsqrt(ms + RMS_EPS) * g


def _dot(a, b):
    return jnp.dot(a, b, preferred_element_type=F32)


def _dot_nt(a, b):
    return lax.dot_general(a, b, (((1,), (1,)), ((), ())), preferred_element_type=F32)


def _dot_tn(a, b):
    return lax.dot_general(a, b, (((0,), (0,)), ((), ())), preferred_element_type=F32)


def _mod_kernel(c_ref, w_ref, b_ref, o_ref):
    a = _silu(c_ref[...])
    o_ref[0] = jnp.dot(a, w_ref[0], preferred_element_type=F32, precision=HIGHEST) + b_ref[0]


def _modulation(cond, w_mod, b_mod):
    tn = 1536
    return pl.pallas_call(
        _mod_kernel,
        grid=(DEPTH, 6 * D_MODEL // tn),
        in_specs=[pl.BlockSpec((N_COND, D_MODEL), lambda l, j: (0, 0)),
                  pl.BlockSpec((1, D_MODEL, tn), lambda l, j: (l, 0, j)),
                  pl.BlockSpec((1, 1, tn), lambda l, j: (l, 0, j))],
        out_specs=pl.BlockSpec((1, N_COND, tn), lambda l, j: (l, 0, j)),
        out_shape=jax.ShapeDtypeStruct((DEPTH, N_COND, 6 * D_MODEL), F32),
        compiler_params=_cparams(("arbitrary", "arbitrary")),
        name="modulation",
    )(cond, w_mod, b_mod.reshape(DEPTH, 1, 6 * D_MODEL))


IN_TM = 512
IN_TN = 256


def _in_proj_kernel(xc_ref, xl_ref, m_ref, g_ref, w_ref, o_ref, gate_ref):
    u = _rms(_pair_read(xc_ref, xl_ref, IN_TM), g_ref[...]) * (1.0 + m_ref[0, 1]) + m_ref[0, 0]
    ub = u.astype(BF16)
    n_mix = IN_MIX_COLS // IN_TN
    for j in range(IN_COLS // IN_TN):
        y = _dot(ub, w_ref[:, j * IN_TN:(j + 1) * IN_TN])
        if j < n_mix:
            o_ref[:, j * IN_TN:(j + 1) * IN_TN] = y
        else:
            gate_ref[:, (j - n_mix) * IN_TN:(j - n_mix + 1) * IN_TN] = _sigmoid(y).astype(BF16)


def _in_proj(x_pair, mod_l, g, w):
    tm = IN_TM
    return pl.pallas_call(
        _in_proj_kernel,
        grid=(N_TOK // tm,),
        in_specs=_pair_specs(tm) + [
            pl.BlockSpec((1, 6, 1, D_MODEL), lambda i: (_mod_row(i, tm), 0, 0, 0)),
            pl.BlockSpec((1, D_MODEL), lambda i: (0, 0)),
            pl.BlockSpec((D_MODEL, IN_COLS), lambda i: (0, 0), pipeline_mode=pl.Buffered(1))],
        out_specs=[pl.BlockSpec((tm, IN_MIX_COLS), lambda i: (i, 0)),
                   pl.BlockSpec((tm, 3 * D_MODEL), lambda i: (i, 0))],
        out_shape=[jax.ShapeDtypeStruct((N_TOK, IN_MIX_COLS), F32),
                   jax.ShapeDtypeStruct((N_TOK, 3 * D_MODEL), BF16)],
        compiler_params=_cparams(("arbitrary",)),
        name="in_proj",
    )(*x_pair, mod_l, g, w)


LOG2E = 1.4426950408889634
A_LOGIT_SCALE = A_HEAD_DIM ** -0.5 * LOG2E


def _softmax_pv(s_list, v_list, sink):
    m = sink
    for s in s_list:
        m = jnp.maximum(m, jnp.max(s, axis=-1, keepdims=True))
    den = jnp.exp2(sink - m)
    out = None
    for s, v in zip(s_list, v_list):
        p = jnp.exp2(s - m)
        den = den + jnp.sum(p, axis=-1, keepdims=True)
        o = _dot(p.astype(BF16), v)
        out = o if out is None else out + o
    return out / den


def _sink_column(sink_ref, h, rows_per_head):
    n = A_GROUP * rows_per_head
    row = lax.broadcasted_iota(jnp.int32, (n, 1), 0)
    col = jnp.full((n, 1), sink_ref[A_GROUP * h + A_GROUP - 1], F32)
    for g in range(A_GROUP - 2, -1, -1):
        col = jnp.where(row < (g + 1) * rows_per_head, sink_ref[A_GROUP * h + g], col)
    return col


def _ctx_attn_kernel(sink_ref, q_ref, k_ref, v_ref, o_ref):
    q = q_ref[...] * A_LOGIT_SCALE
    k = k_ref[...]
    v = v_ref[...]
    outs = []
    for h in range(A_KV_HEADS):
        hs = slice(h * A_HEAD_DIM, (h + 1) * A_HEAD_DIM)
        kh = k[:, hs].astype(BF16)
        vh = v[:, hs].astype(BF16)
        qs = jnp.concatenate(
            [q[:, (A_GROUP * h + g) * A_HEAD_DIM:(A_GROUP * h + g + 1) * A_HEAD_DIM] for g in range(A_GROUP)],
            axis=0).astype(BF16)
        s = _dot_nt(qs, kh)
        o = _softmax_pv([s], [vh], _sink_column(sink_ref, h, SEQ) * LOG2E)
        outs.extend(o[g * SEQ:(g + 1) * SEQ] for g in range(A_GROUP))
    o_ref[...] = jnp.concatenate(outs, axis=-1).astype(BF16)


def _ctx_attention(proj, sink):
    return pl.pallas_call(
        _ctx_attn_kernel,
        grid=(BATCH,),
        in_specs=[pl.BlockSpec(memory_space=pltpu.SMEM),
                  pl.BlockSpec((SEQ, A_WIDTH), lambda b: (b, QA_OFF // A_WIDTH)),
                  pl.BlockSpec((SEQ, 128), lambda b: (b, KA_OFF // 128)),
                  pl.BlockSpec((SEQ, 128), lambda b: (b, VA_OFF // 128))],
        out_specs=pl.BlockSpec((SEQ, A_WIDTH), lambda b: (b, 0)),
        out_shape=jax.ShapeDtypeStruct((N_CTX_TOK, A_WIDTH), BF16),
        compiler_params=_cparams(("arbitrary",)),
        name="ctx_attention",
    )(sink, proj, proj, proj)


def _rope_tables():
    half = A_HEAD_DIM // 4
    inv = ROPE_THETA ** (-np.arange(half, dtype=np.float64) / half)
    t = np.arange(DEC_SEQ)
    row = (t // GRID_W)[:, None] * inv[None, :]
    col = (t % GRID_W)[:, None] * inv[None, :]
    cos = np.concatenate([np.cos(row), np.cos(row), np.cos(col), np.cos(col)], axis=1)
    sin = np.concatenate([-np.sin(row), np.sin(row), -np.sin(col), np.sin(col)], axis=1)
    return (jnp.asarray(np.tile(cos, (1, 2)), F32), jnp.asarray(np.tile(sin, (1, 2)), F32))


def _rope(x, cos, sin, first):
    part = jnp.where(first, pltpu.roll(x, 128 - 16, 1), pltpu.roll(x, 16, 1))
    return x * cos + part * sin


def _lat_attn_kernel(sink_ref, q_ref, km_ref, k0_ref, kp_ref, vm_ref, v0_ref, vp_ref, kc_ref, vc_ref,
                     cm_ref, c0_ref, cp_ref, sm_ref, s0_ref, sp_ref, o_ref):
    j = pl.program_id(1)
    blk = A_WINDOW
    lane = lax.broadcasted_iota(jnp.int32, (blk, 128), 1)
    first = (lane % 32) < 16
    c0 = c0_ref[...]
    s0 = s0_ref[...]
    q = q_ref[...] * A_LOGIT_SCALE
    qr = [_rope(q[:, c * 128:(c + 1) * 128], c0, s0, first) for c in range(A_WIDTH // 128)]
    kw = jnp.concatenate([_rope(km_ref[...], cm_ref[...], sm_ref[...], first),
                          _rope(k0_ref[...], c0, s0, first),
                          _rope(kp_ref[...], cp_ref[...], sp_ref[...], first)], axis=0)
    vw = jnp.concatenate([vm_ref[...], v0_ref[...], vp_ref[...]], axis=0)
    kc = kc_ref[0]
    vc = vc_ref[0]
    t = lax.broadcasted_iota(jnp.int32, (blk, 3 * blk), 0)
    s = lax.broadcasted_iota(jnp.int32, (blk, 3 * blk), 1) - blk
    kpos = j * blk + s
    keep = jnp.where((jnp.abs(s - t) <= A_WINDOW) & (kpos >= 0) & (kpos < DEC_SEQ), 1.0, 0.0)
    keep = jnp.concatenate([keep] * A_GROUP, axis=0) > 0.0
    outs = []
    for h in range(A_KV_HEADS):
        hs = slice(h * A_HEAD_DIM, (h + 1) * A_HEAD_DIM)
        heads = [A_GROUP * h + g for g in range(A_GROUP)]
        qs = jnp.concatenate(
            [qr[hd // 2][:, (hd % 2) * A_HEAD_DIM:(hd % 2 + 1) * A_HEAD_DIM] for hd in heads], axis=0).astype(BF16)
        sw = jnp.where(keep, _dot_nt(qs, kw[:, hs].astype(BF16)), NEG_INF)
        sc = _dot_nt(qs, kc[:, hs].astype(BF16))
        o = _softmax_pv([sw, sc], [vw[:, hs].astype(BF16), vc[:, hs].astype(BF16)],
                        _sink_column(sink_ref, h, blk) * LOG2E)
        outs.extend(o[g * blk:(g + 1) * blk] for g in range(A_GROUP))
    o_ref[...] = jnp.concatenate(outs, axis=-1).astype(BF16)


def _lat_attention(proj, sink, k_ctx, v_ctx, cos, sin):
    blk = A_WINDOW
    nb = DEC_SEQ // blk
    base = N_CTX_TOK // blk

    def rows(off):
        return lambda b, j: (base + b * nb + jnp.clip(j + off, 0, nb - 1))

    def kv_spec(col, off):
        r = rows(off)
        return pl.BlockSpec((blk, 128), lambda b, j: (r(b, j), col))

    def tab_spec(off):
        return pl.BlockSpec((blk, 128), lambda b, j: (jnp.clip(j + off, 0, nb - 1), 0))

    r0 = rows(0)
    ctx_spec = pl.BlockSpec((1, PAST_LEN, 128), lambda b, j: (b, 0, 0))
    return pl.pallas_call(
        _lat_attn_kernel,
        grid=(DEC_BATCH, nb),
        in_specs=[pl.BlockSpec(memory_space=pltpu.SMEM),
                  pl.BlockSpec((blk, A_WIDTH), lambda b, j: (r0(b, j), QA_OFF // A_WIDTH)),
                  kv_spec(KA_OFF // 128, -1), kv_spec(KA_OFF // 128, 0), kv_spec(KA_OFF // 128, 1),
                  kv_spec(VA_OFF // 128, -1), kv_spec(VA_OFF // 128, 0), kv_spec(VA_OFF // 128, 1),
                  ctx_spec, ctx_spec,
                  tab_spec(-1), tab_spec(0), tab_spec(1), tab_spec(-1), tab_spec(0), tab_spec(1)],
        out_specs=pl.BlockSpec((blk, A_WIDTH), lambda b, j: (b * nb + j, 0)),
        out_shape=jax.ShapeDtypeStruct((N_LAT_TOK, A_WIDTH), BF16),
        compiler_params=_cparams(("arbitrary", "arbitrary")),
        name="lat_attention",
    )(sink, proj, proj, proj, proj, proj, proj, proj, k_ctx, v_ctx, cos, cos, cos, sin, sin, sin)


def _hgrn_constants(reverse):
    L = HG_CHUNK
    t = np.arange(L)[:, None]
    s = np.arange(L)[None, :]
    tri = (s <= t).astype(np.float32)
    blocks = [tri]
    masks = []
    for b in HG_LEVELS:
        anchor = (t // (2 * b)) * 2 * b + b - 1
        right = np.where(t % (2 * b) >= b, 1.0, -1.0)
        if b in HG_MAT_LEVELS:
            blocks.append(right * (tri - (s <= anchor).astype(np.float32)))
        masks.append(((t // (2 * b) == s // (2 * b)) & (t % (2 * b) >= b) & (s % (2 * b) < b)).astype(np.float32))
    masks.append((t == s).astype(np.float32))
    if reverse:
        blocks = [m[::-1, ::-1] for m in blocks]
        masks = [m[::-1, ::-1] for m in masks]
    mat = np.concatenate(blocks, axis=0)
    return (jnp.asarray(np.concatenate([mat, mat], axis=1), BF16), jnp.asarray(np.stack(masks), F32))


def _hgrn_chunk(qraw, vv, z, lbp, st, mat, mask_ref, reverse):
    L = HG_CHUNK
    log_lb, log_1mlb, one_mlb = lbp
    q = qraw * _sigmoid(qraw) * (B_KEY_DIM ** -0.5)
    e = jnp.exp(-jnp.abs(z))
    r = 1.0 / (1.0 + e)
    k = one_mlb * jnp.where(z >= 0.0, e * r, r)
    a = log_lb
    b = log_1mlb + (jnp.minimum(z, 0.0) + jnp.log(r))
    lf = (jnp.maximum(a, b) + jnp.log(1.0 + jnp.exp(-jnp.abs(a - b)))) * LOG2E
    hi = lf.astype(BF16)
    lo = (lf - hi.astype(F32)).astype(BF16)
    g = _dot(mat, jnp.concatenate([hi, lo], axis=0))
    cum = g[0:L]
    tot = cum[0:1] if reverse else cum[L - 1:L]
    vb = vv.astype(BF16)
    o = _dot_nt((q * jnp.exp2(cum)).astype(BF16), st.astype(BF16))
    kd = (k * jnp.exp2(tot - cum)).astype(BF16)
    st_new = st * jnp.exp2(tot) + _dot_tn(vb, kd)
    nl = len(HG_LEVELS)
    row = lax.broadcasted_iota(jnp.int32, (L, 1), 0)
    if reverse:
        row = L - 1 - row
    terms = [(q.astype(BF16), k.astype(BF16), nl)]
    for li, bsz in enumerate(HG_LEVELS):
        is_query = (row & (2 * bsz - 1)) >= bsz
        if bsz in HG_MAT_LEVELS:
            j = 1 + HG_MAT_LEVELS.index(bsz)
            d = g[j * L:(j + 1) * L]
        elif bsz == 1:
            d = jnp.where(is_query, lf, 0.0)
        else:
            parts = []
            roles = []
            for base in range(0, L, 2 * bsz):
                lo, hi = slice(base, base + bsz), slice(base + bsz, base + 2 * bsz)
                if reverse:
                    anc = cum[base + bsz:base + bsz + 1]
                    parts += [cum[lo] - anc, anc - cum[hi]]
                    roles += [q[lo], k[hi]]
                else:
                    anc = cum[base + bsz - 1:base + bsz]
                    parts += [anc - cum[lo], cum[hi] - anc]
                    roles += [k[lo], q[hi]]
            zb = (jnp.concatenate(roles, axis=0) * jnp.exp2(jnp.concatenate(parts, axis=0))).astype(BF16)
            terms.append((zb, zb, li))
            continue
        zb = (jnp.where(is_query, q, k) * jnp.exp2(d)).astype(BF16)
        terms.append((zb, zb, li))
    zero = jnp.zeros((L, B_KEY_DIM), BF16)
    sc = None
    for (qa, ka, ma), (qb, kb, mb) in zip(terms[0::2], terms[1::2]):
        lhs = jnp.concatenate([qa, qb], axis=1)
        rhs = jnp.concatenate([jnp.concatenate([ka, zero], axis=1), jnp.concatenate([zero, kb], axis=1)], axis=0)
        s2 = _dot_nt(lhs, rhs)
        part = s2[:, 0:L] * mask_ref[ma] + s2[:, L:2 * L] * mask_ref[mb]
        sc = part if sc is None else sc + part
    o = o + _dot(sc.astype(BF16), vb)
    return o, st_new


def _hgrn(proj, lbp, norm_g, consts, *, nbatch, seq, row_base, state_in=None, want_state=False):
    mat_f, mask_f, mat_b, mask_b = consts
    nseq = max(1, HG_CHAINS * HG_CHUNK // (2 * seq))
    rows = nseq * seq
    rb = row_base // rows

    def col(k):
        return pl.BlockSpec((rows, 128), lambda b, h: (rb + b, (B_OFF + k * B_WIDTH) // 128 + h))

    const2 = lambda shape: pl.BlockSpec(shape, lambda b, h: (0,) * len(shape))
    in_specs = [col(0), col(1), col(2), col(3), col(4),
                pl.BlockSpec((3, 2, 1, 128), lambda b, h: (0, 0, 0, h)),
                const2((1, 128)),
                const2(mat_f.shape), const2(mat_b.shape), const2(mask_f.shape), const2(mask_b.shape)]
    args = [proj, proj, proj, proj, proj, lbp, norm_g, mat_f, mat_b, mask_f, mask_b]
    state_spec = pl.BlockSpec((nseq, 2, 1, 128, 128), lambda b, h: (b, 0, h, 0, 0))
    if state_in is not None:
        in_specs.append(state_spec)
        args.append(state_in)
    out_specs = [pl.BlockSpec((rows, 128), lambda b, h: (b, h))]
    out_shape = [jax.ShapeDtypeStruct((nbatch * seq, B_WIDTH), BF16)]
    if want_state:
        out_specs.append(state_spec)
        out_shape.append(jax.ShapeDtypeStruct((nbatch, 2, B_HEADS, 128, 128), F32))
    kern = functools.partial(_hgrn_kernel_flat, seq=seq, nseq=nseq, has_state_in=state_in is not None,
                             has_state_out=want_state)
    return pl.pallas_call(
        kern,
        grid=(nbatch // nseq, B_HEADS),
        in_specs=in_specs,
        out_specs=out_specs,
        out_shape=out_shape,
        scratch_shapes=[pltpu.VMEM((rows, 128), F32)],
        compiler_params=_cparams(("arbitrary", "arbitrary")),
        name="hgrn_%d" % seq,
    )(*args)


def _hgrn_kernel_flat(*refs, seq, nseq, has_state_in, has_state_out):
    refs = list(refs)
    head = refs[:11]
    pos = 11
    s0_ref = None
    if has_state_in:
        s0_ref = refs[pos]
        pos += 1
    o_ref = refs[pos]
    pos += 1
    s_ref = None
    if has_state_out:
        s_ref = refs[pos]
        pos += 1
    acc_ref = refs[pos]
    _hgrn_body(*head, s0_ref, o_ref, s_ref, acc_ref, seq=seq, nseq=nseq)


def _hgrn_body(q_ref, i_ref, zf_ref, zb_ref, gt_ref, lb_ref, ng_ref, mf_ref, mb_ref, kf_ref, kb_ref,
               s0_ref, o_ref, s_ref, acc_ref, *, seq, nseq):
    L = HG_CHUNK
    n = seq // L
    acc_ref[...] = jnp.zeros_like(acc_ref)
    lbf = (lb_ref[0, 0], lb_ref[1, 0], lb_ref[2, 0])
    lbb = (lb_ref[0, 1], lb_ref[1, 1], lb_ref[2, 1])
    states = []
    for j in range(nseq):
        for d in range(2):
            if s0_ref is not None:
                states.append(s0_ref[j, d, 0].T)
            else:
                states.append(jnp.zeros((B_KEY_DIM, B_KEY_DIM), F32))

    def body(i, carry):
        carry = list(carry)
        for j in range(nseq):
            sf = pl.ds(pl.multiple_of(j * seq + i * L, L), L)
            sb = pl.ds(pl.multiple_of(j * seq + (n - 1 - i) * L, L), L)
            o_f, carry[2 * j] = _hgrn_chunk(q_ref[sf, :], i_ref[sf, :], zf_ref[sf, :], lbf, carry[2 * j],
                                            mf_ref[...], kf_ref, False)
            acc_ref[sf, :] += o_f
            o_b, carry[2 * j + 1] = _hgrn_chunk(q_ref[sb, :], i_ref[sb, :], zb_ref[sb, :], lbb, carry[2 * j + 1],
                                                mb_ref[...], kb_ref, True)
            acc_ref[sb, :] += o_b
        return tuple(carry)

    states = lax.fori_loop(0, n, body, tuple(states), unroll=HG_CHAINS // (2 * nseq))
    o_ref[...] = (_rms(acc_ref[...], ng_ref[...]) * _silu(gt_ref[...])).astype(BF16)
    if s_ref is not None:
        for j in range(nseq):
            for d in range(2):
                s_ref[j, d, 0] = states[2 * j + d].T


def _gelu_tanh(y):
    return 0.5 * y * (1.0 + jnp.tanh(0.7978845608028654 * (y + 0.044715 * (y * y * y))))


def _lru_body(xc_ref, yc_ref, cw_ref, cb_ref, wa_ref, ba_ref, wx_ref, bx_ref, lam_ref, h0_ref, o_ref, s_ref,
              xs_ref, a_ref, u_ref, *, rows, reset, independent):
    T = rows
    seg = T // LRU_SEGS
    row = lax.broadcasted_iota(jnp.int32, (T, 1), 0)
    seq_len = seg if independent else T
    pos = row % seg if independent else row
    pad = 8
    zeros = jnp.zeros((pad, 128), F32)
    xs_ref[0:pad, :] = zeros
    xs_ref[pad:pad + T, :] = xc_ref[...]
    xs_ref[pad + T:2 * pad + T, :] = zeros
    cw = cw_ref[...]
    x = cb_ref[...]
    for tap in range(4):
        off = tap - 2
        term = cw[tap:tap + 1] * xs_ref[pad + off:pad + off + T, :]
        if independent and off != 0:
            term = jnp.where((pos + off >= 0) & (pos + off < seq_len), term, 0.0)
        x = x + term
    xb = x.astype(BF16)
    pitch = seg + LRU_PITCH_PAD
    for d in range(2):
        rg = _sigmoid(_dot(xb, wa_ref[d, 0]) + ba_ref[d])
        ig = _sigmoid(_dot(xb, wx_ref[d, 0]) + bx_ref[d])
        log_a = (C_EXP * rg) * _log_sigmoid(lam_ref[d])
        a = jnp.exp(log_a)
        mult = jnp.sqrt(1.0 - a * a)
        if reset:
            mult = jnp.where(pos == (0 if d == 0 else seq_len - 1), 1.0, mult)
        u = mult * ig * x
        for r in range(LRU_SEGS):
            a_ref[d, r * pitch:r * pitch + seg, :] = a[r * seg:(r + 1) * seg]
            u_ref[d, r * pitch:r * pitch + seg, :] = u[r * seg:(r + 1) * seg]
    sub =lax.broadcasted_iota(jnp.int32, (LRU_SEGS, 1), 0)
    zrow = jnp.zeros((1, 128), F32)
    if h0_ref is not None:
        hf0 = jnp.where(sub == 0, h0_ref[0, 0:1, :], 0.0)
        hb0 = jnp.where(sub == LRU_SEGS - 1, h0_ref[0, 1:2, :], 0.0)
    else:
        hf0 = jnp.zeros((LRU_SEGS, 128), F32)
        hb0 = hf0
    ones = jnp.ones((LRU_SEGS, 128), F32)

    def local_scan(i, carry):
        hf, pf, hb, pb = carry
        sf = pl.ds(i, LRU_SEGS, stride=pitch)
        sb = pl.ds(seg - 1 - i, LRU_SEGS, stride=pitch)
        af = a_ref[0, sf, :]
        hf = af * hf + u_ref[0, sf, :]
        pf = af * pf
        u_ref[0, sf, :] = hf
        a_ref[0, sf, :] = pf
        ab = a_ref[1, sb, :]
        hb = ab * hb + u_ref[1, sb, :]
        pb = ab * pb
        u_ref[1, sb, :] = hb
        a_ref[1, sb, :] = pb
        return hf, pf, hb, pb

    hf, pf, hb, pb = lax.fori_loop(0, seg, local_scan, (hf0, ones, hb0, ones), unroll=4)
    if independent:
        for r in range(LRU_SEGS):
            sl = slice(r * seg, (r + 1) * seg)
            ps = slice(r * pitch, r * pitch + seg)
            o_ref[sl, :] = ((u_ref[0, ps, :] + u_ref[1, ps, :]) * _gelu_tanh(yc_ref[sl, :])).astype(BF16)
        if s_ref is not None:
            s_ref[:, 0, :] = hf
            s_ref[:, 1, :] = hb
        return
    carry_f = [zrow] * LRU_SEGS
    carry_b = [zrow] * LRU_SEGS
    end_f = hf[0:1]
    for r in range(1, LRU_SEGS):
        carry_f[r] = end_f
        end_f = hf[r:r + 1] + pf[r:r + 1] * end_f
    end_b = hb[LRU_SEGS - 1:LRU_SEGS]
    for r in range(LRU_SEGS - 2, -1, -1):
        carry_b[r] = end_b
        end_b = hb[r:r + 1] + pb[r:r + 1] * end_b
    for r in range(LRU_SEGS):
        sl = slice(r * seg, (r + 1) * seg)
        ps = slice(r * pitch, r * pitch + seg)
        h = (u_ref[0, ps, :] + a_ref[0, ps, :] * carry_f[r]) + (u_ref[1, ps, :] + a_ref[1, ps, :] * carry_b[r])
        o_ref[sl, :] = (h * _gelu_tanh(yc_ref[sl, :])).astype(BF16)
    if s_ref is not None:
        s_ref[0, 0:1, :] = end_f
        s_ref[0, 1:2, :] = end_b


def _lru_kernel_flat(*refs, rows, reset, independent, has_state_in, has_state_out):
    refs = list(refs)
    head = refs[:9]
    pos = 9
    h0_ref = None
    if has_state_in:
        h0_ref = refs[pos]
        pos += 1
    o_ref = refs[pos]
    pos += 1
    s_ref = None
    if has_state_out:
        s_ref = refs[pos]
        pos += 1
    _lru_body(*head, h0_ref, o_ref, s_ref, *refs[pos:], rows=rows, reset=reset, independent=independent)


def _lru(proj, lp, *, nbatch, seq, row_base, reset, state_in=None, want_state=False):
    independent = seq < LRU_SEGS * 128
    per_step = LRU_SEGS if independent else 1
    rows = seq * per_step
    nstep = nbatch // per_step
    rb = row_base // rows
    nblk = C_WIDTH // 128

    def col(k):
        return pl.BlockSpec((rows, 128), lambda b, c: (rb + b, (C_OFF + k * C_WIDTH) // 128 + c))

    vec = lambda rows: pl.BlockSpec((rows, 1, 128), lambda b, c: (0, 0, c))
    mat = pl.BlockSpec((2, 1, 128, 128), lambda b, c: (0, c, 0, 0))
    in_specs = [col(0), col(1),
                pl.BlockSpec((4, 128), lambda b, c: (0, c)), pl.BlockSpec((1, 128), lambda b, c: (0, c)),
                mat, vec(2), mat, vec(2), vec(2)]
    args = [proj, proj, lp['conv_w'], lp['conv_b'], lp['wa'], lp['ba'], lp['wx'], lp['bx'], lp['lam']]
    if state_in is not None:
        in_specs.append(pl.BlockSpec((1, 2, 128), lambda b, c: (b, 0, c)))
        args.append(state_in)
    out_specs = [pl.BlockSpec((rows, 128), lambda b, c: (b, c))]
    out_shape = [jax.ShapeDtypeStruct((nbatch * seq, C_WIDTH), BF16)]
    if want_state:
        out_specs.append(pl.BlockSpec((per_step, 2, 128), lambda b, c: (b, 0, c)))
        out_shape.append(jax.ShapeDtypeStruct((nbatch, 2, C_WIDTH), F32))
    kern = functools.partial(_lru_kernel_flat, rows=rows, reset=reset, independent=independent,
                             has_state_in=state_in is not None, has_state_out=want_state)
    scan_rows = rows + LRU_SEGS * LRU_PITCH_PAD
    return pl.pallas_call(
        kern,
        grid=(nstep, nblk),
        in_specs=in_specs,
        out_specs=out_specs,
        out_shape=out_shape,
        scratch_shapes=[pltpu.VMEM((rows + 16, 128), F32), pltpu.VMEM((2, scan_rows, 128), F32),
                        pltpu.VMEM((2, scan_rows, 128), F32)],
        compiler_params=_cparams(("arbitrary", "arbitrary")),
        name="lru_%d" % seq,
    )(*args)


MERGE_TM = 512
ROUTER_ROWS = 16


def _top2_combine(logits):
    idx = lax.broadcasted_iota(jnp.int32, logits.shape, 0).astype(F32)
    none = float(N_EXPERTS)
    m1 = jnp.max(logits, axis=0, keepdims=True)
    i1 = jnp.min(jnp.where(logits == m1, idx, none), axis=0, keepdims=True)
    rest = jnp.where(idx == i1, -jnp.inf, logits)
    m2 = jnp.max(rest, axis=0, keepdims=True)
    i2 = jnp.min(jnp.where(rest == m2, idx, none), axis=0, keepdims=True)
    e = jnp.exp(m2 - m1)
    w1 = 1.0 / (1.0 + e)
    return jnp.where(idx == i1, w1, 0.0) + jnp.where(idx == i2, e * w1, 0.0)


def _merge_body(xc_ref, xl_ref, m_ref, z0_ref, z1_ref, z2_ref, ac_ref, al_ref, hc_ref, hl_ref, lc_ref, ll_ref,
                wba_ref, wbb_ref, wbc_ref, wo_ref, g_ref, r_ref, xoc_ref, xol_ref, u_ref, c_ref):
    tm = MERGE_TM
    mg = (z0_ref[...].astype(F32) * _dot(_pair_read(ac_ref, al_ref, tm), wba_ref[...])
          + z1_ref[...].astype(F32) * _dot(_pair_read(hc_ref, hl_ref, tm), wbb_ref[...])
          + z2_ref[...].astype(F32) * _dot(_pair_read(lc_ref, ll_ref, tm), wbc_ref[...]))
    x = _pair_read(xc_ref, xl_ref, tm) + m_ref[0, 2] * _dot(mg.astype(BF16), wo_ref[...])
    _pair_write(xoc_ref, xol_ref, tm, x)
    u =_rms(x, g_ref[...]) * (1.0 + m_ref[0, 4]) + m_ref[0, 3]
    ub = u.astype(BF16)
    u_ref[...] = ub
    if r_ref is not None:
        ul = (u - ub.astype(F32)).astype(BF16)
        logits = _dot_nt(r_ref[0], ub) + (_dot_nt(r_ref[1], ub) + _dot_nt(r_ref[0], ul))
        comb = _top2_combine(logits[0:N_EXPERTS])
        comb = jnp.concatenate([comb, jnp.zeros((128 - N_EXPERTS, tm), F32)], axis=0)
        c_ref[...] = comb.T


def _merge_kernel_dense(*refs):
    _merge_body(*refs[:17], None, *refs[17:], None)


def _merge(x_pair, mod_l, gates, mixers, lp, router=None):
    tm = MERGE_TM
    row = lambda i: (i, 0)
    const = lambda i: (0, 0)
    res = lambda shape: pl.BlockSpec(shape, const, pipeline_mode=pl.Buffered(1))
    in_specs = _pair_specs(tm) + [
        pl.BlockSpec((1, 6, 1, D_MODEL), lambda i: (_mod_row(i, tm), 0, 0, 0)),
        pl.BlockSpec((tm, D_MODEL), lambda i: (i, 0)),
        pl.BlockSpec((tm, D_MODEL), lambda i: (i, 1)),
        pl.BlockSpec((tm, D_MODEL), lambda i: (i, 2))]
    in_specs += _pair_specs(tm, 512) * 3
    in_specs += [res((A_WIDTH, D_MODEL)), res((B_WIDTH, D_MODEL)), res((C_WIDTH, D_MODEL)), res((D_MODEL, D_MODEL)),
                 pl.BlockSpec((1, D_MODEL), const)]
    args = [*x_pair, mod_l, gates, gates, gates, *mixers, lp['w_ba'], lp['w_bb'], lp['w_bc'], lp['w_out'],
            lp['norm_ffn']]
    out_specs = _pair_specs(tm) + [pl.BlockSpec((tm, D_MODEL), row)]
    out_shape = _pair_shapes(D_MODEL, F32) + [jax.ShapeDtypeStruct((N_TOK, D_MODEL), BF16)]
    kern = _merge_kernel_dense
    if router is not None:
        in_specs.append(pl.BlockSpec((2, ROUTER_ROWS, D_MODEL), lambda i: (0, 0, 0)))
        args.append(router)
        out_specs.append(pl.BlockSpec((tm, 128), row))
        out_shape.append(jax.ShapeDtypeStruct((N_TOK, 128), F32))
        kern = _merge_body
    return pl.pallas_call(
        kern,
        grid=(N_TOK // tm,),
        in_specs=in_specs,
        out_specs=out_specs,
        out_shape=out_shape,
        compiler_params=_cparams(("arbitrary",)),
        name="merge",
    )(*args)


FFN_TM = 512
FFN_CHUNK = 256


def _swiglu_partial(u, wg, wu, wd):
    h = (_silu(_dot(u, wg)) * _dot(u, wu)).astype(BF16)
    return _dot(h, wd)


def _ffn_kernel(u_ref, xc_ref, xl_ref, m_ref, wg_ref, wu_ref, wd_ref, oc_ref, ol_ref):
    u = u_ref[...]
    acc = None
    for c in range(FFN_DIM // FFN_CHUNK):
        sl = slice(c * FFN_CHUNK, (c + 1) * FFN_CHUNK)
        y = _swiglu_partial(u, wg_ref[:, sl], wu_ref[:, sl], wd_ref[sl, :])
        acc = y if acc is None else acc + y
    _pair_write(oc_ref, ol_ref, FFN_TM, _pair_read(xc_ref, xl_ref, FFN_TM) + m_ref[0, 5] * acc)


def _ffn(u, x_pair, mod_l, wg, wu, wd):
    tm = FFN_TM
    res = lambda shape: pl.BlockSpec(shape, lambda i: (0, 0), pipeline_mode=pl.Buffered(1))
    return pl.pallas_call(
        _ffn_kernel,
        grid=(N_TOK // tm,),
        in_specs=[pl.BlockSpec((tm, D_MODEL), lambda i: (i, 0))] + _pair_specs(tm) + [
            pl.BlockSpec((1, 6, 1, D_MODEL), lambda i: (_mod_row(i, tm), 0, 0, 0)),
            res((D_MODEL, FFN_DIM)), res((D_MODEL, FFN_DIM)), res((FFN_DIM, D_MODEL))],
        out_specs=_pair_specs(tm),
        out_shape=_pair_shapes(D_MODEL, F32),
        compiler_params=_cparams(("arbitrary",)),
        name="ffn",
    )(u, *x_pair, mod_l, wg, wu, wd)


MOE_TM = 512
MOE_NT = N_TOK // MOE_TM
MOE_PIECE = 16
MOE_SLAB = 2 * MOE_TM + N_EXPERTS * MOE_PIECE
MOE_BLK = 512
MOE_NB = -(-(2 * N_TOK + MOE_NT * N_EXPERTS * (MOE_PIECE - 1) + N_EXPERTS * (MOE_BLK - 1)) // MOE_BLK)
MOE_ROWS = MOE_NB * MOE_BLK
MOE_FCHUNK = 512


def _moe_count_kernel(c_ref, n_ref):
    n_ref[0] = jnp.sum(jnp.where(c_ref[...] > 0.0, 1.0, 0.0), axis=0, keepdims=True)


def _moe_count(comb):
    return pl.pallas_call(
        _moe_count_kernel,
        grid=(MOE_NT,),
        in_specs=[pl.BlockSpec((MOE_TM, 128), lambda i: (i, 0))],
        out_specs=pl.BlockSpec((1, 1, 128), lambda i: (i, 0, 0)),
        out_shape=jax.ShapeDtypeStruct((MOE_NT, 1, 128), F32),
        compiler_params=_cparams(("arbitrary",)),
        name="moe_count",
    )(comb)


def _moe_tables(counts):
    cnt = counts[:, 0, :N_EXPERTS].astype(jnp.int32)
    pc = (cnt + MOE_PIECE - 1) // MOE_PIECE * MOE_PIECE
    lo = jnp.cumsum(pc, axis=1) - pc
    tot = jnp.sum(pc, axis=0)
    totp = (tot + MOE_BLK - 1) // MOE_BLK * MOE_BLK
    goff = (jnp.cumsum(totp) - totp)[None, :] + jnp.cumsum(pc, axis=0) - pc
    ends = jnp.cumsum(totp // MOE_BLK)
    nact = ends[-1]
    s = jnp.minimum(jnp.arange(MOE_NB, dtype=jnp.int32), nact - 1)
    blk_e = jnp.minimum(jnp.sum((s[:, None] >= ends[None, :]).astype(jnp.int32), axis=1), N_EXPERTS - 1)
    lo_vec = jnp.pad(lo.astype(F32), ((0, 0), (0, 128 - N_EXPERTS))).reshape(MOE_NT, 1, 128)
    return dict(lo=lo.reshape(-1), npc=(pc // MOE_PIECE).reshape(-1), goff=goff.reshape(-1),
                nrow=jnp.sum(pc, axis=1), blk_e=blk_e, nact=nact.reshape(1), lo_vec=lo_vec)


def _moe_pieces(tile, lo_s, npc_s, goff_s, make, wait):
    for e in range(N_EXPERTS):
        k = tile * N_EXPERTS + e

        def body(p, carry, k=k):
            local = pl.multiple_of(lo_s[k] + p * MOE_PIECE, MOE_PIECE)
            glob = pl.multiple_of(goff_s[k] + p * MOE_PIECE, MOE_PIECE)
            cp = make(local, glob)
            if wait:
                cp.wait()
            else:
                cp.start()
            return carry

        lax.fori_loop(0, npc_s[k], body, 0)


def _slot_pair(slot):
    sel = slot >= 0.0
    s_hi = jnp.max(slot, axis=-1, keepdims=True)
    s_lo = jnp.min(jnp.where(sel, slot, float(MOE_SLAB)), axis=-1, keepdims=True)
    return sel, s_hi, s_lo


def _one_hot_rows(s):
    srow = lax.broadcasted_iota(jnp.int32, (MOE_TM, MOE_SLAB), 1).astype(F32)
    return jnp.where(srow == s, 1.0, 0.0)


def _moe_gather_kernel(lo_s, npc_s, goff_s, u_ref, c_ref, lov_ref, tri_ref, init_ref, xs_ref, slot_ref,
                       slab_ref, sem):
    del init_ref
    i = pl.program_id(0)
    routed = c_ref[...] > 0.0
    rank = _dot(tri_ref[...], jnp.where(routed, 1.0, 0.0).astype(BF16))
    slot = jnp.where(routed, lov_ref[0] + rank, -1.0)
    slot_ref[...] = slot
    _, s_hi, s_lo = _slot_pair(slot)
    pt = jnp.maximum(_one_hot_rows(s_hi), _one_hot_rows(s_lo)).astype(BF16)
    buf = i % 2
    slab_ref[buf] = _dot_tn(pt, u_ref[...]).astype(BF16)

    def pieces(tile, b, wait):
        _moe_pieces(tile, lo_s, npc_s, goff_s, lambda local, glob: pltpu.make_async_copy(
            slab_ref.at[b, pl.ds(local, MOE_PIECE), :], xs_ref.at[pl.ds(glob, MOE_PIECE), :], sem.at[b]), wait)

    pieces(i, buf, False)

    @pl.when(i > 0)
    def _():
        pieces(i - 1, 1 - buf, True)

    @pl.when(i == MOE_NT - 1)
    def _():
        pieces(i, buf, True)


def _moe_gather(u, comb, tab):
    tri = jnp.asarray(np.tril(np.ones((MOE_TM, MOE_TM), np.float32), -1), BF16)
    init = jnp.zeros((MOE_ROWS, D_MODEL), BF16)
    row = lambda i, *_: (i, 0)
    grid_spec = pltpu.PrefetchScalarGridSpec(
        num_scalar_prefetch=3,
        grid=(MOE_NT,),
        in_specs=[pl.BlockSpec((MOE_TM, D_MODEL), row), pl.BlockSpec((MOE_TM, 128), row),
                  pl.BlockSpec((1, 1, 128), lambda i, *_: (i, 0, 0)),
                  pl.BlockSpec((MOE_TM, MOE_TM), lambda i, *_: (0, 0)),
                  pl.BlockSpec(memory_space=pl.ANY)],
        out_specs=[pl.BlockSpec(memory_space=pl.ANY), pl.BlockSpec((MOE_TM, 128), row)],
        scratch_shapes=[pltpu.VMEM((2, MOE_SLAB, D_MODEL), BF16), pltpu.SemaphoreType.DMA((2,))],
    )
    return pl.pallas_call(
        _moe_gather_kernel,
        grid_spec=grid_spec,
        out_shape=[jax.ShapeDtypeStruct((MOE_ROWS, D_MODEL), BF16), jax.ShapeDtypeStruct((N_TOK, 128), F32)],
        input_output_aliases={7: 0},
        compiler_params=_cparams(("arbitrary",)),
        name="moe_gather",
    )(tab['lo'], tab['npc'], tab['goff'], u, comb, tab['lo_vec'], tri, init)


class _ExpertWeights:
    def __init__(self, hbm, resident, stages, sem):
        self.hbm, self.resident, self.stages, self.sem = hbm, resident, stages, sem

    def _copies(self, e, c):
        sl = slice(c * MOE_FCHUNK, (c + 1) * MOE_FCHUNK)
        b = c % 2
        srcs = (self.hbm[0].at[e, :, sl], self.hbm[1].at[e, :, sl], self.hbm[2].at[e, sl, :])
        return [pltpu.make_async_copy(src, stage.at[b], self.sem.at[b, k])
                for k, (src, stage) in enumerate(zip(srcs, self.stages))]

    def start(self, e, c):
        for cp in self._copies(e, c):
            cp.start()

    def finish(self, e, c):
        for cp in self._copies(e, c):
            cp.wait()
        sl = slice(c * MOE_FCHUNK, (c + 1) * MOE_FCHUNK)
        b = c % 2
        self.resident[0][:, sl] = self.stages[0][b].astype(BF16)
        self.resident[1][:, sl] = self.stages[1][b].astype(BF16)
        self.resident[2][sl, :] = self.stages[2][b].astype(BF16)


def _moe_block(x_ref, y_ref, w, next_expert):
    nch = EXPERT_DIM // MOE_FCHUNK
    wg_ref, wu_ref, wd_ref = w.resident
    if next_expert is not None:
        w.start(next_expert, 0)
        w.start(next_expert, 1)
    x = x_ref[...]
    acc = None
    for c in range(nch):
        sl = slice(c * MOE_FCHUNK, (c + 1) * MOE_FCHUNK)
        y = _swiglu_partial(x, wg_ref[:, sl], wu_ref[:, sl], wd_ref[sl, :])
        acc = y if acc is None else acc + y
        if next_expert is not None:
            w.finish(next_expert, c)
            if c + 2 < nch:
                w.start(next_expert, c + 2)
    y_ref[...] = acc.astype(BF16)


def _moe_expert_kernel(be_s, nact_s, x_ref, wg_hbm, wu_hbm, wd_hbm, y_ref, wg_ref, wu_ref, wd_ref,
                       g_stage, u_stage, d_stage, sem):
    s = pl.program_id(0)
    nch = EXPERT_DIM // MOE_FCHUNK
    active = s < nact_s[0]
    e = be_s[s]
    e_next = be_s[jnp.minimum(s + 1, MOE_NB - 1)]
    hands_over = e_next != e
    w = _ExpertWeights((wg_hbm, wu_hbm, wd_hbm), (wg_ref, wu_ref, wd_ref), (g_stage, u_stage, d_stage), sem)

    @pl.when(s == 0)
    def _():
        w.start(e, 0)
        for c in range(nch):
            if c + 1 < nch:
                w.start(e, c + 1)
            w.finish(e, c)

    @pl.when(jnp.logical_not(active))
    def _():
        y_ref[...] = jnp.zeros_like(y_ref)

    @pl.when(active & hands_over)
    def _():
        _moe_block(x_ref, y_ref, w, e_next)

    @pl.when(active & jnp.logical_not(hands_over))
    def _():
        _moe_block(x_ref, y_ref, w, None)


def _moe_expert(xs, tab, wg, wu, wd):
    blk = lambda s, be, nact: (jnp.minimum(s, nact[0] - 1), 0)
    hbm = pl.BlockSpec(memory_space=pl.ANY)
    grid_spec = pltpu.PrefetchScalarGridSpec(
        num_scalar_prefetch=2,
        grid=(MOE_NB,),
        in_specs=[pl.BlockSpec((MOE_BLK, D_MODEL), blk), hbm, hbm, hbm],
        out_specs=pl.BlockSpec((MOE_BLK, D_MODEL), lambda s, be, nact: (s, 0)),
        scratch_shapes=[pltpu.VMEM((D_MODEL, EXPERT_DIM), BF16), pltpu.VMEM((D_MODEL, EXPERT_DIM), BF16),
                        pltpu.VMEM((EXPERT_DIM, D_MODEL), BF16),
                        pltpu.VMEM((2, D_MODEL, MOE_FCHUNK), F32), pltpu.VMEM((2, D_MODEL, MOE_FCHUNK), F32),
                        pltpu.VMEM((2, MOE_FCHUNK, D_MODEL), F32), pltpu.SemaphoreType.DMA((2, 3))],
    )
    return pl.pallas_call(
        _moe_expert_kernel,
        grid_spec=grid_spec,
        out_shape=jax.ShapeDtypeStruct((MOE_ROWS, D_MODEL), BF16),
        compiler_params=_cparams(("arbitrary",)),
        name="moe_expert",
    )(tab['blk_e'], tab['nact'], xs, wg, wu, wd)


def _moe_combine_kernel(lo_s, npc_s, goff_s, nrow_s, y_ref, slot_ref, c_ref, xc_ref, xl_ref, m_ref, g_ref,
                        oc_ref, ol_ref, slab_ref, sem):
    i = pl.program_id(0)
    buf = i % 2

    def pieces(tile, b, wait):
        _moe_pieces(tile, lo_s, npc_s, goff_s, lambda local, glob: pltpu.make_async_copy(
            y_ref.at[pl.ds(glob, MOE_PIECE), :], slab_ref.at[b, pl.ds(local, MOE_PIECE), :], sem.at[b]), wait)

    @pl.when(i == 0)
    def _():
        pieces(i, buf, False)

    @pl.when(i + 1 < MOE_NT)
    def _():
        pieces(i + 1, 1 - buf, False)

    slot = slot_ref[...]
    comb = c_ref[...]
    sel, s_hi, s_lo = _slot_pair(slot)
    w_hi = jnp.sum(jnp.where(sel & (slot == s_hi), comb, 0.0), axis=-1, keepdims=True)
    w_lo = jnp.sum(jnp.where(sel & (slot == s_lo), comb, 0.0), axis=-1, keepdims=True)
    w_lo = jnp.where(s_lo == s_hi, 0.0, w_lo)
    srow = lax.broadcasted_iota(jnp.int32, (MOE_SLAB, 1), 0)
    pieces(i, buf, True)
    ys = jnp.where(srow < nrow_s[i], slab_ref[buf], jnp.zeros((), BF16))
    out = (w_hi * _dot(_one_hot_rows(s_hi).astype(BF16), ys) + w_lo * _dot(_one_hot_rows(s_lo).astype(BF16), ys))
    x = _pair_read(xc_ref, xl_ref, MOE_TM)
    _pair_write(oc_ref, ol_ref, MOE_TM, _rms(x + m_ref[0, 5] * out, g_ref[...]))


def _moe_combine(y, slot, comb, x_pair, mod_l, final_g, tab):
    row = lambda i, *_: (i, 0)
    grid_spec = pltpu.PrefetchScalarGridSpec(
        num_scalar_prefetch=4,
        grid=(MOE_NT,),
        in_specs=[pl.BlockSpec(memory_space=pl.ANY),
                  pl.BlockSpec((MOE_TM, 128), row), pl.BlockSpec((MOE_TM, 128), row)] + _pair_specs(MOE_TM) + [
                  pl.BlockSpec((1, 6, 1, D_MODEL), lambda i, *_: (_mod_row(i, MOE_TM), 0, 0, 0)),
                  pl.BlockSpec((1, D_MODEL), lambda i, *_: (0, 0))],
        out_specs=_pair_specs(MOE_TM),
        scratch_shapes=[pltpu.VMEM((2, MOE_SLAB, D_MODEL), BF16), pltpu.SemaphoreType.DMA((2,))],
    )
    return pl.pallas_call(
        _moe_combine_kernel,
        grid_spec=grid_spec,
        out_shape=_pair_shapes(D_MODEL, F32),
        compiler_params=_cparams(("arbitrary",)),
        name="moe_combine",
    )(tab['lo'], tab['npc'], tab['goff'], tab['nrow'], y, slot, comb, *x_pair, mod_l, final_g)


def _moe(u, comb, x_pair, mod_l, final_g, wg, wu, wd):
    tab = _moe_tables(_moe_count(comb))
    xs, slot = _moe_gather(u, comb, tab)
    y = _moe_expert(xs, tab, wg, wu, wd)
    return _moe_combine(y, slot, comb, x_pair, mod_l, final_g, tab)


def _block_diag(w):
    w = w.reshape(2, C_WIDTH // 128, 2, C_BLOCK, C_BLOCK)
    z = jnp.zeros_like(w[:, :, 0])
    top = jnp.concatenate([w[:, :, 0], z], axis=-1)
    bot = jnp.concatenate([z, w[:, :, 1]], axis=-1)
    return jnp.concatenate([top, bot], axis=-2).astype(BF16)


def kernel(x_prompt, x_sample, cache_attn_k, cache_attn_v, state_hgrn, state_lru, c, c_ctx, w_mod, b_mod, norm_mix_g, norm_ffn_g, w_in, attn_sink, hgrn_lower, hgrn_norm_g, lru_conv_w, lru_conv_b, lru_wa, lru_ba, lru_wx, lru_bx, lru_lambda, w_branch_a, w_branch_b, w_branch_c, w_out, ffn_w_gate, ffn_w_up, ffn_w_down, moe_router, moe_w_gate, moe_w_up, moe_w_down, final_norm_g):
    assert DEPTH == 2
    x = (x_prompt.reshape(N_CTX_TOK, D_MODEL), x_sample.reshape(N_LAT_TOK, D_MODEL))
    cond =jnp.concatenate([c_ctx[None, :], c, jnp.zeros((N_COND - 1 - DEC_BATCH, D_MODEL), F32)], axis=0)
    mods = _modulation(cond, w_mod, b_mod).reshape(DEPTH, N_COND, 6, 1, D_MODEL)

    lb_cum = jnp.cumsum(jax.nn.softmax(hgrn_lower.astype(F32), axis=0), axis=0)
    lb_layers = lb_cum - lb_cum[:1]
    cos, sin = _rope_tables()
    hconst = _hgrn_constants(False) + _hgrn_constants(True)

    ks, vs, hs, ls = [], [], [], []
    for l in range(DEPTH):
        w_in_l = w_in[l].astype(BF16)
        lb = lb_layers[l]
        lbp = jnp.stack([jnp.log(lb), jnp.log1p(-lb), 1.0 - lb]).reshape(3, 2, 1, B_WIDTH)
        lp = dict(
            conv_w=lru_conv_w[l], conv_b=lru_conv_b[l].reshape(1, C_WIDTH),
            wa=_block_diag(lru_wa[l]), wx=_block_diag(lru_wx[l]),
            ba=lru_ba[l].reshape(2, 1, C_WIDTH), bx=lru_bx[l].reshape(2, 1, C_WIDTH),
            lam=lru_lambda[l].reshape(2, 1, C_WIDTH),
            w_ba=w_branch_a[l].astype(BF16), w_bb=w_branch_b[l].astype(BF16), w_bc=w_branch_c[l].astype(BF16),
            w_out=w_out[l].astype(BF16), norm_ffn=norm_ffn_g[l].reshape(1, D_MODEL))
        mod_l = mods[l]

        proj, gates = _in_proj(x, mod_l, norm_mix_g[l].reshape(1, D_MODEL), w_in_l)

        attn_c = _ctx_attention(proj, attn_sink[l])
        attn_l = _lat_attention(proj, attn_sink[l], cache_attn_k[:, l].reshape(DEC_BATCH, PAST_LEN, 128),
                                cache_attn_v[:, l].reshape(DEC_BATCH, PAST_LEN, 128), cos, sin)

        ng = hgrn_norm_g[l].reshape(1, B_KEY_DIM)
        hg_c, s_h = _hgrn(proj, lbp, ng, hconst, nbatch=BATCH, seq=SEQ, row_base=0, want_state=True)
        hg_l, = _hgrn(proj, lbp, ng, hconst, nbatch=DEC_BATCH, seq=DEC_SEQ, row_base=N_CTX_TOK,
                      state_in=state_hgrn[:, l])

        lr_c, s_l = _lru(proj, lp, nbatch=BATCH, seq=SEQ, row_base=0, reset=True, want_state=True)
        lr_l, = _lru(proj, lp, nbatch=DEC_BATCH, seq=DEC_SEQ, row_base=N_CTX_TOK, reset=False,
                     state_in=state_lru[:, l])
        mixers = (attn_c, attn_l, hg_c, hg_l, lr_c, lr_l)

        ks.append(proj[:N_CTX_TOK, KA_OFF:KA_OFF + 128].reshape(BATCH, SEQ, A_KV_HEADS, A_HEAD_DIM))
        vs.append(proj[:N_CTX_TOK, VA_OFF:VA_OFF + 128].reshape(BATCH, SEQ, A_KV_HEADS, A_HEAD_DIM))
        hs.append(s_h)
        ls.append(s_l)

        m = l // 2
        if l % 2 == 0:
            xc, xl, u = _merge(x, mod_l, gates, mixers, lp)
            x = _ffn(u, (xc, xl), mod_l, ffn_w_gate[m].astype(BF16), ffn_w_up[m].astype(BF16), ffn_w_down[m].astype(BF16))
        else:
            router = jnp.pad(moe_router[m].T, ((0, ROUTER_ROWS - N_EXPERTS), (0, 0)))
            r_hi = router.astype(BF16)
            router = jnp.stack([r_hi, (router - r_hi.astype(F32)).astype(BF16)])
            xc, xl, u, comb = _merge(x, mod_l, gates, mixers, lp, router=router)
            x = _moe(u, comb, (xc, xl), mod_l, final_norm_g.reshape(1, D_MODEL), moe_w_gate[m], moe_w_up[m],
                     moe_w_down[m])

    y_prompt = x[0].reshape(BATCH, SEQ, D_MODEL)
    y_sample = x[1].reshape(DEC_BATCH, DEC_SEQ, D_MODEL)
    return (y_prompt, y_sample, jnp.stack(ks, axis=1), jnp.stack(vs, axis=1), jnp.stack(hs, axis=1),
            jnp.stack(ls, axis=1))
```

```python
import functools

import numpy as np
import jax
import jax.numpy as jnp
from jax import lax
from jax.experimental import pallas as pl
from jax.experimental.pallas import tpu as pltpu

F32 = jnp.float32
BF16 = jnp.bfloat16
HIGHEST = lax.Precision.HIGHEST

D_MODEL = 1024
BATCH = 32
SEQ = 256
DEPTH = 2
DEC_BATCH = 4
DEC_SEQ = 2048
PAST_LEN = 512
GRID_W = 64
RMS_EPS = 1e-6
A_HEADS = 8
A_KV_HEADS = 2
A_GROUP = 4
A_HEAD_DIM = 64
A_WIDTH = 512
A_WINDOW = 128
ROPE_THETA = 10000.0
NEG_INF = -1e30
B_HEADS = 4
B_KEY_DIM = 128
B_WIDTH = 512
C_WIDTH = 512
C_HEADS = 8
C_BLOCK = 64
C_EXP = 8.0
FFN_DIM = 2816
N_EXPERTS = 8
EXPERT_DIM = 3584

N_CTX_TOK = BATCH * SEQ
N_LAT_TOK = DEC_BATCH * DEC_SEQ
N_TOK = N_CTX_TOK + N_LAT_TOK
N_COND = 8

QA_OFF = 0
KA_OFF = 512
VA_OFF = 640
B_OFF = 768
C_OFF = 3328
IN_COLS = 7424
IN_MIX_COLS = IN_COLS - 3 * D_MODEL

V7X_VMEM_LIMIT = 56 * 1024 * 1024
HG_CHUNK = 128
HG_LEVELS = (64, 32, 16, 8, 4, 2, 1)
HG_MAT_LEVELS = (4, 2)
HG_CHAINS = 32
LRU_SEGS = 16
LRU_PITCH_PAD = 4


def _cparams(sem, vmem=V7X_VMEM_LIMIT):
    return pltpu.CompilerParams(dimension_semantics=sem, vmem_limit_bytes=vmem)


def _mod_row(i, tm):
    nct = N_CTX_TOK // tm
    return jnp.where(i < nct, 0, 1 + (i - nct) // (DEC_SEQ // tm))


def _pair_specs(tm, cols=D_MODEL):
    nct = N_CTX_TOK // tm
    return [pl.BlockSpec((tm, cols), lambda i, *_: (jnp.minimum(i, nct - 1), 0)),
            pl.BlockSpec((tm, cols), lambda i, *_: (jnp.maximum(i - nct, 0), 0))]


def _pair_shapes(cols, dtype):
    return [jax.ShapeDtypeStruct((N_CTX_TOK, cols), dtype), jax.ShapeDtypeStruct((N_LAT_TOK, cols), dtype)]


def _pair_read(c_ref, l_ref, tm):
    return jnp.where(pl.program_id(0) < N_CTX_TOK // tm, c_ref[...], l_ref[...])


def _pair_write(c_ref, l_ref, tm, value):
    is_ctx = pl.program_id(0) < N_CTX_TOK // tm

    @pl.when(is_ctx)
    def _():
        c_ref[...] = value

    @pl.when(jnp.logical_not(is_ctx))
    def _():
        l_ref[...] = value


def _sigmoid(x):
    return 0.5 * jnp.tanh(0.5 * x) + 0.5


def _silu(x):
    return x * _sigmoid(x)


def _log_sigmoid(z):
    return jnp.minimum(z, 0.0) - jnp.log1p(jnp.exp(-jnp.abs(z)))


def _rms(x, g):
    ms = jnp.mean(x * x, axis=-1, keepdims=True)
    return x * lax.rsqrt(ms + RMS_EPS) * g


def _dot(a, b):
    return jnp.dot(a, b, preferred_element_type=F32)


def _dot_nt(a, b):
    return lax.dot_general(a, b, (((1,), (1,)), ((), ())), preferred_element_type=F32)


def _dot_tn(a, b):
    return lax.dot_general(a, b, (((0,), (0,)), ((), ())), preferred_element_type=F32)


def _mod_kernel(c_ref, w_ref, b_ref, o_ref):
    a = _silu(c_ref[...])
    o_ref[0] = jnp.dot(a, w_ref[0], preferred_element_type=F32, precision=HIGHEST) + b_ref[0]


def _modulation(cond, w_mod, b_mod):
    tn = 1536
    return pl.pallas_call(
        _mod_kernel,
        grid=(DEPTH, 6 * D_MODEL // tn),
        in_specs=[pl.BlockSpec((N_COND, D_MODEL), lambda l, j: (0, 0)),
                  pl.BlockSpec((1, D_MODEL, tn), lambda l, j: (l, 0, j)),
                  pl.BlockSpec((1, 1, tn), lambda l, j: (l, 0, j))],
        out_specs=pl.BlockSpec((1, N_COND, tn), lambda l, j: (l, 0, j)),
        out_shape=jax.ShapeDtypeStruct((DEPTH, N_COND, 6 * D_MODEL), F32),
        compiler_params=_cparams(("arbitrary", "arbitrary")),
        name="modulation",
    )(cond, w_mod, b_mod.reshape(DEPTH, 1, 6 * D_MODEL))


IN_TM = 512
IN_TN = 256


def _in_proj_kernel(xc_ref, xl_ref, m_ref, g_ref, w_hbm, o_ref, gate_ref, w_ref, stage_ref, sem, *, layer):
    n = IN_COLS // IN_TN

    @pl.when(pl.program_id(0) == 0)
    def _():
        def copy(j):
            return pltpu.make_async_copy(w_hbm.at[layer, :, j * IN_TN:(j + 1) * IN_TN], stage_ref.at[j % 2],
                                         sem.at[j % 2])

        copy(0).start()
        for j in range(n):
            if j + 1 < n:
                copy(j + 1).start()
            copy(j).wait()
            w_ref[:, j * IN_TN:(j + 1) * IN_TN] = stage_ref[j % 2].astype(BF16)

    u = _rms(_pair_read(xc_ref, xl_ref, IN_TM), g_ref[...]) * (1.0 + m_ref[0, 1]) + m_ref[0, 0]
    ub = u.astype(BF16)
    n_mix = IN_MIX_COLS // IN_TN
    for j in range(IN_COLS // IN_TN):
        y = _dot(ub, w_ref[:, j * IN_TN:(j + 1) * IN_TN])
        if j < n_mix:
            o_ref[:, j * IN_TN:(j + 1) * IN_TN] = y
        else:
            gate_ref[:, (j - n_mix) * IN_TN:(j - n_mix + 1) * IN_TN] = _sigmoid(y).astype(BF16)


def _in_proj(x_pair, mod_l, g, w_all, layer):
    tm = IN_TM
    return pl.pallas_call(
        functools.partial(_in_proj_kernel, layer=layer),
        grid=(N_TOK // tm,),
        in_specs=_pair_specs(tm) + [
            pl.BlockSpec((1, 6, 1, D_MODEL), lambda i: (_mod_row(i, tm), 0, 0, 0)),
            pl.BlockSpec((1, D_MODEL), lambda i: (0, 0)),
            pl.BlockSpec(memory_space=pl.ANY)],
        out_specs=[pl.BlockSpec((tm, IN_MIX_COLS), lambda i: (i, 0)),
                   pl.BlockSpec((tm, 3 * D_MODEL), lambda i: (i, 0))],
        out_shape=[jax.ShapeDtypeStruct((N_TOK, IN_MIX_COLS), F32),
                   jax.ShapeDtypeStruct((N_TOK, 3 * D_MODEL), BF16)],
        scratch_shapes=[pltpu.VMEM((D_MODEL, IN_COLS), BF16), pltpu.VMEM((2, D_MODEL, IN_TN), F32),
                        pltpu.SemaphoreType.DMA((2,))],
        compiler_params=_cparams(("arbitrary",)),
        name="in_proj",
    )(*x_pair, mod_l, g, w_all)


LOG2E = 1.4426950408889634
A_LOGIT_SCALE = A_HEAD_DIM ** -0.5 * LOG2E


def _softmax_pv(s_list, v_list, sink):
    m = sink
    for s in s_list:
        m = jnp.maximum(m, jnp.max(s, axis=-1, keepdims=True))
    den = jnp.exp2(sink - m)
    out = None
    for s, v in zip(s_list, v_list):
        p = jnp.exp2(s - m)
        den = den + jnp.sum(p, axis=-1, keepdims=True)
        o = _dot(p.astype(BF16), v)
        out = o if out is None else out + o
    return out / den


def _sink_column(sink_ref, h, rows_per_head):
    n = A_GROUP * rows_per_head
    row = lax.broadcasted_iota(jnp.int32, (n, 1), 0)
    col = jnp.full((n, 1), sink_ref[A_GROUP * h + A_GROUP - 1], F32)
    for g in range(A_GROUP - 2, -1, -1):
        col = jnp.where(row < (g + 1) * rows_per_head, sink_ref[A_GROUP * h + g], col)
    return col


def _ctx_attn_kernel(sink_ref, q_ref, k_ref, v_ref, o_ref):
    q = q_ref[...] * A_LOGIT_SCALE
    k = k_ref[...]
    v = v_ref[...]
    outs = []
    for h in range(A_KV_HEADS):
        hs = slice(h * A_HEAD_DIM, (h + 1) * A_HEAD_DIM)
        kh = k[:, hs].astype(BF16)
        vh = v[:, hs].astype(BF16)
        qs = jnp.concatenate(
            [q[:, (A_GROUP * h + g) * A_HEAD_DIM:(A_GROUP * h + g + 1) * A_HEAD_DIM] for g in range(A_GROUP)],
            axis=0).astype(BF16)
        s = _dot_nt(qs, kh)
        o = _softmax_pv([s], [vh], _sink_column(sink_ref, h, SEQ) * LOG2E)
        outs.extend(o[g * SEQ:(g + 1) * SEQ] for g in range(A_GROUP))
    o_ref[...] = jnp.concatenate(outs, axis=-1).astype(BF16)


def _ctx_attention(proj, sink):
    return pl.pallas_call(
        _ctx_attn_kernel,
        grid=(BATCH,),
        in_specs=[pl.BlockSpec(memory_space=pltpu.SMEM),
                  pl.BlockSpec((SEQ, A_WIDTH), lambda b: (b, QA_OFF // A_WIDTH)),
                  pl.BlockSpec((SEQ, 128), lambda b: (b, KA_OFF // 128)),
                  pl.BlockSpec((SEQ, 128), lambda b: (b, VA_OFF // 128))],
        out_specs=pl.BlockSpec((SEQ, A_WIDTH), lambda b: (b, 0)),
        out_shape=jax.ShapeDtypeStruct((N_CTX_TOK, A_WIDTH), BF16),
        compiler_params=_cparams(("arbitrary",)),
        name="ctx_attention",
    )(sink, proj, proj, proj)


def _rope_tables():
    half = A_HEAD_DIM // 4
    inv = ROPE_THETA ** (-np.arange(half, dtype=np.float64) / half)
    t = np.arange(DEC_SEQ)
    row = (t // GRID_W)[:, None] * inv[None, :]
    col = (t % GRID_W)[:, None] * inv[None, :]
    cos = np.concatenate([np.cos(row), np.cos(row), np.cos(col), np.cos(col)], axis=1)
    sin = np.concatenate([-np.sin(row), np.sin(row), -np.sin(col), np.sin(col)], axis=1)
    return (jnp.asarray(np.tile(cos, (1, 2)), F32), jnp.asarray(np.tile(sin, (1, 2)), F32))


def _rope(x, cos, sin, first):
    part = jnp.where(first, pltpu.roll(x, 128 - 16, 1), pltpu.roll(x, 16, 1))
    return x * cos + part * sin


def _lat_attn_kernel(sink_ref, q_ref, km_ref, k0_ref, kp_ref, vm_ref, v0_ref, vp_ref, kc_ref, vc_ref,
                     cm_ref, c0_ref, cp_ref, sm_ref, s0_ref, sp_ref, o_ref):
    j = pl.program_id(1)
    blk = A_WINDOW
    lane = lax.broadcasted_iota(jnp.int32, (blk, 128), 1)
    first = (lane % 32) < 16
    c0 = c0_ref[...]
    s0 = s0_ref[...]
    q = q_ref[...] * A_LOGIT_SCALE
    qr = [_rope(q[:, c * 128:(c + 1) * 128], c0, s0, first) for c in range(A_WIDTH // 128)]
    kw = jnp.concatenate([_rope(km_ref[...], cm_ref[...], sm_ref[...], first),
                          _rope(k0_ref[...], c0, s0, first),
                          _rope(kp_ref[...], cp_ref[...], sp_ref[...], first)], axis=0)
    vw = jnp.concatenate([vm_ref[...], v0_ref[...], vp_ref[...]], axis=0)
    kc = kc_ref[0]
    vc = vc_ref[0]
    t = lax.broadcasted_iota(jnp.int32, (blk, 3 * blk), 0)
    s = lax.broadcasted_iota(jnp.int32, (blk, 3 * blk), 1) - blk
    kpos = j * blk + s
    keep = jnp.where((jnp.abs(s - t) <= A_WINDOW) & (kpos >= 0) & (kpos < DEC_SEQ), 1.0, 0.0)
    keep = jnp.concatenate([keep] * A_GROUP, axis=0) > 0.0
    outs = []
    for h in range(A_KV_HEADS):
        hs = slice(h * A_HEAD_DIM, (h + 1) * A_HEAD_DIM)
        heads = [A_GROUP * h + g for g in range(A_GROUP)]
        qs = jnp.concatenate(
            [qr[hd // 2][:, (hd % 2) * A_HEAD_DIM:(hd % 2 + 1) * A_HEAD_DIM] for hd in heads], axis=0).astype(BF16)
        sw = jnp.where(keep, _dot_nt(qs, kw[:, hs].astype(BF16)), NEG_INF)
        sc = _dot_nt(qs, kc[:, hs].astype(BF16))
        o = _softmax_pv([sw, sc], [vw[:, hs].astype(BF16), vc[:, hs].astype(BF16)],
                        _sink_column(sink_ref, h, blk) * LOG2E)
        outs.extend(o[g * blk:(g + 1) * blk] for g in range(A_GROUP))
    o_ref[...] = jnp.concatenate(outs, axis=-1).astype(BF16)


def _lat_attention(proj, sink, k_ctx, v_ctx, cos, sin):
    blk = A_WINDOW
    nb = DEC_SEQ // blk
    base = N_CTX_TOK // blk

    def rows(off):
        return lambda b, j: (base + b * nb + jnp.clip(j + off, 0, nb - 1))

    def kv_spec(col, off):
        r = rows(off)
        return pl.BlockSpec((blk, 128), lambda b, j: (r(b, j), col))

    def tab_spec(off):
        return pl.BlockSpec((blk, 128), lambda b, j: (jnp.clip(j + off, 0, nb - 1), 0))

    r0 = rows(0)
    ctx_spec = pl.BlockSpec((1, PAST_LEN, 128), lambda b, j: (b, 0, 0))
    return pl.pallas_call(
        _lat_attn_kernel,
        grid=(DEC_BATCH, nb),
        in_specs=[pl.BlockSpec(memory_space=pltpu.SMEM),
                  pl.BlockSpec((blk, A_WIDTH), lambda b, j: (r0(b, j), QA_OFF // A_WIDTH)),
                  kv_spec(KA_OFF // 128, -1), kv_spec(KA_OFF // 128, 0), kv_spec(KA_OFF // 128, 1),
                  kv_spec(VA_OFF // 128, -1), kv_spec(VA_OFF // 128, 0), kv_spec(VA_OFF // 128, 1),
                  ctx_spec, ctx_spec,
                  tab_spec(-1), tab_spec(0), tab_spec(1), tab_spec(-1), tab_spec(0), tab_spec(1)],
        out_specs=pl.BlockSpec((blk, A_WIDTH), lambda b, j: (b * nb + j, 0)),
        out_shape=jax.ShapeDtypeStruct((N_LAT_TOK, A_WIDTH), BF16),
        compiler_params=_cparams(("arbitrary", "arbitrary")),
        name="lat_attention",
    )(sink, proj, proj, proj, proj, proj, proj, proj, k_ctx, v_ctx, cos, cos, cos, sin, sin, sin)


def _hgrn_constants(reverse):
    L = HG_CHUNK
    t = np.arange(L)[:, None]
    s = np.arange(L)[None, :]
    tri = (s <= t).astype(np.float32)
    blocks = [tri]
    masks = []
    for b in HG_LEVELS:
        anchor = (t // (2 * b)) * 2 * b + b - 1
        right = np.where(t % (2 * b) >= b, 1.0, -1.0)
        if b in HG_MAT_LEVELS:
            blocks.append(right * (tri - (s <= anchor).astype(np.float32)))
        masks.append(((t // (2 * b) == s // (2 * b)) & (t % (2 * b) >= b) & (s % (2 * b) < b)).astype(np.float32))
    masks.append((t == s).astype(np.float32))
    if reverse:
        blocks = [m[::-1, ::-1] for m in blocks]
        masks = [m[::-1, ::-1] for m in masks]
    mat = np.concatenate(blocks, axis=0)
    return (jnp.asarray(np.concatenate([mat, mat], axis=1), BF16), jnp.asarray(np.stack(masks), F32))


def _hgrn_chunk(qraw, vv, z, lbp, st, mat, mask_ref, reverse):
    L = HG_CHUNK
    log_lb, log_1mlb, one_mlb = lbp
    q = qraw * _sigmoid(qraw) * (B_KEY_DIM ** -0.5)
    e = jnp.exp(-jnp.abs(z))
    r = 1.0 / (1.0 + e)
    k = one_mlb * jnp.where(z >= 0.0, e * r, r)
    a = log_lb
    b = log_1mlb + (jnp.minimum(z, 0.0) + jnp.log(r))
    lf = (jnp.maximum(a, b) + jnp.log(1.0 + jnp.exp(-jnp.abs(a - b)))) * LOG2E
    hi = lf.astype(BF16)
    lo = (lf - hi.astype(F32)).astype(BF16)
    g = _dot(mat, jnp.concatenate([hi, lo], axis=0))
    cum = g[0:L]
    tot = cum[0:1] if reverse else cum[L - 1:L]
    vb = vv.astype(BF16)
    o = _dot_nt((q * jnp.exp2(cum)).astype(BF16), st.astype(BF16))
    kd = (k * jnp.exp2(tot - cum)).astype(BF16)
    st_new = st * jnp.exp2(tot) + _dot_tn(vb, kd)
    nl = len(HG_LEVELS)
    row = lax.broadcasted_iota(jnp.int32, (L, 1), 0)
    if reverse:
        row = L - 1 - row
    terms = [(q.astype(BF16), k.astype(BF16), nl)]
    for li, bsz in enumerate(HG_LEVELS):
        is_query = (row & (2 * bsz - 1)) >= bsz
        if bsz in HG_MAT_LEVELS:
            j = 1 + HG_MAT_LEVELS.index(bsz)
            d = g[j * L:(j + 1) * L]
        elif bsz == 1:
            d = jnp.where(is_query, lf, 0.0)
        else:
            parts = []
            roles = []
            for base in range(0, L, 2 * bsz):
                lo, hi = slice(base, base + bsz), slice(base + bsz, base + 2 * bsz)
                if reverse:
                    anc = cum[base + bsz:base + bsz + 1]
                    parts += [cum[lo] - anc, anc - cum[hi]]
                    roles += [q[lo], k[hi]]
                else:
                    anc = cum[base + bsz - 1:base + bsz]
                    parts += [anc - cum[lo], cum[hi] - anc]
                    roles += [k[lo], q[hi]]
            zb = (jnp.concatenate(roles, axis=0) * jnp.exp2(jnp.concatenate(parts, axis=0))).astype(BF16)
            terms.append((zb, zb, li))
            continue
        zb = (jnp.where(is_query, q, k) * jnp.exp2(d)).astype(BF16)
        terms.append((zb, zb, li))
    zero = jnp.zeros((L, B_KEY_DIM), BF16)
    sc = None
    for (qa, ka, ma), (qb, kb, mb) in zip(terms[0::2], terms[1::2]):
        lhs = jnp.concatenate([qa, qb], axis=1)
        rhs = jnp.concatenate([jnp.concatenate([ka, zero], axis=1), jnp.concatenate([zero, kb], axis=1)], axis=0)
        s2 = _dot_nt(lhs, rhs)
        part = s2[:, 0:L] * mask_ref[ma] + s2[:, L:2 * L] * mask_ref[mb]
        sc = part if sc is None else sc + part
    o = o + _dot(sc.astype(BF16), vb)
    return o, st_new


def _hgrn(proj, lbp, norm_g, consts, *, nbatch, seq, row_base, state_in=None, want_state=False):
    mat_f, mask_f, mat_b, mask_b = consts
    nseq = max(1, HG_CHAINS * HG_CHUNK // (2 * seq))
    rows = nseq * seq
    rb = row_base // rows

    def col(k):
        return pl.BlockSpec((rows, 128), lambda b, h: (rb + b, (B_OFF + k * B_WIDTH) // 128 + h))

    const2 = lambda shape: pl.BlockSpec(shape, lambda b, h: (0,) * len(shape))
    in_specs = [col(0), col(1), col(2), col(3), col(4),
                pl.BlockSpec((3, 2, 1, 128), lambda b, h: (0, 0, 0, h)),
                const2((1, 128)),
                const2(mat_f.shape), const2(mat_b.shape), const2(mask_f.shape), const2(mask_b.shape)]
    args = [proj, proj, proj, proj, proj, lbp, norm_g, mat_f, mat_b, mask_f, mask_b]
    state_spec = pl.BlockSpec((nseq, 2, 1, 128, 128), lambda b, h: (b, 0, h, 0, 0))
    if state_in is not None:
        in_specs.append(state_spec)
        args.append(state_in)
    out_specs = [pl.BlockSpec((rows, 128), lambda b, h: (b, h))]
    out_shape = [jax.ShapeDtypeStruct((nbatch * seq, B_WIDTH), BF16)]
    if want_state:
        out_specs.append(state_spec)
        out_shape.append(jax.ShapeDtypeStruct((nbatch, 2, B_HEADS, 128, 128), F32))
    kern = functools.partial(_hgrn_kernel_flat, seq=seq, nseq=nseq, has_state_in=state_in is not None,
                             has_state_out=want_state)
    return pl.pallas_call(
        kern,
        grid=(nbatch // nseq, B_HEADS),
        in_specs=in_specs,
        out_specs=out_specs,
        out_shape=out_shape,
        scratch_shapes=[pltpu.VMEM((rows, 128), F32)],
        compiler_params=_cparams(("arbitrary", "arbitrary")),
        name="hgrn_%d" % seq,
    )(*args)


def _hgrn_kernel_flat(*refs, seq, nseq, has_state_in, has_state_out):
    refs = list(refs)
    head = refs[:11]
    pos = 11
    s0_ref = None
    if has_state_in:
        s0_ref = refs[pos]
        pos += 1
    o_ref = refs[pos]
    pos += 1
    s_ref = None
    if has_state_out:
        s_ref = refs[pos]
        pos += 1
    acc_ref = refs[pos]
    _hgrn_body(*head, s0_ref, o_ref, s_ref, acc_ref, seq=seq, nseq=nseq)


def _hgrn_body(q_ref, i_ref, zf_ref, zb_ref, gt_ref, lb_ref, ng_ref, mf_ref, mb_ref, kf_ref, kb_ref,
               s0_ref, o_ref, s_ref, acc_ref, *, seq, nseq):
    L = HG_CHUNK
    n = seq // L
    acc_ref[...] = jnp.zeros_like(acc_ref)
    lbf = (lb_ref[0, 0], lb_ref[1, 0], lb_ref[2, 0])
    lbb = (lb_ref[0, 1], lb_ref[1, 1], lb_ref[2, 1])
    states = []
    for j in range(nseq):
        for d in range(2):
            if s0_ref is not None:
                states.append(s0_ref[j, d, 0].T)
            else:
                states.append(jnp.zeros((B_KEY_DIM, B_KEY_DIM), F32))

    def body(i, carry):
        carry = list(carry)
        for j in range(nseq):
            sf = pl.ds(pl.multiple_of(j * seq + i * L, L), L)
            sb = pl.ds(pl.multiple_of(j * seq + (n - 1 - i) * L, L), L)
            o_f, carry[2 * j] = _hgrn_chunk(q_ref[sf, :], i_ref[sf, :], zf_ref[sf, :], lbf, carry[2 * j],
                                            mf_ref[...], kf_ref, False)
            acc_ref[sf, :] += o_f
            o_b, carry[2 * j + 1] = _hgrn_chunk(q_ref[sb, :], i_ref[sb, :], zb_ref[sb, :], lbb, carry[2 * j + 1],
                                                mb_ref[...], kb_ref, True)
            acc_ref[sb, :] += o_b
        return tuple(carry)

    states = lax.fori_loop(0, n, body, tuple(states), unroll=HG_CHAINS // (2 * nseq))
    o_ref[...] = (_rms(acc_ref[...], ng_ref[...]) * _silu(gt_ref[...])).astype(BF16)
    if s_ref is not None:
        for j in range(nseq):
            for d in range(2):
                s_ref[j, d, 0] = states[2 * j + d].T


def _gelu_tanh(y):
    return 0.5 * y * (1.0 + jnp.tanh(0.7978845608028654 * (y + 0.044715 * (y * y * y))))


def _lru_body(xc_ref, yc_ref, cw_ref, cb_ref, wa_ref, ba_ref, wx_ref, bx_ref, lam_ref, h0_ref, o_ref, s_ref,
              xs_ref, a_ref, u_ref, *, rows, reset, independent):
    T = rows
    seg = T // LRU_SEGS
    row = lax.broadcasted_iota(jnp.int32, (T, 1), 0)
    seq_len = seg if independent else T
    pos = row % seg if independent else row
    pad = 8
    zeros = jnp.zeros((pad, 128), F32)
    xs_ref[0:pad, :] = zeros
    xs_ref[pad:pad + T, :] = xc_ref[...]
    xs_ref[pad + T:2 * pad + T, :] = zeros
    cw = cw_ref[...]
    x = cb_ref[...]
    for tap in range(4):
        off = tap - 2
        term = cw[tap:tap + 1] * xs_ref[pad + off:pad + off + T, :]
        if independent and off != 0:
            term = jnp.where((pos + off >= 0) & (pos + off < seq_len), term, 0.0)
        x = x + term
    xb = x.astype(BF16)
    pitch = seg + LRU_PITCH_PAD
    for d in range(2):
        rg = _sigmoid(_dot(xb, wa_ref[d, 0]) + ba_ref[d])
        ig = _sigmoid(_dot(xb, wx_ref[d, 0]) + bx_ref[d])
        log_a = (C_EXP * rg) * _log_sigmoid(lam_ref[d])
        a = jnp.exp(log_a)
        mult = jnp.sqrt(1.0 - a * a)
        if reset:
            mult = jnp.where(pos == (0 if d == 0 else seq_len - 1), 1.0, mult)
        u = mult * ig * x
        for r in range(LRU_SEGS):
            a_ref[d, r * pitch:r * pitch + seg, :] = a[r * seg:(r + 1) * seg]
            u_ref[d, r * pitch:r * pitch + seg, :] = u[r * seg:(r + 1) * seg]
    sub =lax.broadcasted_iota(jnp.int32, (LRU_SEGS, 1), 0)
    zrow = jnp.zeros((1, 128), F32)
    if h0_ref is not None:
        hf0 = jnp.where(sub == 0, h0_ref[0, 0:1, :], 0.0)
        hb0 = jnp.where(sub == LRU_SEGS - 1, h0_ref[0, 1:2, :], 0.0)
    else:
        hf0 = jnp.zeros((LRU_SEGS, 128), F32)
        hb0 = hf0
    ones = jnp.ones((LRU_SEGS, 128), F32)

    def local_scan(i, carry):
        hf, pf, hb, pb = carry
        sf = pl.ds(i, LRU_SEGS, stride=pitch)
        sb = pl.ds(seg - 1 - i, LRU_SEGS, stride=pitch)
        af = a_ref[0, sf, :]
        hf = af * hf + u_ref[0, sf, :]
        pf = af * pf
        u_ref[0, sf, :] = hf
        a_ref[0, sf, :] = pf
        ab = a_ref[1, sb, :]
        hb = ab * hb + u_ref[1, sb, :]
        pb = ab * pb
        u_ref[1, sb, :] = hb
        a_ref[1, sb, :] = pb
        return hf, pf, hb, pb

    hf, pf, hb, pb = lax.fori_loop(0, seg, local_scan, (hf0, ones, hb0, ones), unroll=4)
    if independent:
        for r in range(LRU_SEGS):
            sl = slice(r * seg, (r + 1) * seg)
            ps = slice(r * pitch, r * pitch + seg)
            o_ref[sl, :] = ((u_ref[0, ps, :] + u_ref[1, ps, :]) * _gelu_tanh(yc_ref[sl, :])).astype(BF16)
        if s_ref is not None:
            s_ref[:, 0, :] = hf
            s_ref[:, 1, :] = hb
        return
    carry_f = [zrow] * LRU_SEGS
    carry_b = [zrow] * LRU_SEGS
    end_f = hf[0:1]
    for r in range(1, LRU_SEGS):
        carry_f[r] = end_f
        end_f = hf[r:r + 1] + pf[r:r + 1] * end_f
    end_b = hb[LRU_SEGS - 1:LRU_SEGS]
    for r in range(LRU_SEGS - 2, -1, -1):
        carry_b[r] = end_b
        end_b = hb[r:r + 1] + pb[r:r + 1] * end_b
    for r in range(LRU_SEGS):
        sl = slice(r * seg, (r + 1) * seg)
        ps = slice(r * pitch, r * pitch + seg)
        h = (u_ref[0, ps, :] + a_ref[0, ps, :] * carry_f[r]) + (u_ref[1, ps, :] + a_ref[1, ps, :] * carry_b[r])
        o_ref[sl, :] = (h * _gelu_tanh(yc_ref[sl, :])).astype(BF16)
    if s_ref is not None:
        s_ref[0, 0:1, :] = end_f
        s_ref[0, 1:2, :] = end_b


def _lru_kernel_flat(*refs, rows, reset, independent, has_state_in, has_state_out):
    refs = list(refs)
    head = refs[:9]
    pos = 9
    h0_ref = None
    if has_state_in:
        h0_ref = refs[pos]
        pos += 1
    o_ref = refs[pos]
    pos += 1
    s_ref = None
    if has_state_out:
        s_ref = refs[pos]
        pos += 1
    _lru_body(*head, h0_ref, o_ref, s_ref, *refs[pos:], rows=rows, reset=reset, independent=independent)


def _lru(proj, lp, *, nbatch, seq, row_base, reset, state_in=None, want_state=False):
    independent = seq < LRU_SEGS * 128
    per_step = LRU_SEGS if independent else 1
    rows = seq * per_step
    nstep = nbatch // per_step
    rb = row_base // rows
    nblk = C_WIDTH // 128

    def col(k):
        return pl.BlockSpec((rows, 128), lambda b, c: (rb + b, (C_OFF + k * C_WIDTH) // 128 + c))

    vec = lambda rows: pl.BlockSpec((rows, 1, 128), lambda b, c: (0, 0, c))
    mat = pl.BlockSpec((2, 1, 128, 128), lambda b, c: (0, c, 0, 0))
    in_specs = [col(0), col(1),
                pl.BlockSpec((4, 128), lambda b, c: (0, c)), pl.BlockSpec((1, 128), lambda b, c: (0, c)),
                mat, vec(2), mat, vec(2), vec(2)]
    args = [proj, proj, lp['conv_w'], lp['conv_b'], lp['wa'], lp['ba'], lp['wx'], lp['bx'], lp['lam']]
    if state_in is not None:
        in_specs.append(pl.BlockSpec((1, 2, 128), lambda b, c: (b, 0, c)))
        args.append(state_in)
    out_specs = [pl.BlockSpec((rows, 128), lambda b, c: (b, c))]
    out_shape = [jax.ShapeDtypeStruct((nbatch * seq, C_WIDTH), BF16)]
    if want_state:
        out_specs.append(pl.BlockSpec((per_step, 2, 128), lambda b, c: (b, 0, c)))
        out_shape.append(jax.ShapeDtypeStruct((nbatch, 2, C_WIDTH), F32))
    kern = functools.partial(_lru_kernel_flat, rows=rows, reset=reset, independent=independent,
                             has_state_in=state_in is not None, has_state_out=want_state)
    scan_rows = rows + LRU_SEGS * LRU_PITCH_PAD
    return pl.pallas_call(
        kern,
        grid=(nstep, nblk),
        in_specs=in_specs,
        out_specs=out_specs,
        out_shape=out_shape,
        scratch_shapes=[pltpu.VMEM((rows + 16, 128), F32), pltpu.VMEM((2, scan_rows, 128), F32),
                        pltpu.VMEM((2, scan_rows, 128), F32)],
        compiler_params=_cparams(("arbitrary", "arbitrary")),
        name="lru_%d" % seq,
    )(*args)


MERGE_TM = 512
ROUTER_ROWS = 16


def _top2_combine(logits):
    idx = lax.broadcasted_iota(jnp.int32, logits.shape, 0).astype(F32)
    none = float(N_EXPERTS)
    m1 = jnp.max(logits, axis=0, keepdims=True)
    i1 = jnp.min(jnp.where(logits == m1, idx, none), axis=0, keepdims=True)
    rest = jnp.where(idx == i1, -jnp.inf, logits)
    m2 = jnp.max(rest, axis=0, keepdims=True)
    i2 = jnp.min(jnp.where(rest == m2, idx, none), axis=0, keepdims=True)
    e = jnp.exp(m2 - m1)
    w1 = 1.0 / (1.0 + e)
    return jnp.where(idx == i1, w1, 0.0) + jnp.where(idx == i2, e * w1, 0.0)


def _merge_body(xc_ref, xl_ref, m_ref, z0_ref, z1_ref, z2_ref, ac_ref, al_ref, hc_ref, hl_ref, lc_ref, ll_ref,
                wba_ref, wbb_ref, wbc_ref, wo_ref, g_ref, r_ref, xoc_ref, xol_ref, u_ref, c_ref):
    tm = MERGE_TM
    mg = (z0_ref[...].astype(F32) * _dot(_pair_read(ac_ref, al_ref, tm), wba_ref[...])
          + z1_ref[...].astype(F32) * _dot(_pair_read(hc_ref, hl_ref, tm), wbb_ref[...])
          + z2_ref[...].astype(F32) * _dot(_pair_read(lc_ref, ll_ref, tm), wbc_ref[...]))
    x = _pair_read(xc_ref, xl_ref, tm) + m_ref[0, 2] * _dot(mg.astype(BF16), wo_ref[...])
    _pair_write(xoc_ref, xol_ref, tm, x)
    u =_rms(x, g_ref[...]) * (1.0 + m_ref[0, 4]) + m_ref[0, 3]
    ub = u.astype(BF16)
    u_ref[...] = ub
    if r_ref is not None:
        ul = (u - ub.astype(F32)).astype(BF16)
        logits = _dot_nt(r_ref[0], ub) + (_dot_nt(r_ref[1], ub) + _dot_nt(r_ref[0], ul))
        comb = _top2_combine(logits[0:N_EXPERTS])
        comb = jnp.concatenate([comb, jnp.zeros((128 - N_EXPERTS, tm), F32)], axis=0)
        c_ref[...] = comb.T


def _merge_kernel_dense(*refs):
    _merge_body(*refs[:17], None, *refs[17:], None)


def _merge(x_pair, mod_l, gates, mixers, lp, router=None):
    tm = MERGE_TM
    row = lambda i: (i, 0)
    const = lambda i: (0, 0)
    res = lambda shape: pl.BlockSpec(shape, const, pipeline_mode=pl.Buffered(1))
    in_specs = _pair_specs(tm) + [
        pl.BlockSpec((1, 6, 1, D_MODEL), lambda i: (_mod_row(i, tm), 0, 0, 0)),
        pl.BlockSpec((tm, D_MODEL), lambda i: (i, 0)),
        pl.BlockSpec((tm, D_MODEL), lambda i: (i, 1)),
        pl.BlockSpec((tm, D_MODEL), lambda i: (i, 2))]
    in_specs += _pair_specs(tm, 512) * 3
    in_specs += [res((A_WIDTH, D_MODEL)), res((B_WIDTH, D_MODEL)), res((C_WIDTH, D_MODEL)), res((D_MODEL, D_MODEL)),
                 pl.BlockSpec((1, D_MODEL), const)]
    args = [*x_pair, mod_l, gates, gates, gates, *mixers, lp['w_ba'], lp['w_bb'], lp['w_bc'], lp['w_out'],
            lp['norm_ffn']]
    out_specs = _pair_specs(tm) + [pl.BlockSpec((tm, D_MODEL), row)]
    out_shape = _pair_shapes(D_MODEL, F32) + [jax.ShapeDtypeStruct((N_TOK, D_MODEL), BF16)]
    kern = _merge_kernel_dense
    if router is not None:
        in_specs.append(pl.BlockSpec((2, ROUTER_ROWS, D_MODEL), lambda i: (0, 0, 0)))
        args.append(router)
        out_specs.append(pl.BlockSpec((tm, 128), row))
        out_shape.append(jax.ShapeDtypeStruct((N_TOK, 128), F32))
        kern = _merge_body
    return pl.pallas_call(
        kern,
        grid=(N_TOK // tm,),
        in_specs=in_specs,
        out_specs=out_specs,
        out_shape=out_shape,
        compiler_params=_cparams(("arbitrary",)),
        name="merge",
    )(*args)


FFN_TM = 512
FFN_CHUNK = 256


def _swiglu_partial(u, wg, wu, wd):
    h = (_silu(_dot(u, wg)) * _dot(u, wu)).astype(BF16)
    return _dot(h, wd)


def _ffn_kernel(u_ref, xc_ref, xl_ref, m_ref, wg_ref, wu_ref, wd_ref, oc_ref, ol_ref):
    u = u_ref[...]
    acc = None
    for c in range(FFN_DIM // FFN_CHUNK):
        sl = slice(c * FFN_CHUNK, (c + 1) * FFN_CHUNK)
        y = _swiglu_partial(u, wg_ref[:, sl], wu_ref[:, sl], wd_ref[sl, :])
        acc = y if acc is None else acc + y
    _pair_write(oc_ref, ol_ref, FFN_TM, _pair_read(xc_ref, xl_ref, FFN_TM) + m_ref[0, 5] * acc)


def _ffn(u, x_pair, mod_l, wg, wu, wd):
    tm = FFN_TM
    res = lambda shape: pl.BlockSpec(shape, lambda i: (0, 0), pipeline_mode=pl.Buffered(1))
    return pl.pallas_call(
        _ffn_kernel,
        grid=(N_TOK // tm,),
        in_specs=[pl.BlockSpec((tm, D_MODEL), lambda i: (i, 0))] + _pair_specs(tm) + [
            pl.BlockSpec((1, 6, 1, D_MODEL), lambda i: (_mod_row(i, tm), 0, 0, 0)),
            res((D_MODEL, FFN_DIM)), res((D_MODEL, FFN_DIM)), res((FFN_DIM, D_MODEL))],
        out_specs=_pair_specs(tm),
        out_shape=_pair_shapes(D_MODEL, F32),
        compiler_params=_cparams(("arbitrary",)),
        name="ffn",
    )(u, *x_pair, mod_l, wg, wu, wd)


MOE_TM = 512
MOE_NT = N_TOK // MOE_TM
MOE_PIECE = 16
MOE_SLAB = 2 * MOE_TM + N_EXPERTS * MOE_PIECE
MOE_BLK = 512
MOE_NB = -(-(2 * N_TOK + MOE_NT * N_EXPERTS * (MOE_PIECE - 1) + N_EXPERTS * (MOE_BLK - 1)) // MOE_BLK)
MOE_ROWS = MOE_NB * MOE_BLK
MOE_FCHUNK = 512


def _moe_count_kernel(c_ref, n_ref):
    n_ref[0] = jnp.sum(jnp.where(c_ref[...] > 0.0, 1.0, 0.0), axis=0, keepdims=True)


def _moe_count(comb):
    return pl.pallas_call(
        _moe_count_kernel,
        grid=(MOE_NT,),
        in_specs=[pl.BlockSpec((MOE_TM, 128), lambda i: (i, 0))],
        out_specs=pl.BlockSpec((1, 1, 128), lambda i: (i, 0, 0)),
        out_shape=jax.ShapeDtypeStruct((MOE_NT, 1, 128), F32),
        compiler_params=_cparams(("arbitrary",)),
        name="moe_count",
    )(comb)


def _moe_tables(counts):
    cnt = counts[:, 0, :N_EXPERTS].astype(jnp.int32)
    pc = (cnt + MOE_PIECE - 1) // MOE_PIECE * MOE_PIECE
    lo = jnp.cumsum(pc, axis=1) - pc
    tot = jnp.sum(pc, axis=0)
    totp = (tot + MOE_BLK - 1) // MOE_BLK * MOE_BLK
    goff = (jnp.cumsum(totp) - totp)[None, :] + jnp.cumsum(pc, axis=0) - pc
    ends = jnp.cumsum(totp // MOE_BLK)
    nact = ends[-1]
    s = jnp.minimum(jnp.arange(MOE_NB, dtype=jnp.int32), nact - 1)
    blk_e = jnp.minimum(jnp.sum((s[:, None] >= ends[None, :]).astype(jnp.int32), axis=1), N_EXPERTS - 1)
    lo_vec = jnp.pad(lo.astype(F32), ((0, 0), (0, 128 - N_EXPERTS))).reshape(MOE_NT, 1, 128)
    return dict(lo=lo.reshape(-1), npc=(pc // MOE_PIECE).reshape(-1), goff=goff.reshape(-1),
                nrow=jnp.sum(pc, axis=1), blk_e=blk_e, nact=nact.reshape(1), lo_vec=lo_vec)


def _moe_pieces(tile, lo_s, npc_s, goff_s, make, wait):
    for e in range(N_EXPERTS):
        k = tile * N_EXPERTS + e

        def body(p, carry, k=k):
            local = pl.multiple_of(lo_s[k] + p * MOE_PIECE, MOE_PIECE)
            glob = pl.multiple_of(goff_s[k] + p * MOE_PIECE, MOE_PIECE)
            cp = make(local, glob)
            if wait:
                cp.wait()
            else:
                cp.start()
            return carry

        lax.fori_loop(0, npc_s[k], body, 0)


def _slot_pair(slot):
    sel = slot >= 0.0
    s_hi = jnp.max(slot, axis=-1, keepdims=True)
    s_lo = jnp.min(jnp.where(sel, slot, float(MOE_SLAB)), axis=-1, keepdims=True)
    return sel, s_hi, s_lo


def _one_hot_rows(s):
    srow = lax.broadcasted_iota(jnp.int32, (MOE_TM, MOE_SLAB), 1).astype(F32)
    return jnp.where(srow == s, 1.0, 0.0)


def _moe_gather_kernel(lo_s, npc_s, goff_s, u_ref, c_ref, lov_ref, tri_ref, init_ref, xs_ref, slot_ref,
                       slab_ref, sem):
    del init_ref
    i = pl.program_id(0)
    routed = c_ref[...] > 0.0
    rank = _dot(tri_ref[...], jnp.where(routed, 1.0, 0.0).astype(BF16))
    slot = jnp.where(routed, lov_ref[0] + rank, -1.0)
    slot_ref[...] = slot
    _, s_hi, s_lo = _slot_pair(slot)
    pt = jnp.maximum(_one_hot_rows(s_hi), _one_hot_rows(s_lo)).astype(BF16)
    buf = i % 2
    slab_ref[buf] = _dot_tn(pt, u_ref[...]).astype(BF16)

    def pieces(tile, b, wait):
        _moe_pieces(tile, lo_s, npc_s, goff_s, lambda local, glob: pltpu.make_async_copy(
            slab_ref.at[b, pl.ds(local, MOE_PIECE), :], xs_ref.at[pl.ds(glob, MOE_PIECE), :], sem.at[b]), wait)

    pieces(i, buf, False)

    @pl.when(i > 0)
    def _():
        pieces(i - 1, 1 - buf, True)

    @pl.when(i == MOE_NT - 1)
    def _():
        pieces(i, buf, True)


def _moe_gather(u, comb, tab):
    tri = jnp.asarray(np.tril(np.ones((MOE_TM, MOE_TM), np.float32), -1), BF16)
    init = jnp.zeros((MOE_ROWS, D_MODEL), BF16)
    row = lambda i, *_: (i, 0)
    grid_spec = pltpu.PrefetchScalarGridSpec(
        num_scalar_prefetch=3,
        grid=(MOE_NT,),
        in_specs=[pl.BlockSpec((MOE_TM, D_MODEL), row), pl.BlockSpec((MOE_TM, 128), row),
                  pl.BlockSpec((1, 1, 128), lambda i, *_: (i, 0, 0)),
                  pl.BlockSpec((MOE_TM, MOE_TM), lambda i, *_: (0, 0)),
                  pl.BlockSpec(memory_space=pl.ANY)],
        out_specs=[pl.BlockSpec(memory_space=pl.ANY), pl.BlockSpec((MOE_TM, 128), row)],
        scratch_shapes=[pltpu.VMEM((2, MOE_SLAB, D_MODEL), BF16), pltpu.SemaphoreType.DMA((2,))],
    )
    return pl.pallas_call(
        _moe_gather_kernel,
        grid_spec=grid_spec,
        out_shape=[jax.ShapeDtypeStruct((MOE_ROWS, D_MODEL), BF16), jax.ShapeDtypeStruct((N_TOK, 128), F32)],
        input_output_aliases={7: 0},
        compiler_params=_cparams(("arbitrary",)),
        name="moe_gather",
    )(tab['lo'], tab['npc'], tab['goff'], u, comb, tab['lo_vec'], tri, init)


class _ExpertWeights:
    def __init__(self, hbm, resident, stages, sem):
        self.hbm, self.resident, self.stages, self.sem = hbm, resident, stages, sem

    def _copies(self, e, c):
        sl = slice(c * MOE_FCHUNK, (c + 1) * MOE_FCHUNK)
        b = c % 2
        srcs = (self.hbm[0].at[e, :, sl], self.hbm[1].at[e, :, sl], self.hbm[2].at[e, sl, :])
        return [pltpu.make_async_copy(src, stage.at[b], self.sem.at[b, k])
                for k, (src, stage) in enumerate(zip(srcs, self.stages))]

    def start(self, e, c):
        for cp in self._copies(e, c):
            cp.start()

    def finish(self, e, c):
        for cp in self._copies(e, c):
            cp.wait()
        sl = slice(c * MOE_FCHUNK, (c + 1) * MOE_FCHUNK)
        b = c % 2
        self.resident[0][:, sl] = self.stages[0][b].astype(BF16)
        self.resident[1][:, sl] = self.stages[1][b].astype(BF16)
        self.resident[2][sl, :] = self.stages[2][b].astype(BF16)


def _moe_block(x_ref, y_ref, w, next_expert):
    nch = EXPERT_DIM // MOE_FCHUNK
    wg_ref, wu_ref, wd_ref = w.resident
    if next_expert is not None:
        w.start(next_expert, 0)
        w.start(next_expert, 1)
    x = x_ref[...]
    acc = None
    for c in range(nch):
        sl = slice(c * MOE_FCHUNK, (c + 1) * MOE_FCHUNK)
        y = _swiglu_partial(x, wg_ref[:, sl], wu_ref[:, sl], wd_ref[sl, :])
        acc = y if acc is None else acc + y
        if next_expert is not None:
            w.finish(next_expert, c)
            if c + 2 < nch:
                w.start(next_expert, c + 2)
    y_ref[...] = acc.astype(BF16)


def _moe_expert_kernel(be_s, nact_s, x_ref, wg_hbm, wu_hbm, wd_hbm, y_ref, wg_ref, wu_ref, wd_ref,
                       g_stage, u_stage, d_stage, sem):
    s = pl.program_id(0)
    nch = EXPERT_DIM // MOE_FCHUNK
    active = s < nact_s[0]
    e = be_s[s]
    e_next = be_s[jnp.minimum(s + 1, MOE_NB - 1)]
    hands_over = e_next != e
    w = _ExpertWeights((wg_hbm, wu_hbm, wd_hbm), (wg_ref, wu_ref, wd_ref), (g_stage, u_stage, d_stage), sem)

    @pl.when(s == 0)
    def _():
        w.start(e, 0)
        for c in range(nch):
            if c + 1 < nch:
                w.start(e, c + 1)
            w.finish(e, c)

    @pl.when(jnp.logical_not(active))
    def _():
        y_ref[...] = jnp.zeros_like(y_ref)

    @pl.when(active & hands_over)
    def _():
        _moe_block(x_ref, y_ref, w, e_next)

    @pl.when(active & jnp.logical_not(hands_over))
    def _():
        _moe_block(x_ref, y_ref, w, None)


def _moe_expert(xs, tab, wg, wu, wd):
    blk = lambda s, be, nact: (jnp.minimum(s, nact[0] - 1), 0)
    hbm = pl.BlockSpec(memory_space=pl.ANY)
    grid_spec = pltpu.PrefetchScalarGridSpec(
        num_scalar_prefetch=2,
        grid=(MOE_NB,),
        in_specs=[pl.BlockSpec((MOE_BLK, D_MODEL), blk), hbm, hbm, hbm],
        out_specs=pl.BlockSpec((MOE_BLK, D_MODEL), lambda s, be, nact: (s, 0)),
        scratch_shapes=[pltpu.VMEM((D_MODEL, EXPERT_DIM), BF16), pltpu.VMEM((D_MODEL, EXPERT_DIM), BF16),
                        pltpu.VMEM((EXPERT_DIM, D_MODEL), BF16),
                        pltpu.VMEM((2, D_MODEL, MOE_FCHUNK), F32), pltpu.VMEM((2, D_MODEL, MOE_FCHUNK), F32),
                        pltpu.VMEM((2, MOE_FCHUNK, D_MODEL), F32), pltpu.SemaphoreType.DMA((2, 3))],
    )
    return pl.pallas_call(
        _moe_expert_kernel,
        grid_spec=grid_spec,
        out_shape=jax.ShapeDtypeStruct((MOE_ROWS, D_MODEL), BF16),
        compiler_params=_cparams(("arbitrary",)),
        name="moe_expert",
    )(tab['blk_e'], tab['nact'], xs, wg, wu, wd)


def _moe_combine_kernel(lo_s, npc_s, goff_s, nrow_s, y_ref, slot_ref, c_ref, xc_ref, xl_ref, m_ref, g_ref,
                        oc_ref, ol_ref, slab_ref, sem):
    i = pl.program_id(0)
    buf = i % 2

    def pieces(tile, b, wait):
        _moe_pieces(tile, lo_s, npc_s, goff_s, lambda local, glob: pltpu.make_async_copy(
            y_ref.at[pl.ds(glob, MOE_PIECE), :], slab_ref.at[b, pl.ds(local, MOE_PIECE), :], sem.at[b]), wait)

    @pl.when(i == 0)
    def _():
        pieces(i, buf, False)

    @pl.when(i + 1 < MOE_NT)
    def _():
        pieces(i + 1, 1 - buf, False)

    slot = slot_ref[...]
    comb = c_ref[...]
    sel, s_hi, s_lo = _slot_pair(slot)
    w_hi = jnp.sum(jnp.where(sel & (slot == s_hi), comb, 0.0), axis=-1, keepdims=True)
    w_lo = jnp.sum(jnp.where(sel & (slot == s_lo), comb, 0.0), axis=-1, keepdims=True)
    w_lo = jnp.where(s_lo == s_hi, 0.0, w_lo)
    srow = lax.broadcasted_iota(jnp.int32, (MOE_SLAB, 1), 0)
    pieces(i, buf, True)
    ys = jnp.where(srow < nrow_s[i], slab_ref[buf], jnp.zeros((), BF16))
    out = (w_hi * _dot(_one_hot_rows(s_hi).astype(BF16), ys) + w_lo * _dot(_one_hot_rows(s_lo).astype(BF16), ys))
    x = _pair_read(xc_ref, xl_ref, MOE_TM)
    _pair_write(oc_ref, ol_ref, MOE_TM, _rms(x + m_ref[0, 5] * out, g_ref[...]))


def _moe_combine(y, slot, comb, x_pair, mod_l, final_g, tab):
    row = lambda i, *_: (i, 0)
    grid_spec = pltpu.PrefetchScalarGridSpec(
        num_scalar_prefetch=4,
        grid=(MOE_NT,),
        in_specs=[pl.BlockSpec(memory_space=pl.ANY),
                  pl.BlockSpec((MOE_TM, 128), row), pl.BlockSpec((MOE_TM, 128), row)] + _pair_specs(MOE_TM) + [
                  pl.BlockSpec((1, 6, 1, D_MODEL), lambda i, *_: (_mod_row(i, MOE_TM), 0, 0, 0)),
                  pl.BlockSpec((1, D_MODEL), lambda i, *_: (0, 0))],
        out_specs=_pair_specs(MOE_TM),
        scratch_shapes=[pltpu.VMEM((2, MOE_SLAB, D_MODEL), BF16), pltpu.SemaphoreType.DMA((2,))],
    )
    return pl.pallas_call(
        _moe_combine_kernel,
        grid_spec=grid_spec,
        out_shape=_pair_shapes(D_MODEL, F32),
        compiler_params=_cparams(("arbitrary",)),
        name="moe_combine",
    )(tab['lo'], tab['npc'], tab['goff'], tab['nrow'], y, slot, comb, *x_pair, mod_l, final_g)


def _moe(u, comb, x_pair, mod_l, final_g, wg, wu, wd):
    tab = _moe_tables(_moe_count(comb))
    xs, slot = _moe_gather(u, comb, tab)
    y = _moe_expert(xs, tab, wg, wu, wd)
    return _moe_combine(y, slot, comb, x_pair, mod_l, final_g, tab)


def _block_diag(w):
    w = w.reshape(2, C_WIDTH // 128, 2, C_BLOCK, C_BLOCK)
    z = jnp.zeros_like(w[:, :, 0])
    top = jnp.concatenate([w[:, :, 0], z], axis=-1)
    bot = jnp.concatenate([z, w[:, :, 1]], axis=-1)
    return jnp.concatenate([top, bot], axis=-2).astype(BF16)


def kernel(x_prompt, x_sample, cache_attn_k, cache_attn_v, state_hgrn, state_lru, c, c_ctx, w_mod, b_mod, norm_mix_g, norm_ffn_g, w_in, attn_sink, hgrn_lower, hgrn_norm_g, lru_conv_w, lru_conv_b, lru_wa, lru_ba, lru_wx, lru_bx, lru_lambda, w_branch_a, w_branch_b, w_branch_c, w_out, ffn_w_gate, ffn_w_up, ffn_w_down, moe_router, moe_w_gate, moe_w_up, moe_w_down, final_norm_g):
    assert DEPTH == 2
    x = (x_prompt.reshape(N_CTX_TOK, D_MODEL), x_sample.reshape(N_LAT_TOK, D_MODEL))
    cond =jnp.concatenate([c_ctx[None, :], c, jnp.zeros((N_COND - 1 - DEC_BATCH, D_MODEL), F32)], axis=0)
    mods = _modulation(cond, w_mod, b_mod).reshape(DEPTH, N_COND, 6, 1, D_MODEL)

    lb_cum = jnp.cumsum(jax.nn.softmax(hgrn_lower.astype(F32), axis=0), axis=0)
    lb_layers = lb_cum - lb_cum[:1]
    cos, sin = _rope_tables()
    hconst = _hgrn_constants(False) + _hgrn_constants(True)

    ks, vs, hs, ls = [], [], [], []
    for l in range(DEPTH):
        lb = lb_layers[l]
        lbp = jnp.stack([jnp.log(lb), jnp.log1p(-lb), 1.0 - lb]).reshape(3, 2, 1, B_WIDTH)
        lp = dict(
            conv_w=lru_conv_w[l], conv_b=lru_conv_b[l].reshape(1, C_WIDTH),
            wa=_block_diag(lru_wa[l]), wx=_block_diag(lru_wx[l]),
            ba=lru_ba[l].reshape(2, 1, C_WIDTH), bx=lru_bx[l].reshape(2, 1, C_WIDTH),
            lam=lru_lambda[l].reshape(2, 1, C_WIDTH),
            w_ba=w_branch_a[l].astype(BF16), w_bb=w_branch_b[l].astype(BF16), w_bc=w_branch_c[l].astype(BF16),
            w_out=w_out[l].astype(BF16), norm_ffn=norm_ffn_g[l].reshape(1, D_MODEL))
        mod_l = mods[l]

        proj, gates = _in_proj(x, mod_l, norm_mix_g[l].reshape(1, D_MODEL), w_in, l)

        attn_c = _ctx_attention(proj, attn_sink[l])
        attn_l = _lat_attention(proj, attn_sink[l], cache_attn_k[:, l].reshape(DEC_BATCH, PAST_LEN, 128),
                                cache_attn_v[:, l].reshape(DEC_BATCH, PAST_LEN, 128), cos, sin)

        ng = hgrn_norm_g[l].reshape(1, B_KEY_DIM)
        hg_c, s_h = _hgrn(proj, lbp, ng, hconst, nbatch=BATCH, seq=SEQ, row_base=0, want_state=True)
        hg_l, = _hgrn(proj, lbp, ng, hconst, nbatch=DEC_BATCH, seq=DEC_SEQ, row_base=N_CTX_TOK,
                      state_in=state_hgrn[:, l])

        lr_c, s_l = _lru(proj, lp, nbatch=BATCH, seq=SEQ, row_base=0, reset=True, want_state=True)
        lr_l, = _lru(proj, lp, nbatch=DEC_BATCH, seq=DEC_SEQ, row_base=N_CTX_TOK, reset=False,
                     state_in=state_lru[:, l])
        mixers = (attn_c, attn_l, hg_c, hg_l, lr_c, lr_l)

        ks.append(proj[:N_CTX_TOK, KA_OFF:KA_OFF + 128].reshape(BATCH, SEQ, A_KV_HEADS, A_HEAD_DIM))
        vs.append(proj[:N_CTX_TOK, VA_OFF:VA_OFF + 128].reshape(BATCH, SEQ, A_KV_HEADS, A_HEAD_DIM))
        hs.append(s_h)
        ls.append(s_l)

        m = l // 2
        if l % 2 == 0:
            xc, xl, u = _merge(x, mod_l, gates, mixers, lp)
            x = _ffn(u, (xc, xl), mod_l, ffn_w_gate[m].astype(BF16), ffn_w_up[m].astype(BF16), ffn_w_down[m].astype(BF16))
        else:
            router = jnp.pad(moe_router[m].T, ((0, ROUTER_ROWS - N_EXPERTS), (0, 0)))
            r_hi = router.astype(BF16)
            router = jnp.stack([r_hi, (router - r_hi.astype(F32)).astype(BF16)])
            xc, xl, u, comb = _merge(x, mod_l, gates, mixers, lp, router=router)
            x = _moe(u, comb, (xc, xl), mod_l, final_norm_g.reshape(1, D_MODEL), moe_w_gate[m], moe_w_up[m],
                     moe_w_down[m])

    y_prompt = x[0].reshape(BATCH, SEQ, D_MODEL)
    y_sample = x[1].reshape(DEC_BATCH, DEC_SEQ, D_MODEL)
    return (y_prompt, y_sample, jnp.stack(ks, axis=1), jnp.stack(vs, axis=1), jnp.stack(hs, axis=1),
            jnp.stack(ls, axis=1))
```
